```python
import jax, jax.numpy as jnp
from jax import lax
import numpy as np

D_MODEL = 1024
BATCH = 8
SEQ = 2048
DEPTH = 2
DEC_BATCH = 128
DEC_SEQ = 4
PAST_LEN = 16384
PAGE_SIZE = 128

N_BRANCH = 4
MIX_W = D_MODEL // N_BRANCH
POOL_WINDOWS = (2, 4, 8, 16)
POOL_GROUPS = len(POOL_WINDOWS)
POOL_GW = MIX_W // POOL_GROUPS
POOL_PAST = max(POOL_WINDOWS) - 1
RET_HEADS = 4
RET_DK = MIX_W // RET_HEADS
RET_DV = MIX_W // RET_HEADS
RET_CHUNK = 128
ROPE_BASE = 10000.0
SC_WIDTH = 3
RWKV_HEADS = 4
RWKV_HD = MIX_W // RWKV_HEADS
LORA_W = 64
LORA_A = 64
LORA_G = 128
RWKV_COLS = 3 * MIX_W + LORA_W + LORA_A + LORA_G
D_FF = 2816
FFN_WIDTH = 3
NORM_EPS = 1e-6
GN_EPS = 1e-6
RWKV_LN_EPS = 64e-5
L2_EPS = 1e-12

COL_POOL = 0
COL_RET = COL_POOL + MIX_W
COL_SC = COL_RET + 4 * MIX_W
COL_RWKV = COL_SC + 3 * MIX_W
COL_GATE = COL_RWKV + RWKV_COLS
IN_COLS = COL_GATE + N_BRANCH * D_MODEL

kernel_name = 'hybrid_pool_ret_sconv_rwkv7_step'


def rmsnorm(x, g):
    xf = x.astype(jnp.float32)
    y = xf * lax.rsqrt(jnp.mean(xf * xf, axis=-1, keepdims=True) + NORM_EPS)
    return (y * g.astype(jnp.float32)).astype(x.dtype)


def causal_dwconv(z, past, w):
    full = jnp.concatenate([past.astype(z.dtype), z], axis=1)
    T = z.shape[1]
    K = w.shape[0]
    y = sum(full[:, k:k + T] * w[k] for k in range(K))
    return y, full[:, -(K - 1):]


def pool_mixer(u, past, pos, pool_w, pool_scale):
    B, T, _ = u.shape
    full = jnp.concatenate([past.astype(u.dtype), u], axis=1)
    cs = jnp.cumsum(full.astype(jnp.float32), axis=1)
    cs = jnp.concatenate([jnp.zeros_like(cs[:, :1]), cs], axis=1)
    hi = cs[:, POOL_PAST + 1:POOL_PAST + 1 + T]
    means = []
    for g, w in enumerate(POOL_WINDOWS):
        sl = slice(g * POOL_GW, (g + 1) * POOL_GW)
        lo = cs[:, POOL_PAST + 1 - w:POOL_PAST + 1 - w + T, sl]
        cnt = jnp.minimum(w, pos + 1).astype(jnp.float32)[None, :, None]
        means.append((hi[..., sl] - lo) / cnt)
    m = jnp.concatenate(means, axis=-1) - u.astype(jnp.float32)
    m = m.astype(u.dtype).reshape(B, T, POOL_GROUPS, POOL_GW)
    y = jnp.einsum('btgc,gcd->btgd', m, pool_w).reshape(B, T, MIX_W) * pool_scale
    return y, full[:, -POOL_PAST:]


def rotary(x, pos):
    half = x.shape[-1] // 2
    inv = ROPE_BASE ** (-jnp.arange(half, dtype=jnp.float32) / half)
    ang = pos.astype(jnp.float32)[:, None] * inv[None, :]
    cos = jnp.cos(ang)[None, :, None, :]
    sin = jnp.sin(ang)[None, :, None, :]
    xf = x.astype(jnp.float32)
    x1, x2 = xf[..., :half], xf[..., half:]
    return jnp.concatenate([x1 * cos - x2 * sin, x1 * sin + x2 * cos], axis=-1).astype(x.dtype)


def retention(q, k, v, S0):
    f32 = jnp.float32
    B, T, H, DK = q.shape
    C = RET_CHUNK if T % RET_CHUNK == 0 else T
    N = T // C
    lg = jnp.log1p(-(2.0 ** (-5.0 - jnp.arange(H, dtype=f32))))
    idx = jnp.arange(C, dtype=f32)
    diff = idx[:, None] - idx[None, :]
    dmask = jnp.where(diff >= 0, jnp.exp(lg[:, None, None] * jnp.maximum(diff, 0.0)), 0.0)
    q_dec = jnp.exp(lg[:, None] * (idx[None, :] + 1.0))
    k_dec = jnp.exp(lg[:, None] * (C - 1.0 - idx[None, :]))
    c_dec = jnp.exp(lg * C)

    def to_chunks(t):
        return t.astype(f32).reshape(B, N, C, H, t.shape[-1]).transpose(1, 0, 2, 3, 4)

    def step(S, inp):
        qb, kb, vb = inp
        s = jnp.einsum('bihd,bjhd->bhij', qb, kb) * dmask
        o = jnp.einsum('bhij,bjhe->bihe', s, vb)
        o = o + jnp.einsum('bihd,hi,bhde->bihe', qb, q_dec, S)
        S = S * c_dec[None, :, None, None] + jnp.einsum('bjhd,hj,bjhe->bhde', kb, k_dec, vb)
        return S, o

    S, o = lax.scan(step, S0.astype(f32), (to_chunks(q), to_chunks(k), to_chunks(v)))
    o = o.transpose(1, 0, 2, 3, 4).reshape(B, T, H, v.shape[-1])
    return o, S


def rwkv7_scan(r, w, k, v, a_vec, b_vec, S0):
    def step(S, inp):
        rt, wt, kt, vt, at, bt = inp
        S = (S * wt[:, :, None, :]
             + jnp.einsum('bhij,bhj->bhi', S, at)[..., None] * bt[:, :, None, :]
             + vt[..., :, None] * kt[..., None, :])
        return S, jnp.einsum('bhij,bhj->bhi', S, rt)

    xs = tuple(jnp.moveaxis(t, 1, 0) for t in (r, w, k, v, a_vec, b_vec))
    S, y = lax.scan(step, S0, xs)
    return jnp.moveaxis(y, 0, 1), S


def rwkv7_mixer(z, shift_prev, S0, p):
    f32 = jnp.float32
    B, T, _ = z.shape
    prev = jnp.concatenate([shift_prev[:, None].astype(z.dtype), z[:, :-1]], axis=1)
    zs = z + (prev - z) * p['rw_mu']
    o = 3 * MIX_W
    r = zs[..., :MIX_W].astype(f32)
    k = zs[..., MIX_W:2 * MIX_W].astype(f32)
    v = zs[..., 2 * MIX_W:o].astype(f32)
    wl = zs[..., o:o + LORA_W]
    al = zs[..., o + LORA_W:o + LORA_W + LORA_A]
    gl = zs[..., o + LORA_W + LORA_A:]
    w = -jax.nn.softplus(-(p['rw_w0'] + jnp.tanh(wl) @ p['rw_w_lora']).astype(f32)) - 0.5
    decay = jnp.exp(-jnp.exp(w))
    a = jax.nn.sigmoid((p['rw_a0'] + al @ p['rw_a_lora']).astype(f32))
    g = (jax.nn.sigmoid(gl) @ p['rw_g_lora']).astype(f32)

    def hs(t):
        return t.reshape(B, T, RWKV_HEADS, RWKV_HD)

    kk = hs(k * p['rw_k_k'].astype(f32))
    kk = kk / jnp.maximum(jnp.sqrt(jnp.sum(kk * kk, axis=-1, keepdims=True)), L2_EPS)
    k = k * (1.0 + (a - 1.0) * p['rw_k_a'].astype(f32))
    rh, kh, vh = hs(r), hs(k), hs(v)
    y, S = rwkv7_scan(rh, hs(decay), kh, vh, -kk, kk * hs(a), S0.astype(f32))
    mean = jnp.mean(y, axis=-1, keepdims=True)
    var = jnp.mean(jnp.square(y - mean), axis=-1, keepdims=True)
    yn = ((y - mean) * lax.rsqrt(var + RWKV_LN_EPS)).reshape(B, T, MIX_W)
    yn = yn * p['rw_ln_g'].astype(f32) + p['rw_ln_b'].astype(f32)
    bonus = jnp.sum(rh * kh * p['rw_r_k'].astype(f32), axis=-1, keepdims=True) * vh
    out = (yn + bonus.reshape(B, T, MIX_W)) * g
    return out.astype(z.dtype), z[:, -1], S


def block(x, c, pos0, states, p):
    st_pool, st_ret, st_sc, st_shift, st_wkv, st_ffn = states
    f32 = jnp.float32
    B, T, _ = x.shape
    pos = pos0 + jnp.arange(T, dtype=jnp.int32)
    mod = (jax.nn.silu(c) @ p['w_ada'] + p['b_ada']).reshape(B, 6, 1, D_MODEL)
    h = rmsnorm(x, p['norm1_g']) * (1 + mod[:, 1]) + mod[:, 0]
    z = h @ p['w_in']

    yA, n_pool = pool_mixer(z[..., COL_POOL:COL_RET], st_pool, pos, p['pool_w'], p['pool_scale'])

    zr = z[..., COL_RET:COL_SC].reshape(B, T, 4, RET_HEADS, RET_DK)
    q = rotary(zr[:, :, 0], pos)
    kr = rotary(zr[:, :, 1], pos) * (RET_DK ** -0.5)
    o, n_ret = retention(q, kr, zr[:, :, 2], st_ret)
    o = o * lax.rsqrt(jnp.mean(o * o, axis=-1, keepdims=True) + GN_EPS)
    yB = (o.reshape(B, T, MIX_W) * jax.nn.silu(zr[:, :, 3].reshape(B, T, MIX_W).astype(f32))).astype(x.dtype)

    zc = z[..., COL_SC:COL_RWKV]
    conv_y, n_sc = causal_dwconv(zc[..., 2 * MIX_W:] * zc[..., :MIX_W], st_sc, p['sc_w'])
    yC = zc[..., MIX_W:2 * MIX_W] * conv_y

    yD, n_shift, n_wkv = rwkv7_mixer(z[..., COL_RWKV:COL_GATE], st_shift, st_wkv, p)

    branches = jnp.stack([yA, yB, yC, yD], axis=2)
    proj = jnp.einsum('btnc,ncd->btnd', branches, p['w_br'])
    gates = jax.nn.sigmoid(z[..., COL_GATE:].reshape(B, T, N_BRANCH, D_MODEL))
    mixed = jnp.sum(gates * proj, axis=2) @ p['w_out']
    x = x + mod[:, 2] * mixed

    h2 = rmsnorm(x, p['norm2_g']) * (1 + mod[:, 4]) + mod[:, 3]
    up, n_ffn = causal_dwconv(h2 @ p['w_up'], st_ffn, p['ffn_w'])
    act = jax.nn.silu(up[..., :D_FF]) * up[..., D_FF:]
    x = x + mod[:, 5] * (act @ p['w_down'])
    new = tuple(t.astype(x.dtype) for t in (n_pool, n_ret, n_sc, n_shift, n_wkv, n_ffn))
    return x, new


def zero_states(b, dtype):
    return (jnp.zeros((b, POOL_PAST, MIX_W), dtype),
            jnp.zeros((b, RET_HEADS, RET_DK, RET_DV), dtype),
            jnp.zeros((b, SC_WIDTH - 1, MIX_W), dtype),
            jnp.zeros((b, RWKV_COLS), dtype),
            jnp.zeros((b, RWKV_HEADS, RWKV_HD, RWKV_HD), dtype),
            jnp.zeros((b, FFN_WIDTH - 1, 2 * D_FF), dtype))


def setup_inputs(seed: int = 0) -> dict:
    key = jax.random.key(seed)
    ks = iter(jax.random.split(key, 64))
    f32 = jnp.float32

    def nrm(shape, s):
        return jax.random.normal(next(ks), shape, f32) * s

    def uni(shape, lo, hi):
        return jax.random.uniform(next(ks), shape, f32, lo, hi)

    L = DEPTH
    D = D_MODEL
    return {
        'x_prompt': nrm((BATCH, SEQ, D), 1.0),
        'x_sample': nrm((DEC_BATCH, DEC_SEQ, D), 1.0),
        'c_prompt': nrm((BATCH, D), 1.0),
        'c_sample': nrm((DEC_BATCH, D), 1.0),
        'state_pool': nrm((L, DEC_BATCH, POOL_PAST, MIX_W), 1.0),
        'state_ret': nrm((L, DEC_BATCH, RET_HEADS, RET_DK, RET_DV), 0.5),
        'state_sconv': nrm((L, DEC_BATCH, SC_WIDTH - 1, MIX_W), 1.0),
        'state_shift': nrm((L, DEC_BATCH, RWKV_COLS), 1.0),
        'state_wkv': nrm((L, DEC_BATCH, RWKV_HEADS, RWKV_HD, RWKV_HD), 0.3),
        'state_ffn': nrm((L, DEC_BATCH, FFN_WIDTH - 1, 2 * D_FF), 1.0),
        'w_ada': nrm((L, D, 6 * D), 0.5 * D ** -0.5),
        'b_ada': nrm((L, 6 * D), 0.02),
        'norm1_g': 1.0 + nrm((L, D), 0.05),
        'norm2_g': 1.0 + nrm((L, D), 0.05),
        'w_in': nrm((L, D, IN_COLS), D ** -0.5),
        'pool_w': nrm((L, POOL_GROUPS, POOL_GW, POOL_GW), POOL_GW ** -0.5),
        'pool_scale': 1.0 + nrm((L, MIX_W), 0.1),
        'sc_w': nrm((L, SC_WIDTH, MIX_W), SC_WIDTH ** -0.5),
        'rw_mu': uni((L, RWKV_COLS), 0.0, 1.0),
        'rw_w0': uni((L, MIX_W), -6.0, -1.0),
        'rw_w_lora': nrm((L, LORA_W, MIX_W), 0.1 * LORA_W ** -0.5),
        'rw_a0': nrm((L, MIX_W), 0.1),
        'rw_a_lora': nrm((L, LORA_A, MIX_W), 0.1 * LORA_A ** -0.5),
        'rw_g_lora': nrm((L, LORA_G, MIX_W), LORA_G ** -0.5),
        'rw_k_k': 0.85 + nrm((L, MIX_W), 0.05),
        'rw_k_a': 1.0 + nrm((L, MIX_W), 0.05),
        'rw_r_k': nrm((L, RWKV_HEADS, RWKV_HD), 0.1),
        'rw_ln_g': 1.0 + nrm((L, MIX_W), 0.05),
        'rw_ln_b': nrm((L, MIX_W), 0.02),
        'w_br': nrm((L, N_BRANCH, MIX_W, D), MIX_W ** -0.5),
        'w_out': nrm((L, D, D), D ** -0.5),
        'w_up': nrm((L, D, 2 * D_FF), D ** -0.5),
        'ffn_w': nrm((L, FFN_WIDTH, 2 * D_FF), FFN_WIDTH ** -0.5),
        'w_down': nrm((L, D_FF, D), D_FF ** -0.5),
        'final_g': 1.0 + nrm((D,), 0.05),
    }


def reference(x_prompt, x_sample, c_prompt, c_sample, state_pool, state_ret, state_sconv, state_shift,
              state_wkv, state_ffn, w_ada, b_ada, norm1_g, norm2_g, w_in, pool_w, pool_scale, sc_w,
              rw_mu, rw_w0, rw_w_lora, rw_a0, rw_a_lora, rw_g_lora, rw_k_k, rw_k_a, rw_r_k, rw_ln_g,
              rw_ln_b, w_br, w_out, w_up, ffn_w, w_down, final_g):
    xp, xs = x_prompt, x_sample
    new_p, new_s = [], []
    for l in range(DEPTH):
        p = {'w_ada': w_ada[l], 'b_ada': b_ada[l], 'norm1_g': norm1_g[l], 'norm2_g': norm2_g[l],
             'w_in': w_in[l], 'pool_w': pool_w[l], 'pool_scale': pool_scale[l], 'sc_w': sc_w[l],
             'rw_mu': rw_mu[l], 'rw_w0': rw_w0[l], 'rw_w_lora': rw_w_lora[l], 'rw_a0': rw_a0[l],
             'rw_a_lora': rw_a_lora[l], 'rw_g_lora': rw_g_lora[l], 'rw_k_k': rw_k_k[l],
             'rw_k_a': rw_k_a[l], 'rw_r_k': rw_r_k[l], 'rw_ln_g': rw_ln_g[l], 'rw_ln_b': rw_ln_b[l],
             'w_br': w_br[l], 'w_out': w_out[l], 'w_up': w_up[l], 'ffn_w': ffn_w[l],
             'w_down': w_down[l]}
        xp, sp = block(xp, c_prompt, 0, zero_states(xp.shape[0], xp.dtype), p)
        xs, ss = block(xs, c_sample, PAST_LEN,
                       (state_pool[l], state_ret[l], state_sconv[l], state_shift[l], state_wkv[l], state_ffn[l]), p)
        new_p.append(sp)
        new_s.append(ss)
    y_prompt = rmsnorm(xp, final_g)
    y_sample = rmsnorm(xs, final_g)
    p_pool = jnp.stack([s[0] for s in new_p])
    p_ret = jnp.stack([s[1] for s in new_p])
    p_sconv = jnp.stack([s[2] for s in new_p])
    p_shift = jnp.stack([s[3] for s in new_p])
    p_wkv = jnp.stack([s[4] for s in new_p])
    p_ffn = jnp.stack([s[5] for s in new_p])
    s_pool = jnp.stack([s[0] for s in new_s])
    s_ret = jnp.stack([s[1] for s in new_s])
    s_sconv = jnp.stack([s[2] for s in new_s])
    s_shift = jnp.stack([s[3] for s in new_s])
    s_wkv = jnp.stack([s[4] for s in new_s])
    s_ffn = jnp.stack([s[5] for s in new_s])
    return (y_prompt, y_sample, p_pool, p_ret, p_sconv, p_shift, p_wkv, p_ffn,
            s_pool, s_ret, s_sconv, s_shift, s_wkv, s_ffn)
```

```python
import functools
import math

import jax
import jax.numpy as jnp
from jax import lax
from jax.experimental import pallas as pl
from jax.experimental.pallas import tpu as pltpu

F32 = jnp.float32
BF16 = jnp.bfloat16
HI = lax.Precision.HIGHEST

D_MODEL = 1024
MIX_W = 256
HEADS = 4
HEAD_D = 64
POOL_WINDOWS = (2, 4, 8, 16)
POOL_PAST = 15
ROPE_BASE = 10000.0
LORA_W = 64
LORA_A = 64
LORA_G = 128
D_FF = 2816
PAST_LEN = 16384
NORM_EPS = 1e-6
GN_EPS = 1e-6
RWKV_LN_EPS = 64e-5
L2_EPS = 1e-12

COL_RET = MIX_W
COL_SC = COL_RET + 4 * MIX_W
COL_RWKV = COL_SC + 3 * MIX_W
COL_GATE = COL_RWKV + 3 * MIX_W + LORA_W + LORA_A + LORA_G
IN_COLS = COL_GATE + 4 * D_MODEL

SUBLANES = 8
VMEM_LIMIT = 56 * 1024 * 1024
RWKV_CHUNK = 64
INV_BASE = 16

(V_POOL_SCALE, V_SC0, V_SC1, V_SC2, V_W0, V_A0, V_KK, V_KA, V_RK, V_LNG, V_LNB) = range(11)
N_VECS = 16


def _dot(a, b, precision=None):
    return jnp.dot(a, b, preferred_element_type=F32, precision=precision)


def _dot_nt(a, b, precision=None):
    return lax.dot_general(a, b, (((1,), (1,)), ((), ())), preferred_element_type=F32, precision=precision)


def _dot_tn(a, b, precision=None):
    return lax.dot_general(a, b, (((0,), (0,)), ((), ())), preferred_element_type=F32, precision=precision)


def _wdot(a, w_bf16):
    return jnp.dot(a.astype(BF16), w_bf16, preferred_element_type=F32)


def _sigmoid(x):
    return 1.0 / (1.0 + jnp.exp(-x))


def _silu(x):
    return x * _sigmoid(x)


def _softplus(x):
    return jnp.maximum(x, 0.0) + jnp.log(1.0 + jnp.exp(-jnp.abs(x)))


def _rmsnorm(x, g):
    return x * lax.rsqrt(jnp.mean(x * x, axis=-1, keepdims=True) + NORM_EPS) * g


def _round_up(n, m):
    return (n + m - 1) // m * m


def _mod_kernel(c_ref, w_ref, b_ref, o_ref):
    o_ref[0] = _wdot(_silu(c_ref[...]), w_ref[0].astype(BF16)) + b_ref[0]


def _mod_call(c_all, w_ada, b_ada):
    depth, d, cols = w_ada.shape
    rows = c_all.shape[0]
    cb = D_MODEL
    return pl.pallas_call(
        _mod_kernel,
        grid=(depth, cols // cb),
        in_specs=[pl.BlockSpec((rows, d), lambda l, c: (0, 0)),
                  pl.BlockSpec((1, d, cb), lambda l, c: (l, 0, c)),
                  pl.BlockSpec((1, 1, cb), lambda l, c: (l, 0, c))],
        out_specs=pl.BlockSpec((1, rows, cb), lambda l, c: (l, 0, c)),
        out_shape=jax.ShapeDtypeStruct((depth, rows, cols), F32),
        name="adaln_mod",
    )(c_all, w_ada, b_ada.reshape(depth, 1, cols))


def _pre_kernel(x_ref, mod_ref, g_ref, w_ref, z_ref):
    h = _rmsnorm(x_ref[0], g_ref[...]) * (1.0 + mod_ref[1]) + mod_ref[0]
    z_ref[0] = _wdot(h, w_ref[...])


def _post_kernel(x_ref, zg0_ref, zg1_ref, zg2_ref, zg3_ref, y_ref, mod_ref, g_ref, wbr_ref, wout_ref, wup_ref,
                 x1_ref, up_ref):
    mixed = None
    for n, zg_ref in enumerate((zg0_ref, zg1_ref, zg2_ref, zg3_ref)):
        proj = _wdot(y_ref[0, :, n * MIX_W:(n + 1) * MIX_W], wbr_ref[n])
        term = _sigmoid(zg_ref[0]) * proj
        mixed = term if mixed is None else mixed + term
    x1 = x_ref[0] + mod_ref[2] * _wdot(mixed, wout_ref[...])
    x1_ref[0] = x1
    h2 = _rmsnorm(x1, g_ref[...]) * (1.0 + mod_ref[4]) + mod_ref[3]
    up_ref[0] = _wdot(h2, wup_ref[...])


def _down_kernel(act_ref, x1_ref, mod_ref, w_ref, fg_ref, o_ref, *, final):
    x2 = x1_ref[0] + mod_ref[5] * _wdot(act_ref[0], w_ref[...])
    if final:
        x2 = _rmsnorm(x2, fg_ref[...])
    o_ref[0] = x2


def _row_block(rows):
    return min(rows, 256)


def _rows_spec(rb, cols, col_block=0):
    def index_map(g, r):
        return (g, r, col_block)

    return pl.BlockSpec((1, rb, cols), index_map)


def _mod_spec(mod):
    rm = mod.shape[2]
    if mod.shape[1] > 1:
        return pl.BlockSpec((6, None, rm, D_MODEL), lambda g, r: (0, g, 0, 0))
    return pl.BlockSpec((6, None, rm, D_MODEL), lambda g, r: (0, 0, 0, 0))


def _const_spec(shape):
    zeros = (0,) * len(shape)
    return pl.BlockSpec(shape, lambda g, r: zeros, pipeline_mode=pl.Buffered(1))


def _params(sem):
    return pltpu.CompilerParams(dimension_semantics=sem, vmem_limit_bytes=VMEM_LIMIT)


def _pre_call(x, mod, g, w_in):
    G, R, D = x.shape
    rb = _row_block(R)
    cols = w_in.shape[1]
    return pl.pallas_call(
        _pre_kernel,
        grid=(G, R // rb),
        in_specs=[_rows_spec(rb, D), _mod_spec(mod), _const_spec((1, D)), _const_spec(w_in.shape)],
        out_specs=_rows_spec(rb, cols),
        out_shape=jax.ShapeDtypeStruct((G, R, cols), F32),
        compiler_params=_params(("parallel", "parallel")),
        name="pre_in_proj",
    )(x, mod, g, w_in)


def _post_call(x, z, y, mod, g2, w_br, w_out, w_up):
    G, R, D = x.shape
    rb = _row_block(R)
    up_cols = w_up.shape[1]
    gate_block0 = COL_GATE // D_MODEL
    return pl.pallas_call(
        _post_kernel,
        grid=(G, R // rb),
        in_specs=[_rows_spec(rb, D)]
        + [_rows_spec(rb, D_MODEL, gate_block0 + n) for n in range(4)]
        + [_rows_spec(rb, D), _mod_spec(mod), _const_spec((1, D)), _const_spec(w_br.shape),
           _const_spec(w_out.shape), _const_spec(w_up.shape)],
        out_specs=[_rows_spec(rb, D), _rows_spec(rb, up_cols)],
        out_shape=[jax.ShapeDtypeStruct((G, R, D), F32), jax.ShapeDtypeStruct((G, R, up_cols), F32)],
        compiler_params=_params(("parallel", "parallel")),
        name="post_merge_up",
    )(x, z, z, z, z, y, mod, g2, w_br, w_out, w_up)


def _down_call(act, x1, mod, w_down, final_g, final):
    G, R, D = x1.shape
    rb = _row_block(R)
    return pl.pallas_call(
        functools.partial(_down_kernel, final=final),
        grid=(G, R // rb),
        in_specs=[_rows_spec(rb, act.shape[2]), _rows_spec(rb, D), _mod_spec(mod), _const_spec(w_down.shape),
                  _const_spec((1, D))],
        out_specs=_rows_spec(rb, D),
        out_shape=jax.ShapeDtypeStruct((G, R, D), F32),
        compiler_params=_params(("parallel", "parallel")),
        name="down_proj",
    )(act, x1, mod, w_down, final_g)


def _tri_inverse(lower, n):
    i = lax.broadcasted_iota(jnp.int32, (n, n), 0)
    j = lax.broadcasted_iota(jnp.int32, (n, n), 1)
    eye = (i == j).astype(F32)

    def same_block(size):
        sh = int(math.log2(size))
        return jnp.right_shift(i, sh) == jnp.right_shift(j, sh)

    base = min(n, INV_BASE)
    d = jnp.where(same_block(base), lower, 0.0) if base < n else lower
    x = eye + d
    power, p = d, 1
    while 2 * p < base:
        power = _dot(power, power, HI)
        x = x + _dot(x, power, HI)
        p *= 2
    size = base
    while size < n:
        off = jnp.where(same_block(2 * size) & jnp.logical_not(same_block(size)), lower, 0.0)
        x = x + _dot(_dot(x, off, HI), x, HI)
        size *= 2
    return x


def _seq_kernel(z_ref, cos_ref, sin_ref, pool0_ref, ret0_ref, sc0_ref, sh0_ref, wkv0_ref,
                poolw_ref, vecs_ref, mu_ref, wlora_ref, alora_ref, glora_ref,
                y_ref, pool_o_ref, ret_o_ref, sc_o_ref, sh_o_ref, wkv_o_ref,
                pool_ext, sc_ext, sh_ext, sret, swkv, ywkv, *, tc, pos0, nt):
    j = pl.program_id(1)
    last = j == nt - 1
    head_rows = 16
    hist = SUBLANES

    @pl.when(j == 0)
    def _init():
        pool_ext[...] = jnp.zeros(pool_ext.shape, F32)
        pool_ext[head_rows - POOL_PAST:head_rows] = pool0_ref[0]
        sc_ext[...] = jnp.zeros(sc_ext.shape, F32)
        sc_ext[hist - 2:hist] = sc0_ref[0]
        sh_ext[...] = jnp.zeros(sh_ext.shape, F32)
        sh_ext[hist - 1:hist] = sh0_ref[0]
        sret[...] = ret0_ref[0]
        swkv[...] = wkv0_ref[0]

    def vec(row):
        return vecs_ref[row:row + 1, :]

    lane = lax.broadcasted_iota(jnp.int32, (1, MIX_W), 1)
    row_i = lax.broadcasted_iota(jnp.int32, (tc, 1), 0)

    u = z_ref[0, :, 0:MIX_W]
    pool_ext[head_rows:head_rows + tc] = u
    e = pool_ext[...]
    s2 = e + pltpu.roll(e, 1, 0)
    s4 = s2 + pltpu.roll(s2, 2, 0)
    s8 = s4 + pltpu.roll(s4, 4, 0)
    s16 = s8 + pltpu.roll(s8, 8, 0)
    grp = jnp.right_shift(lane, 6)
    wsum = jnp.where(grp == 0, s2, jnp.where(grp == 1, s4, jnp.where(grp == 2, s8, s16)))[head_rows:head_rows + tc]
    win = jnp.where(grp == 0, 2, jnp.where(grp == 1, 4, jnp.where(grp == 2, 8, 16)))
    pos = pos0 + j * tc + row_i
    cnt = jnp.minimum(win, pos + 1).astype(F32)
    m = wsum / cnt - u
    y_ref[0, :, 0:MIX_W] = _wdot(m, poolw_ref[...]) * vec(V_POOL_SCALE)
    new_pool = pool_ext[tc + head_rows - POOL_PAST:tc + head_rows]
    pool_ext[head_rows - POOL_PAST:head_rows] = new_pool

    @pl.when(last)
    def _():
        pool_o_ref[0] = new_pool

    first_half = jnp.bitwise_and(lane, HEAD_D - 1) < HEAD_D // 2
    cos = cos_ref[...]
    sin = sin_ref[...]

    def rope(x):
        swapped = jnp.where(first_half, pltpu.roll(x, MIX_W - HEAD_D // 2, 1), pltpu.roll(x, HEAD_D // 2, 1))
        return x * cos + swapped * sin

    q = rope(z_ref[0, :, COL_RET:COL_RET + MIX_W])
    k = rope(z_ref[0, :, COL_RET + MIX_W:COL_RET + 2 * MIX_W]) * (HEAD_D ** -0.5)
    v = z_ref[0, :, COL_RET + 2 * MIX_W:COL_RET + 3 * MIX_W]
    gate = z_ref[0, :, COL_RET + 3 * MIX_W:COL_RET + 4 * MIX_W]
    col_i = lax.broadcasted_iota(jnp.int32, (1, tc), 1)
    diff = (row_i - col_i).astype(F32)
    row_f = row_i.astype(F32)
    for h in range(HEADS):
        sl = slice(h * HEAD_D, (h + 1) * HEAD_D)
        lg = math.log1p(-(2.0 ** (-5.0 - h)))
        dmask = jnp.where(diff >= 0, jnp.exp(lg * jnp.maximum(diff, 0.0)), 0.0)
        qh, kh, vh = q[:, sl], k[:, sl], v[:, sl]
        s = _dot_nt(qh, kh, HI) * dmask
        state = sret[h]
        o = _dot(s, vh, HI) + _dot(qh * jnp.exp(lg * (row_f + 1.0)), state, HI)
        sret[h] = state * math.exp(lg * tc) + _dot_tn(kh * jnp.exp(lg * (tc - 1.0 - row_f)), vh, HI)
        o = o * lax.rsqrt(jnp.mean(o * o, axis=-1, keepdims=True) + GN_EPS)
        y_ref[0, :, MIX_W + h * HEAD_D:MIX_W + (h + 1) * HEAD_D] = o * _silu(gate[:, sl])

    @pl.when(last)
    def _():
        ret_o_ref[0] = sret[...]

    hc = z_ref[0, :, COL_SC:COL_SC + MIX_W]
    bg = z_ref[0, :, COL_SC + MIX_W:COL_SC + 2 * MIX_W]
    cg = z_ref[0, :, COL_SC + 2 * MIX_W:COL_SC + 3 * MIX_W]
    sc_ext[hist:hist + tc] = cg * hc
    e = sc_ext[...]
    conv = vec(V_SC2) * e + vec(V_SC1) * pltpu.roll(e, 1, 0) + vec(V_SC0) * pltpu.roll(e, 2, 0)
    y_ref[0, :, 2 * MIX_W:3 * MIX_W] = bg * conv[hist:hist + tc]
    new_sc = sc_ext[hist + tc - 2:hist + tc]
    sc_ext[hist - 2:hist] = new_sc

    @pl.when(last)
    def _():
        sc_o_ref[0] = new_sc

    zz = z_ref[0, :, COL_RWKV:COL_GATE]
    sh_ext[hist:hist + tc] = zz
    prev = pltpu.roll(sh_ext[...], 1, 0)[hist:hist + tc]
    zs = zz + (prev - zz) * mu_ref[...]
    new_sh = sh_ext[hist + tc - 1:hist + tc]
    sh_ext[hist - 1:hist] = new_sh

    @pl.when(last)
    def _():
        sh_o_ref[0] = new_sh

    r = zs[:, 0:MIX_W]
    kx = zs[:, MIX_W:2 * MIX_W]
    vx = zs[:, 2 * MIX_W:3 * MIX_W]
    o3 = 3 * MIX_W
    wl = zs[:, o3:o3 + LORA_W]
    al = zs[:, o3 + LORA_W:o3 + LORA_W + LORA_A]
    gl = zs[:, o3 + LORA_W + LORA_A:]
    wlog = -_softplus(-(vec(V_W0) + _dot(jnp.tanh(wl), wlora_ref[...], HI))) - 0.5
    logw = -jnp.exp(wlog)
    asig = _sigmoid(vec(V_A0) + _dot(al, alora_ref[...], HI))
    g_out = _dot(_sigmoid(gl), glora_ref[...], HI)
    kk_raw = kx * vec(V_KK)
    k2 = kx * (1.0 + (asig - 1.0) * vec(V_KA))

    c = min(RWKV_CHUNK, tc)
    ci = lax.broadcasted_iota(jnp.int32, (c, c), 0)
    cj = lax.broadcasted_iota(jnp.int32, (c, c), 1)
    strict = ci > cj
    incl = ci >= cj
    tri = incl.astype(F32)
    for sub in range(tc // c):
        rows = slice(sub * c, (sub + 1) * c)
        lw = logw[rows]
        cum = _dot(tri, lw, HI)
        p_in = jnp.exp(cum)
        p_ex = jnp.exp(cum - lw)
        p_inv = jnp.exp(-cum)
        p_end = p_in[c - 1:c, :]
        for h in range(HEADS):
            sl = slice(h * HEAD_D, (h + 1) * HEAD_D)
            kkh = kk_raw[rows, sl]
            kkh = kkh / jnp.maximum(jnp.sqrt(jnp.sum(kkh * kkh, axis=-1, keepdims=True)), L2_EPS)
            a_t = -kkh * p_ex[:, sl]
            b_t = kkh * asig[rows, sl] * p_inv[:, sl]
            k_t = k2[rows, sl] * p_inv[:, sl]
            r_t = r[rows, sl] * p_in[:, sl]
            vh = vx[rows, sl]
            l_ab = jnp.where(strict, _dot_nt(a_t, b_t, HI), 0.0)
            l_ak = jnp.where(strict, _dot_nt(a_t, k_t, HI), 0.0)
            m_rb = jnp.where(incl, _dot_nt(r_t, b_t, HI), 0.0)
            m_rk = jnp.where(incl, _dot_nt(r_t, k_t, HI), 0.0)
            t_inv = _tri_inverse(l_ab, c)
            state = swkv[h]
            u_mat = _dot(t_inv, _dot_nt(a_t, state, HI) + _dot(l_ak, vh, HI), HI)
            ywkv[rows, sl] = _dot_nt(r_t, state, HI) + _dot(m_rb, u_mat, HI) + _dot(m_rk, vh, HI)
            swkv[h] = (state + _dot_tn(u_mat, b_t, HI) + _dot_tn(vh, k_t, HI)) * p_end[:, sl]

    @pl.when(last)
    def _():
        wkv_o_ref[0] = swkv[...]

    for h in range(HEADS):
        sl = slice(h * HEAD_D, (h + 1) * HEAD_D)
        yh = ywkv[:, sl]
        mean = jnp.mean(yh, axis=-1, keepdims=True)
        cen = yh - mean
        var = jnp.mean(cen * cen, axis=-1, keepdims=True)
        yn = cen * lax.rsqrt(var + RWKV_LN_EPS) * vec(V_LNG)[:, sl] + vec(V_LNB)[:, sl]
        bonus = jnp.sum(r[:, sl] * k2[:, sl] * vec(V_RK)[:, sl], axis=-1, keepdims=True) * vx[:, sl]
        y_ref[0, :, 3 * MIX_W + h * HEAD_D:3 * MIX_W + (h + 1) * HEAD_D] = (yn + bonus) * g_out[:, sl]


def _seq_chunk(t):
    return min(t, 256)


def _seq_call(z, cos_t, sin_t, states, poolw, vecs, mu, wlora, alora, glora, pos0):
    B, T, _ = z.shape
    tc = _seq_chunk(T)
    nt = T // tc
    tp = _round_up(tc, SUBLANES)
    pool0, ret0, sc0, sh0, wkv0 = states
    sh0 = sh0.reshape(B, 1, D_MODEL)

    def batch_spec(arr):
        shape = arr.shape[1:]
        zeros = (0,) * len(shape)
        return pl.BlockSpec((1,) + shape, lambda b, j: (b,) + zeros)

    def const_spec(arr):
        zeros = (0,) * arr.ndim
        return pl.BlockSpec(arr.shape, lambda b, j: zeros)

    state_specs = [batch_spec(a) for a in (pool0, ret0, sc0, sh0, wkv0)]
    outs = pl.pallas_call(
        functools.partial(_seq_kernel, tc=tc, pos0=pos0, nt=nt),
        grid=(B, nt),
        in_specs=[pl.BlockSpec((1, tc, IN_COLS), lambda b, j: (b, j, 0)),
                  pl.BlockSpec((tc, MIX_W), lambda b, j: (j, 0)),
                  pl.BlockSpec((tc, MIX_W), lambda b, j: (j, 0))]
        + state_specs + [const_spec(a) for a in (poolw, vecs, mu, wlora, alora, glora)],
        out_specs=[pl.BlockSpec((1, tc, D_MODEL), lambda b, j: (b, j, 0))] + state_specs,
        out_shape=[jax.ShapeDtypeStruct((B, T, D_MODEL), F32)]
        + [jax.ShapeDtypeStruct(a.shape, F32) for a in (pool0, ret0, sc0, sh0, wkv0)],
        scratch_shapes=[pltpu.VMEM((16 + tp, MIX_W), F32),
                        pltpu.VMEM((SUBLANES + tp, MIX_W), F32),
                        pltpu.VMEM((SUBLANES + tp, D_MODEL), F32),
                        pltpu.VMEM((HEADS, HEAD_D, HEAD_D), F32),
                        pltpu.VMEM((HEADS, HEAD_D, HEAD_D), F32),
                        pltpu.VMEM((tc, MIX_W), F32)],
        compiler_params=_params(("parallel", "arbitrary")),
        name="seq_mixers",
    )(z, cos_t, sin_t, pool0, ret0, sc0, sh0, wkv0, poolw, vecs, mu, wlora, alora, glora)
    y, n_pool, n_ret, n_sc, n_sh, n_wkv = outs
    return y, (n_pool, n_ret, n_sc, n_sh.reshape(B, D_MODEL), n_wkv)


CONV_COLS = 512


def _conv_kernel(up_ref, st0_ref, w_ref, act_ref, st_o_ref, ext, *, tc, nt):
    j = pl.program_id(1)
    hist = SUBLANES

    @pl.when(j == 0)
    def _init():
        ext[...] = jnp.zeros(ext.shape, F32)
        ext[hist - 2:hist] = st0_ref[0]

    ext[hist:hist + tc] = up_ref[0]

    def conv_cols(lo, width):
        e = ext[:, lo:lo + width]
        w = w_ref[:, lo:lo + width]
        y = w[2:3] * e + w[1:2] * pltpu.roll(e, 1, 0) + w[0:1] * pltpu.roll(e, 2, 0)
        return y[hist:hist + tc]

    for lo in range(0, D_FF, CONV_COLS):
        width = min(CONV_COLS, D_FF - lo)
        act_ref[0, :, lo:lo + width] = _silu(conv_cols(lo, width)) * conv_cols(D_FF + lo, width)
    new_st = ext[hist + tc - 2:hist + tc]
    ext[hist - 2:hist] = new_st

    @pl.when(j == nt - 1)
    def _():
        st_o_ref[0] = new_st


def _conv_call(up, st0, ffn_w):
    B, T, cols = up.shape
    tc = _seq_chunk(T)
    nt = T // tc
    tp = _round_up(tc, SUBLANES)
    st_spec = pl.BlockSpec((1, 2, cols), lambda b, j: (b, 0, 0))
    return pl.pallas_call(
        functools.partial(_conv_kernel, tc=tc, nt=nt),
        grid=(B, nt),
        in_specs=[pl.BlockSpec((1, tc, cols), lambda b, j: (b, j, 0)), st_spec,
                  pl.BlockSpec(ffn_w.shape, lambda b, j: (0, 0))],
        out_specs=[pl.BlockSpec((1, tc, D_FF), lambda b, j: (b, j, 0)), st_spec],
        out_shape=[jax.ShapeDtypeStruct((B, T, D_FF), F32), jax.ShapeDtypeStruct(st0.shape, F32)],
        scratch_shapes=[pltpu.VMEM((SUBLANES + tp, cols), F32)],
        compiler_params=_params(("parallel", "arbitrary")),
        name="ffn_conv_gate",
    )(up, st0, ffn_w)


def _rope_tables(t, pos0):
    half = HEAD_D // 2
    inv = ROPE_BASE ** (-jnp.arange(half, dtype=F32) / half)
    pos = pos0 + jnp.arange(t, dtype=jnp.int32)
    ang = pos.astype(F32)[:, None] * inv[None, :]
    cos, sin = jnp.cos(ang), jnp.sin(ang)
    cos_t = jnp.tile(jnp.concatenate([cos, cos], axis=-1), (1, HEADS))
    sin_t = jnp.tile(jnp.concatenate([-sin, sin], axis=-1), (1, HEADS))
    return cos_t, sin_t


def _block_diag(pool_w):
    groups, gw, _ = pool_w.shape
    out = jnp.zeros((groups * gw, groups * gw), pool_w.dtype)
    for g in range(groups):
        out = out.at[g * gw:(g + 1) * gw, g * gw:(g + 1) * gw].set(pool_w[g])
    return out


def kernel(x_prompt, x_sample, c_prompt, c_sample, state_pool, state_ret, state_sconv, state_shift, state_wkv, state_ffn, w_ada, b_ada, norm1_g, norm2_g, w_in, pool_w, pool_scale, sc_w, rw_mu, rw_w0, rw_w_lora, rw_a0, rw_a_lora, rw_g_lora, rw_k_k, rw_k_a, rw_r_k, rw_ln_g, rw_ln_b, w_br, w_out, w_up, ffn_w, w_down, final_g):
    depth = w_in.shape[0]
    bp, tp_, d = x_prompt.shape
    bs, ts, _ = x_sample.shape

    mods = _mod_call(jnp.concatenate([c_prompt, c_sample], axis=0), w_ada, b_ada)
    mods = mods.reshape(depth, bp + bs, 6, d)
    cos_p, sin_p = _rope_tables(tp_, 0)
    cos_s, sin_s = _rope_tables(ts, PAST_LEN)
    final_g2 = final_g.reshape(1, d)

    xp = x_prompt
    xs = jnp.transpose(x_sample, (1, 0, 2))
    new_p, new_s = [], []
    for l in range(depth):
        mod_p = jnp.transpose(mods[l, :bp], (1, 0, 2))[:, :, None, :]
        mod_s = jnp.transpose(mods[l, bp:], (1, 0, 2))[:, None, :, :]
        w_in_l = w_in[l].astype(BF16)
        w_br_l = w_br[l].astype(BF16)
        w_out_l = w_out[l].astype(BF16)
        w_up_l = w_up[l].astype(BF16)
        w_down_l = w_down[l].astype(BF16)
        poolw = _block_diag(pool_w[l]).astype(BF16)
        rows = [pool_scale[l], sc_w[l, 0], sc_w[l, 1], sc_w[l, 2], rw_w0[l], rw_a0[l], rw_k_k[l], rw_k_a[l],
                rw_r_k[l].reshape(MIX_W), rw_ln_g[l], rw_ln_b[l]]
        vecs = jnp.concatenate([jnp.stack(rows), jnp.zeros((N_VECS - len(rows), MIX_W), F32)], axis=0)
        mu = rw_mu[l].reshape(1, d)
        g1 = norm1_g[l].reshape(1, d)
        g2 = norm2_g[l].reshape(1, d)
        seq_w = (poolw, vecs, mu, rw_w_lora[l], rw_a_lora[l], rw_g_lora[l])
        final = l == depth - 1

        z = _pre_call(xp, mod_p, g1, w_in_l)
        zero = (jnp.zeros((bp, POOL_PAST, MIX_W), F32), jnp.zeros((bp, HEADS, HEAD_D, HEAD_D), F32),
                jnp.zeros((bp, 2, MIX_W), F32), jnp.zeros((bp, d), F32),
                jnp.zeros((bp, HEADS, HEAD_D, HEAD_D), F32))
        y, st = _seq_call(z, cos_p, sin_p, zero, *seq_w, pos0=0)
        x1, up = _post_call(xp, z, y, mod_p, g2, w_br_l, w_out_l, w_up_l)
        act, n_ffn = _conv_call(up, jnp.zeros((bp, 2, 2 * D_FF), F32), ffn_w[l])
        xp = _down_call(act, x1, mod_p, w_down_l, final_g2, final)
        new_p.append(st + (n_ffn,))

        z = _pre_call(xs, mod_s, g1, w_in_l)
        st_in = (state_pool[l], state_ret[l], state_sconv[l], state_shift[l], state_wkv[l])
        y, st = _seq_call(jnp.transpose(z, (1, 0, 2)), cos_s, sin_s, st_in, *seq_w, pos0=PAST_LEN)
        x1, up = _post_call(xs, z, jnp.transpose(y, (1, 0, 2)), mod_s, g2, w_br_l, w_out_l, w_up_l)
        act, n_ffn = _conv_call(jnp.transpose(up, (1, 0, 2)), state_ffn[l], ffn_w[l])
        xs = _down_call(jnp.transpose(act, (1, 0, 2)), x1, mod_s, w_down_l, final_g2, final)
        new_s.append(st + (n_ffn,))

    y_prompt = xp
    y_sample = jnp.transpose(xs, (1, 0, 2))
    outs_p = [jnp.stack([s[i] for s in new_p]) for i in range(6)]
    outs_s = [jnp.stack([s[i] for s in new_s]) for i in range(6)]
    return (y_prompt, y_sample, *outs_p, *outs_s)
```

```python
import functools
import math

import jax
import jax.numpy as jnp
from jax import lax
from jax.experimental import pallas as pl
from jax.experimental.pallas import tpu as pltpu

F32 = jnp.float32
BF16 = jnp.bfloat16
HI = lax.Precision.HIGHEST

D_MODEL = 1024
MIX_W = 256
HEADS = 4
HEAD_D = 64
POOL_WINDOWS = (2, 4, 8, 16)
POOL_PAST = 15
ROPE_BASE = 10000.0
LORA_W = 64
LORA_A = 64
LORA_G = 128
D_FF = 2816
PAST_LEN = 16384
NORM_EPS = 1e-6
GN_EPS = 1e-6
RWKV_LN_EPS = 64e-5
L2_EPS = 1e-12

COL_RET = MIX_W
COL_SC = COL_RET + 4 * MIX_W
COL_RWKV = COL_SC + 3 * MIX_W
COL_GATE = COL_RWKV + 3 * MIX_W + LORA_W + LORA_A + LORA_G
IN_COLS = COL_GATE + 4 * D_MODEL

SUBLANES = 8
VMEM_LIMIT = 56 * 1024 * 1024
RWKV_CHUNK = 64
INV_BASE = 16

(V_POOL_SCALE, V_SC0, V_SC1, V_SC2, V_W0, V_A0, V_KK, V_KA, V_RK, V_LNG, V_LNB) = range(11)
N_VECS = 16


def _bdot(a, b):
    return jnp.dot(a.astype(BF16), b.astype(BF16), preferred_element_type=F32)


def _bdot_nt(a, b):
    return lax.dot_general(a.astype(BF16), b.astype(BF16), (((1,), (1,)), ((), ())), preferred_element_type=F32)


def _bdot_tn(a, b):
    return lax.dot_general(a.astype(BF16), b.astype(BF16), (((0,), (0,)), ((), ())), preferred_element_type=F32)


def _wdot(a, w_bf16):
    return jnp.dot(a.astype(BF16), w_bf16, preferred_element_type=F32)


def _sigmoid(x):
    return 1.0 / (1.0 + jnp.exp(-x))


def _silu(x):
    return x * _sigmoid(x)


def _softplus(x):
    return jnp.maximum(x, 0.0) + jnp.log(1.0 + jnp.exp(-jnp.abs(x)))


def _rmsnorm(x, g):
    return x * lax.rsqrt(jnp.mean(x * x, axis=-1, keepdims=True) + NORM_EPS) * g


def _round_up(n, m):
    return (n + m - 1) // m * m


def _mod_kernel(c_ref, w_ref, b_ref, o_ref):
    o_ref[0] = _wdot(_silu(c_ref[...]), w_ref[0].astype(BF16)) + b_ref[0]


def _mod_call(c_all, w_ada, b_ada):
    depth, d, cols = w_ada.shape
    rows = c_all.shape[0]
    cb = D_MODEL
    return pl.pallas_call(
        _mod_kernel,
        grid=(depth, cols // cb),
        in_specs=[pl.BlockSpec((rows, d), lambda l, c: (0, 0)),
                  pl.BlockSpec((1, d, cb), lambda l, c: (l, 0, c)),
                  pl.BlockSpec((1, 1, cb), lambda l, c: (l, 0, c))],
        out_specs=pl.BlockSpec((1, rows, cb), lambda l, c: (l, 0, c)),
        out_shape=jax.ShapeDtypeStruct((depth, rows, cols), F32),
        name="adaln_mod",
    )(c_all, w_ada, b_ada.reshape(depth, 1, cols))


def _pre_kernel(x_ref, mod_ref, g_ref, w_ref, z_ref):
    h = _rmsnorm(x_ref[0], g_ref[...]) * (1.0 + mod_ref[1]) + mod_ref[0]
    z_ref[0] = _wdot(h, w_ref[...])


def _post_kernel(x_ref, zg0_ref, zg1_ref, zg2_ref, zg3_ref, y_ref, mod_ref, g_ref, wbr_ref, wout_ref, wup_ref,
                 x1_ref, up_ref):
    mixed = None
    for n, zg_ref in enumerate((zg0_ref, zg1_ref, zg2_ref, zg3_ref)):
        proj = _wdot(y_ref[0, :, n * MIX_W:(n + 1) * MIX_W], wbr_ref[n])
        term = _sigmoid(zg_ref[0]) * proj
        mixed = term if mixed is None else mixed + term
    x1 = x_ref[0] + mod_ref[2] * _wdot(mixed, wout_ref[...])
    x1_ref[0] = x1
    h2 = _rmsnorm(x1, g_ref[...]) * (1.0 + mod_ref[4]) + mod_ref[3]
    up_ref[0] = _wdot(h2, wup_ref[...])


def _down_kernel(act_ref, x1_ref, mod_ref, w_ref, fg_ref, o_ref, *, final):
    x2 = x1_ref[0] + mod_ref[5] * _wdot(act_ref[0], w_ref[...])
    if final:
        x2 = _rmsnorm(x2, fg_ref[...])
    o_ref[0] = x2


def _row_block(rows):
    return min(rows, 256)


def _rows_spec(rb, cols, col_block=0):
    def index_map(g, r):
        return (g, r, col_block)

    return pl.BlockSpec((1, rb, cols), index_map)


def _mod_spec(mod):
    rm = mod.shape[2]
    if mod.shape[1] > 1:
        return pl.BlockSpec((6, None, rm, D_MODEL), lambda g, r: (0, g, 0, 0))
    return pl.BlockSpec((6, None, rm, D_MODEL), lambda g, r: (0, 0, 0, 0))


def _const_spec(shape):
    zeros = (0,) * len(shape)
    return pl.BlockSpec(shape, lambda g, r: zeros, pipeline_mode=pl.Buffered(1))


def _params(sem):
    return pltpu.CompilerParams(dimension_semantics=sem, vmem_limit_bytes=VMEM_LIMIT)


def _pre_call(x, mod, g, w_in):
    G, R, D = x.shape
    rb = _row_block(R)
    cols = w_in.shape[1]
    return pl.pallas_call(
        _pre_kernel,
        grid=(G, R // rb),
        in_specs=[_rows_spec(rb, D), _mod_spec(mod), _const_spec((1, D)), _const_spec(w_in.shape)],
        out_specs=_rows_spec(rb, cols),
        out_shape=jax.ShapeDtypeStruct((G, R, cols), F32),
        compiler_params=_params(("parallel", "parallel")),
        name="pre_in_proj",
    )(x, mod, g, w_in)


def _post_call(x, z, y, mod, g2, w_br, w_out, w_up):
    G, R, D = x.shape
    rb = _row_block(R)
    up_cols = w_up.shape[1]
    gate_block0 = COL_GATE // D_MODEL
    return pl.pallas_call(
        _post_kernel,
        grid=(G, R // rb),
        in_specs=[_rows_spec(rb, D)]
        + [_rows_spec(rb, D_MODEL, gate_block0 + n) for n in range(4)]
        + [_rows_spec(rb, D), _mod_spec(mod), _const_spec((1, D)), _const_spec(w_br.shape),
           _const_spec(w_out.shape), _const_spec(w_up.shape)],
        out_specs=[_rows_spec(rb, D), _rows_spec(rb, up_cols)],
        out_shape=[jax.ShapeDtypeStruct((G, R, D), F32), jax.ShapeDtypeStruct((G, R, up_cols), F32)],
        compiler_params=_params(("parallel", "parallel")),
        name="post_merge_up",
    )(x, z, z, z, z, y, mod, g2, w_br, w_out, w_up)


def _down_call(act, x1, mod, w_down, final_g, final):
    G, R, D = x1.shape
    rb = _row_block(R)
    return pl.pallas_call(
        functools.partial(_down_kernel, final=final),
        grid=(G, R // rb),
        in_specs=[_rows_spec(rb, act.shape[2]), _rows_spec(rb, D), _mod_spec(mod), _const_spec(w_down.shape),
                  _const_spec((1, D))],
        out_specs=_rows_spec(rb, D),
        out_shape=jax.ShapeDtypeStruct((G, R, D), F32),
        compiler_params=_params(("parallel", "parallel")),
        name="down_proj",
    )(act, x1, mod, w_down, final_g)


def _tri_inverse_minus_eye(lowers, n):
    i = lax.broadcasted_iota(jnp.int32, (n, n), 0)
    j = lax.broadcasted_iota(jnp.int32, (n, n), 1)

    def same_block(size):
        sh = int(math.log2(size))
        return jnp.right_shift(i, sh) == jnp.right_shift(j, sh)

    base = min(n, INV_BASE)
    if base < n:
        diag_mask = same_block(base)
        xs = [jnp.where(diag_mask, low, 0.0) for low in lowers]
    else:
        xs = list(lowers)
    powers, p = xs, 1
    while 2 * p < base:
        powers = [_bdot(pw, pw) for pw in powers]
        prods = [_bdot(x, pw) for x, pw in zip(xs, powers)]
        xs = [x + pw + pr for x, pw, pr in zip(xs, powers, prods)]
        p *= 2
    size = base
    while size < n:
        off_mask = same_block(2 * size) & jnp.logical_not(same_block(size))
        offs = [jnp.where(off_mask, low, 0.0) for low in lowers]
        lefts = [off + _bdot(x, off) for x, off in zip(xs, offs)]
        xs = [x + left + _bdot(left, x) for x, left in zip(xs, lefts)]
        size *= 2
    return xs


def _seq_kernel(z_ref, cos_ref, sin_ref, pool0_ref, ret0_ref, sc0_ref, sh0_ref, wkv0_ref,
                poolw_ref, vecs_ref, mu_ref, wlora_ref, alora_ref, glora_ref,
                y_ref, pool_o_ref, ret_o_ref, sc_o_ref, sh_o_ref, wkv_o_ref,
                pool_ext, sc_ext, sh_ext, sret, swkv, ywkv, *, tc, pos0, nt):
    j = pl.program_id(1)
    last = j == nt - 1
    head_rows = 16
    hist = SUBLANES

    @pl.when(j == 0)
    def _init():
        pool_ext[...] = jnp.zeros(pool_ext.shape, F32)
        pool_ext[head_rows - POOL_PAST:head_rows] = pool0_ref[0]
        sc_ext[...] = jnp.zeros(sc_ext.shape, F32)
        sc_ext[hist - 2:hist] = sc0_ref[0]
        sh_ext[...] = jnp.zeros(sh_ext.shape, F32)
        sh_ext[hist - 1:hist] = sh0_ref[0]
        sret[...] = ret0_ref[0]
        swkv[...] = wkv0_ref[0]

    def vec(row):
        return vecs_ref[row:row + 1, :]

    lane = lax.broadcasted_iota(jnp.int32, (1, MIX_W), 1)
    row_i = lax.broadcasted_iota(jnp.int32, (tc, 1), 0)

    u = z_ref[0, :, 0:MIX_W]
    pool_ext[head_rows:head_rows + tc] = u
    e = pool_ext[...]
    s2 = e + pltpu.roll(e, 1, 0)
    s4 = s2 + pltpu.roll(s2, 2, 0)
    s8 = s4 + pltpu.roll(s4, 4, 0)
    s16 = s8 + pltpu.roll(s8, 8, 0)
    grp = jnp.right_shift(lane, 6)
    wsum = jnp.where(grp == 0, s2, jnp.where(grp == 1, s4, jnp.where(grp == 2, s8, s16)))[head_rows:head_rows + tc]
    win = jnp.where(grp == 0, 2, jnp.where(grp == 1, 4, jnp.where(grp == 2, 8, 16)))
    pos = pos0 + j * tc + row_i
    cnt = jnp.minimum(win, pos + 1).astype(F32)
    m = wsum / cnt - u
    y_ref[0, :, 0:MIX_W] = _wdot(m, poolw_ref[...]) * vec(V_POOL_SCALE)
    new_pool = pool_ext[tc + head_rows - POOL_PAST:tc + head_rows]
    pool_ext[head_rows - POOL_PAST:head_rows] = new_pool

    @pl.when(last)
    def _():
        pool_o_ref[0] = new_pool

    first_half = jnp.bitwise_and(lane, HEAD_D - 1) < HEAD_D // 2
    cos = cos_ref[...]
    sin = sin_ref[...]

    def rope(x):
        swapped = jnp.where(first_half, pltpu.roll(x, MIX_W - HEAD_D // 2, 1), pltpu.roll(x, HEAD_D // 2, 1))
        return x * cos + swapped * sin

    q = rope(z_ref[0, :, COL_RET:COL_RET + MIX_W])
    k = rope(z_ref[0, :, COL_RET + MIX_W:COL_RET + 2 * MIX_W]) * (HEAD_D ** -0.5)
    v = z_ref[0, :, COL_RET + 2 * MIX_W:COL_RET + 3 * MIX_W]
    gate = z_ref[0, :, COL_RET + 3 * MIX_W:COL_RET + 4 * MIX_W]
    col_i = lax.broadcasted_iota(jnp.int32, (1, tc), 1)
    diff = (row_i - col_i).astype(F32)
    row_f = row_i.astype(F32)
    heads = range(HEADS)
    hsl = [slice(h * HEAD_D, (h + 1) * HEAD_D) for h in heads]
    lgs = [math.log1p(-(2.0 ** (-5.0 - h))) for h in heads]
    qs, ks, vs = [q[:, sl] for sl in hsl], [k[:, sl] for sl in hsl], [v[:, sl] for sl in hsl]
    ret_states = [sret[h] for h in heads]
    scores = [_bdot_nt(qs[h], ks[h]) for h in heads]
    cross = [_bdot(qs[h] * jnp.exp(lgs[h] * (row_f + 1.0)), ret_states[h]) for h in heads]
    kv = [_bdot_tn(ks[h] * jnp.exp(lgs[h] * (tc - 1.0 - row_f)), vs[h]) for h in heads]
    scores = [scores[h] * jnp.where(diff >= 0, jnp.exp(lgs[h] * jnp.maximum(diff, 0.0)), 0.0) for h in heads]
    outs = [_bdot(scores[h], vs[h]) + cross[h] for h in heads]
    for h in heads:
        sret[h] = ret_states[h] * math.exp(lgs[h] * tc) + kv[h]
        o = outs[h]
        o = o * lax.rsqrt(jnp.mean(o * o, axis=-1, keepdims=True) + GN_EPS)
        y_ref[0, :, MIX_W + h * HEAD_D:MIX_W + (h + 1) * HEAD_D] = o * _silu(gate[:, hsl[h]])

    @pl.when(last)
    def _():
        ret_o_ref[0] = sret[...]

    hc = z_ref[0, :, COL_SC:COL_SC + MIX_W]
    bg = z_ref[0, :, COL_SC + MIX_W:COL_SC + 2 * MIX_W]
    cg = z_ref[0, :, COL_SC + 2 * MIX_W:COL_SC + 3 * MIX_W]
    sc_ext[hist:hist + tc] = cg * hc
    e = sc_ext[...]
    conv = vec(V_SC2) * e + vec(V_SC1) * pltpu.roll(e, 1, 0) + vec(V_SC0) * pltpu.roll(e, 2, 0)
    y_ref[0, :, 2 * MIX_W:3 * MIX_W] = bg * conv[hist:hist + tc]
    new_sc = sc_ext[hist + tc - 2:hist + tc]
    sc_ext[hist - 2:hist] = new_sc

    @pl.when(last)
    def _():
        sc_o_ref[0] = new_sc

    zz = z_ref[0, :, COL_RWKV:COL_GATE]
    sh_ext[hist:hist + tc] = zz
    prev = pltpu.roll(sh_ext[...], 1, 0)[hist:hist + tc]
    zs = zz + (prev - zz) * mu_ref[...]
    new_sh = sh_ext[hist + tc - 1:hist + tc]
    sh_ext[hist - 1:hist] = new_sh

    @pl.when(last)
    def _():
        sh_o_ref[0] = new_sh

    r = zs[:, 0:MIX_W]
    kx = zs[:, MIX_W:2 * MIX_W]
    vx = zs[:, 2 * MIX_W:3 * MIX_W]
    o3 = 3 * MIX_W
    wl = zs[:, o3:o3 + LORA_W]
    al = zs[:, o3 + LORA_W:o3 + LORA_W + LORA_A]
    gl = zs[:, o3 + LORA_W + LORA_A:]
    wlog = -_softplus(-(vec(V_W0) + _bdot(jnp.tanh(wl), wlora_ref[...]))) - 0.5
    logw = -jnp.exp(wlog)
    asig = _sigmoid(vec(V_A0) + _bdot(al, alora_ref[...]))
    g_out = _bdot(_sigmoid(gl), glora_ref[...])
    kk_raw = kx * vec(V_KK)
    k2 = kx * (1.0 + (asig - 1.0) * vec(V_KA))

    c = min(RWKV_CHUNK, tc)
    ci = lax.broadcasted_iota(jnp.int32, (c, c), 0)
    cj = lax.broadcasted_iota(jnp.int32, (c, c), 1)
    strict = ci > cj
    incl = ci >= cj
    nsub = tc // c
    merged = c % SUBLANES == 0

    ti = lax.broadcasted_iota(jnp.int32, (tc, tc), 0)
    tj = lax.broadcasted_iota(jnp.int32, (tc, tc), 1)
    shift_c = int(math.log2(c))
    tri = ((ti >= tj) & (jnp.right_shift(ti, shift_c) == jnp.right_shift(tj, shift_c))).astype(BF16)
    lw_hi = logw.astype(BF16)
    lw_lo = (logw - lw_hi.astype(F32)).astype(BF16)
    cum = jnp.dot(tri, lw_hi, preferred_element_type=F32) + jnp.dot(tri, lw_lo, preferred_element_type=F32)
    p_in = jnp.exp(cum)
    p_inv = jnp.exp(-cum)
    kk_cols = []
    for sl in hsl:
        kkh = kk_raw[:, sl]
        kk_cols.append(kkh / jnp.maximum(jnp.sqrt(jnp.sum(kkh * kkh, axis=-1, keepdims=True)), L2_EPS))
    kk = jnp.concatenate(kk_cols, axis=1)
    a_all = -kk * jnp.exp(cum - logw)
    b_all = kk * asig * p_inv
    k_all = k2 * p_inv
    r_all = r * p_in

    blocks = [(sub, h) for sub in range(nsub) for h in heads]

    def blk(x, sub, h):
        return x[sub * c:(sub + 1) * c, hsl[h]]

    a_l = [blk(a_all, *b) for b in blocks]
    b_l = [blk(b_all, *b) for b in blocks]
    k_l = [blk(k_all, *b) for b in blocks]
    r_l = [blk(r_all, *b) for b in blocks]
    v_l = [blk(vx, *b) for b in blocks]
    nb = range(len(blocks))
    if merged:
        bk_l = [jnp.concatenate([b_l[i], k_l[i]], axis=0) for i in nb]
        gram = [_bdot_nt(jnp.concatenate([a_l[i], r_l[i]], axis=0), bk_l[i]) for i in nb]
        g_ab, g_ak = [g[:c, :c] for g in gram], [g[:c, c:] for g in gram]
        g_rb, g_rk = [g[c:, :c] for g in gram], [g[c:, c:] for g in gram]
    else:
        g_ab = [_bdot_nt(a_l[i], b_l[i]) for i in nb]
        g_ak = [_bdot_nt(a_l[i], k_l[i]) for i in nb]
        g_rb = [_bdot_nt(r_l[i], b_l[i]) for i in nb]
        g_rk = [_bdot_nt(r_l[i], k_l[i]) for i in nb]
    l_ab = [jnp.where(strict, g, 0.0) for g in g_ab]
    l_ak = [jnp.where(strict, g, 0.0) for g in g_ak]
    m_rb = [jnp.where(incl, g, 0.0) for g in g_rb]
    m_rk = [jnp.where(incl, g, 0.0) for g in g_rk]
    n_inv = _tri_inverse_minus_eye(l_ab, c)
    if merged:
        lv = [_bdot(jnp.concatenate([l_ak[i], m_rk[i]], axis=0), v_l[i]) for i in nb]
        lakv, mv = [x[:c] for x in lv], [x[c:] for x in lv]
    else:
        lakv = [_bdot(l_ak[i], v_l[i]) for i in nb]
        mv = [_bdot(m_rk[i], v_l[i]) for i in nb]
    nx = [_bdot(n_inv[i], jnp.concatenate([lakv[i], a_l[i]], axis=1)) for i in nb]
    tlv = [lakv[i] + nx[i][:, :HEAD_D] for i in nb]
    ta = [a_l[i] + nx[i][:, HEAD_D:] for i in nb]

    wkv_states = [swkv[h] for h in heads]
    for sub in range(nsub):
        ids = [sub * HEADS + h for h in heads]
        p_end = p_in[(sub + 1) * c - 1:(sub + 1) * c, :]
        u_mat = [_bdot_nt(ta[i], wkv_states[h]) + tlv[i] for h, i in zip(heads, ids)]
        y_st = [_bdot_nt(r_l[i], wkv_states[h]) for h, i in zip(heads, ids)]
        if merged:
            upd = [_bdot_tn(jnp.concatenate([u_mat[h], v_l[i]], axis=0), bk_l[i] * p_end[:, hsl[h]])
                   for h, i in zip(heads, ids)]
        else:
            upd = [_bdot_tn(u_mat[h], b_l[i] * p_end[:, hsl[h]]) + _bdot_tn(v_l[i], k_l[i] * p_end[:, hsl[h]])
                   for h, i in zip(heads, ids)]
        y_u = [_bdot(m_rb[i], u_mat[h]) for h, i in zip(heads, ids)]
        for h, i in zip(heads, ids):
            ywkv[sub * c:(sub + 1) * c, hsl[h]] = y_st[h] + y_u[h] + mv[i]
        wkv_states = [wkv_states[h] * p_end[:, hsl[h]] + upd[h] for h in heads]
    for h in heads:
        swkv[h] = wkv_states[h]

    @pl.when(last)
    def _():
        wkv_o_ref[0] = swkv[...]

    for h in range(HEADS):
        sl = slice(h * HEAD_D, (h + 1) * HEAD_D)
        yh = ywkv[:, sl]
        mean = jnp.mean(yh, axis=-1, keepdims=True)
        cen = yh - mean
        var = jnp.mean(cen * cen, axis=-1, keepdims=True)
        yn = cen * lax.rsqrt(var + RWKV_LN_EPS) * vec(V_LNG)[:, sl] + vec(V_LNB)[:, sl]
        bonus = jnp.sum(r[:, sl] * k2[:, sl] * vec(V_RK)[:, sl], axis=-1, keepdims=True) * vx[:, sl]
        y_ref[0, :, 3 * MIX_W + h * HEAD_D:3 * MIX_W + (h + 1) * HEAD_D] = (yn + bonus) * g_out[:, sl]


def _seq_chunk(t):
    return min(t, 256)


def _seq_call(z, cos_t, sin_t, states, poolw, vecs, mu, wlora, alora, glora, pos0):
    B, T, _ = z.shape
    tc = _seq_chunk(T)
    nt = T // tc
    tp = _round_up(tc, SUBLANES)
    pool0, ret0, sc0, sh0, wkv0 = states
    sh0 = sh0.reshape(B, 1, D_MODEL)

    def batch_spec(arr):
        shape = arr.shape[1:]
        zeros = (0,) * len(shape)
        return pl.BlockSpec((1,) + shape, lambda b, j: (b,) + zeros)

    def const_spec(arr):
        zeros = (0,) * arr.ndim
        return pl.BlockSpec(arr.shape, lambda b, j: zeros)

    state_specs = [batch_spec(a) for a in (pool0, ret0, sc0, sh0, wkv0)]
    outs = pl.pallas_call(
        functools.partial(_seq_kernel, tc=tc, pos0=pos0, nt=nt),
        grid=(B, nt),
        in_specs=[pl.BlockSpec((1, tc, IN_COLS), lambda b, j: (b, j, 0)),
                  pl.BlockSpec((tc, MIX_W), lambda b, j: (j, 0)),
                  pl.BlockSpec((tc, MIX_W), lambda b, j: (j, 0))]
        + state_specs + [const_spec(a) for a in (poolw, vecs, mu, wlora, alora, glora)],
        out_specs=[pl.BlockSpec((1, tc, D_MODEL), lambda b, j: (b, j, 0))] + state_specs,
        out_shape=[jax.ShapeDtypeStruct((B, T, D_MODEL), F32)]
        + [jax.ShapeDtypeStruct(a.shape, F32) for a in (pool0, ret0, sc0, sh0, wkv0)],
        scratch_shapes=[pltpu.VMEM((16 + tp, MIX_W), F32),
                        pltpu.VMEM((SUBLANES + tp, MIX_W), F32),
                        pltpu.VMEM((SUBLANES + tp, D_MODEL), F32),
                        pltpu.VMEM((HEADS, HEAD_D, HEAD_D), F32),
                        pltpu.VMEM((HEADS, HEAD_D, HEAD_D), F32),
                        pltpu.VMEM((tc, MIX_W), F32)],
        compiler_params=_params(("parallel", "arbitrary")),
        name="seq_mixers",
    )(z, cos_t, sin_t, pool0, ret0, sc0, sh0, wkv0, poolw, vecs, mu, wlora, alora, glora)
    y, n_pool, n_ret, n_sc, n_sh, n_wkv = outs
    return y, (n_pool, n_ret, n_sc, n_sh.reshape(B, D_MODEL), n_wkv)


CONV_COLS = 512


def _conv_kernel(up_ref, st0_ref, w_ref, act_ref, st_o_ref, ext, *, tc, nt):
    j = pl.program_id(1)
    hist = SUBLANES

    @pl.when(j == 0)
    def _init():
        ext[...] = jnp.zeros(ext.shape, F32)
        ext[hist - 2:hist] = st0_ref[0]

    ext[hist:hist + tc] = up_ref[0]

    def conv_cols(lo, width):
        e = ext[:, lo:lo + width]
        w = w_ref[:, lo:lo + width]
        y = w[2:3] * e + w[1:2] * pltpu.roll(e, 1, 0) + w[0:1] * pltpu.roll(e, 2, 0)
        return y[hist:hist + tc]

    for lo in range(0, D_FF, CONV_COLS):
        width = min(CONV_COLS, D_FF - lo)
        act_ref[0, :, lo:lo + width] = _silu(conv_cols(lo, width)) * conv_cols(D_FF + lo, width)
    new_st = ext[hist + tc - 2:hist + tc]
    ext[hist - 2:hist] = new_st

    @pl.when(j == nt - 1)
    def _():
        st_o_ref[0] = new_st


def _conv_call(up, st0, ffn_w):
    B, T, cols = up.shape
    tc = _seq_chunk(T)
    nt = T // tc
    tp = _round_up(tc, SUBLANES)
    st_spec = pl.BlockSpec((1, 2, cols), lambda b, j: (b, 0, 0))
    return pl.pallas_call(
        functools.partial(_conv_kernel, tc=tc, nt=nt),
        grid=(B, nt),
        in_specs=[pl.BlockSpec((1, tc, cols), lambda b, j: (b, j, 0)), st_spec,
                  pl.BlockSpec(ffn_w.shape, lambda b, j: (0, 0))],
        out_specs=[pl.BlockSpec((1, tc, D_FF), lambda b, j: (b, j, 0)), st_spec],
        out_shape=[jax.ShapeDtypeStruct((B, T, D_FF), F32), jax.ShapeDtypeStruct(st0.shape, F32)],
        scratch_shapes=[pltpu.VMEM((SUBLANES + tp, cols), F32)],
        compiler_params=_params(("parallel", "arbitrary")),
        name="ffn_conv_gate",
    )(up, st0, ffn_w)


def _rope_tables(t, pos0):
    half = HEAD_D // 2
    inv = ROPE_BASE ** (-jnp.arange(half, dtype=F32) / half)
    pos = pos0 + jnp.arange(t, dtype=jnp.int32)
    ang = pos.astype(F32)[:, None] * inv[None, :]
    cos, sin = jnp.cos(ang), jnp.sin(ang)
    cos_t = jnp.tile(jnp.concatenate([cos, cos], axis=-1), (1, HEADS))
    sin_t = jnp.tile(jnp.concatenate([-sin, sin], axis=-1), (1, HEADS))
    return cos_t, sin_t


def _block_diag(pool_w):
    groups, gw, _ = pool_w.shape
    out = jnp.zeros((groups * gw, groups * gw), pool_w.dtype)
    for g in range(groups):
        out = out.at[g * gw:(g + 1) * gw, g * gw:(g + 1) * gw].set(pool_w[g])
    return out


def kernel(x_prompt, x_sample, c_prompt, c_sample, state_pool, state_ret, state_sconv, state_shift, state_wkv, state_ffn, w_ada, b_ada, norm1_g, norm2_g, w_in, pool_w, pool_scale, sc_w, rw_mu, rw_w0, rw_w_lora, rw_a0, rw_a_lora, rw_g_lora, rw_k_k, rw_k_a, rw_r_k, rw_ln_g, rw_ln_b, w_br, w_out, w_up, ffn_w, w_down, final_g):
    depth = w_in.shape[0]
    bp, tp_, d = x_prompt.shape
    bs, ts, _ = x_sample.shape

    mods = _mod_call(jnp.concatenate([c_prompt, c_sample], axis=0), w_ada, b_ada)
    mods = mods.reshape(depth, bp + bs, 6, d)
    cos_p, sin_p = _rope_tables(tp_, 0)
    cos_s, sin_s = _rope_tables(ts, PAST_LEN)
    final_g2 = final_g.reshape(1, d)

    xp = x_prompt
    xs = jnp.transpose(x_sample, (1, 0, 2))
    new_p, new_s = [], []
    for l in range(depth):
        mod_p = jnp.transpose(mods[l, :bp], (1, 0, 2))[:, :, None, :]
        mod_s = jnp.transpose(mods[l, bp:], (1, 0, 2))[:, None, :, :]
        w_in_l = w_in[l].astype(BF16)
        w_br_l = w_br[l].astype(BF16)
        w_out_l = w_out[l].astype(BF16)
        w_up_l = w_up[l].astype(BF16)
        w_down_l = w_down[l].astype(BF16)
        poolw = _block_diag(pool_w[l]).astype(BF16)
        rows = [pool_scale[l], sc_w[l, 0], sc_w[l, 1], sc_w[l, 2], rw_w0[l], rw_a0[l], rw_k_k[l], rw_k_a[l],
                rw_r_k[l].reshape(MIX_W), rw_ln_g[l], rw_ln_b[l]]
        vecs = jnp.concatenate([jnp.stack(rows), jnp.zeros((N_VECS - len(rows), MIX_W), F32)], axis=0)
        mu = rw_mu[l].reshape(1, d)
        g1 = norm1_g[l].reshape(1, d)
        g2 = norm2_g[l].reshape(1, d)
        seq_w = (poolw, vecs, mu, rw_w_lora[l], rw_a_lora[l], rw_g_lora[l])
        final = l == depth - 1

        z = _pre_call(xp, mod_p, g1, w_in_l)
        zero = (jnp.zeros((bp, POOL_PAST, MIX_W), F32), jnp.zeros((bp, HEADS, HEAD_D, HEAD_D), F32),
                jnp.zeros((bp, 2, MIX_W), F32), jnp.zeros((bp, d), F32),
                jnp.zeros((bp, HEADS, HEAD_D, HEAD_D), F32))
        y, st = _seq_call(z, cos_p, sin_p, zero, *seq_w, pos0=0)
        x1, up = _post_call(xp, z, y, mod_p, g2, w_br_l, w_out_l, w_up_l)
        act, n_ffn = _conv_call(up, jnp.zeros((bp, 2, 2 * D_FF), F32), ffn_w[l])
        xp = _down_call(act, x1, mod_p, w_down_l, final_g2, final)
        new_p.append(st + (n_ffn,))

        z = _pre_call(xs, mod_s, g1, w_in_l)
        st_in = (state_pool[l], state_ret[l], state_sconv[l], state_shift[l], state_wkv[l])
        y, st = _seq_call(jnp.transpose(z, (1, 0, 2)), cos_s, sin_s, st_in, *seq_w, pos0=PAST_LEN)
        x1, up = _post_call(xs, z, jnp.transpose(y, (1, 0, 2)), mod_s, g2, w_br_l, w_out_l, w_up_l)
        act, n_ffn = _conv_call(jnp.transpose(up, (1, 0, 2)), state_ffn[l], ffn_w[l])
        xs = _down_call(jnp.transpose(act, (1, 0, 2)), x1, mod_s, w_down_l, final_g2, final)
        new_s.append(st + (n_ffn,))

    y_prompt = xp
    y_sample = jnp.transpose(xs, (1, 0, 2))
    outs_p = [jnp.stack([s[i] for s in new_p]) for i in range(6)]
    outs_s = [jnp.stack([s[i] for s in new_s]) for i in range(6)]
    return (y_prompt, y_sample, *outs_p, *outs_s)
```

```python
import functools
import math

import jax
import jax.numpy as jnp
from jax import lax
from jax.experimental import pallas as pl
from jax.experimental.pallas import tpu as pltpu

F32 = jnp.float32
BF16 = jnp.bfloat16

D_MODEL = 1024
MIX_W = 256
HEADS = 4
HEAD_D = 64
POOL_PAST = 15
ROPE_BASE = 10000.0
LORA_W = 64
LORA_A = 64
LORA_G = 128
D_FF = 2816
PAST_LEN = 16384
NORM_EPS = 1e-6
GN_EPS = 1e-6
RWKV_LN_EPS = 64e-5
L2_EPS = 1e-12

COL_RET = MIX_W
COL_SC = COL_RET + 4 * MIX_W
COL_RWKV = COL_SC + 3 * MIX_W
COL_GATE = COL_RWKV + 3 * MIX_W + LORA_W + LORA_A + LORA_G
IN_COLS = COL_GATE + 4 * D_MODEL

SUBLANES = 8
VMEM_LIMIT = 56 * 1024 * 1024
RWKV_CHUNK = 64
INV_BASE = 16
POOL_HIST = 16
HIST = SUBLANES
CONV_COLS = 512

(V_POOL_SCALE, V_SC0, V_SC1, V_SC2, V_W0, V_A0, V_KK, V_KA, V_RK, V_LNG, V_LNB) = range(11)
N_VECS = 16


def _bdot(a, b):
    return jnp.dot(a.astype(BF16), b.astype(BF16), preferred_element_type=F32)


def _bdot_nt(a, b):
    return lax.dot_general(a.astype(BF16), b.astype(BF16), (((1,), (1,)), ((), ())), preferred_element_type=F32)


def _bdot_tn(a, b):
    return lax.dot_general(a.astype(BF16), b.astype(BF16), (((0,), (0,)), ((), ())), preferred_element_type=F32)


def _wdot(a, w_bf16):
    return jnp.dot(a.astype(BF16), w_bf16, preferred_element_type=F32)


def _sigmoid(x):
    return 1.0 / (1.0 + jnp.exp(-x))


def _silu(x):
    return x * _sigmoid(x)


def _softplus(x):
    return jnp.maximum(x, 0.0) + jnp.log(1.0 + jnp.exp(-jnp.abs(x)))


def _rmsnorm(x, g):
    return x * lax.rsqrt(jnp.mean(x * x, axis=-1, keepdims=True) + NORM_EPS) * g


def _round_up(n, m):
    return (n + m - 1) // m * m


def _mod_row(mod_ref, k, lb):
    if len(mod_ref.shape) == 4:
        return mod_ref[k, lb]
    return mod_ref[k, lb:lb + 1, :]


def _mod_operand(mod, bb):
    six, _, d = mod.shape
    if bb == 1:
        return mod[:, :, None, :], pl.BlockSpec((six, 1, 1, d), lambda i, j: (0, i, 0, 0))
    return mod, pl.BlockSpec((six, bb, d), lambda i, j: (0, i, 0))


def _mod_kernel(c_ref, w_ref, b_ref, o_ref):
    o_ref[0] = _wdot(_silu(c_ref[...]), w_ref[0].astype(BF16)) + b_ref[0]


def _mod_call(c_all, w_ada, b_ada):
    depth, d, cols = w_ada.shape
    rows = c_all.shape[0]
    cb = D_MODEL
    return pl.pallas_call(
        _mod_kernel,
        grid=(depth, cols // cb),
        in_specs=[pl.BlockSpec((rows, d), lambda l, c: (0, 0)),
                  pl.BlockSpec((1, d, cb), lambda l, c: (l, 0, c)),
                  pl.BlockSpec((1, 1, cb), lambda l, c: (l, 0, c))],
        out_specs=pl.BlockSpec((1, rows, cb), lambda l, c: (l, 0, c)),
        out_shape=jax.ShapeDtypeStruct((depth, rows, cols), F32),
        name="adaln_mod",
    )(c_all, w_ada, b_ada.reshape(depth, 1, cols))


def _tri_inverse_minus_eye(lowers, n):
    i = lax.broadcasted_iota(jnp.int32, (n, n), 0)
    j = lax.broadcasted_iota(jnp.int32, (n, n), 1)

    def same_block(size):
        sh = int(math.log2(size))
        return jnp.right_shift(i, sh) == jnp.right_shift(j, sh)

    base = min(n, INV_BASE)
    if base < n:
        diag_mask = same_block(base)
        xs = [jnp.where(diag_mask, low, 0.0) for low in lowers]
    else:
        xs = list(lowers)
    powers, p = xs, 1
    while 2 * p < base:
        powers = [_bdot(pw, pw) for pw in powers]
        prods = [_bdot(x, pw) for x, pw in zip(xs, powers)]
        xs = [x + pw + pr for x, pw, pr in zip(xs, powers, prods)]
        p *= 2
    size = base
    while size < n:
        off_mask = same_block(2 * size) & jnp.logical_not(same_block(size))
        offs = [jnp.where(off_mask, low, 0.0) for low in lowers]
        lefts = [off + _bdot(x, off) for x, off in zip(xs, offs)]
        xs = [x + left + _bdot(left, x) for x, left in zip(xs, lefts)]
        size *= 2
    return xs


def _mix_kernel(x_ref, mod_ref, g_ref, win_ref, cos_ref, sin_ref,
                pool0_ref, ret0_ref, sc0_ref, sh0_ref, wkv0_ref,
                poolw_ref, vecs_ref, mu_ref, wlora_ref, alora_ref, glora_ref, wbr_ref, wout_ref,
                x1_ref, pool_ref, ret_ref, sc_ref, sh_ref, wkv_ref,
                h_s, z_s, y_s, pool_ext, sc_ext, sh_ext, ywkv, *, bb, tc, pos0):
    j = pl.program_id(1)
    seqs = range(bb)
    heads = range(HEADS)
    hsl = [slice(h * HEAD_D, (h + 1) * HEAD_D) for h in heads]

    def rows_of(lb):
        return slice(lb * tc, (lb + 1) * tc)

    @pl.when(j == 0)
    def _init():
        pool_ext[...] = jnp.zeros(pool_ext.shape, F32)
        pool_ext[:, POOL_HIST - POOL_PAST:POOL_HIST] = pool0_ref[...]
        sc_ext[...] = jnp.zeros(sc_ext.shape, F32)
        sc_ext[:, HIST - 2:HIST] = sc0_ref[...]
        sh_ext[...] = jnp.zeros(sh_ext.shape, F32)
        sh_ext[:, HIST - 1:HIST] = sh0_ref[...]
        ret_ref[...] = ret0_ref[...]
        wkv_ref[...] = wkv0_ref[...]

    def vec(row):
        return vecs_ref[row:row + 1, :]

    x = x_ref[...]
    hn = _rmsnorm(x, g_ref[...])
    for lb in seqs:
        h_s[rows_of(lb)] = hn[rows_of(lb)] * (1.0 + _mod_row(mod_ref, 1, lb)) + _mod_row(mod_ref, 0, lb)
    z_s[...] = _wdot(h_s[...], win_ref[...])

    lane = lax.broadcasted_iota(jnp.int32, (1, MIX_W), 1)
    row_i = lax.broadcasted_iota(jnp.int32, (tc, 1), 0)
    col_i = lax.broadcasted_iota(jnp.int32, (1, tc), 1)
    diff = (row_i - col_i).astype(F32)
    row_f = row_i.astype(F32)
    grp = jnp.right_shift(lane, 6)
    win = jnp.where(grp == 0, 2, jnp.where(grp == 1, 4, jnp.where(grp == 2, 8, 16)))
    cnt = jnp.minimum(win, pos0 + j * tc + row_i + 1).astype(F32)
    first_half = jnp.bitwise_and(lane, HEAD_D - 1) < HEAD_D // 2
    cos = cos_ref[...]
    sin = sin_ref[...]

    def rope(t):
        swapped = jnp.where(first_half, pltpu.roll(t, MIX_W - HEAD_D // 2, 1), pltpu.roll(t, HEAD_D // 2, 1))
        return t * cos + swapped * sin

    c = min(RWKV_CHUNK, tc)
    nsub = tc // c
    merged = c % SUBLANES == 0
    ci = lax.broadcasted_iota(jnp.int32, (c, c), 0)
    cj = lax.broadcasted_iota(jnp.int32, (c, c), 1)
    strict = ci > cj
    incl = ci >= cj
    ti = lax.broadcasted_iota(jnp.int32, (tc, tc), 0)
    tj = lax.broadcasted_iota(jnp.int32, (tc, tc), 1)
    shift_c = int(math.log2(c))
    tri = ((ti >= tj) & (jnp.right_shift(ti, shift_c) == jnp.right_shift(tj, shift_c))).astype(BF16)

    ret_in, rw = [], []
    for lb in seqs:
        rows = rows_of(lb)

        u = z_s[rows, 0:MIX_W]
        pool_ext[lb, POOL_HIST:POOL_HIST + tc] = u
        e = pool_ext[lb]
        s2 = e + pltpu.roll(e, 1, 0)
        s4 = s2 + pltpu.roll(s2, 2, 0)
        s8 = s4 + pltpu.roll(s4, 4, 0)
        s16 = s8 + pltpu.roll(s8, 8, 0)
        wsum = jnp.where(grp == 0, s2, jnp.where(grp == 1, s4, jnp.where(grp == 2, s8, s16)))
        m = wsum[POOL_HIST:POOL_HIST + tc] / cnt - u
        y_s[rows, 0:MIX_W] = _wdot(m, poolw_ref[...]) * vec(V_POOL_SCALE)
        new_pool = pool_ext[lb, tc + POOL_HIST - POOL_PAST:tc + POOL_HIST]
        pool_ext[lb, POOL_HIST - POOL_PAST:POOL_HIST] = new_pool
        pool_ref[lb] = new_pool

        hc = z_s[rows, COL_SC:COL_SC + MIX_W]
        bg = z_s[rows, COL_SC + MIX_W:COL_SC + 2 * MIX_W]
        cg = z_s[rows, COL_SC + 2 * MIX_W:COL_SC + 3 * MIX_W]
        sc_ext[lb, HIST:HIST + tc] = cg * hc
        e = sc_ext[lb]
        conv = vec(V_SC2) * e + vec(V_SC1) * pltpu.roll(e, 1, 0) + vec(V_SC0) * pltpu.roll(e, 2, 0)
        y_s[rows, 2 * MIX_W:3 * MIX_W] = bg * conv[HIST:HIST + tc]
        new_sc = sc_ext[lb, HIST + tc - 2:HIST + tc]
        sc_ext[lb, HIST - 2:HIST] = new_sc
        sc_ref[lb] = new_sc

        q = rope(z_s[rows, COL_RET:COL_RET + MIX_W])
        k = rope(z_s[rows, COL_RET + MIX_W:COL_RET + 2 * MIX_W]) * (HEAD_D ** -0.5)
        v = z_s[rows, COL_RET + 2 * MIX_W:COL_RET + 3 * MIX_W]
        ret_in.append((q, k, v))

        zz = z_s[rows, COL_RWKV:COL_GATE]
        sh_ext[lb, HIST:HIST + tc] = zz
        prev = pltpu.roll(sh_ext[lb], 1, 0)[HIST:HIST + tc]
        zs = zz + (prev - zz) * mu_ref[...]
        new_sh = sh_ext[lb, HIST + tc - 1:HIST + tc]
        sh_ext[lb, HIST - 1:HIST] = new_sh
        sh_ref[lb] = new_sh
        r = zs[:, 0:MIX_W]
        kx = zs[:, MIX_W:2 * MIX_W]
        vx = zs[:, 2 * MIX_W:3 * MIX_W]
        o3 = 3 * MIX_W
        wl = zs[:, o3:o3 + LORA_W]
        al = zs[:, o3 + LORA_W:o3 + LORA_W + LORA_A]
        gl = zs[:, o3 + LORA_W + LORA_A:]
        wlog = -_softplus(-(vec(V_W0) + _bdot(jnp.tanh(wl), wlora_ref[...]))) - 0.5
        logw = -jnp.exp(wlog)
        asig = _sigmoid(vec(V_A0) + _bdot(al, alora_ref[...]))
        g_out = _bdot(_sigmoid(gl), glora_ref[...])
        k2 = kx * (1.0 + (asig - 1.0) * vec(V_KA))
        kk_raw = kx * vec(V_KK)
        kk_cols = []
        for sl in hsl:
            kkh = kk_raw[:, sl]
            kk_cols.append(kkh / jnp.maximum(jnp.sqrt(jnp.sum(kkh * kkh, axis=-1, keepdims=True)), L2_EPS))
        kk = jnp.concatenate(kk_cols, axis=1)
        lw_hi = logw.astype(BF16)
        lw_lo = (logw - lw_hi.astype(F32)).astype(BF16)
        cum = jnp.dot(tri, lw_hi, preferred_element_type=F32) + jnp.dot(tri, lw_lo, preferred_element_type=F32)
        p_in = jnp.exp(cum)
        p_inv = jnp.exp(-cum)
        rw.append(dict(a=-kk * jnp.exp(cum - logw), b=kk * asig * p_inv, k=k2 * p_inv, r=r * p_in, v=vx,
                       p_in=p_in, bonus_rk=r * k2 * vec(V_RK), g_out=g_out))

    lgs = [math.log1p(-(2.0 ** (-5.0 - h))) for h in heads]
    pairs = [(lb, h) for lb in seqs for h in heads]
    qs = [ret_in[lb][0][:, hsl[h]] for lb, h in pairs]
    ks = [ret_in[lb][1][:, hsl[h]] for lb, h in pairs]
    vs = [ret_in[lb][2][:, hsl[h]] for lb, h in pairs]
    states = [ret_ref[lb, h] for lb, h in pairs]
    np_ = range(len(pairs))
    scores = [_bdot_nt(qs[i], ks[i]) for i in np_]
    cross = [_bdot(qs[i] * jnp.exp(lgs[pairs[i][1]] * (row_f + 1.0)), states[i]) for i in np_]
    kv = [_bdot_tn(ks[i] * jnp.exp(lgs[pairs[i][1]] * (tc - 1.0 - row_f)), vs[i]) for i in np_]
    dmasks = [jnp.where(diff >= 0, jnp.exp(lg * jnp.maximum(diff, 0.0)), 0.0) for lg in lgs]
    outs = [_bdot(scores[i] * dmasks[pairs[i][1]], vs[i]) + cross[i] for i in np_]
    for i, (lb, h) in enumerate(pairs):
        ret_ref[lb, h] = states[i] * math.exp(lgs[h] * tc) + kv[i]
        o = outs[i]
        o = o * lax.rsqrt(jnp.mean(o * o, axis=-1, keepdims=True) + GN_EPS)
        gate = z_s[rows_of(lb), COL_RET + 3 * MIX_W + h * HEAD_D:COL_RET + 3 * MIX_W + (h + 1) * HEAD_D]
        y_s[rows_of(lb), MIX_W + h * HEAD_D:MIX_W + (h + 1) * HEAD_D] = o * _silu(gate)

    blocks = [(lb, sub, h) for lb in seqs for sub in range(nsub) for h in heads]

    def blk(name):
        return [rw[lb][name][sub * c:(sub + 1) * c, hsl[h]] for lb, sub, h in blocks]

    a_l, b_l, k_l, r_l, v_l = blk("a"), blk("b"), blk("k"), blk("r"), blk("v")
    nb = range(len(blocks))
    if merged:
        bk_l = [jnp.concatenate([b_l[i], k_l[i]], axis=0) for i in nb]
        gram = [_bdot_nt(jnp.concatenate([a_l[i], r_l[i]], axis=0), bk_l[i]) for i in nb]
        g_ab, g_ak = [g[:c, :c] for g in gram], [g[:c, c:] for g in gram]
        g_rb, g_rk = [g[c:, :c] for g in gram], [g[c:, c:] for g in gram]
    else:
        g_ab = [_bdot_nt(a_l[i], b_l[i]) for i in nb]
        g_ak = [_bdot_nt(a_l[i], k_l[i]) for i in nb]
        g_rb = [_bdot_nt(r_l[i], b_l[i]) for i in nb]
        g_rk = [_bdot_nt(r_l[i], k_l[i]) for i in nb]
    l_ab = [jnp.where(strict, g, 0.0) for g in g_ab]
    l_ak = [jnp.where(strict, g, 0.0) for g in g_ak]
    m_rb = [jnp.where(incl, g, 0.0) for g in g_rb]
    m_rk = [jnp.where(incl, g, 0.0) for g in g_rk]
    n_inv = _tri_inverse_minus_eye(l_ab, c)
    if merged:
        lv = [_bdot(jnp.concatenate([l_ak[i], m_rk[i]], axis=0), v_l[i]) for i in nb]
        lakv, mv = [t[:c] for t in lv], [t[c:] for t in lv]
    else:
        lakv = [_bdot(l_ak[i], v_l[i]) for i in nb]
        mv = [_bdot(m_rk[i], v_l[i]) for i in nb]
    nx = [_bdot(n_inv[i], jnp.concatenate([lakv[i], a_l[i]], axis=1)) for i in nb]
    tlv = [lakv[i] + nx[i][:, :HEAD_D] for i in nb]
    ta = [a_l[i] + nx[i][:, HEAD_D:] for i in nb]

    wkv_states = [wkv_ref[lb, h] for lb, h in pairs]
    for sub in range(nsub):
        ids = [(lb * nsub + sub) * HEADS + h for lb, h in pairs]
        p_end = [rw[lb]["p_in"][(sub + 1) * c - 1:(sub + 1) * c, hsl[h]] for lb, h in pairs]
        u_mat = [_bdot_nt(ta[i], wkv_states[n]) + tlv[i] for n, i in enumerate(ids)]
        y_st = [_bdot_nt(r_l[i], wkv_states[n]) for n, i in enumerate(ids)]
        if merged:
            upd = [_bdot_tn(jnp.concatenate([u_mat[n], v_l[i]], axis=0), bk_l[i] * p_end[n])
                   for n, i in enumerate(ids)]
        else:
            upd = [_bdot_tn(u_mat[n], b_l[i] * p_end[n]) + _bdot_tn(v_l[i], k_l[i] * p_end[n])
                   for n, i in enumerate(ids)]
        y_u = [_bdot(m_rb[i], u_mat[n]) for n, i in enumerate(ids)]
        for n, i in enumerate(ids):
            lb, h = pairs[n]
            ywkv[lb * tc + sub * c:lb * tc + (sub + 1) * c, hsl[h]] = y_st[n] + y_u[n] + mv[i]
        wkv_states = [wkv_states[n] * p_end[n] + upd[n] for n in np_]
    for n, (lb, h) in enumerate(pairs):
        wkv_ref[lb, h] = wkv_states[n]

    for lb, h in pairs:
        rows, sl = rows_of(lb), hsl[h]
        yh = ywkv[rows, sl]
        mean = jnp.mean(yh, axis=-1, keepdims=True)
        cen = yh - mean
        var = jnp.mean(cen * cen, axis=-1, keepdims=True)
        yn = cen * lax.rsqrt(var + RWKV_LN_EPS) * vec(V_LNG)[:, sl] + vec(V_LNB)[:, sl]
        bonus = jnp.sum(rw[lb]["bonus_rk"][:, sl], axis=-1, keepdims=True) * rw[lb]["v"][:, sl]
        y_s[rows, 3 * MIX_W + h * HEAD_D:3 * MIX_W + (h + 1) * HEAD_D] = (yn + bonus) * rw[lb]["g_out"][:, sl]

    mixed = None
    for n in range(4):
        proj = _wdot(y_s[:, n * MIX_W:(n + 1) * MIX_W], wbr_ref[n])
        term = _sigmoid(z_s[:, COL_GATE + n * D_MODEL:COL_GATE + (n + 1) * D_MODEL]) * proj
        mixed = term if mixed is None else mixed + term
    out = _wdot(mixed, wout_ref[...])
    for lb in seqs:
        x1_ref[rows_of(lb)] = x[rows_of(lb)] + _mod_row(mod_ref, 2, lb) * out[rows_of(lb)]


def _batch_block_spec(arr, bb):
    shape = arr.shape[1:]
    zeros = (0,) * len(shape)
    return pl.BlockSpec((bb,) + shape, lambda i, j: (i,) + zeros)


def _resident_spec(arr):
    zeros = (0,) * arr.ndim
    return pl.BlockSpec(arr.shape, lambda i, j: zeros, pipeline_mode=pl.Buffered(1))


def _rows_spec(bb, tc, nt, cols):
    return pl.BlockSpec((bb * tc, cols), lambda i, j: (i * nt + j, 0))


def _params():
    return pltpu.CompilerParams(dimension_semantics=("parallel", "arbitrary"), vmem_limit_bytes=VMEM_LIMIT)


def _blocking(batch, t):
    if t >= 256:
        return 1, 256
    return min(batch, 64 // t), t


def _mix_call(x2d, mod, g1, w_in, cos_t, sin_t, states, poolw, vecs, mu, wlora, alora, glora, w_br, w_out,
              batch, t, pos0):
    bb, tc = _blocking(batch, t)
    nt = t // tc
    tp = _round_up(tc, SUBLANES)
    d = x2d.shape[1]
    pool0, ret0, sc0, sh0, wkv0 = states
    sh0 = sh0.reshape(batch, 1, d)
    state_arrs = (pool0, ret0, sc0, sh0, wkv0)
    state_specs = [_batch_block_spec(a, bb) for a in state_arrs]
    weights = (poolw, vecs, mu, wlora, alora, glora, w_br, w_out)
    mod, mod_spec = _mod_operand(mod, bb)
    outs = pl.pallas_call(
        functools.partial(_mix_kernel, bb=bb, tc=tc, pos0=pos0),
        grid=(batch // bb, nt),
        in_specs=[_rows_spec(bb, tc, nt, d), mod_spec,
                  _resident_spec(g1), _resident_spec(w_in),
                  pl.BlockSpec((tc, MIX_W), lambda i, j: (j, 0)),
                  pl.BlockSpec((tc, MIX_W), lambda i, j: (j, 0))]
        + state_specs + [_resident_spec(a) for a in weights],
        out_specs=[_rows_spec(bb, tc, nt, d)] + state_specs,
        out_shape=[jax.ShapeDtypeStruct(x2d.shape, F32)] + [jax.ShapeDtypeStruct(a.shape, F32) for a in state_arrs],
        scratch_shapes=[pltpu.VMEM((bb * tc, d), F32),
                        pltpu.VMEM((bb * tc, IN_COLS), F32),
                        pltpu.VMEM((bb * tc, d), F32),
                        pltpu.VMEM((bb, POOL_HIST + tp, MIX_W), F32),
                        pltpu.VMEM((bb, HIST + tp, MIX_W), F32),
                        pltpu.VMEM((bb, HIST + tp, d), F32),
                        pltpu.VMEM((bb * tc, MIX_W), F32)],
        compiler_params=_params(),
        name="mix",
    )(x2d, mod, g1, w_in, cos_t, sin_t, *state_arrs, *weights)
    x1, n_pool, n_ret, n_sc, n_sh, n_wkv = outs
    return x1, (n_pool, n_ret, n_sc, n_sh.reshape(batch, d), n_wkv)


def _ffn_kernel(x1_ref, mod_ref, g_ref, wup_ref, ffnw_ref, st0_ref, wdown_ref, fg_ref,
                o_ref, st_ref, h_s, ext, act_s, *, bb, tc, final):
    j = pl.program_id(1)
    seqs = range(bb)

    def rows_of(lb):
        return slice(lb * tc, (lb + 1) * tc)

    @pl.when(j == 0)
    def _init():
        ext[...] = jnp.zeros(ext.shape, F32)
        ext[:, HIST - 2:HIST] = st0_ref[...]

    x1 = x1_ref[...]
    hn = _rmsnorm(x1, g_ref[...])
    for lb in seqs:
        h_s[rows_of(lb)] = hn[rows_of(lb)] * (1.0 + _mod_row(mod_ref, 4, lb)) + _mod_row(mod_ref, 3, lb)
    if bb == 1:
        ext[0, HIST:HIST + tc] = _wdot(h_s[...], wup_ref[...])
    else:
        up = _wdot(h_s[...], wup_ref[...])
        for lb in seqs:
            ext[lb, HIST:HIST + tc] = up[rows_of(lb)]

    for lb in seqs:
        def conv_cols(lo, width):
            e = ext[lb, :, lo:lo + width]
            w = ffnw_ref[:, lo:lo + width]
            y = w[2:3] * e + w[1:2] * pltpu.roll(e, 1, 0) + w[0:1] * pltpu.roll(e, 2, 0)
            return y[HIST:HIST + tc]

        for lo in range(0, D_FF, CONV_COLS):
            width = min(CONV_COLS, D_FF - lo)
            act_s[rows_of(lb), lo:lo + width] = _silu(conv_cols(lo, width)) * conv_cols(D_FF + lo, width)
        new_st = ext[lb, HIST + tc - 2:HIST + tc]
        ext[lb, HIST - 2:HIST] = new_st
        st_ref[lb] = new_st

    dn = _wdot(act_s[...], wdown_ref[...])
    for lb in seqs:
        o_ref[rows_of(lb)] = x1[rows_of(lb)] + _mod_row(mod_ref, 5, lb) * dn[rows_of(lb)]
    if final:
        o_ref[...] = _rmsnorm(o_ref[...], fg_ref[...])


def _ffn_call(x1, mod, g2, w_up, ffn_w, st0, w_down, final_g, batch, t, final):
    bb, tc = _blocking(batch, t)
    nt = t // tc
    tp = _round_up(tc, SUBLANES)
    d = x1.shape[1]
    st_spec = _batch_block_spec(st0, bb)
    mod, mod_spec = _mod_operand(mod, bb)
    return pl.pallas_call(
        functools.partial(_ffn_kernel, bb=bb, tc=tc, final=final),
        grid=(batch // bb, nt),
        in_specs=[_rows_spec(bb, tc, nt, d), mod_spec,
                  _resident_spec(g2), _resident_spec(w_up), _resident_spec(ffn_w), st_spec,
                  _resident_spec(w_down), _resident_spec(final_g)],
        out_specs=[_rows_spec(bb, tc, nt, d), st_spec],
        out_shape=[jax.ShapeDtypeStruct(x1.shape, F32), jax.ShapeDtypeStruct(st0.shape, F32)],
        scratch_shapes=[pltpu.VMEM((bb * tc, d), F32),
                        pltpu.VMEM((bb, HIST + tp, 2 * D_FF), F32),
                        pltpu.VMEM((bb * tc, D_FF), F32)],
        compiler_params=_params(),
        name="ffn",
    )(x1, mod, g2, w_up, ffn_w, st0, w_down, final_g)


def _rope_tables(t, pos0):
    half = HEAD_D // 2
    inv = ROPE_BASE ** (-jnp.arange(half, dtype=F32) / half)
    pos = pos0 + jnp.arange(t, dtype=jnp.int32)
    ang = pos.astype(F32)[:, None] * inv[None, :]
    cos, sin = jnp.cos(ang), jnp.sin(ang)
    cos_t = jnp.tile(jnp.concatenate([cos, cos], axis=-1), (1, HEADS))
    sin_t = jnp.tile(jnp.concatenate([-sin, sin], axis=-1), (1, HEADS))
    return cos_t, sin_t


def _block_diag(pool_w):
    groups, gw, _ = pool_w.shape
    out = jnp.zeros((groups * gw, groups * gw), pool_w.dtype)
    for g in range(groups):
        out = out.at[g * gw:(g + 1) * gw, g * gw:(g + 1) * gw].set(pool_w[g])
    return out


def kernel(x_prompt, x_sample, c_prompt, c_sample, state_pool, state_ret, state_sconv, state_shift, state_wkv, state_ffn, w_ada, b_ada, norm1_g, norm2_g, w_in, pool_w, pool_scale, sc_w, rw_mu, rw_w0, rw_w_lora, rw_a0, rw_a_lora, rw_g_lora, rw_k_k, rw_k_a, rw_r_k, rw_ln_g, rw_ln_b, w_br, w_out, w_up, ffn_w, w_down, final_g):
    depth = w_in.shape[0]
    bp, tp_, d = x_prompt.shape
    bs, ts, _ = x_sample.shape

    mods = _mod_call(jnp.concatenate([c_prompt, c_sample], axis=0), w_ada, b_ada)
    mods = mods.reshape(depth, bp + bs, 6, d)
    cos_p, sin_p = _rope_tables(tp_, 0)
    cos_s, sin_s = _rope_tables(ts, PAST_LEN)
    final_g2 = final_g.reshape(1, d)

    xp = x_prompt.reshape(bp * tp_, d)
    xs = x_sample.reshape(bs * ts, d)
    new_p, new_s = [], []
    for l in range(depth):
        mod_l = jnp.transpose(mods[l], (1, 0, 2))
        mod_p, mod_s = mod_l[:, :bp], mod_l[:, bp:]
        rows = [pool_scale[l], sc_w[l, 0], sc_w[l, 1], sc_w[l, 2], rw_w0[l], rw_a0[l], rw_k_k[l], rw_k_a[l],
                rw_r_k[l].reshape(MIX_W), rw_ln_g[l], rw_ln_b[l]]
        vecs = jnp.concatenate([jnp.stack(rows), jnp.zeros((N_VECS - len(rows), MIX_W), F32)], axis=0)
        mix_w = (norm1_g[l].reshape(1, d), w_in[l].astype(BF16))
        seq_w = (_block_diag(pool_w[l]).astype(BF16), vecs, rw_mu[l].reshape(1, d), rw_w_lora[l], rw_a_lora[l],
                 rw_g_lora[l], w_br[l].astype(BF16), w_out[l].astype(BF16))
        ffn_ws = (norm2_g[l].reshape(1, d), w_up[l].astype(BF16), ffn_w[l])
        w_down_l = w_down[l].astype(BF16)
        final = l == depth - 1

        zero = (jnp.zeros((bp, POOL_PAST, MIX_W), F32), jnp.zeros((bp, HEADS, HEAD_D, HEAD_D), F32),
                jnp.zeros((bp, 2, MIX_W), F32), jnp.zeros((bp, d), F32),
                jnp.zeros((bp, HEADS, HEAD_D, HEAD_D), F32))
        x1, st = _mix_call(xp, mod_p, *mix_w, cos_p, sin_p, zero, *seq_w, batch=bp, t=tp_, pos0=0)
        xp, n_ffn = _ffn_call(x1, mod_p, *ffn_ws, jnp.zeros((bp, 2, 2 * D_FF), F32), w_down_l, final_g2,
                              batch=bp, t=tp_, final=final)
        new_p.append(st + (n_ffn,))

        st_in = (state_pool[l], state_ret[l], state_sconv[l], state_shift[l], state_wkv[l])
        x1, st = _mix_call(xs, mod_s, *mix_w, cos_s, sin_s, st_in, *seq_w, batch=bs, t=ts, pos0=PAST_LEN)
        xs, n_ffn = _ffn_call(x1, mod_s, *ffn_ws, state_ffn[l], w_down_l, final_g2, batch=bs, t=ts, final=final)
        new_s.append(st + (n_ffn,))

    y_prompt = xp.reshape(bp, tp_, d)
    y_sample = xs.reshape(bs, ts, d)
    outs_p = [jnp.stack([s[i] for s in new_p]) for i in range(6)]
    outs_s = [jnp.stack([s[i] for s in new_s]) for i in range(6)]
    return (y_prompt, y_sample, *outs_p, *outs_s)
```

```python
import functools
import math

import jax
import jax.numpy as jnp
from jax import lax
from jax.experimental import pallas as pl
from jax.experimental.pallas import tpu as pltpu

F32 = jnp.float32
BF16 = jnp.bfloat16

D_MODEL = 1024
MIX_W = 256
HEADS = 4
HEAD_D = 64
POOL_PAST = 15
ROPE_BASE = 10000.0
LORA_W = 64
LORA_A = 64
LORA_G = 128
D_FF = 2816
PAST_LEN = 16384
NORM_EPS = 1e-6
GN_EPS = 1e-6
RWKV_LN_EPS = 64e-5
L2_EPS = 1e-12

COL_RET = MIX_W
COL_SC = COL_RET + 4 * MIX_W
COL_RWKV = COL_SC + 3 * MIX_W
COL_GATE = COL_RWKV + 3 * MIX_W + LORA_W + LORA_A + LORA_G
IN_COLS = COL_GATE + 4 * D_MODEL

SUBLANES = 8
VMEM_LIMIT = 56 * 1024 * 1024
RWKV_CHUNK = 64
INV_BASE = 16
POOL_HIST = 16
HIST = SUBLANES
CONV_COLS = 512

(V_POOL_SCALE, V_SC0, V_SC1, V_SC2, V_W0, V_A0, V_KK, V_KA, V_RK, V_LNG, V_LNB) = range(11)
N_VECS = 16


def _bdot(a, b):
    return jnp.dot(a.astype(BF16), b.astype(BF16), preferred_element_type=F32)


def _bdot_nt(a, b):
    return lax.dot_general(a.astype(BF16), b.astype(BF16), (((1,), (1,)), ((), ())), preferred_element_type=F32)


def _bdot_tn(a, b):
    return lax.dot_general(a.astype(BF16), b.astype(BF16), (((0,), (0,)), ((), ())), preferred_element_type=F32)


def _wdot(a, w_bf16):
    return jnp.dot(a.astype(BF16), w_bf16, preferred_element_type=F32)


def _sigmoid(x):
    return 1.0 / (1.0 + jnp.exp(-x))


def _silu(x):
    half = 0.5 * x
    return half + half * jnp.tanh(half)


def _softplus(x):
    return jnp.maximum(x, 0.0) + jnp.log(1.0 + jnp.exp(-jnp.abs(x)))


def _rmsnorm(x, g):
    return x * lax.rsqrt(jnp.mean(x * x, axis=-1, keepdims=True) + NORM_EPS) * g


def _round_up(n, m):
    return (n + m - 1) // m * m


def _mod_row(mod_ref, k, lb):
    if len(mod_ref.shape) == 4:
        return mod_ref[k, lb]
    return mod_ref[k, lb:lb + 1, :]


def _mod_operand(mod, bb):
    six, _, d = mod.shape
    if bb == 1:
        return mod[:, :, None, :], pl.BlockSpec((six, 1, 1, d), lambda i, j: (0, i, 0, 0))
    return mod, pl.BlockSpec((six, bb, d), lambda i, j: (0, i, 0))


def _mod_kernel(c_ref, w_ref, b_ref, o_ref):
    o_ref[0] = _wdot(_silu(c_ref[...]), w_ref[0].astype(BF16)) + b_ref[0]


def _mod_call(c_all, w_ada, b_ada):
    depth, d, cols = w_ada.shape
    rows = c_all.shape[0]
    cb = D_MODEL
    return pl.pallas_call(
        _mod_kernel,
        grid=(depth, cols // cb),
        in_specs=[pl.BlockSpec((rows, d), lambda l, c: (0, 0)),
                  pl.BlockSpec((1, d, cb), lambda l, c: (l, 0, c)),
                  pl.BlockSpec((1, 1, cb), lambda l, c: (l, 0, c))],
        out_specs=pl.BlockSpec((1, rows, cb), lambda l, c: (l, 0, c)),
        out_shape=jax.ShapeDtypeStruct((depth, rows, cols), F32),
        name="adaln_mod",
    )(c_all, w_ada, b_ada.reshape(depth, 1, cols))


def _tri_inverse_minus_eye(lowers, n):
    i = lax.broadcasted_iota(jnp.int32, (n, n), 0)
    j = lax.broadcasted_iota(jnp.int32, (n, n), 1)

    def same_block(size):
        sh = int(math.log2(size))
        return jnp.right_shift(i, sh) == jnp.right_shift(j, sh)

    base = min(n, INV_BASE)
    if base < n:
        diag_mask = same_block(base)
        xs = [jnp.where(diag_mask, low, 0.0) for low in lowers]
    else:
        xs = list(lowers)
    powers, p = xs, 1
    while 2 * p < base:
        powers = [_bdot(pw, pw) for pw in powers]
        prods = [_bdot(x, pw) for x, pw in zip(xs, powers)]
        xs = [x + pw + pr for x, pw, pr in zip(xs, powers, prods)]
        p *= 2
    size = base
    while size < n:
        off_mask = same_block(2 * size) & jnp.logical_not(same_block(size))
        offs = [jnp.where(off_mask, low, 0.0) for low in lowers]
        lefts = [off + _bdot(x, off) for x, off in zip(xs, offs)]
        xs = [x + left + _bdot(left, x) for x, left in zip(xs, lefts)]
        size *= 2
    return xs


def _mix_kernel(*refs, bb, tc, pos0, zero_init, n_alias):
    refs = list(refs)
    x_ref, mod_ref, g_ref, win_ref, cos_ref, sin_ref = refs[:6]
    del refs[:6]
    if not zero_init:
        pool0_ref, ret0_ref, sc0_ref, sh0_ref, wkv0_ref = refs[:5]
        del refs[:5]
    poolw_ref, vecs_ref, mu_ref, wlora_ref, alora_ref, glora_ref, wbr_ref, wout_ref = refs[:8]
    del refs[:8 + n_alias]
    x1_ref, pool_ref, ret_ref, sc_ref, sh_ref, wkv_ref = refs[:6]
    h_s, z_s, y_s, dm_s, pool_ext, sc_ext, sh_ext, ywkv = refs[6:]
    j = pl.program_id(1)
    seqs = range(bb)
    heads = range(HEADS)
    hsl = [slice(h * HEAD_D, (h + 1) * HEAD_D) for h in heads]

    def rows_of(lb):
        return slice(lb * tc, (lb + 1) * tc)

    lane = lax.broadcasted_iota(jnp.int32, (1, MIX_W), 1)
    row_i = lax.broadcasted_iota(jnp.int32, (tc, 1), 0)
    col_i = lax.broadcasted_iota(jnp.int32, (1, tc), 1)
    row_f = row_i.astype(F32)
    lgs = [math.log1p(-(2.0 ** (-5.0 - h))) for h in heads]

    @pl.when(j == 0)
    def _init():
        diff = (row_i - col_i).astype(F32)
        for h in heads:
            dm_s[h] = jnp.where(diff >= 0, jnp.exp(lgs[h] * jnp.maximum(diff, 0.0)), 0.0)
        pool_ext[...] = jnp.zeros(pool_ext.shape, F32)
        sc_ext[...] = jnp.zeros(sc_ext.shape, F32)
        sh_ext[...] = jnp.zeros(sh_ext.shape, F32)
        if zero_init:
            ret_ref[...] = jnp.zeros(ret_ref.shape, F32)
            wkv_ref[...] = jnp.zeros(wkv_ref.shape, F32)
        else:
            pool_ext[:, POOL_HIST - POOL_PAST:POOL_HIST] = pool0_ref[...]
            sc_ext[:, HIST - 2:HIST] = sc0_ref[...]
            sh_ext[:, HIST - 1:HIST] = sh0_ref[...]
            ret_ref[...] = ret0_ref[...]
            wkv_ref[...] = wkv0_ref[...]

    def vec(row):
        return vecs_ref[row:row + 1, :]

    x = x_ref[...]
    hn = _rmsnorm(x, g_ref[...])
    for lb in seqs:
        h_s[rows_of(lb)] = hn[rows_of(lb)] * (1.0 + _mod_row(mod_ref, 1, lb)) + _mod_row(mod_ref, 0, lb)
    z_s[...] = _wdot(h_s[...], win_ref[...])

    def gated_proj(n):
        proj = _wdot(y_s[:, n * MIX_W:(n + 1) * MIX_W], wbr_ref[n])
        return _sigmoid(z_s[:, COL_GATE + n * D_MODEL:COL_GATE + (n + 1) * D_MODEL]) * proj

    grp = jnp.right_shift(lane, 6)
    win = jnp.where(grp == 0, 2, jnp.where(grp == 1, 4, jnp.where(grp == 2, 8, 16)))
    cnt = jnp.minimum(win, pos0 + j * tc + row_i + 1).astype(F32)
    first_half = jnp.bitwise_and(lane, HEAD_D - 1) < HEAD_D // 2
    cos = cos_ref[...]
    sin = sin_ref[...]

    def rope(t):
        swapped = jnp.where(first_half, pltpu.roll(t, MIX_W - HEAD_D // 2, 1), pltpu.roll(t, HEAD_D // 2, 1))
        return t * cos + swapped * sin

    c = min(RWKV_CHUNK, tc)
    nsub = tc // c
    merged = c % SUBLANES == 0
    ci = lax.broadcasted_iota(jnp.int32, (c, c), 0)
    cj = lax.broadcasted_iota(jnp.int32, (c, c), 1)
    strict = ci > cj
    incl = ci >= cj
    ti = lax.broadcasted_iota(jnp.int32, (tc, tc), 0)
    tj = lax.broadcasted_iota(jnp.int32, (tc, tc), 1)
    shift_c = int(math.log2(c))
    tri = ((ti >= tj) & (jnp.right_shift(ti, shift_c) == jnp.right_shift(tj, shift_c))).astype(BF16)

    ret_in, rw = [], []
    for lb in seqs:
        rows = rows_of(lb)

        u = z_s[rows, 0:MIX_W]
        pool_ext[lb, POOL_HIST:POOL_HIST + tc] = u
        e = pool_ext[lb]
        s2 = e + pltpu.roll(e, 1, 0)
        s4 = s2 + pltpu.roll(s2, 2, 0)
        s8 = s4 + pltpu.roll(s4, 4, 0)
        s16 = s8 + pltpu.roll(s8, 8, 0)
        wsum = jnp.where(grp == 0, s2, jnp.where(grp == 1, s4, jnp.where(grp == 2, s8, s16)))
        m = wsum[POOL_HIST:POOL_HIST + tc] / cnt - u
        y_s[rows, 0:MIX_W] = _wdot(m, poolw_ref[...]) * vec(V_POOL_SCALE)
        new_pool = pool_ext[lb, tc + POOL_HIST - POOL_PAST:tc + POOL_HIST]
        pool_ext[lb, POOL_HIST - POOL_PAST:POOL_HIST] = new_pool
        pool_ref[lb] = new_pool

        hc = z_s[rows, COL_SC:COL_SC + MIX_W]
        bg = z_s[rows, COL_SC + MIX_W:COL_SC + 2 * MIX_W]
        cg = z_s[rows, COL_SC + 2 * MIX_W:COL_SC + 3 * MIX_W]
        sc_ext[lb, HIST:HIST + tc] = cg * hc
        e = sc_ext[lb]
        conv = vec(V_SC2) * e + vec(V_SC1) * pltpu.roll(e, 1, 0) + vec(V_SC0) * pltpu.roll(e, 2, 0)
        y_s[rows, 2 * MIX_W:3 * MIX_W] = bg * conv[HIST:HIST + tc]
        new_sc = sc_ext[lb, HIST + tc - 2:HIST + tc]
        sc_ext[lb, HIST - 2:HIST] = new_sc
        sc_ref[lb] = new_sc

        q = rope(z_s[rows, COL_RET:COL_RET + MIX_W])
        k = rope(z_s[rows, COL_RET + MIX_W:COL_RET + 2 * MIX_W]) * (HEAD_D ** -0.5)
        v = z_s[rows, COL_RET + 2 * MIX_W:COL_RET + 3 * MIX_W]
        ret_in.append((q, k, v))

        zz = z_s[rows, COL_RWKV:COL_GATE]
        sh_ext[lb, HIST:HIST + tc] = zz
        prev = pltpu.roll(sh_ext[lb], 1, 0)[HIST:HIST + tc]
        zs = zz + (prev - zz) * mu_ref[...]
        new_sh = sh_ext[lb, HIST + tc - 1:HIST + tc]
        sh_ext[lb, HIST - 1:HIST] = new_sh
        sh_ref[lb] = new_sh
        r = zs[:, 0:MIX_W]
        kx = zs[:, MIX_W:2 * MIX_W]
        vx = zs[:, 2 * MIX_W:3 * MIX_W]
        o3 = 3 * MIX_W
        wl = zs[:, o3:o3 + LORA_W]
        al = zs[:, o3 + LORA_W:o3 + LORA_W + LORA_A]
        gl = zs[:, o3 + LORA_W + LORA_A:]
        wlog = -_softplus(-(vec(V_W0) + _bdot(jnp.tanh(wl), wlora_ref[...]))) - 0.5
        logw = -jnp.exp(wlog)
        asig = _sigmoid(vec(V_A0) + _bdot(al, alora_ref[...]))
        g_out = _bdot(_sigmoid(gl), glora_ref[...])
        k2 = kx * (1.0 + (asig - 1.0) * vec(V_KA))
        kk_raw = kx * vec(V_KK)
        kk_cols = []
        for sl in hsl:
            kkh = kk_raw[:, sl]
            kk_cols.append(kkh / jnp.maximum(jnp.sqrt(jnp.sum(kkh * kkh, axis=-1, keepdims=True)), L2_EPS))
        kk = jnp.concatenate(kk_cols, axis=1)
        lw_hi = logw.astype(BF16)
        lw_lo = (logw - lw_hi.astype(F32)).astype(BF16)
        cum = jnp.dot(tri, lw_hi, preferred_element_type=F32) + jnp.dot(tri, lw_lo, preferred_element_type=F32)
        p_in = jnp.exp(cum)
        p_inv = jnp.exp(-cum)
        rw.append(dict(a=-kk * jnp.exp(cum - logw), b=kk * asig * p_inv, k=k2 * p_inv, r=r * p_in, v=vx,
                       p_in=p_in, bonus_rk=r * k2 * vec(V_RK), g_out=g_out))

    pairs = [(lb, h) for lb in seqs for h in heads]
    qs = [ret_in[lb][0][:, hsl[h]] for lb, h in pairs]
    ks = [ret_in[lb][1][:, hsl[h]] for lb, h in pairs]
    vs = [ret_in[lb][2][:, hsl[h]] for lb, h in pairs]
    states = [ret_ref[lb, h] for lb, h in pairs]
    np_ = range(len(pairs))
    scores = [_bdot_nt(qs[i], ks[i]) for i in np_]
    cross = [_bdot(qs[i] * jnp.exp(lgs[pairs[i][1]] * (row_f + 1.0)), states[i]) for i in np_]
    kv = [_bdot_tn(ks[i] * jnp.exp(lgs[pairs[i][1]] * (tc - 1.0 - row_f)), vs[i]) for i in np_]
    outs = [_bdot(scores[i] * dm_s[pairs[i][1]], vs[i]) + cross[i] for i in np_]
    for i, (lb, h) in enumerate(pairs):
        ret_ref[lb, h] = states[i] * math.exp(lgs[h] * tc) + kv[i]
        o = outs[i]
        o = o * lax.rsqrt(jnp.mean(o * o, axis=-1, keepdims=True) + GN_EPS)
        gate = z_s[rows_of(lb), COL_RET + 3 * MIX_W + h * HEAD_D:COL_RET + 3 * MIX_W + (h + 1) * HEAD_D]
        y_s[rows_of(lb), MIX_W + h * HEAD_D:MIX_W + (h + 1) * HEAD_D] = o * _silu(gate)

    blocks = [(lb, sub, h) for lb in seqs for sub in range(nsub) for h in heads]

    def blk(name):
        return [rw[lb][name][sub * c:(sub + 1) * c, hsl[h]] for lb, sub, h in blocks]

    a_l, b_l, k_l, r_l, v_l = blk("a"), blk("b"), blk("k"), blk("r"), blk("v")
    nb = range(len(blocks))
    if merged:
        bk_l = [jnp.concatenate([b_l[i], k_l[i]], axis=0) for i in nb]
        gram = [_bdot_nt(jnp.concatenate([a_l[i], r_l[i]], axis=0), bk_l[i]) for i in nb]
        g_ab, g_ak = [g[:c, :c] for g in gram], [g[:c, c:] for g in gram]
        g_rb, g_rk = [g[c:, :c] for g in gram], [g[c:, c:] for g in gram]
    else:
        g_ab = [_bdot_nt(a_l[i], b_l[i]) for i in nb]
        g_ak = [_bdot_nt(a_l[i], k_l[i]) for i in nb]
        g_rb = [_bdot_nt(r_l[i], b_l[i]) for i in nb]
        g_rk = [_bdot_nt(r_l[i], k_l[i]) for i in nb]
    l_ab = [jnp.where(strict, g, 0.0) for g in g_ab]
    l_ak = [jnp.where(strict, g, 0.0) for g in g_ak]
    m_rb = [jnp.where(incl, g, 0.0) for g in g_rb]
    m_rk = [jnp.where(incl, g, 0.0) for g in g_rk]
    n_inv = _tri_inverse_minus_eye(l_ab, c)
    if merged:
        lv = [_bdot(jnp.concatenate([l_ak[i], m_rk[i]], axis=0), v_l[i]) for i in nb]
        lakv, mv = [t[:c] for t in lv], [t[c:] for t in lv]
    else:
        lakv = [_bdot(l_ak[i], v_l[i]) for i in nb]
        mv = [_bdot(m_rk[i], v_l[i]) for i in nb]
    nx = [_bdot(n_inv[i], jnp.concatenate([lakv[i], a_l[i]], axis=1)) for i in nb]
    tlv = [lakv[i] + nx[i][:, :HEAD_D] for i in nb]
    ta = [a_l[i] + nx[i][:, HEAD_D:] for i in nb]

    wkv_states = [wkv_ref[lb, h] for lb, h in pairs]
    for sub in range(nsub):
        ids = [(lb * nsub + sub) * HEADS + h for lb, h in pairs]
        p_end = [rw[lb]["p_in"][(sub + 1) * c - 1:(sub + 1) * c, hsl[h]] for lb, h in pairs]
        u_mat = [_bdot_nt(ta[i], wkv_states[n]) + tlv[i] for n, i in enumerate(ids)]
        y_st = [_bdot_nt(r_l[i], wkv_states[n]) for n, i in enumerate(ids)]
        if merged:
            upd = [_bdot_tn(jnp.concatenate([u_mat[n], v_l[i]], axis=0), bk_l[i] * p_end[n])
                   for n, i in enumerate(ids)]
        else:
            upd = [_bdot_tn(u_mat[n], b_l[i] * p_end[n]) + _bdot_tn(v_l[i], k_l[i] * p_end[n])
                   for n, i in enumerate(ids)]
        y_u = [_bdot(m_rb[i], u_mat[n]) for n, i in enumerate(ids)]
        for n, i in enumerate(ids):
            lb, h = pairs[n]
            ywkv[lb * tc + sub * c:lb * tc + (sub + 1) * c, hsl[h]] = y_st[n] + y_u[n] + mv[i]
        wkv_states = [wkv_states[n] * p_end[n] + upd[n] for n in np_]
    for n, (lb, h) in enumerate(pairs):
        wkv_ref[lb, h] = wkv_states[n]

    for lb, h in pairs:
        rows, sl = rows_of(lb), hsl[h]
        yh = ywkv[rows, sl]
        mean = jnp.mean(yh, axis=-1, keepdims=True)
        cen = yh - mean
        var = jnp.mean(cen * cen, axis=-1, keepdims=True)
        yn = cen * lax.rsqrt(var + RWKV_LN_EPS) * vec(V_LNG)[:, sl] + vec(V_LNB)[:, sl]
        bonus = jnp.sum(rw[lb]["bonus_rk"][:, sl], axis=-1, keepdims=True) * rw[lb]["v"][:, sl]
        y_s[rows, 3 * MIX_W + h * HEAD_D:3 * MIX_W + (h + 1) * HEAD_D] = (yn + bonus) * rw[lb]["g_out"][:, sl]

    out = _wdot(gated_proj(0) + gated_proj(1) + gated_proj(2) + gated_proj(3), wout_ref[...])
    for lb in seqs:
        x1_ref[rows_of(lb)] = x[rows_of(lb)] + _mod_row(mod_ref, 2, lb) * out[rows_of(lb)]


def _layer_block_spec(arr, bb, layer):
    shape = arr.shape[2:]
    zeros = (0,) * len(shape)
    return pl.BlockSpec((None, bb) + shape, lambda i, j: (layer, i) + zeros)


def _resident_spec(arr):
    zeros = (0,) * arr.ndim
    return pl.BlockSpec(arr.shape, lambda i, j: zeros, pipeline_mode=pl.Buffered(1))


def _rows_spec(bb, tc, nt, cols):
    return pl.BlockSpec((bb * tc, cols), lambda i, j: (i * nt + j, 0))


def _params():
    return pltpu.CompilerParams(dimension_semantics=("parallel", "arbitrary"), vmem_limit_bytes=VMEM_LIMIT)


def _blocking(batch, t):
    if t >= 256:
        return 1, 256
    return min(batch, 64 // t), t


def _passthrough(prev):
    if prev is None:
        return [], []
    return list(prev), [pl.BlockSpec(memory_space=pl.ANY) for _ in prev]


def _mix_call(x2d, mod, g1, w_in, cos_t, sin_t, states, poolw, vecs, mu, wlora, alora, glora, w_br, w_out,
              batch, t, pos0, layer, depth, prev):
    bb, tc = _blocking(batch, t)
    nt = t // tc
    tp = _round_up(tc, SUBLANES)
    d = x2d.shape[1]
    state_shapes = [(depth, batch, POOL_PAST, MIX_W), (depth, batch, HEADS, HEAD_D, HEAD_D), (depth, batch, 2, MIX_W),
                    (depth, batch, 1, d), (depth, batch, HEADS, HEAD_D, HEAD_D)]
    out_structs = [jax.ShapeDtypeStruct(sh, F32) for sh in state_shapes]
    state_specs = [_layer_block_spec(st, bb, layer) for st in out_structs]
    state_ops = [] if states is None else [st.reshape(sh) for st, sh in zip(states, state_shapes)]
    weights = (poolw, vecs, mu, wlora, alora, glora, w_br, w_out)
    mod, mod_spec = _mod_operand(mod, bb)
    prev_ops, prev_specs = _passthrough(prev)
    n_in = 6 + len(state_ops) + len(weights)
    outs = pl.pallas_call(
        functools.partial(_mix_kernel, bb=bb, tc=tc, pos0=pos0, zero_init=states is None, n_alias=len(prev_ops)),
        grid=(batch // bb, nt),
        in_specs=[_rows_spec(bb, tc, nt, d), mod_spec,
                  _resident_spec(g1), _resident_spec(w_in),
                  pl.BlockSpec((tc, MIX_W), lambda i, j: (j, 0)),
                  pl.BlockSpec((tc, MIX_W), lambda i, j: (j, 0))]
        + state_specs[:len(state_ops)] + [_resident_spec(a) for a in weights] + prev_specs,
        out_specs=[_rows_spec(bb, tc, nt, d)] + state_specs,
        out_shape=[jax.ShapeDtypeStruct(x2d.shape, F32)] + out_structs,
        input_output_aliases={n_in + k: 1 + k for k in range(len(prev_ops))},
        scratch_shapes=[pltpu.VMEM((bb * tc, d), F32),
                        pltpu.VMEM((bb * tc, IN_COLS), F32),
                        pltpu.VMEM((bb * tc, d), F32),
                        pltpu.VMEM((HEADS, tc, tc), F32),
                        pltpu.VMEM((bb, POOL_HIST + tp, MIX_W), F32),
                        pltpu.VMEM((bb, HIST + tp, MIX_W), F32),
                        pltpu.VMEM((bb, HIST + tp, d), F32),
                        pltpu.VMEM((bb * tc, MIX_W), F32)],
        compiler_params=_params(),
        name="mix",
    )(x2d, mod, g1, w_in, cos_t, sin_t, *state_ops, *weights, *prev_ops)
    return outs[0], tuple(outs[1:])


def _ffn_kernel(*refs, bb, tc, final, zero_init, n_alias):
    refs = list(refs)
    x1_ref, mod_ref, g_ref, wup_ref, ffnw_ref, wdown_ref, fg_ref = refs[:7]
    del refs[:7]
    if not zero_init:
        st0_ref = refs.pop(0)
    del refs[:n_alias]
    o_ref, st_ref, h_s, ext, act_s = refs
    j = pl.program_id(1)
    seqs = range(bb)

    def rows_of(lb):
        return slice(lb * tc, (lb + 1) * tc)

    @pl.when(j == 0)
    def _init():
        ext[...] = jnp.zeros(ext.shape, F32)
        if not zero_init:
            ext[:, HIST - 2:HIST] = st0_ref[...]

    x1 = x1_ref[...]
    hn = _rmsnorm(x1, g_ref[...])
    for lb in seqs:
        h_s[rows_of(lb)] = hn[rows_of(lb)] * (1.0 + _mod_row(mod_ref, 4, lb)) + _mod_row(mod_ref, 3, lb)
    if bb == 1:
        ext[0, HIST:HIST + tc] = _wdot(h_s[...], wup_ref[...])
    else:
        up = _wdot(h_s[...], wup_ref[...])
        for lb in seqs:
            ext[lb, HIST:HIST + tc] = up[rows_of(lb)]

    for lb in seqs:
        def conv_cols(lo, width):
            e = ext[lb, :, lo:lo + width]
            w = ffnw_ref[:, lo:lo + width]
            y = w[2:3] * e + w[1:2] * pltpu.roll(e, 1, 0) + w[0:1] * pltpu.roll(e, 2, 0)
            return y[HIST:HIST + tc]

        for lo in range(0, D_FF, CONV_COLS):
            width = min(CONV_COLS, D_FF - lo)
            act_s[rows_of(lb), lo:lo + width] = _silu(conv_cols(lo, width)) * conv_cols(D_FF + lo, width)
        new_st = ext[lb, HIST + tc - 2:HIST + tc]
        ext[lb, HIST - 2:HIST] = new_st
        st_ref[lb] = new_st

    dn = _wdot(act_s[...], wdown_ref[...])
    for lb in seqs:
        o_ref[rows_of(lb)] = x1[rows_of(lb)] + _mod_row(mod_ref, 5, lb) * dn[rows_of(lb)]
    if final:
        o_ref[...] = _rmsnorm(o_ref[...], fg_ref[...])


def _ffn_call(x1, mod, g2, w_up, ffn_w, w_down, final_g, state, batch, t, final, layer, depth, prev):
    bb, tc = _blocking(batch, t)
    nt = t // tc
    tp = _round_up(tc, SUBLANES)
    d = x1.shape[1]
    st_struct = jax.ShapeDtypeStruct((depth, batch, 2, 2 * D_FF), F32)
    st_spec = _layer_block_spec(st_struct, bb, layer)
    state_ops = [] if state is None else [state]
    mod, mod_spec = _mod_operand(mod, bb)
    prev_ops, prev_specs = _passthrough(prev)
    n_in = 7 + len(state_ops)
    return pl.pallas_call(
        functools.partial(_ffn_kernel, bb=bb, tc=tc, final=final, zero_init=state is None, n_alias=len(prev_ops)),
        grid=(batch // bb, nt),
        in_specs=[_rows_spec(bb, tc, nt, d), mod_spec,
                  _resident_spec(g2), _resident_spec(w_up), _resident_spec(ffn_w),
                  _resident_spec(w_down), _resident_spec(final_g)]
        + [st_spec] * len(state_ops) + prev_specs,
        out_specs=[_rows_spec(bb, tc, nt, d), st_spec],
        out_shape=[jax.ShapeDtypeStruct(x1.shape, F32), st_struct],
        input_output_aliases={n_in + k: 1 + k for k in range(len(prev_ops))},
        scratch_shapes=[pltpu.VMEM((bb * tc, d), F32),
                        pltpu.VMEM((bb, HIST + tp, 2 * D_FF), F32),
                        pltpu.VMEM((bb * tc, D_FF), F32)],
        compiler_params=_params(),
        name="ffn",
    )(x1, mod, g2, w_up, ffn_w, w_down, final_g, *state_ops, *prev_ops)


def _rope_tables(t, pos0):
    half = HEAD_D // 2
    inv = ROPE_BASE ** (-jnp.arange(half, dtype=F32) / half)
    pos = pos0 + jnp.arange(t, dtype=jnp.int32)
    ang = pos.astype(F32)[:, None] * inv[None, :]
    cos, sin = jnp.cos(ang), jnp.sin(ang)
    cos_t = jnp.tile(jnp.concatenate([cos, cos], axis=-1), (1, HEADS))
    sin_t = jnp.tile(jnp.concatenate([-sin, sin], axis=-1), (1, HEADS))
    return cos_t, sin_t


def _block_diag(pool_w):
    groups, gw, _ = pool_w.shape
    out = jnp.zeros((groups * gw, groups * gw), pool_w.dtype)
    for g in range(groups):
        out = out.at[g * gw:(g + 1) * gw, g * gw:(g + 1) * gw].set(pool_w[g])
    return out


def kernel(x_prompt, x_sample, c_prompt, c_sample, state_pool, state_ret, state_sconv, state_shift, state_wkv, state_ffn, w_ada, b_ada, norm1_g, norm2_g, w_in, pool_w, pool_scale, sc_w, rw_mu, rw_w0, rw_w_lora, rw_a0, rw_a_lora, rw_g_lora, rw_k_k, rw_k_a, rw_r_k, rw_ln_g, rw_ln_b, w_br, w_out, w_up, ffn_w, w_down, final_g):
    depth = w_in.shape[0]
    bp, tp_, d = x_prompt.shape
    bs, ts, _ = x_sample.shape

    mods = _mod_call(jnp.concatenate([c_prompt, c_sample], axis=0), w_ada, b_ada)
    mods = mods.reshape(depth, bp + bs, 6, d)
    cos_p, sin_p = _rope_tables(tp_, 0)
    cos_s, sin_s = _rope_tables(ts, PAST_LEN)
    final_g2 = final_g.reshape(1, d)

    xp = x_prompt.reshape(bp * tp_, d)
    xs = x_sample.reshape(bs * ts, d)
    sample_states = (state_pool, state_ret, state_sconv, state_shift, state_wkv)
    st_p = st_s = ffn_p = ffn_s = None
    for l in range(depth):
        mod_l = jnp.transpose(mods[l], (1, 0, 2))
        mod_p, mod_s = mod_l[:, :bp], mod_l[:, bp:]
        rows = [pool_scale[l], sc_w[l, 0], sc_w[l, 1], sc_w[l, 2], rw_w0[l], rw_a0[l], rw_k_k[l], rw_k_a[l],
                rw_r_k[l].reshape(MIX_W), rw_ln_g[l], rw_ln_b[l]]
        vecs = jnp.concatenate([jnp.stack(rows), jnp.zeros((N_VECS - len(rows), MIX_W), F32)], axis=0)
        mix_w = (norm1_g[l].reshape(1, d), w_in[l].astype(BF16))
        seq_w = (_block_diag(pool_w[l]).astype(BF16), vecs, rw_mu[l].reshape(1, d), rw_w_lora[l], rw_a_lora[l],
                 rw_g_lora[l], w_br[l].astype(BF16), w_out[l].astype(BF16))
        ffn_ws = (norm2_g[l].reshape(1, d), w_up[l].astype(BF16), ffn_w[l], w_down[l].astype(BF16), final_g2)
        final = l == depth - 1

        x1, st_p = _mix_call(xp, mod_p, *mix_w, cos_p, sin_p, None, *seq_w, batch=bp, t=tp_, pos0=0,
                             layer=l, depth=depth, prev=st_p)
        xp, ffn_p = _ffn_call(x1, mod_p, *ffn_ws, None, batch=bp, t=tp_, final=final, layer=l, depth=depth,
                              prev=None if ffn_p is None else (ffn_p,))
        x1, st_s = _mix_call(xs, mod_s, *mix_w, cos_s, sin_s, sample_states, *seq_w, batch=bs, t=ts,
                             pos0=PAST_LEN, layer=l, depth=depth, prev=st_s)
        xs, ffn_s = _ffn_call(x1, mod_s, *ffn_ws, state_ffn, batch=bs, t=ts, final=final, layer=l, depth=depth,
                              prev=None if ffn_s is None else (ffn_s,))

    def finish(st, ffn_st, batch):
        pool, ret, sc, sh, wkv = st
        return (pool, ret, sc, sh.reshape(depth, batch, d), wkv, ffn_st)

    return (xp.reshape(bp, tp_, d), xs.reshape(bs, ts, d), *finish(st_p, ffn_p, bp), *finish(st_s, ffn_s, bs))
```

```python
import functools
import math

import jax
import jax.numpy as jnp
from jax import lax
from jax.experimental import pallas as pl
from jax.experimental.pallas import tpu as pltpu

F32 = jnp.float32
BF16 = jnp.bfloat16

D_MODEL = 1024
MIX_W = 256
HEADS = 4
HEAD_D = 64
POOL_PAST = 15
ROPE_BASE = 10000.0
LORA_W = 64
LORA_A = 64
LORA_G = 128
D_FF = 2816
PAST_LEN = 16384
NORM_EPS = 1e-6
GN_EPS = 1e-6
RWKV_LN_EPS = 64e-5
L2_EPS = 1e-12

COL_RET = MIX_W
COL_SC = COL_RET + 4 * MIX_W
COL_RWKV = COL_SC + 3 * MIX_W
COL_GATE = COL_RWKV + 3 * MIX_W + LORA_W + LORA_A + LORA_G
IN_COLS = COL_GATE + 4 * D_MODEL

SUBLANES = 8
VMEM_LIMIT = 56 * 1024 * 1024
RWKV_CHUNK = 64
INV_BASE = 16
POOL_HIST = 16
HIST = SUBLANES
CONV_COLS = 512

(V_POOL_SCALE, V_SC0, V_SC1, V_SC2, V_W0, V_A0, V_KK, V_KA, V_RK, V_LNG, V_LNB) = range(11)
N_VECS = 16


def _bdot(a, b):
    return jnp.dot(a.astype(BF16), b.astype(BF16), preferred_element_type=F32)


def _bdot_nt(a, b):
    return lax.dot_general(a.astype(BF16), b.astype(BF16), (((1,), (1,)), ((), ())), preferred_element_type=F32)


def _bdot_tn(a, b):
    return lax.dot_general(a.astype(BF16), b.astype(BF16), (((0,), (0,)), ((), ())), preferred_element_type=F32)


def _wdot(a, w_bf16):
    return jnp.dot(a.astype(BF16), w_bf16, preferred_element_type=F32)


def _sigmoid(x):
    return 1.0 / (1.0 + jnp.exp(-x))


def _silu(x):
    half = 0.5 * x
    return half + half * jnp.tanh(half)


def _softplus(x):
    return jnp.maximum(x, 0.0) + jnp.log(1.0 + jnp.exp(-jnp.abs(x)))


def _rmsnorm(x, g):
    return x * lax.rsqrt(jnp.mean(x * x, axis=-1, keepdims=True) + NORM_EPS) * g


def _round_up(n, m):
    return (n + m - 1) // m * m


def _mod_row(mod_ref, k, lb):
    if len(mod_ref.shape) == 4:
        return mod_ref[k, lb]
    return mod_ref[k, lb:lb + 1, :]


def _mod_operand(mod, bb):
    six, _, d = mod.shape
    if bb == 1:
        return mod[:, :, None, :], pl.BlockSpec((six, 1, 1, d), lambda i, j: (0, i, 0, 0))
    return mod, pl.BlockSpec((six, bb, d), lambda i, j: (0, i, 0))


def _mod_kernel(c_ref, w_ref, b_ref, o_ref):
    o_ref[0, 0] = _wdot(_silu(c_ref[...]), w_ref[0].astype(BF16)) + b_ref[0]


def _mod_call(c_all, w_ada, b_ada):
    depth, d, cols = w_ada.shape
    rows = c_all.shape[0]
    cb = D_MODEL
    return pl.pallas_call(
        _mod_kernel,
        grid=(depth, cols // cb),
        in_specs=[pl.BlockSpec((rows, d), lambda l, c: (0, 0)),
                  pl.BlockSpec((1, d, cb), lambda l, c: (l, 0, c)),
                  pl.BlockSpec((1, 1, cb), lambda l, c: (l, 0, c))],
        out_specs=pl.BlockSpec((1, 1, rows, cb), lambda l, c: (l, c, 0, 0)),
        out_shape=jax.ShapeDtypeStruct((depth, cols // cb, rows, cb), F32),
        name="adaln_mod",
    )(c_all, w_ada, b_ada.reshape(depth, 1, cols))


def _tri_inverse_minus_eye(lowers, n):
    i = lax.broadcasted_iota(jnp.int32, (n, n), 0)
    j = lax.broadcasted_iota(jnp.int32, (n, n), 1)

    def same_block(size):
        sh = int(math.log2(size))
        return jnp.right_shift(i, sh) == jnp.right_shift(j, sh)

    base = min(n, INV_BASE)
    if base < n:
        diag_mask = same_block(base)
        xs = [jnp.where(diag_mask, low, 0.0) for low in lowers]
    else:
        xs = list(lowers)
    powers, p = xs, 1
    while 2 * p < base:
        powers = [_bdot(pw, pw) for pw in powers]
        prods = [_bdot(x, pw) for x, pw in zip(xs, powers)]
        xs = [x + pw + pr for x, pw, pr in zip(xs, powers, prods)]
        p *= 2
    size = base
    while size < n:
        off_mask = same_block(2 * size) & jnp.logical_not(same_block(size))
        offs = [jnp.where(off_mask, low, 0.0) for low in lowers]
        lefts = [off + _bdot(x, off) for x, off in zip(xs, offs)]
        xs = [x + left + _bdot(left, x) for x, left in zip(xs, lefts)]
        size *= 2
    return xs


def _mix_kernel(*refs, bb, tc, pos0, zero_init, n_alias):
    refs = list(refs)
    x_ref, mod_ref, g_ref, win_ref, cos_ref, sin_ref = refs[:6]
    del refs[:6]
    if not zero_init:
        pool0_ref, ret0_ref, sc0_ref, sh0_ref, wkv0_ref = refs[:5]
        del refs[:5]
    poolw_ref, vecs_ref, mu_ref, wlora_ref, alora_ref, glora_ref, wbr_ref, wout_ref = refs[:8]
    del refs[:8 + n_alias]
    x1_ref, pool_ref, ret_ref, sc_ref, sh_ref, wkv_ref = refs[:6]
    h_s, z_s, y_s, dm_s, pool_ext, sc_ext, sh_ext, ywkv, wsum_s, conv_s, prev_s = refs[6:]
    j = pl.program_id(1)
    seqs = range(bb)
    heads = range(HEADS)
    hsl = [slice(h * HEAD_D, (h + 1) * HEAD_D) for h in heads]

    def rows_of(lb):
        return slice(lb * tc, (lb + 1) * tc)

    lane = lax.broadcasted_iota(jnp.int32, (1, MIX_W), 1)
    row_i = lax.broadcasted_iota(jnp.int32, (tc, 1), 0)
    col_i = lax.broadcasted_iota(jnp.int32, (1, tc), 1)
    row_f = row_i.astype(F32)
    lgs = [math.log1p(-(2.0 ** (-5.0 - h))) for h in heads]

    @pl.when(j == 0)
    def _init():
        diff = (row_i - col_i).astype(F32)
        for h in heads:
            dm_s[h] = jnp.where(diff >= 0, jnp.exp(lgs[h] * jnp.maximum(diff, 0.0)), 0.0)
        pool_ext[...] = jnp.zeros(pool_ext.shape, F32)
        sc_ext[...] = jnp.zeros(sc_ext.shape, F32)
        sh_ext[...] = jnp.zeros(sh_ext.shape, F32)
        if zero_init:
            ret_ref[...] = jnp.zeros(ret_ref.shape, F32)
            wkv_ref[...] = jnp.zeros(wkv_ref.shape, F32)
        else:
            pool_ext[:, POOL_HIST - POOL_PAST:POOL_HIST] = pool0_ref[...]
            sc_ext[:, HIST - 2:HIST] = sc0_ref[...]
            sh_ext[:, HIST - 1:HIST] = sh0_ref[...]
            ret_ref[...] = ret0_ref[...]
            wkv_ref[...] = wkv0_ref[...]

    def vec(row):
        return vecs_ref[row:row + 1, :]

    x = x_ref[...]
    hn = _rmsnorm(x, g_ref[...])
    for lb in seqs:
        h_s[rows_of(lb)] = hn[rows_of(lb)] * (1.0 + _mod_row(mod_ref, 1, lb)) + _mod_row(mod_ref, 0, lb)
    z_s[...] = _wdot(h_s[...], win_ref[...])

    def gated_proj(n):
        proj = _wdot(y_s[:, n * MIX_W:(n + 1) * MIX_W], wbr_ref[n])
        return _sigmoid(z_s[:, COL_GATE + n * D_MODEL:COL_GATE + (n + 1) * D_MODEL]) * proj

    rows_all = bb * tc
    grp = jnp.right_shift(lane, 6)
    win = jnp.where(grp == 0, 2, jnp.where(grp == 1, 4, jnp.where(grp == 2, 8, 16)))
    brow = lax.broadcasted_iota(jnp.int32, (rows_all, 1), 0)
    cnt = jnp.minimum(win, pos0 + j * tc + jnp.bitwise_and(brow, tc - 1) + 1).astype(F32)
    first_half = jnp.bitwise_and(lane, HEAD_D - 1) < HEAD_D // 2
    cos = cos_ref[...]
    sin = sin_ref[...]

    def rope(t):
        swapped = jnp.where(first_half, pltpu.roll(t, MIX_W - HEAD_D // 2, 1), pltpu.roll(t, HEAD_D // 2, 1))
        return t * cos + swapped * sin

    si = lax.broadcasted_iota(jnp.int32, (MIX_W, MIX_W), 0)
    sj = lax.broadcasted_iota(jnp.int32, (MIX_W, MIX_W), 1)
    head_ones = (jnp.right_shift(si, 6) == jnp.right_shift(sj, 6)).astype(BF16)

    def split_dot(ones, t):
        hi = t.astype(BF16)
        lo = (t - hi.astype(F32)).astype(BF16)
        return jnp.dot(ones, hi, preferred_element_type=F32) + jnp.dot(ones, lo, preferred_element_type=F32)

    def head_sum(t):
        hi = t.astype(BF16)
        lo = (t - hi.astype(F32)).astype(BF16)
        return (jnp.dot(hi, head_ones, preferred_element_type=F32)
                + jnp.dot(lo, head_ones, preferred_element_type=F32))

    c = min(RWKV_CHUNK, tc)
    nsub = tc // c
    merged = c % SUBLANES == 0
    ci = lax.broadcasted_iota(jnp.int32, (c, c), 0)
    cj = lax.broadcasted_iota(jnp.int32, (c, c), 1)
    strict = ci > cj
    incl = ci >= cj
    ti = lax.broadcasted_iota(jnp.int32, (rows_all, rows_all), 0)
    tj = lax.broadcasted_iota(jnp.int32, (rows_all, rows_all), 1)
    shift_c = int(math.log2(c))
    tri = ((ti >= tj) & (jnp.right_shift(ti, shift_c) == jnp.right_shift(tj, shift_c))).astype(BF16)

    for lb in seqs:
        rows = rows_of(lb)
        pool_ext[lb, POOL_HIST:POOL_HIST + tc] = z_s[rows, 0:MIX_W]
        e = pool_ext[lb]
        s2 = e + pltpu.roll(e, 1, 0)
        s4 = s2 + pltpu.roll(s2, 2, 0)
        s8 = s4 + pltpu.roll(s4, 4, 0)
        s16 = s8 + pltpu.roll(s8, 8, 0)
        wsum = jnp.where(grp == 0, s2, jnp.where(grp == 1, s4, jnp.where(grp == 2, s8, s16)))
        wsum_s[rows] = wsum[POOL_HIST:POOL_HIST + tc]
        new_pool = pool_ext[lb, tc + POOL_HIST - POOL_PAST:tc + POOL_HIST]
        pool_ext[lb, POOL_HIST - POOL_PAST:POOL_HIST] = new_pool
        pool_ref[lb] = new_pool

        sc_ext[lb, HIST:HIST + tc] = (z_s[rows, COL_SC + 2 * MIX_W:COL_SC + 3 * MIX_W]
                                      * z_s[rows, COL_SC:COL_SC + MIX_W])
        e = sc_ext[lb]
        conv = vec(V_SC2) * e + vec(V_SC1) * pltpu.roll(e, 1, 0) + vec(V_SC0) * pltpu.roll(e, 2, 0)
        conv_s[rows] = conv[HIST:HIST + tc]
        new_sc = sc_ext[lb, HIST + tc - 2:HIST + tc]
        sc_ext[lb, HIST - 2:HIST] = new_sc
        sc_ref[lb] = new_sc

        sh_ext[lb, HIST:HIST + tc] = z_s[rows, COL_RWKV:COL_GATE]
        prev_s[rows] = pltpu.roll(sh_ext[lb], 1, 0)[HIST:HIST + tc]
        new_sh = sh_ext[lb, HIST + tc - 1:HIST + tc]
        sh_ext[lb, HIST - 1:HIST] = new_sh
        sh_ref[lb] = new_sh

    u = z_s[:, 0:MIX_W]
    y_s[:, 0:MIX_W] = _wdot(wsum_s[...] / cnt - u, poolw_ref[...]) * vec(V_POOL_SCALE)
    y_s[:, 2 * MIX_W:3 * MIX_W] = z_s[:, COL_SC + MIX_W:COL_SC + 2 * MIX_W] * conv_s[...]

    q_all = rope(z_s[:, COL_RET:COL_RET + MIX_W])
    k_all = rope(z_s[:, COL_RET + MIX_W:COL_RET + 2 * MIX_W]) * (HEAD_D ** -0.5)
    v_all = z_s[:, COL_RET + 2 * MIX_W:COL_RET + 3 * MIX_W]

    zz = z_s[:, COL_RWKV:COL_GATE]
    zs = zz + (prev_s[...] - zz) * mu_ref[...]
    r = zs[:, 0:MIX_W]
    kx = zs[:, MIX_W:2 * MIX_W]
    vx = zs[:, 2 * MIX_W:3 * MIX_W]
    o3 = 3 * MIX_W
    wl = zs[:, o3:o3 + LORA_W]
    al = zs[:, o3 + LORA_W:o3 + LORA_W + LORA_A]
    gl = zs[:, o3 + LORA_W + LORA_A:]
    wlog = -_softplus(-(vec(V_W0) + _bdot(jnp.tanh(wl), wlora_ref[...]))) - 0.5
    logw = -jnp.exp(wlog)
    asig = _sigmoid(vec(V_A0) + _bdot(al, alora_ref[...]))
    g_out = _bdot(_sigmoid(gl), glora_ref[...])
    k2 = kx * (1.0 + (asig - 1.0) * vec(V_KA))
    kk_raw = kx * vec(V_KK)
    kk = kk_raw * lax.rsqrt(jnp.maximum(head_sum(kk_raw * kk_raw), L2_EPS * L2_EPS))
    cum = split_dot(tri, logw)
    p_in = jnp.exp(cum)
    p_inv = jnp.exp(-cum)
    rw = dict(a=-kk * jnp.exp(cum - logw), b=kk * asig * p_inv, k=k2 * p_inv, r=r * p_in, v=vx, p_in=p_in)

    pairs = [(lb, h) for lb in seqs for h in heads]
    qs = [q_all[rows_of(lb), hsl[h]] for lb, h in pairs]
    ks = [k_all[rows_of(lb), hsl[h]] for lb, h in pairs]
    vs = [v_all[rows_of(lb), hsl[h]] for lb, h in pairs]
    states = [ret_ref[lb, h] for lb, h in pairs]
    np_ = range(len(pairs))
    scores = [_bdot_nt(qs[i], ks[i]) for i in np_]
    cross = [_bdot(qs[i] * jnp.exp(lgs[pairs[i][1]] * (row_f + 1.0)), states[i]) for i in np_]
    kv = [_bdot_tn(ks[i] * jnp.exp(lgs[pairs[i][1]] * (tc - 1.0 - row_f)), vs[i]) for i in np_]
    outs = [_bdot(scores[i] * dm_s[pairs[i][1]], vs[i]) + cross[i] for i in np_]
    for i, (lb, h) in enumerate(pairs):
        ret_ref[lb, h] = states[i] * math.exp(lgs[h] * tc) + kv[i]
        y_s[rows_of(lb), MIX_W + h * HEAD_D:MIX_W + (h + 1) * HEAD_D] = outs[i]
    o_all = y_s[:, MIX_W:2 * MIX_W]
    o_all = o_all * lax.rsqrt(head_sum(o_all * o_all) * (1.0 / HEAD_D) + GN_EPS)
    y_s[:, MIX_W:2 * MIX_W] = o_all * _silu(z_s[:, COL_RET + 3 * MIX_W:COL_RET + 4 * MIX_W])

    blocks = [(lb, sub, h) for lb in seqs for sub in range(nsub) for h in heads]

    def blk(name):
        return [rw[name][lb * tc + sub * c:lb * tc + (sub + 1) * c, hsl[h]] for lb, sub, h in blocks]

    a_l, b_l, k_l, r_l, v_l = blk("a"), blk("b"), blk("k"), blk("r"), blk("v")
    nb = range(len(blocks))
    if merged:
        bk_l = [jnp.concatenate([b_l[i], k_l[i]], axis=0) for i in nb]
        gram = [_bdot_nt(jnp.concatenate([a_l[i], r_l[i]], axis=0), bk_l[i]) for i in nb]
        g_ab, g_ak = [g[:c, :c] for g in gram], [g[:c, c:] for g in gram]
        g_rb, g_rk = [g[c:, :c] for g in gram], [g[c:, c:] for g in gram]
    else:
        g_ab = [_bdot_nt(a_l[i], b_l[i]) for i in nb]
        g_ak = [_bdot_nt(a_l[i], k_l[i]) for i in nb]
        g_rb = [_bdot_nt(r_l[i], b_l[i]) for i in nb]
        g_rk = [_bdot_nt(r_l[i], k_l[i]) for i in nb]
    l_ab = [jnp.where(strict, g, 0.0) for g in g_ab]
    l_ak = [jnp.where(strict, g, 0.0) for g in g_ak]
    m_rb = [jnp.where(incl, g, 0.0) for g in g_rb]
    m_rk = [jnp.where(incl, g, 0.0) for g in g_rk]
    n_inv = _tri_inverse_minus_eye(l_ab, c)
    if merged:
        lv = [_bdot(jnp.concatenate([l_ak[i], m_rk[i]], axis=0), v_l[i]) for i in nb]
        lakv, mv = [t[:c] for t in lv], [t[c:] for t in lv]
    else:
        lakv = [_bdot(l_ak[i], v_l[i]) for i in nb]
        mv = [_bdot(m_rk[i], v_l[i]) for i in nb]
    nx = [_bdot(n_inv[i], jnp.concatenate([lakv[i], a_l[i]], axis=1)) for i in nb]
    tlv = [lakv[i] + nx[i][:, :HEAD_D] for i in nb]
    ta = [a_l[i] + nx[i][:, HEAD_D:] for i in nb]

    wkv_states = [wkv_ref[lb, h] for lb, h in pairs]
    for sub in range(nsub):
        ids = [(lb * nsub + sub) * HEADS + h for lb, h in pairs]
        p_end = [rw["p_in"][lb * tc + (sub + 1) * c - 1:lb * tc + (sub + 1) * c, hsl[h]] for lb, h in pairs]
        u_mat = [_bdot_nt(ta[i], wkv_states[n]) + tlv[i] for n, i in enumerate(ids)]
        y_st = [_bdot_nt(r_l[i], wkv_states[n]) for n, i in enumerate(ids)]
        if merged:
            upd = [_bdot_tn(jnp.concatenate([u_mat[n], v_l[i]], axis=0), bk_l[i] * p_end[n])
                   for n, i in enumerate(ids)]
        else:
            upd = [_bdot_tn(u_mat[n], b_l[i] * p_end[n]) + _bdot_tn(v_l[i], k_l[i] * p_end[n])
                   for n, i in enumerate(ids)]
        y_u = [_bdot(m_rb[i], u_mat[n]) for n, i in enumerate(ids)]
        for n, i in enumerate(ids):
            lb, h = pairs[n]
            ywkv[lb * tc + sub * c:lb * tc + (sub + 1) * c, hsl[h]] = y_st[n] + y_u[n] + mv[i]
        wkv_states = [wkv_states[n] * p_end[n] + upd[n] for n in np_]
    for n, (lb, h) in enumerate(pairs):
        wkv_ref[lb, h] = wkv_states[n]

    y_all = ywkv[...]
    cen = y_all - head_sum(y_all) * (1.0 / HEAD_D)
    var = head_sum(cen * cen) * (1.0 / HEAD_D)
    yn = cen * lax.rsqrt(var + RWKV_LN_EPS) * vec(V_LNG) + vec(V_LNB)
    bonus = head_sum(r * k2 * vec(V_RK)) * vx
    y_s[:, 3 * MIX_W:4 * MIX_W] = (yn + bonus) * g_out

    out = _wdot(gated_proj(0) + gated_proj(1) + gated_proj(2) + gated_proj(3), wout_ref[...])
    for lb in seqs:
        x1_ref[rows_of(lb)] = x[rows_of(lb)] + _mod_row(mod_ref, 2, lb) * out[rows_of(lb)]


def _layer_block_spec(arr, bb, layer):
    shape = arr.shape[2:]
    zeros = (0,) * len(shape)
    return pl.BlockSpec((None, bb) + shape, lambda i, j: (layer, i) + zeros)


def _resident_spec(arr, layer=None):
    if layer is None:
        zeros = (0,) * arr.ndim
        return pl.BlockSpec(arr.shape, lambda i, j: zeros, pipeline_mode=pl.Buffered(1))
    zeros = (0,) * (arr.ndim - 1)
    return pl.BlockSpec((None,) + arr.shape[1:], lambda i, j: (layer,) + zeros, pipeline_mode=pl.Buffered(1))


def _rows_spec(bb, tc, nt, cols):
    return pl.BlockSpec((bb * tc, cols), lambda i, j: (i * nt + j, 0))


def _params():
    return pltpu.CompilerParams(dimension_semantics=("parallel", "arbitrary"), vmem_limit_bytes=VMEM_LIMIT)


def _blocking(batch, t):
    if t >= 256:
        return 1, 256
    return min(batch, 64 // t), t


def _passthrough(prev):
    if prev is None:
        return [], []
    return list(prev), [pl.BlockSpec(memory_space=pl.ANY) for _ in prev]


def _mix_call(x2d, mod, g1, w_in, cos_t, sin_t, states, poolw, vecs, mu, wlora, alora, glora, w_br, w_out,
              batch, t, pos0, layer, depth, prev):
    bb, tc = _blocking(batch, t)
    nt = t // tc
    tp = _round_up(tc, SUBLANES)
    d = x2d.shape[1]
    state_shapes = [(depth, batch, POOL_PAST, MIX_W), (depth, batch, HEADS, HEAD_D, HEAD_D), (depth, batch, 2, MIX_W),
                    (depth, batch, 1, d), (depth, batch, HEADS, HEAD_D, HEAD_D)]
    out_structs = [jax.ShapeDtypeStruct(sh, F32) for sh in state_shapes]
    state_specs = [_layer_block_spec(st, bb, layer) for st in out_structs]
    state_ops = [] if states is None else [st.reshape(sh) for st, sh in zip(states, state_shapes)]
    weights = (poolw, vecs, mu, wlora, alora, glora, w_br, w_out)
    mod, mod_spec = _mod_operand(mod, bb)
    prev_ops, prev_specs = _passthrough(prev)
    n_in = 6 + len(state_ops) + len(weights)
    outs = pl.pallas_call(
        functools.partial(_mix_kernel, bb=bb, tc=tc, pos0=pos0, zero_init=states is None, n_alias=len(prev_ops)),
        grid=(batch // bb, nt),
        in_specs=[_rows_spec(bb, tc, nt, d), mod_spec,
                  _resident_spec(g1, layer), _resident_spec(w_in, layer),
                  pl.BlockSpec((bb * tc, MIX_W), lambda i, j: (j, 0)),
                  pl.BlockSpec((bb * tc, MIX_W), lambda i, j: (j, 0))]
        + state_specs[:len(state_ops)] + [_resident_spec(a, layer) for a in weights] + prev_specs,
        out_specs=[_rows_spec(bb, tc, nt, d)] + state_specs,
        out_shape=[jax.ShapeDtypeStruct(x2d.shape, F32)] + out_structs,
        input_output_aliases={n_in + k: 1 + k for k in range(len(prev_ops))},
        scratch_shapes=[pltpu.VMEM((bb * tc, d), F32),
                        pltpu.VMEM((bb * tc, IN_COLS), F32),
                        pltpu.VMEM((bb * tc, d), F32),
                        pltpu.VMEM((HEADS, tc, tc), F32),
                        pltpu.VMEM((bb, POOL_HIST + tp, MIX_W), F32),
                        pltpu.VMEM((bb, HIST + tp, MIX_W), F32),
                        pltpu.VMEM((bb, HIST + tp, d), F32),
                        pltpu.VMEM((bb * tc, MIX_W), F32),
                        pltpu.VMEM((bb * tc, MIX_W), F32),
                        pltpu.VMEM((bb * tc, MIX_W), F32),
                        pltpu.VMEM((bb * tc, COL_GATE - COL_RWKV), F32)],
        compiler_params=_params(),
        name="mix",
    )(x2d, mod, g1, w_in, jnp.tile(cos_t, (bb, 1)), jnp.tile(sin_t, (bb, 1)), *state_ops, *weights, *prev_ops)
    return outs[0], tuple(outs[1:])


def _ffn_kernel(*refs, bb, tc, final, zero_init, n_alias):
    refs = list(refs)
    x1_ref, mod_ref, g_ref, wup_ref, ffnw_ref, wdown_ref, fg_ref = refs[:7]
    del refs[:7]
    if not zero_init:
        st0_ref = refs.pop(0)
    del refs[:n_alias]
    o_ref, st_ref, h_s, ext, act_s = refs
    j = pl.program_id(1)
    seqs = range(bb)

    def rows_of(lb):
        return slice(lb * tc, (lb + 1) * tc)

    @pl.when(j == 0)
    def _init():
        ext[...] = jnp.zeros(ext.shape, F32)
        if not zero_init:
            ext[:, HIST - 2:HIST] = st0_ref[...]

    x1 = x1_ref[...]
    hn = _rmsnorm(x1, g_ref[...])
    for lb in seqs:
        h_s[rows_of(lb)] = hn[rows_of(lb)] * (1.0 + _mod_row(mod_ref, 4, lb)) + _mod_row(mod_ref, 3, lb)
    if bb == 1:
        ext[0, HIST:HIST + tc] = _wdot(h_s[...], wup_ref[...])
    else:
        up = _wdot(h_s[...], wup_ref[...])
        for lb in seqs:
            ext[lb, HIST:HIST + tc] = up[rows_of(lb)]

    for lb in seqs:
        def conv_cols(lo, width):
            e = ext[lb, :, lo:lo + width]
            w = ffnw_ref[:, lo:lo + width]
            y = w[2:3] * e + w[1:2] * pltpu.roll(e, 1, 0) + w[0:1] * pltpu.roll(e, 2, 0)
            return y[HIST:HIST + tc]

        for lo in range(0, D_FF, CONV_COLS):
            width = min(CONV_COLS, D_FF - lo)
            act_s[rows_of(lb), lo:lo + width] = _silu(conv_cols(lo, width)) * conv_cols(D_FF + lo, width)
        new_st = ext[lb, HIST + tc - 2:HIST + tc]
        ext[lb, HIST - 2:HIST] = new_st
        st_ref[lb] = new_st

    dn = _wdot(act_s[...], wdown_ref[...])
    for lb in seqs:
        o_ref[rows_of(lb)] = x1[rows_of(lb)] + _mod_row(mod_ref, 5, lb) * dn[rows_of(lb)]
    if final:
        o_ref[...] = _rmsnorm(o_ref[...], fg_ref[...])


def _ffn_call(x1, mod, g2, w_up, ffn_w, w_down, final_g, state, batch, t, final, layer, depth, prev):
    bb, tc = _blocking(batch, t)
    nt = t // tc
    tp = _round_up(tc, SUBLANES)
    d = x1.shape[1]
    st_struct = jax.ShapeDtypeStruct((depth, batch, 2, 2 * D_FF), F32)
    st_spec = _layer_block_spec(st_struct, bb, layer)
    state_ops = [] if state is None else [state]
    mod, mod_spec = _mod_operand(mod, bb)
    prev_ops, prev_specs = _passthrough(prev)
    n_in = 7 + len(state_ops)
    return pl.pallas_call(
        functools.partial(_ffn_kernel, bb=bb, tc=tc, final=final, zero_init=state is None, n_alias=len(prev_ops)),
        grid=(batch // bb, nt),
        in_specs=[_rows_spec(bb, tc, nt, d), mod_spec,
                  _resident_spec(g2, layer), _resident_spec(w_up, layer), _resident_spec(ffn_w, layer),
                  _resident_spec(w_down, layer), _resident_spec(final_g)]
        + [st_spec] * len(state_ops) + prev_specs,
        out_specs=[_rows_spec(bb, tc, nt, d), st_spec],
        out_shape=[jax.ShapeDtypeStruct(x1.shape, F32), st_struct],
        input_output_aliases={n_in + k: 1 + k for k in range(len(prev_ops))},
        scratch_shapes=[pltpu.VMEM((bb * tc, d), F32),
                        pltpu.VMEM((bb, HIST + tp, 2 * D_FF), F32),
                        pltpu.VMEM((bb * tc, D_FF), F32)],
        compiler_params=_params(),
        name="ffn",
    )(x1, mod, g2, w_up, ffn_w, w_down, final_g, *state_ops, *prev_ops)


def _rope_tables(t, pos0):
    half = HEAD_D // 2
    inv = ROPE_BASE ** (-jnp.arange(half, dtype=F32) / half)
    pos = pos0 + jnp.arange(t, dtype=jnp.int32)
    ang = pos.astype(F32)[:, None] * inv[None, :]
    cos, sin = jnp.cos(ang), jnp.sin(ang)
    cos_t = jnp.tile(jnp.concatenate([cos, cos], axis=-1), (1, HEADS))
    sin_t = jnp.tile(jnp.concatenate([-sin, sin], axis=-1), (1, HEADS))
    return cos_t, sin_t


def _block_diag(pool_w):
    groups, gw, _ = pool_w.shape
    out = jnp.zeros((groups * gw, groups * gw), pool_w.dtype)
    for g in range(groups):
        out = out.at[g * gw:(g + 1) * gw, g * gw:(g + 1) * gw].set(pool_w[g])
    return out


def kernel(x_prompt, x_sample, c_prompt, c_sample, state_pool, state_ret, state_sconv, state_shift, state_wkv, state_ffn, w_ada, b_ada, norm1_g, norm2_g, w_in, pool_w, pool_scale, sc_w, rw_mu, rw_w0, rw_w_lora, rw_a0, rw_a_lora, rw_g_lora, rw_k_k, rw_k_a, rw_r_k, rw_ln_g, rw_ln_b, w_br, w_out, w_up, ffn_w, w_down, final_g):
    depth = w_in.shape[0]
    bp, tp_, d = x_prompt.shape
    bs, ts, _ = x_sample.shape

    mods = _mod_call(jnp.concatenate([c_prompt, c_sample], axis=0), w_ada, b_ada)
    cos_p, sin_p = _rope_tables(tp_, 0)
    cos_s, sin_s = _rope_tables(ts, PAST_LEN)
    final_g2 = final_g.reshape(1, d)

    xp = x_prompt.reshape(bp * tp_, d)
    xs = x_sample.reshape(bs * ts, d)
    sample_states = (state_pool, state_ret, state_sconv, state_shift, state_wkv)
    vec_rows = [pool_scale, sc_w[:, 0], sc_w[:, 1], sc_w[:, 2], rw_w0, rw_a0, rw_k_k, rw_k_a,
                rw_r_k.reshape(depth, MIX_W), rw_ln_g, rw_ln_b]
    vecs = jnp.concatenate([jnp.stack(vec_rows, axis=1),
                            jnp.zeros((depth, N_VECS - len(vec_rows), MIX_W), F32)], axis=1)
    mix_w = (norm1_g.reshape(depth, 1, d), w_in.astype(BF16))
    seq_w = (jnp.stack([_block_diag(pool_w[l]) for l in range(depth)]).astype(BF16), vecs,
             rw_mu.reshape(depth, 1, d), rw_w_lora, rw_a_lora, rw_g_lora, w_br.astype(BF16), w_out.astype(BF16))
    ffn_ws = (norm2_g.reshape(depth, 1, d), w_up.astype(BF16), ffn_w, w_down.astype(BF16), final_g2)
    st_p = st_s = ffn_p = ffn_s = None
    for l in range(depth):
        mod_p, mod_s = mods[l, :, :bp], mods[l, :, bp:]
        final = l == depth - 1

        x1, st_p = _mix_call(xp, mod_p, *mix_w, cos_p, sin_p, None, *seq_w, batch=bp, t=tp_, pos0=0,
                             layer=l, depth=depth, prev=st_p)
        xp, ffn_p = _ffn_call(x1, mod_p, *ffn_ws, None, batch=bp, t=tp_, final=final, layer=l, depth=depth,
                              prev=None if ffn_p is None else (ffn_p,))
        x1, st_s = _mix_call(xs, mod_s, *mix_w, cos_s, sin_s, sample_states, *seq_w, batch=bs, t=ts,
                             pos0=PAST_LEN, layer=l, depth=depth, prev=st_s)
        xs, ffn_s = _ffn_call(x1, mod_s, *ffn_ws, state_ffn, batch=bs, t=ts, final=final, layer=l, depth=depth,
                              prev=None if ffn_s is None else (ffn_s,))

    def finish(st, ffn_st, batch):
        pool, ret, sc, sh, wkv = st
        return (pool, ret, sc, sh.reshape(depth, batch, d), wkv, ffn_st)

    return (xp.reshape(bp, tp_, d), xs.reshape(bs, ts, d), *finish(st_p, ffn_p, bp), *finish(st_s, ffn_s, bs))
```

```python
import functools
import math

import jax
import jax.numpy as jnp
from jax import lax
from jax.experimental import pallas as pl
from jax.experimental.pallas import tpu as pltpu

F32 = jnp.float32
BF16 = jnp.bfloat16

D_MODEL = 1024
MIX_W = 256
HEADS = 4
HEAD_D = 64
POOL_PAST = 15
ROPE_BASE = 10000.0
LORA_W = 64
LORA_A = 64
LORA_G = 128
D_FF = 2816
PAST_LEN = 16384
NORM_EPS = 1e-6
GN_EPS = 1e-6
RWKV_LN_EPS = 64e-5
L2_EPS = 1e-12

COL_RET = MIX_W
COL_SC = COL_RET + 4 * MIX_W
COL_RWKV = COL_SC + 3 * MIX_W
COL_GATE = COL_RWKV + 3 * MIX_W + LORA_W + LORA_A + LORA_G
IN_COLS = COL_GATE + 4 * D_MODEL

SUBLANES = 8
VMEM_LIMIT = 56 * 1024 * 1024
RWKV_CHUNK = 64
INV_BASE = 16
POOL_HIST = 16
HIST = SUBLANES
CONV_COLS = 512
MIX_SPLIT = 4

(V_POOL_SCALE, V_SC0, V_SC1, V_SC2, V_W0, V_A0, V_KK, V_KA, V_RK, V_LNG, V_LNB) = range(11)
N_VECS = 16


def _bdot(a, b):
    return jnp.dot(a.astype(BF16), b.astype(BF16), preferred_element_type=F32)


def _bdot_nt(a, b):
    return lax.dot_general(a.astype(BF16), b.astype(BF16), (((1,), (1,)), ((), ())), preferred_element_type=F32)


def _bdot_tn(a, b):
    return lax.dot_general(a.astype(BF16), b.astype(BF16), (((0,), (0,)), ((), ())), preferred_element_type=F32)


def _wdot(a, w_bf16):
    return jnp.dot(a.astype(BF16), w_bf16, preferred_element_type=F32)


def _sigmoid(x):
    return 1.0 / (1.0 + jnp.exp(-x))


def _silu(x):
    half = 0.5 * x
    return half + half * jnp.tanh(half)


def _softplus(x):
    return jnp.maximum(x, 0.0) + jnp.log(1.0 + jnp.exp(-jnp.abs(x)))


def _rmsnorm(x, g):
    return x * lax.rsqrt(jnp.mean(x * x, axis=-1, keepdims=True) + NORM_EPS) * g


def _round_up(n, m):
    return (n + m - 1) // m * m


def _mod_row(mod_ref, k, lb):
    if len(mod_ref.shape) == 4:
        return mod_ref[k, lb]
    return mod_ref[k, lb:lb + 1, :]


def _mod_operand(mod, bb):
    six, _, d = mod.shape
    if bb % SUBLANES:
        return mod[:, :, None, :], pl.BlockSpec((six, bb, 1, d), lambda i, j: (0, i, 0, 0))
    return mod, pl.BlockSpec((six, bb, d), lambda i, j: (0, i, 0))


def _read_rows(ref):
    val = ref[...]
    return val.reshape(-1, val.shape[-1]) if val.ndim == 3 else val


def _write_seq_rows(ref, lb, tc, val):
    if len(ref.shape) == 3:
        ref[lb] = val
    else:
        ref[lb * tc:(lb + 1) * tc] = val


def _mod_kernel(c_ref, w_ref, b_ref, o_ref):
    o_ref[0, 0] = _wdot(_silu(c_ref[...]), w_ref[0].astype(BF16)) + b_ref[0]


def _mod_call(c_all, w_ada, b_ada):
    depth, d, cols = w_ada.shape
    rows = c_all.shape[0]
    cb = D_MODEL
    return pl.pallas_call(
        _mod_kernel,
        grid=(depth, cols // cb),
        in_specs=[pl.BlockSpec((rows, d), lambda l, c: (0, 0)),
                  pl.BlockSpec((1, d, cb), lambda l, c: (l, 0, c)),
                  pl.BlockSpec((1, 1, cb), lambda l, c: (l, 0, c))],
        out_specs=pl.BlockSpec((1, 1, rows, cb), lambda l, c: (l, c, 0, 0)),
        out_shape=jax.ShapeDtypeStruct((depth, cols // cb, rows, cb), F32),
        name="adaln_mod",
    )(c_all, w_ada, b_ada.reshape(depth, 1, cols))


def _tri_inverse_minus_eye(lowers, n):
    i = lax.broadcasted_iota(jnp.int32, (n, n), 0)
    j = lax.broadcasted_iota(jnp.int32, (n, n), 1)

    def same_block(size):
        sh = int(math.log2(size))
        return jnp.right_shift(i, sh) == jnp.right_shift(j, sh)

    base = min(n, INV_BASE)
    if base < n:
        diag_mask = same_block(base)
        xs = [jnp.where(diag_mask, low, 0.0) for low in lowers]
    else:
        xs = list(lowers)
    powers, p = xs, 1
    while 2 * p < base:
        powers = [_bdot(pw, pw) for pw in powers]
        prods = [_bdot(x, pw) for x, pw in zip(xs, powers)]
        xs = [x + pw + pr for x, pw, pr in zip(xs, powers, prods)]
        p *= 2
    size = base
    while size < n:
        off_mask = same_block(2 * size) & jnp.logical_not(same_block(size))
        offs = [jnp.where(off_mask, low, 0.0) for low in lowers]
        lefts = [off + _bdot(x, off) for x, off in zip(xs, offs)]
        xs = [x + left + _bdot(left, x) for x, left in zip(xs, lefts)]
        size *= 2
    return xs


def _mix_kernel(*refs, bb, tc, pos0, zero_init, n_alias):
    refs = list(refs)
    x_ref, mod_ref, g_ref, win_ref, cos_ref, sin_ref = refs[:6]
    del refs[:6]
    if not zero_init:
        pool0_ref, ret0_ref, sc0_ref, sh0_ref, wkv0_ref = refs[:5]
        del refs[:5]
    poolw_ref, vecs_ref, mu_ref, wlora_ref, alora_ref, glora_ref, wbr_ref, wout_ref = refs[:8]
    del refs[:8 + n_alias]
    x1_ref, pool_ref, ret_ref, sc_ref, sh_ref, wkv_ref = refs[:6]
    h_s, z_s, y_s, dm_s, pool_ext, sc_ext, sh_ext, ywkv, wsum_s, conv_s, prev_s = refs[6:]
    j = pl.program_id(1)
    seqs = range(bb)
    heads = range(HEADS)
    hsl = [slice(h * HEAD_D, (h + 1) * HEAD_D) for h in heads]

    def rows_of(lb):
        return slice(lb * tc, (lb + 1) * tc)

    lane = lax.broadcasted_iota(jnp.int32, (1, MIX_W), 1)
    row_i = lax.broadcasted_iota(jnp.int32, (tc, 1), 0)
    col_i = lax.broadcasted_iota(jnp.int32, (1, tc), 1)
    row_f = row_i.astype(F32)
    lgs = [math.log1p(-(2.0 ** (-5.0 - h))) for h in heads]

    @pl.when(j == 0)
    def _init():
        diff = (row_i - col_i).astype(F32)
        for h in heads:
            dm_s[h] = jnp.where(diff >= 0, jnp.exp(lgs[h] * jnp.maximum(diff, 0.0)), 0.0)
        pool_ext[...] = jnp.zeros(pool_ext.shape, F32)
        sc_ext[...] = jnp.zeros(sc_ext.shape, F32)
        sh_ext[...] = jnp.zeros(sh_ext.shape, F32)
        if zero_init:
            ret_ref[...] = jnp.zeros(ret_ref.shape, F32)
            wkv_ref[...] = jnp.zeros(wkv_ref.shape, F32)
        else:
            pool_ext[:, POOL_HIST - POOL_PAST:POOL_HIST] = pool0_ref[...]
            sc_ext[:, HIST - 2:HIST] = sc0_ref[...]
            sh_ext[:, HIST - 1:HIST] = sh0_ref[...]
            ret_ref[...] = ret0_ref[...]
            wkv_ref[...] = wkv0_ref[...]

    def vec(row):
        return vecs_ref[row:row + 1, :]

    x = _read_rows(x_ref)
    hn = _rmsnorm(x, g_ref[...])
    for lb in seqs:
        h_s[rows_of(lb)] = hn[rows_of(lb)] * (1.0 + _mod_row(mod_ref, 1, lb)) + _mod_row(mod_ref, 0, lb)
    z_s[...] = _wdot(h_s[...], win_ref[...])

    def gated_proj(n):
        proj = _wdot(y_s[:, n * MIX_W:(n + 1) * MIX_W], wbr_ref[n])
        return _sigmoid(z_s[:, COL_GATE + n * D_MODEL:COL_GATE + (n + 1) * D_MODEL]) * proj

    rows_all = bb * tc
    grp = jnp.right_shift(lane, 6)
    win = jnp.where(grp == 0, 2, jnp.where(grp == 1, 4, jnp.where(grp == 2, 8, 16)))
    brow = lax.broadcasted_iota(jnp.int32, (rows_all, 1), 0)
    cnt = jnp.minimum(win, pos0 + j * tc + jnp.bitwise_and(brow, tc - 1) + 1).astype(F32)
    first_half = jnp.bitwise_and(lane, HEAD_D - 1) < HEAD_D // 2
    cos = cos_ref[...]
    sin = sin_ref[...]

    def rope(t):
        swapped = jnp.where(first_half, pltpu.roll(t, MIX_W - HEAD_D // 2, 1), pltpu.roll(t, HEAD_D // 2, 1))
        return t * cos + swapped * sin

    si = lax.broadcasted_iota(jnp.int32, (MIX_W, MIX_W), 0)
    sj = lax.broadcasted_iota(jnp.int32, (MIX_W, MIX_W), 1)
    head_ones = (jnp.right_shift(si, 6) == jnp.right_shift(sj, 6)).astype(BF16)

    def split_dot(ones, t):
        hi = t.astype(BF16)
        lo = (t - hi.astype(F32)).astype(BF16)
        return jnp.dot(ones, hi, preferred_element_type=F32) + jnp.dot(ones, lo, preferred_element_type=F32)

    def head_sum(t):
        hi = t.astype(BF16)
        lo = (t - hi.astype(F32)).astype(BF16)
        return (jnp.dot(hi, head_ones, preferred_element_type=F32)
                + jnp.dot(lo, head_ones, preferred_element_type=F32))

    c = min(RWKV_CHUNK, tc)
    nsub = tc // c
    merged = c % SUBLANES == 0
    ci = lax.broadcasted_iota(jnp.int32, (c, c), 0)
    cj = lax.broadcasted_iota(jnp.int32, (c, c), 1)
    strict = ci > cj
    incl = ci >= cj
    ti = lax.broadcasted_iota(jnp.int32, (rows_all, rows_all), 0)
    tj = lax.broadcasted_iota(jnp.int32, (rows_all, rows_all), 1)
    shift_c = int(math.log2(c))
    tri = ((ti >= tj) & (jnp.right_shift(ti, shift_c) == jnp.right_shift(tj, shift_c))).astype(BF16)

    for lb in seqs:
        rows = rows_of(lb)
        pool_ext[lb, POOL_HIST:POOL_HIST + tc] = z_s[rows, 0:MIX_W]
        e = pool_ext[lb]
        s2 = e + pltpu.roll(e, 1, 0)
        s4 = s2 + pltpu.roll(s2, 2, 0)
        s8 = s4 + pltpu.roll(s4, 4, 0)
        s16 = s8 + pltpu.roll(s8, 8, 0)
        wsum = jnp.where(grp == 0, s2, jnp.where(grp == 1, s4, jnp.where(grp == 2, s8, s16)))
        wsum_s[rows] = wsum[POOL_HIST:POOL_HIST + tc]
        new_pool = pool_ext[lb, tc + POOL_HIST - POOL_PAST:tc + POOL_HIST]
        pool_ext[lb, POOL_HIST - POOL_PAST:POOL_HIST] = new_pool
        pool_ref[lb] = new_pool

        sc_ext[lb, HIST:HIST + tc] = (z_s[rows, COL_SC + 2 * MIX_W:COL_SC + 3 * MIX_W]
                                      * z_s[rows, COL_SC:COL_SC + MIX_W])
        e = sc_ext[lb]
        conv = vec(V_SC2) * e + vec(V_SC1) * pltpu.roll(e, 1, 0) + vec(V_SC0) * pltpu.roll(e, 2, 0)
        conv_s[rows] = conv[HIST:HIST + tc]
        new_sc = sc_ext[lb, HIST + tc - 2:HIST + tc]
        sc_ext[lb, HIST - 2:HIST] = new_sc
        sc_ref[lb] = new_sc

        sh_ext[lb, HIST:HIST + tc] = z_s[rows, COL_RWKV:COL_GATE]
        prev_s[rows] = pltpu.roll(sh_ext[lb], 1, 0)[HIST:HIST + tc]
        new_sh = sh_ext[lb, HIST + tc - 1:HIST + tc]
        sh_ext[lb, HIST - 1:HIST] = new_sh
        sh_ref[lb] = new_sh

    u = z_s[:, 0:MIX_W]
    y_s[:, 0:MIX_W] = _wdot(wsum_s[...] / cnt - u, poolw_ref[...]) * vec(V_POOL_SCALE)
    y_s[:, 2 * MIX_W:3 * MIX_W] = z_s[:, COL_SC + MIX_W:COL_SC + 2 * MIX_W] * conv_s[...]

    q_all = rope(z_s[:, COL_RET:COL_RET + MIX_W])
    k_all = rope(z_s[:, COL_RET + MIX_W:COL_RET + 2 * MIX_W]) * (HEAD_D ** -0.5)
    v_all = z_s[:, COL_RET + 2 * MIX_W:COL_RET + 3 * MIX_W]

    zz = z_s[:, COL_RWKV:COL_GATE]
    zs = zz + (prev_s[...] - zz) * mu_ref[...]
    r = zs[:, 0:MIX_W]
    kx = zs[:, MIX_W:2 * MIX_W]
    vx = zs[:, 2 * MIX_W:3 * MIX_W]
    o3 = 3 * MIX_W
    wl = zs[:, o3:o3 + LORA_W]
    al = zs[:, o3 + LORA_W:o3 + LORA_W + LORA_A]
    gl = zs[:, o3 + LORA_W + LORA_A:]
    wlog = -_softplus(-(vec(V_W0) + _bdot(jnp.tanh(wl), wlora_ref[...]))) - 0.5
    logw = -jnp.exp(wlog)
    asig = _sigmoid(vec(V_A0) + _bdot(al, alora_ref[...]))
    g_out = _bdot(_sigmoid(gl), glora_ref[...])
    k2 = kx * (1.0 + (asig - 1.0) * vec(V_KA))
    kk_raw = kx * vec(V_KK)
    kk = kk_raw * lax.rsqrt(jnp.maximum(head_sum(kk_raw * kk_raw), L2_EPS * L2_EPS))
    cum = split_dot(tri, logw)
    p_in = jnp.exp(cum)
    p_inv = jnp.exp(-cum)
    rw = dict(a=-kk * jnp.exp(cum - logw), b=kk * asig * p_inv, k=k2 * p_inv, r=r * p_in, v=vx, p_in=p_in)

    pairs = [(lb, h) for lb in seqs for h in heads]
    qs = [q_all[rows_of(lb), hsl[h]] for lb, h in pairs]
    ks = [k_all[rows_of(lb), hsl[h]] for lb, h in pairs]
    vs = [v_all[rows_of(lb), hsl[h]] for lb, h in pairs]
    states = [ret_ref[lb, h] for lb, h in pairs]
    np_ = range(len(pairs))
    scores = [_bdot_nt(qs[i], ks[i]) for i in np_]
    cross = [_bdot(qs[i] * jnp.exp(lgs[pairs[i][1]] * (row_f + 1.0)), states[i]) for i in np_]
    kv = [_bdot_tn(ks[i] * jnp.exp(lgs[pairs[i][1]] * (tc - 1.0 - row_f)), vs[i]) for i in np_]
    outs = [_bdot(scores[i] * dm_s[pairs[i][1]], vs[i]) + cross[i] for i in np_]
    for i, (lb, h) in enumerate(pairs):
        ret_ref[lb, h] = states[i] * math.exp(lgs[h] * tc) + kv[i]
        y_s[rows_of(lb), MIX_W + h * HEAD_D:MIX_W + (h + 1) * HEAD_D] = outs[i]
    o_all = y_s[:, MIX_W:2 * MIX_W]
    o_all = o_all * lax.rsqrt(head_sum(o_all * o_all) * (1.0 / HEAD_D) + GN_EPS)
    y_s[:, MIX_W:2 * MIX_W] = o_all * _silu(z_s[:, COL_RET + 3 * MIX_W:COL_RET + 4 * MIX_W])

    blocks = [(lb, sub, h) for lb in seqs for sub in range(nsub) for h in heads]

    def blk(name):
        return [rw[name][lb * tc + sub * c:lb * tc + (sub + 1) * c, hsl[h]] for lb, sub, h in blocks]

    a_l, b_l, k_l, r_l, v_l = blk("a"), blk("b"), blk("k"), blk("r"), blk("v")
    nb = range(len(blocks))
    if merged:
        bk_l = [jnp.concatenate([b_l[i], k_l[i]], axis=0) for i in nb]
        gram = [_bdot_nt(jnp.concatenate([a_l[i], r_l[i]], axis=0), bk_l[i]) for i in nb]
        g_ab, g_ak = [g[:c, :c] for g in gram], [g[:c, c:] for g in gram]
        g_rb, g_rk = [g[c:, :c] for g in gram], [g[c:, c:] for g in gram]
    else:
        g_ab = [_bdot_nt(a_l[i], b_l[i]) for i in nb]
        g_ak = [_bdot_nt(a_l[i], k_l[i]) for i in nb]
        g_rb = [_bdot_nt(r_l[i], b_l[i]) for i in nb]
        g_rk = [_bdot_nt(r_l[i], k_l[i]) for i in nb]
    l_ab = [jnp.where(strict, g, 0.0) for g in g_ab]
    l_ak = [jnp.where(strict, g, 0.0) for g in g_ak]
    m_rb = [jnp.where(incl, g, 0.0) for g in g_rb]
    m_rk = [jnp.where(incl, g, 0.0) for g in g_rk]
    n_inv = _tri_inverse_minus_eye(l_ab, c)
    if merged:
        lv = [_bdot(jnp.concatenate([l_ak[i], m_rk[i]], axis=0), v_l[i]) for i in nb]
        lakv, mv = [t[:c] for t in lv], [t[c:] for t in lv]
    else:
        lakv = [_bdot(l_ak[i], v_l[i]) for i in nb]
        mv = [_bdot(m_rk[i], v_l[i]) for i in nb]
    nx = [_bdot(n_inv[i], jnp.concatenate([lakv[i], a_l[i]], axis=1)) for i in nb]
    tlv = [lakv[i] + nx[i][:, :HEAD_D] for i in nb]
    ta = [a_l[i] + nx[i][:, HEAD_D:] for i in nb]

    wkv_states = [wkv_ref[lb, h] for lb, h in pairs]
    for sub in range(nsub):
        ids = [(lb * nsub + sub) * HEADS + h for lb, h in pairs]
        p_end = [rw["p_in"][lb * tc + (sub + 1) * c - 1:lb * tc + (sub + 1) * c, hsl[h]] for lb, h in pairs]
        u_mat = [_bdot_nt(ta[i], wkv_states[n]) + tlv[i] for n, i in enumerate(ids)]
        y_st = [_bdot_nt(r_l[i], wkv_states[n]) for n, i in enumerate(ids)]
        if merged:
            upd = [_bdot_tn(jnp.concatenate([u_mat[n], v_l[i]], axis=0), bk_l[i] * p_end[n])
                   for n, i in enumerate(ids)]
        else:
            upd = [_bdot_tn(u_mat[n], b_l[i] * p_end[n]) + _bdot_tn(v_l[i], k_l[i] * p_end[n])
                   for n, i in enumerate(ids)]
        y_u = [_bdot(m_rb[i], u_mat[n]) for n, i in enumerate(ids)]
        for n, i in enumerate(ids):
            lb, h = pairs[n]
            ywkv[lb * tc + sub * c:lb * tc + (sub + 1) * c, hsl[h]] = y_st[n] + y_u[n] + mv[i]
        wkv_states = [wkv_states[n] * p_end[n] + upd[n] for n in np_]
    for n, (lb, h) in enumerate(pairs):
        wkv_ref[lb, h] = wkv_states[n]

    y_all = ywkv[...]
    cen = y_all - head_sum(y_all) * (1.0 / HEAD_D)
    var = head_sum(cen * cen) * (1.0 / HEAD_D)
    yn = cen * lax.rsqrt(var + RWKV_LN_EPS) * vec(V_LNG) + vec(V_LNB)
    bonus = head_sum(r * k2 * vec(V_RK)) * vx
    y_s[:, 3 * MIX_W:4 * MIX_W] = (yn + bonus) * g_out

    out = _wdot(gated_proj(0) + gated_proj(1) + gated_proj(2) + gated_proj(3), wout_ref[...])
    for lb in seqs:
        _write_seq_rows(x1_ref, lb, tc, x[rows_of(lb)] + _mod_row(mod_ref, 2, lb) * out[rows_of(lb)])


def _layer_block_spec(arr, bb, layer):
    shape = arr.shape[2:]
    zeros = (0,) * len(shape)
    return pl.BlockSpec((None, bb) + shape, lambda i, j: (layer, i) + zeros)


def _resident_spec(arr, layer=None):
    if layer is None:
        zeros = (0,) * arr.ndim
        return pl.BlockSpec(arr.shape, lambda i, j: zeros, pipeline_mode=pl.Buffered(1))
    zeros = (0,) * (arr.ndim - 1)
    return pl.BlockSpec((None,) + arr.shape[1:], lambda i, j: (layer,) + zeros, pipeline_mode=pl.Buffered(1))


def _rows_operand(x2d, batch, t, bb, tc):
    d = x2d.shape[1]
    if tc == t:
        return x2d, pl.BlockSpec((bb * tc, d), lambda i, j: (i, 0))
    return x2d.reshape(batch, t, d), pl.BlockSpec((bb, tc, d), lambda i, j: (i, j, 0))


def _params():
    return pltpu.CompilerParams(dimension_semantics=("parallel", "arbitrary"), vmem_limit_bytes=VMEM_LIMIT)


def _blocking(batch, t, split=1):
    if t >= 256:
        return (split, 256 // split) if batch % split == 0 else (1, 256)
    return min(batch, 64 // t), t


def _passthrough(prev):
    if prev is None:
        return [], []
    return list(prev), [pl.BlockSpec(memory_space=pl.ANY) for _ in prev]


def _mix_call(x2d, mod, g1, w_in, cos_t, sin_t, states, poolw, vecs, mu, wlora, alora, glora, w_br, w_out,
              batch, t, pos0, layer, depth, prev):
    bb, tc = _blocking(batch, t, MIX_SPLIT)
    nt = t // tc
    tp = _round_up(tc, SUBLANES)
    d = x2d.shape[1]
    state_shapes = [(depth, batch, POOL_PAST, MIX_W), (depth, batch, HEADS, HEAD_D, HEAD_D), (depth, batch, 2, MIX_W),
                    (depth, batch, 1, d), (depth, batch, HEADS, HEAD_D, HEAD_D)]
    out_structs = [jax.ShapeDtypeStruct(sh, F32) for sh in state_shapes]
    state_specs = [_layer_block_spec(st, bb, layer) for st in out_structs]
    state_ops = [] if states is None else [st.reshape(sh) for st, sh in zip(states, state_shapes)]
    weights = (poolw, vecs, mu, wlora, alora, glora, w_br, w_out)
    mod, mod_spec = _mod_operand(mod, bb)
    prev_ops, prev_specs = _passthrough(prev)
    n_in = 6 + len(state_ops) + len(weights)
    x_op, x_spec = _rows_operand(x2d, batch, t, bb, tc)

    def block_table(tab):
        return jnp.tile(tab.reshape(nt, tc, MIX_W), (1, bb, 1)).reshape(nt * bb * tc, MIX_W)

    outs = pl.pallas_call(
        functools.partial(_mix_kernel, bb=bb, tc=tc, pos0=pos0, zero_init=states is None, n_alias=len(prev_ops)),
        grid=(batch // bb, nt),
        in_specs=[x_spec, mod_spec,
                  _resident_spec(g1, layer), _resident_spec(w_in, layer),
                  pl.BlockSpec((bb * tc, MIX_W), lambda i, j: (j, 0)),
                  pl.BlockSpec((bb * tc, MIX_W), lambda i, j: (j, 0))]
        + state_specs[:len(state_ops)] + [_resident_spec(a, layer) for a in weights] + prev_specs,
        out_specs=[x_spec] + state_specs,
        out_shape=[jax.ShapeDtypeStruct(x_op.shape, F32)] + out_structs,
        input_output_aliases={n_in + k: 1 + k for k in range(len(prev_ops))},
        scratch_shapes=[pltpu.VMEM((bb * tc, d), F32),
                        pltpu.VMEM((bb * tc, IN_COLS), F32),
                        pltpu.VMEM((bb * tc, d), F32),
                        pltpu.VMEM((HEADS, tc, tc), F32),
                        pltpu.VMEM((bb, POOL_HIST + tp, MIX_W), F32),
                        pltpu.VMEM((bb, HIST + tp, MIX_W), F32),
                        pltpu.VMEM((bb, HIST + tp, d), F32),
                        pltpu.VMEM((bb * tc, MIX_W), F32),
                        pltpu.VMEM((bb * tc, MIX_W), F32),
                        pltpu.VMEM((bb * tc, MIX_W), F32),
                        pltpu.VMEM((bb * tc, COL_GATE - COL_RWKV), F32)],
        compiler_params=_params(),
        name="mix",
    )(x_op, mod, g1, w_in, block_table(cos_t), block_table(sin_t), *state_ops, *weights, *prev_ops)
    return outs[0].reshape(x2d.shape), tuple(outs[1:])


def _ffn_kernel(*refs, bb, tc, final, zero_init, n_alias):
    refs = list(refs)
    x1_ref, mod_ref, g_ref, wup_ref, ffnw_ref, wdown_ref, fg_ref = refs[:7]
    del refs[:7]
    if not zero_init:
        st0_ref = refs.pop(0)
    del refs[:n_alias]
    o_ref, st_ref, h_s, ext, act_s = refs
    j = pl.program_id(1)
    seqs = range(bb)

    def rows_of(lb):
        return slice(lb * tc, (lb + 1) * tc)

    @pl.when(j == 0)
    def _init():
        ext[...] = jnp.zeros(ext.shape, F32)
        if not zero_init:
            ext[:, HIST - 2:HIST] = st0_ref[...]

    x1 = _read_rows(x1_ref)
    hn = _rmsnorm(x1, g_ref[...])
    for lb in seqs:
        h_s[rows_of(lb)] = hn[rows_of(lb)] * (1.0 + _mod_row(mod_ref, 4, lb)) + _mod_row(mod_ref, 3, lb)
    if bb == 1:
        ext[0, HIST:HIST + tc] = _wdot(h_s[...], wup_ref[...])
    else:
        up = _wdot(h_s[...], wup_ref[...])
        for lb in seqs:
            ext[lb, HIST:HIST + tc] = up[rows_of(lb)]

    for lb in seqs:
        def conv_cols(lo, width):
            e = ext[lb, :, lo:lo + width]
            w = ffnw_ref[:, lo:lo + width]
            y = w[2:3] * e + w[1:2] * pltpu.roll(e, 1, 0) + w[0:1] * pltpu.roll(e, 2, 0)
            return y[HIST:HIST + tc]

        for lo in range(0, D_FF, CONV_COLS):
            width = min(CONV_COLS, D_FF - lo)
            act_s[rows_of(lb), lo:lo + width] = _silu(conv_cols(lo, width)) * conv_cols(D_FF + lo, width)
        new_st = ext[lb, HIST + tc - 2:HIST + tc]
        ext[lb, HIST - 2:HIST] = new_st
        st_ref[lb] = new_st

    dn = _wdot(act_s[...], wdown_ref[...])
    for lb in seqs:
        _write_seq_rows(o_ref, lb, tc, x1[rows_of(lb)] + _mod_row(mod_ref, 5, lb) * dn[rows_of(lb)])
    if final:
        o_ref[...] = _rmsnorm(o_ref[...], fg_ref[...])


def _ffn_call(x1, mod, g2, w_up, ffn_w, w_down, final_g, state, batch, t, final, layer, depth, prev):
    bb, tc = _blocking(batch, t)
    nt = t // tc
    tp = _round_up(tc, SUBLANES)
    d = x1.shape[1]
    st_struct = jax.ShapeDtypeStruct((depth, batch, 2, 2 * D_FF), F32)
    st_spec = _layer_block_spec(st_struct, bb, layer)
    state_ops = [] if state is None else [state]
    mod, mod_spec = _mod_operand(mod, bb)
    prev_ops, prev_specs = _passthrough(prev)
    n_in = 7 + len(state_ops)
    x_op, x_spec = _rows_operand(x1, batch, t, bb, tc)
    out, new_state = pl.pallas_call(
        functools.partial(_ffn_kernel, bb=bb, tc=tc, final=final, zero_init=state is None, n_alias=len(prev_ops)),
        grid=(batch // bb, nt),
        in_specs=[x_spec, mod_spec,
                  _resident_spec(g2, layer), _resident_spec(w_up, layer), _resident_spec(ffn_w, layer),
                  _resident_spec(w_down, layer), _resident_spec(final_g)]
        + [st_spec] * len(state_ops) + prev_specs,
        out_specs=[x_spec, st_spec],
        out_shape=[jax.ShapeDtypeStruct(x_op.shape, F32), st_struct],
        input_output_aliases={n_in + k: 1 + k for k in range(len(prev_ops))},
        scratch_shapes=[pltpu.VMEM((bb * tc, d), F32),
                        pltpu.VMEM((bb, HIST + tp, 2 * D_FF), F32),
                        pltpu.VMEM((bb * tc, D_FF), F32)],
        compiler_params=_params(),
        name="ffn",
    )(x_op, mod, g2, w_up, ffn_w, w_down, final_g, *state_ops, *prev_ops)
    return out.reshape(x1.shape), new_state


def _rope_tables(t, pos0):
    half = HEAD_D // 2
    inv = ROPE_BASE ** (-jnp.arange(half, dtype=F32) / half)
    pos = pos0 + jnp.arange(t, dtype=jnp.int32)
    ang = pos.astype(F32)[:, None] * inv[None, :]
    cos, sin = jnp.cos(ang), jnp.sin(ang)
    cos_t = jnp.tile(jnp.concatenate([cos, cos], axis=-1), (1, HEADS))
    sin_t = jnp.tile(jnp.concatenate([-sin, sin], axis=-1), (1, HEADS))
    return cos_t, sin_t


def _block_diag(pool_w):
    groups, gw, _ = pool_w.shape
    out = jnp.zeros((groups * gw, groups * gw), pool_w.dtype)
    for g in range(groups):
        out = out.at[g * gw:(g + 1) * gw, g * gw:(g + 1) * gw].set(pool_w[g])
    return out


def kernel(x_prompt, x_sample, c_prompt, c_sample, state_pool, state_ret, state_sconv, state_shift, state_wkv, state_ffn, w_ada, b_ada, norm1_g, norm2_g, w_in, pool_w, pool_scale, sc_w, rw_mu, rw_w0, rw_w_lora, rw_a0, rw_a_lora, rw_g_lora, rw_k_k, rw_k_a, rw_r_k, rw_ln_g, rw_ln_b, w_br, w_out, w_up, ffn_w, w_down, final_g):
    depth = w_in.shape[0]
    bp, tp_, d = x_prompt.shape
    bs, ts, _ = x_sample.shape

    mods = _mod_call(jnp.concatenate([c_prompt, c_sample], axis=0), w_ada, b_ada)
    cos_p, sin_p = _rope_tables(tp_, 0)
    cos_s, sin_s = _rope_tables(ts, PAST_LEN)
    final_g2 = final_g.reshape(1, d)

    xp = x_prompt.reshape(bp * tp_, d)
    xs = x_sample.reshape(bs * ts, d)
    sample_states = (state_pool, state_ret, state_sconv, state_shift, state_wkv)
    vec_rows = [pool_scale, sc_w[:, 0], sc_w[:, 1], sc_w[:, 2], rw_w0, rw_a0, rw_k_k, rw_k_a,
                rw_r_k.reshape(depth, MIX_W), rw_ln_g, rw_ln_b]
    vecs = jnp.concatenate([jnp.stack(vec_rows, axis=1),
                            jnp.zeros((depth, N_VECS - len(vec_rows), MIX_W), F32)], axis=1)
    mix_w = (norm1_g.reshape(depth, 1, d), w_in.astype(BF16))
    seq_w = (jnp.stack([_block_diag(pool_w[l]) for l in range(depth)]).astype(BF16), vecs,
             rw_mu.reshape(depth, 1, d), rw_w_lora, rw_a_lora, rw_g_lora, w_br.astype(BF16), w_out.astype(BF16))
    ffn_ws = (norm2_g.reshape(depth, 1, d), w_up.astype(BF16), ffn_w, w_down.astype(BF16), final_g2)
    st_p = st_s = ffn_p = ffn_s = None
    for l in range(depth):
        mod_p, mod_s = mods[l, :, :bp], mods[l, :, bp:]
        final = l == depth - 1

        x1, st_p = _mix_call(xp, mod_p, *mix_w, cos_p, sin_p, None, *seq_w, batch=bp, t=tp_, pos0=0,
                             layer=l, depth=depth, prev=st_p)
        xp, ffn_p = _ffn_call(x1, mod_p, *ffn_ws, None, batch=bp, t=tp_, final=final, layer=l, depth=depth,
                              prev=None if ffn_p is None else (ffn_p,))
        x1, st_s = _mix_call(xs, mod_s, *mix_w, cos_s, sin_s, sample_states, *seq_w, batch=bs, t=ts,
                             pos0=PAST_LEN, layer=l, depth=depth, prev=st_s)
        xs, ffn_s = _ffn_call(x1, mod_s, *ffn_ws, state_ffn, batch=bs, t=ts, final=final, layer=l, depth=depth,
                              prev=None if ffn_s is None else (ffn_s,))

    def finish(st, ffn_st, batch):
        pool, ret, sc, sh, wkv = st
        return (pool, ret, sc, sh.reshape(depth, batch, d), wkv, ffn_st)

    return (xp.reshape(bp, tp_, d), xs.reshape(bs, ts, d), *finish(st_p, ffn_p, bp), *finish(st_s, ffn_s, bs))
```

```python
import functools
import math

import jax
import jax.numpy as jnp
from jax import lax
from jax.experimental import pallas as pl
from jax.experimental.pallas import tpu as pltpu

F32 = jnp.float32
BF16 = jnp.bfloat16

D_MODEL = 1024
MIX_W = 256
HEADS = 4
HEAD_D = 64
POOL_PAST = 15
ROPE_BASE = 10000.0
LORA_W = 64
LORA_A = 64
LORA_G = 128
D_FF = 2816
PAST_LEN = 16384
NORM_EPS = 1e-6
GN_EPS = 1e-6
RWKV_LN_EPS = 64e-5
L2_EPS = 1e-12

COL_RET = MIX_W
COL_SC = COL_RET + 4 * MIX_W
COL_RWKV = COL_SC + 3 * MIX_W
COL_GATE = COL_RWKV + 3 * MIX_W + LORA_W + LORA_A + LORA_G
IN_COLS = COL_GATE + 4 * D_MODEL

SUBLANES = 8
VMEM_LIMIT = 56 * 1024 * 1024
RWKV_CHUNK = 64
INV_BASE = 16
POOL_HIST = 16
HIST = SUBLANES
CONV_COLS = 512
MIX_SPLIT = 4

(V_POOL_SCALE, V_SC0, V_SC1, V_SC2, V_W0, V_A0, V_KK, V_KA, V_RK, V_LNG, V_LNB) = range(11)
N_VECS = 16


def _bdot(a, b):
    return jnp.dot(a.astype(BF16), b.astype(BF16), preferred_element_type=F32)


def _bdot_nt(a, b):
    return lax.dot_general(a.astype(BF16), b.astype(BF16), (((1,), (1,)), ((), ())), preferred_element_type=F32)


def _bdot_tn(a, b):
    return lax.dot_general(a.astype(BF16), b.astype(BF16), (((0,), (0,)), ((), ())), preferred_element_type=F32)


def _wdot(a, w_bf16):
    return jnp.dot(a.astype(BF16), w_bf16, preferred_element_type=F32)


def _sigmoid(x):
    return 1.0 / (1.0 + jnp.exp(-x))


def _silu(x):
    half = 0.5 * x
    return half + half * jnp.tanh(half)


def _softplus(x):
    return jnp.maximum(x, 0.0) + jnp.log(1.0 + jnp.exp(-jnp.abs(x)))


def _rmsnorm(x, g):
    return x * lax.rsqrt(jnp.mean(x * x, axis=-1, keepdims=True) + NORM_EPS) * g


def _round_up(n, m):
    return (n + m - 1) // m * m


def _mod_row(mod_ref, k, lb):
    if len(mod_ref.shape) == 4:
        return mod_ref[k, lb]
    return mod_ref[k, lb:lb + 1, :]


def _mod_operand(mod, bb):
    six, _, d = mod.shape
    if bb % SUBLANES:
        return mod[:, :, None, :], pl.BlockSpec((six, bb, 1, d), lambda i, j: (0, i, 0, 0))
    return mod, pl.BlockSpec((six, bb, d), lambda i, j: (0, i, 0))


def _read_rows(ref):
    val = ref[...]
    return val.reshape(-1, val.shape[-1]) if val.ndim == 3 else val


def _write_seq_rows(ref, lb, tc, val):
    if len(ref.shape) == 3:
        ref[lb] = val
    else:
        ref[lb * tc:(lb + 1) * tc] = val


def _mod_kernel(c_ref, w_ref, b_ref, o_ref):
    o_ref[0, 0] = _wdot(_silu(c_ref[...]), w_ref[0].astype(BF16)) + b_ref[0]


def _mod_call(c_all, w_ada, b_ada):
    depth, d, cols = w_ada.shape
    rows = c_all.shape[0]
    cb = D_MODEL
    return pl.pallas_call(
        _mod_kernel,
        grid=(depth, cols // cb),
        in_specs=[pl.BlockSpec((rows, d), lambda l, c: (0, 0)),
                  pl.BlockSpec((1, d, cb), lambda l, c: (l, 0, c)),
                  pl.BlockSpec((1, 1, cb), lambda l, c: (l, 0, c))],
        out_specs=pl.BlockSpec((1, 1, rows, cb), lambda l, c: (l, c, 0, 0)),
        out_shape=jax.ShapeDtypeStruct((depth, cols // cb, rows, cb), F32),
        name="adaln_mod",
    )(c_all, w_ada, b_ada.reshape(depth, 1, cols))


def _tri_inverse_minus_eye(lowers, n):
    i = lax.broadcasted_iota(jnp.int32, (n, n), 0)
    j = lax.broadcasted_iota(jnp.int32, (n, n), 1)

    def same_block(size):
        sh = int(math.log2(size))
        return jnp.right_shift(i, sh) == jnp.right_shift(j, sh)

    base = min(n, INV_BASE)
    if base < n:
        diag_mask = same_block(base)
        xs = [jnp.where(diag_mask, low, 0.0) for low in lowers]
    else:
        xs = list(lowers)
    powers, p = xs, 1
    while 2 * p < base:
        powers = [_bdot(pw, pw) for pw in powers]
        prods = [_bdot(x, pw) for x, pw in zip(xs, powers)]
        xs = [x + pw + pr for x, pw, pr in zip(xs, powers, prods)]
        p *= 2
    size = base
    while size < n:
        off_mask = same_block(2 * size) & jnp.logical_not(same_block(size))
        offs = [jnp.where(off_mask, low, 0.0) for low in lowers]
        lefts = [off + _bdot(x, off) for x, off in zip(xs, offs)]
        xs = [x + left + _bdot(left, x) for x, left in zip(xs, lefts)]
        size *= 2
    return xs


def _mix_kernel(*refs, bb, tc, pos0, zero_init, n_alias):
    refs = list(refs)
    x_ref, mod_ref, g_ref, win_ref, cos_ref, sin_ref = refs[:6]
    del refs[:6]
    if not zero_init:
        pool0_ref, ret0_ref, sc0_ref, sh0_ref, wkv0_ref = refs[:5]
        del refs[:5]
    poolw_ref, vecs_ref, mu_ref, wlora_ref, alora_ref, glora_ref, wbr_ref, wout_ref = refs[:8]
    del refs[:8 + n_alias]
    x1_ref, pool_ref, ret_ref, sc_ref, sh_ref, wkv_ref = refs[:6]
    h_s, z_s, y_s, dm_s, pool_ext, sc_ext, sh_ext, ywkv, wsum_s, conv_s, prev_s = refs[6:]
    j = pl.program_id(1)
    seqs = range(bb)
    heads = range(HEADS)
    hsl = [slice(h * HEAD_D, (h + 1) * HEAD_D) for h in heads]

    def rows_of(lb):
        return slice(lb * tc, (lb + 1) * tc)

    lane = lax.broadcasted_iota(jnp.int32, (1, MIX_W), 1)
    row_i = lax.broadcasted_iota(jnp.int32, (tc, 1), 0)
    col_i = lax.broadcasted_iota(jnp.int32, (1, tc), 1)
    row_f = row_i.astype(F32)
    lgs = [math.log1p(-(2.0 ** (-5.0 - h))) for h in heads]

    @pl.when(j == 0)
    def _init():
        diff = (row_i - col_i).astype(F32)
        for h in heads:
            dm_s[h] = jnp.where(diff >= 0, jnp.exp(lgs[h] * jnp.maximum(diff, 0.0)), 0.0)
        pool_ext[...] = jnp.zeros(pool_ext.shape, F32)
        sc_ext[...] = jnp.zeros(sc_ext.shape, F32)
        sh_ext[...] = jnp.zeros(sh_ext.shape, F32)
        if zero_init:
            ret_ref[...] = jnp.zeros(ret_ref.shape, F32)
            wkv_ref[...] = jnp.zeros(wkv_ref.shape, F32)
        else:
            pool_ext[:, POOL_HIST - POOL_PAST:POOL_HIST] = pool0_ref[...]
            sc_ext[:, HIST - 2:HIST] = sc0_ref[...]
            sh_ext[:, HIST - 1:HIST] = sh0_ref[...]
            ret_ref[...] = ret0_ref[...]
            wkv_ref[...] = wkv0_ref[...]

    def vec(row):
        return vecs_ref[row:row + 1, :]

    x = _read_rows(x_ref)
    hn = _rmsnorm(x, g_ref[...])
    for lb in seqs:
        h_s[rows_of(lb)] = hn[rows_of(lb)] * (1.0 + _mod_row(mod_ref, 1, lb)) + _mod_row(mod_ref, 0, lb)
    z_s[...] = _wdot(h_s[...], win_ref[...])

    def gated_proj(n):
        proj = _wdot(y_s[:, n * MIX_W:(n + 1) * MIX_W], wbr_ref[n])
        return _sigmoid(z_s[:, COL_GATE + n * D_MODEL:COL_GATE + (n + 1) * D_MODEL]) * proj

    rows_all = bb * tc
    grp = jnp.right_shift(lane, 6)
    win = jnp.where(grp == 0, 2, jnp.where(grp == 1, 4, jnp.where(grp == 2, 8, 16)))
    brow = lax.broadcasted_iota(jnp.int32, (rows_all, 1), 0)
    cnt = jnp.minimum(win, pos0 + j * tc + jnp.bitwise_and(brow, tc - 1) + 1).astype(F32)
    first_half = jnp.bitwise_and(lane, HEAD_D - 1) < HEAD_D // 2
    cos = cos_ref[...]
    sin = sin_ref[...]

    def rope(t):
        swapped = jnp.where(first_half, pltpu.roll(t, MIX_W - HEAD_D // 2, 1), pltpu.roll(t, HEAD_D // 2, 1))
        return t * cos + swapped * sin

    si = lax.broadcasted_iota(jnp.int32, (MIX_W, MIX_W), 0)
    sj = lax.broadcasted_iota(jnp.int32, (MIX_W, MIX_W), 1)
    head_ones = (jnp.right_shift(si, 6) == jnp.right_shift(sj, 6)).astype(BF16)

    def split_dot(ones, t):
        hi = t.astype(BF16)
        lo = (t - hi.astype(F32)).astype(BF16)
        return jnp.dot(ones, hi, preferred_element_type=F32) + jnp.dot(ones, lo, preferred_element_type=F32)

    def head_sum(t):
        hi = t.astype(BF16)
        lo = (t - hi.astype(F32)).astype(BF16)
        return (jnp.dot(hi, head_ones, preferred_element_type=F32)
                + jnp.dot(lo, head_ones, preferred_element_type=F32))

    c = min(RWKV_CHUNK, tc)
    nsub = tc // c
    merged = c % SUBLANES == 0
    ci = lax.broadcasted_iota(jnp.int32, (c, c), 0)
    cj = lax.broadcasted_iota(jnp.int32, (c, c), 1)
    strict = ci > cj
    incl = ci >= cj
    ti = lax.broadcasted_iota(jnp.int32, (rows_all, rows_all), 0)
    tj = lax.broadcasted_iota(jnp.int32, (rows_all, rows_all), 1)
    shift_c = int(math.log2(c))
    tri = ((ti >= tj) & (jnp.right_shift(ti, shift_c) == jnp.right_shift(tj, shift_c))).astype(BF16)

    for lb in seqs:
        rows = rows_of(lb)
        pool_ext[lb, POOL_HIST:POOL_HIST + tc] = z_s[rows, 0:MIX_W]
        e = pool_ext[lb]
        s2 = e + pltpu.roll(e, 1, 0)
        s4 = s2 + pltpu.roll(s2, 2, 0)
        s8 = s4 + pltpu.roll(s4, 4, 0)
        s16 = s8 + pltpu.roll(s8, 8, 0)
        wsum = jnp.where(grp == 0, s2, jnp.where(grp == 1, s4, jnp.where(grp == 2, s8, s16)))
        wsum_s[rows] = wsum[POOL_HIST:POOL_HIST + tc]
        new_pool = pool_ext[lb, tc + POOL_HIST - POOL_PAST:tc + POOL_HIST]
        pool_ext[lb, POOL_HIST - POOL_PAST:POOL_HIST] = new_pool
        pool_ref[lb] = new_pool

        sc_ext[lb, HIST:HIST + tc] = (z_s[rows, COL_SC + 2 * MIX_W:COL_SC + 3 * MIX_W]
                                      * z_s[rows, COL_SC:COL_SC + MIX_W])
        e = sc_ext[lb]
        conv = vec(V_SC2) * e + vec(V_SC1) * pltpu.roll(e, 1, 0) + vec(V_SC0) * pltpu.roll(e, 2, 0)
        conv_s[rows] = conv[HIST:HIST + tc]
        new_sc = sc_ext[lb, HIST + tc - 2:HIST + tc]
        sc_ext[lb, HIST - 2:HIST] = new_sc
        sc_ref[lb] = new_sc

        sh_ext[lb, HIST:HIST + tc] = z_s[rows, COL_RWKV:COL_GATE]
        prev_s[rows] = pltpu.roll(sh_ext[lb], 1, 0)[HIST:HIST + tc]
        new_sh = sh_ext[lb, HIST + tc - 1:HIST + tc]
        sh_ext[lb, HIST - 1:HIST] = new_sh
        sh_ref[lb] = new_sh

    u = z_s[:, 0:MIX_W]
    y_s[:, 0:MIX_W] = _wdot(wsum_s[...] / cnt - u, poolw_ref[...]) * vec(V_POOL_SCALE)
    y_s[:, 2 * MIX_W:3 * MIX_W] = z_s[:, COL_SC + MIX_W:COL_SC + 2 * MIX_W] * conv_s[...]

    q_all = rope(z_s[:, COL_RET:COL_RET + MIX_W])
    k_all = rope(z_s[:, COL_RET + MIX_W:COL_RET + 2 * MIX_W]) * (HEAD_D ** -0.5)
    v_all = z_s[:, COL_RET + 2 * MIX_W:COL_RET + 3 * MIX_W]

    zz = z_s[:, COL_RWKV:COL_GATE]
    zs = zz + (prev_s[...] - zz) * mu_ref[...]
    r = zs[:, 0:MIX_W]
    kx = zs[:, MIX_W:2 * MIX_W]
    vx = zs[:, 2 * MIX_W:3 * MIX_W]
    o3 = 3 * MIX_W
    wl = zs[:, o3:o3 + LORA_W]
    al = zs[:, o3 + LORA_W:o3 + LORA_W + LORA_A]
    gl = zs[:, o3 + LORA_W + LORA_A:]
    wlog = -_softplus(-(vec(V_W0) + _bdot(jnp.tanh(wl), wlora_ref[...]))) - 0.5
    logw = -jnp.exp(wlog)
    asig = _sigmoid(vec(V_A0) + _bdot(al, alora_ref[...]))
    g_out = _bdot(_sigmoid(gl), glora_ref[...])
    k2 = kx * (1.0 + (asig - 1.0) * vec(V_KA))
    kk_raw = kx * vec(V_KK)
    kk = kk_raw * lax.rsqrt(jnp.maximum(head_sum(kk_raw * kk_raw), L2_EPS * L2_EPS))
    cum = split_dot(tri, logw)
    p_in = jnp.exp(cum)
    p_inv = jnp.exp(-cum)
    rw = dict(a=-kk * jnp.exp(cum - logw), b=kk * asig * p_inv, k=k2 * p_inv, r=r * p_in, v=vx, p_in=p_in)

    pairs = [(lb, h) for lb in seqs for h in heads]
    qs = [q_all[rows_of(lb), hsl[h]] for lb, h in pairs]
    ks = [k_all[rows_of(lb), hsl[h]] for lb, h in pairs]
    vs = [v_all[rows_of(lb), hsl[h]] for lb, h in pairs]
    states = [ret_ref[lb, h] for lb, h in pairs]
    np_ = range(len(pairs))
    scores = [_bdot_nt(qs[i], ks[i]) for i in np_]
    cross = [_bdot(qs[i] * jnp.exp(lgs[pairs[i][1]] * (row_f + 1.0)), states[i]) for i in np_]
    kv = [_bdot_tn(ks[i] * jnp.exp(lgs[pairs[i][1]] * (tc - 1.0 - row_f)), vs[i]) for i in np_]
    outs = [_bdot(scores[i] * dm_s[pairs[i][1]], vs[i]) + cross[i] for i in np_]
    for i, (lb, h) in enumerate(pairs):
        ret_ref[lb, h] = states[i] * math.exp(lgs[h] * tc) + kv[i]
        y_s[rows_of(lb), MIX_W + h * HEAD_D:MIX_W + (h + 1) * HEAD_D] = outs[i]
    o_all = y_s[:, MIX_W:2 * MIX_W]
    o_all = o_all * lax.rsqrt(head_sum(o_all * o_all) * (1.0 / HEAD_D) + GN_EPS)
    y_s[:, MIX_W:2 * MIX_W] = o_all * _silu(z_s[:, COL_RET + 3 * MIX_W:COL_RET + 4 * MIX_W])

    blocks = [(lb, sub, h) for lb in seqs for sub in range(nsub) for h in heads]

    def blk(name):
        return [rw[name][lb * tc + sub * c:lb * tc + (sub + 1) * c, hsl[h]] for lb, sub, h in blocks]

    a_l, b_l, k_l, r_l, v_l = blk("a"), blk("b"), blk("k"), blk("r"), blk("v")
    nb = range(len(blocks))
    if merged:
        bk_l = [jnp.concatenate([b_l[i], k_l[i]], axis=0) for i in nb]
        gram = [_bdot_nt(jnp.concatenate([a_l[i], r_l[i]], axis=0), bk_l[i]) for i in nb]
        g_ab, g_ak = [g[:c, :c] for g in gram], [g[:c, c:] for g in gram]
        g_rb, g_rk = [g[c:, :c] for g in gram], [g[c:, c:] for g in gram]
    else:
        g_ab = [_bdot_nt(a_l[i], b_l[i]) for i in nb]
        g_ak = [_bdot_nt(a_l[i], k_l[i]) for i in nb]
        g_rb = [_bdot_nt(r_l[i], b_l[i]) for i in nb]
        g_rk = [_bdot_nt(r_l[i], k_l[i]) for i in nb]
    l_ab = [jnp.where(strict, g, 0.0) for g in g_ab]
    l_ak = [jnp.where(strict, g, 0.0) for g in g_ak]
    m_rb = [jnp.where(incl, g, 0.0) for g in g_rb]
    m_rk = [jnp.where(incl, g, 0.0) for g in g_rk]
    n_inv = _tri_inverse_minus_eye(l_ab, c)
    if merged:
        lv = [_bdot(jnp.concatenate([l_ak[i], m_rk[i]], axis=0), v_l[i]) for i in nb]
        lakv, mv = [t[:c] for t in lv], [t[c:] for t in lv]
    else:
        lakv = [_bdot(l_ak[i], v_l[i]) for i in nb]
        mv = [_bdot(m_rk[i], v_l[i]) for i in nb]
    nx = [_bdot(n_inv[i], jnp.concatenate([lakv[i], a_l[i]], axis=1)) for i in nb]
    tlv = [lakv[i] + nx[i][:, :HEAD_D] for i in nb]
    ta = [a_l[i] + nx[i][:, HEAD_D:] for i in nb]

    wkv_states = [wkv_ref[lb, h] for lb, h in pairs]
    for sub in range(nsub):
        ids = [(lb * nsub + sub) * HEADS + h for lb, h in pairs]
        p_end = [rw["p_in"][lb * tc + (sub + 1) * c - 1:lb * tc + (sub + 1) * c, hsl[h]] for lb, h in pairs]
        u_mat = [_bdot_nt(ta[i], wkv_states[n]) + tlv[i] for n, i in enumerate(ids)]
        y_st = [_bdot_nt(r_l[i], wkv_states[n]) for n, i in enumerate(ids)]
        if merged:
            upd = [_bdot_tn(jnp.concatenate([u_mat[n], v_l[i]], axis=0), bk_l[i] * p_end[n])
                   for n, i in enumerate(ids)]
        else:
            upd = [_bdot_tn(u_mat[n], b_l[i] * p_end[n]) + _bdot_tn(v_l[i], k_l[i] * p_end[n])
                   for n, i in enumerate(ids)]
        y_u = [_bdot(m_rb[i], u_mat[n]) for n, i in enumerate(ids)]
        for n, i in enumerate(ids):
            lb, h = pairs[n]
            ywkv[lb * tc + sub * c:lb * tc + (sub + 1) * c, hsl[h]] = y_st[n] + y_u[n] + mv[i]
        wkv_states = [wkv_states[n] * p_end[n] + upd[n] for n in np_]
    for n, (lb, h) in enumerate(pairs):
        wkv_ref[lb, h] = wkv_states[n]

    y_all = ywkv[...]
    cen = y_all - head_sum(y_all) * (1.0 / HEAD_D)
    var = head_sum(cen * cen) * (1.0 / HEAD_D)
    yn = cen * lax.rsqrt(var + RWKV_LN_EPS) * vec(V_LNG) + vec(V_LNB)
    bonus = head_sum(r * k2 * vec(V_RK)) * vx
    y_s[:, 3 * MIX_W:4 * MIX_W] = (yn + bonus) * g_out

    out = _wdot(gated_proj(0) + gated_proj(1) + gated_proj(2) + gated_proj(3), wout_ref[...])
    for lb in seqs:
        _write_seq_rows(x1_ref, lb, tc, x[rows_of(lb)] + _mod_row(mod_ref, 2, lb) * out[rows_of(lb)])


def _layer_block_spec(arr, bb, layer):
    shape = arr.shape[2:]
    zeros = (0,) * len(shape)
    return pl.BlockSpec((None, bb) + shape, lambda i, j: (layer, i) + zeros)


def _resident_spec(arr, layer=None):
    if layer is None:
        zeros = (0,) * arr.ndim
        return pl.BlockSpec(arr.shape, lambda i, j: zeros, pipeline_mode=pl.Buffered(1))
    zeros = (0,) * (arr.ndim - 1)
    return pl.BlockSpec((None,) + arr.shape[1:], lambda i, j: (layer,) + zeros, pipeline_mode=pl.Buffered(1))


def _rows_operand(x2d, batch, t, bb, tc):
    d = x2d.shape[1]
    if tc == t:
        return x2d, pl.BlockSpec((bb * tc, d), lambda i, j: (i, 0))
    return x2d.reshape(batch, t, d), pl.BlockSpec((bb, tc, d), lambda i, j: (i, j, 0))


def _params():
    return pltpu.CompilerParams(dimension_semantics=("parallel", "arbitrary"), vmem_limit_bytes=VMEM_LIMIT)


def _blocking(batch, t, split=1):
    if t >= 256:
        return (split, 256 // split) if batch % split == 0 else (1, 256)
    return min(batch, 64 // t), t


def _passthrough(prev, structs):
    if prev is None:
        prev = [jnp.zeros(st.shape, st.dtype) for st in structs]
    return list(prev), [pl.BlockSpec(memory_space=pl.ANY) for _ in prev]


def _mix_call(x2d, mod, g1, w_in, cos_t, sin_t, states, poolw, vecs, mu, wlora, alora, glora, w_br, w_out,
              batch, t, pos0, layer, depth, prev):
    bb, tc = _blocking(batch, t, MIX_SPLIT)
    nt = t // tc
    tp = _round_up(tc, SUBLANES)
    d = x2d.shape[1]
    state_shapes = [(depth, batch, POOL_PAST, MIX_W), (depth, batch, HEADS, HEAD_D, HEAD_D), (depth, batch, 2, MIX_W),
                    (depth, batch, 1, d), (depth, batch, HEADS, HEAD_D, HEAD_D)]
    out_structs = [jax.ShapeDtypeStruct(sh, F32) for sh in state_shapes]
    state_specs = [_layer_block_spec(st, bb, layer) for st in out_structs]
    state_ops = [] if states is None else [st.reshape(sh) for st, sh in zip(states, state_shapes)]
    weights = (poolw, vecs, mu, wlora, alora, glora, w_br, w_out)
    mod, mod_spec = _mod_operand(mod, bb)
    prev_ops, prev_specs = _passthrough(prev, out_structs)
    n_in = 6 + len(state_ops) + len(weights)
    x_op, x_spec = _rows_operand(x2d, batch, t, bb, tc)

    def block_table(tab):
        return jnp.tile(tab.reshape(nt, tc, MIX_W), (1, bb, 1)).reshape(nt * bb * tc, MIX_W)

    outs = pl.pallas_call(
        functools.partial(_mix_kernel, bb=bb, tc=tc, pos0=pos0, zero_init=states is None, n_alias=len(prev_ops)),
        grid=(batch // bb, nt),
        in_specs=[x_spec, mod_spec,
                  _resident_spec(g1, layer), _resident_spec(w_in, layer),
                  pl.BlockSpec((bb * tc, MIX_W), lambda i, j: (j, 0)),
                  pl.BlockSpec((bb * tc, MIX_W), lambda i, j: (j, 0))]
        + state_specs[:len(state_ops)] + [_resident_spec(a, layer) for a in weights] + prev_specs,
        out_specs=[x_spec] + state_specs,
        out_shape=[jax.ShapeDtypeStruct(x_op.shape, F32)] + out_structs,
        input_output_aliases={n_in + k: 1 + k for k in range(len(prev_ops))},
        scratch_shapes=[pltpu.VMEM((bb * tc, d), F32),
                        pltpu.VMEM((bb * tc, IN_COLS), F32),
                        pltpu.VMEM((bb * tc, d), F32),
                        pltpu.VMEM((HEADS, tc, tc), F32),
                        pltpu.VMEM((bb, POOL_HIST + tp, MIX_W), F32),
                        pltpu.VMEM((bb, HIST + tp, MIX_W), F32),
                        pltpu.VMEM((bb, HIST + tp, d), F32),
                        pltpu.VMEM((bb * tc, MIX_W), F32),
                        pltpu.VMEM((bb * tc, MIX_W), F32),
                        pltpu.VMEM((bb * tc, MIX_W), F32),
                        pltpu.VMEM((bb * tc, COL_GATE - COL_RWKV), F32)],
        compiler_params=_params(),
        name="mix",
    )(x_op, mod, g1, w_in, block_table(cos_t), block_table(sin_t), *state_ops, *weights, *prev_ops)
    return outs[0].reshape(x2d.shape), tuple(outs[1:])


def _ffn_kernel(*refs, bb, tc, final, zero_init, n_alias):
    refs = list(refs)
    x1_ref, mod_ref, g_ref, wup_ref, ffnw_ref, wdown_ref, fg_ref = refs[:7]
    del refs[:7]
    if not zero_init:
        st0_ref = refs.pop(0)
    del refs[:n_alias]
    o_ref, st_ref, h_s, ext, act_s = refs
    j = pl.program_id(1)
    seqs = range(bb)

    def rows_of(lb):
        return slice(lb * tc, (lb + 1) * tc)

    @pl.when(j == 0)
    def _init():
        ext[...] = jnp.zeros(ext.shape, F32)
        if not zero_init:
            ext[:, HIST - 2:HIST] = st0_ref[...]

    x1 = _read_rows(x1_ref)
    hn = _rmsnorm(x1, g_ref[...])
    for lb in seqs:
        h_s[rows_of(lb)] = hn[rows_of(lb)] * (1.0 + _mod_row(mod_ref, 4, lb)) + _mod_row(mod_ref, 3, lb)
    if bb == 1:
        ext[0, HIST:HIST + tc] = _wdot(h_s[...], wup_ref[...])
    else:
        up = _wdot(h_s[...], wup_ref[...])
        for lb in seqs:
            ext[lb, HIST:HIST + tc] = up[rows_of(lb)]

    for lb in seqs:
        def conv_cols(lo, width):
            e = ext[lb, :, lo:lo + width]
            w = ffnw_ref[:, lo:lo + width]
            y = w[2:3] * e + w[1:2] * pltpu.roll(e, 1, 0) + w[0:1] * pltpu.roll(e, 2, 0)
            return y[HIST:HIST + tc]

        for lo in range(0, D_FF, CONV_COLS):
            width = min(CONV_COLS, D_FF - lo)
            act_s[rows_of(lb), lo:lo + width] = _silu(conv_cols(lo, width)) * conv_cols(D_FF + lo, width)
        new_st = ext[lb, HIST + tc - 2:HIST + tc]
        ext[lb, HIST - 2:HIST] = new_st
        st_ref[lb] = new_st

    dn = _wdot(act_s[...], wdown_ref[...])
    for lb in seqs:
        _write_seq_rows(o_ref, lb, tc, x1[rows_of(lb)] + _mod_row(mod_ref, 5, lb) * dn[rows_of(lb)])
    if final:
        o_ref[...] = _rmsnorm(o_ref[...], fg_ref[...])


def _ffn_call(x1, mod, g2, w_up, ffn_w, w_down, final_g, state, batch, t, final, layer, depth, prev):
    bb, tc = _blocking(batch, t)
    nt = t // tc
    tp = _round_up(tc, SUBLANES)
    d = x1.shape[1]
    st_struct = jax.ShapeDtypeStruct((depth, batch, 2, 2 * D_FF), F32)
    st_spec = _layer_block_spec(st_struct, bb, layer)
    state_ops = [] if state is None else [state]
    mod, mod_spec = _mod_operand(mod, bb)
    prev_ops, prev_specs = _passthrough(prev, [st_struct])
    n_in = 7 + len(state_ops)
    x_op, x_spec = _rows_operand(x1, batch, t, bb, tc)
    out, new_state = pl.pallas_call(
        functools.partial(_ffn_kernel, bb=bb, tc=tc, final=final, zero_init=state is None, n_alias=len(prev_ops)),
        grid=(batch // bb, nt),
        in_specs=[x_spec, mod_spec,
                  _resident_spec(g2, layer), _resident_spec(w_up, layer), _resident_spec(ffn_w, layer),
                  _resident_spec(w_down, layer), _resident_spec(final_g)]
        + [st_spec] * len(state_ops) + prev_specs,
        out_specs=[x_spec, st_spec],
        out_shape=[jax.ShapeDtypeStruct(x_op.shape, F32), st_struct],
        input_output_aliases={n_in + k: 1 + k for k in range(len(prev_ops))},
        scratch_shapes=[pltpu.VMEM((bb * tc, d), F32),
                        pltpu.VMEM((bb, HIST + tp, 2 * D_FF), F32),
                        pltpu.VMEM((bb * tc, D_FF), F32)],
        compiler_params=_params(),
        name="ffn",
    )(x_op, mod, g2, w_up, ffn_w, w_down, final_g, *state_ops, *prev_ops)
    return out.reshape(x1.shape), new_state


def _rope_tables(t, pos0):
    half = HEAD_D // 2
    inv = ROPE_BASE ** (-jnp.arange(half, dtype=F32) / half)
    pos = pos0 + jnp.arange(t, dtype=jnp.int32)
    ang = pos.astype(F32)[:, None] * inv[None, :]
    cos, sin = jnp.cos(ang), jnp.sin(ang)
    cos_t = jnp.tile(jnp.concatenate([cos, cos], axis=-1), (1, HEADS))
    sin_t = jnp.tile(jnp.concatenate([-sin, sin], axis=-1), (1, HEADS))
    return cos_t, sin_t


def _block_diag(pool_w):
    groups, gw, _ = pool_w.shape
    out = jnp.zeros((groups * gw, groups * gw), pool_w.dtype)
    for g in range(groups):
        out = out.at[g * gw:(g + 1) * gw, g * gw:(g + 1) * gw].set(pool_w[g])
    return out


def kernel(x_prompt, x_sample, c_prompt, c_sample, state_pool, state_ret, state_sconv, state_shift, state_wkv, state_ffn, w_ada, b_ada, norm1_g, norm2_g, w_in, pool_w, pool_scale, sc_w, rw_mu, rw_w0, rw_w_lora, rw_a0, rw_a_lora, rw_g_lora, rw_k_k, rw_k_a, rw_r_k, rw_ln_g, rw_ln_b, w_br, w_out, w_up, ffn_w, w_down, final_g):
    depth = w_in.shape[0]
    bp, tp_, d = x_prompt.shape
    bs, ts, _ = x_sample.shape

    mods = _mod_call(jnp.concatenate([c_prompt, c_sample], axis=0), w_ada, b_ada)
    cos_p, sin_p = _rope_tables(tp_, 0)
    cos_s, sin_s = _rope_tables(ts, PAST_LEN)
    final_g2 = final_g.reshape(1, d)

    xp = x_prompt.reshape(bp * tp_, d)
    xs = x_sample.reshape(bs * ts, d)
    sample_states = (state_pool, state_ret, state_sconv, state_shift, state_wkv)
    vec_rows = [pool_scale, sc_w[:, 0], sc_w[:, 1], sc_w[:, 2], rw_w0, rw_a0, rw_k_k, rw_k_a,
                rw_r_k.reshape(depth, MIX_W), rw_ln_g, rw_ln_b]
    vecs = jnp.concatenate([jnp.stack(vec_rows, axis=1),
                            jnp.zeros((depth, N_VECS - len(vec_rows), MIX_W), F32)], axis=1)
    mix_w = (norm1_g.reshape(depth, 1, d), w_in.astype(BF16))
    seq_w = (jnp.stack([_block_diag(pool_w[l]) for l in range(depth)]).astype(BF16), vecs,
             rw_mu.reshape(depth, 1, d), rw_w_lora, rw_a_lora, rw_g_lora, w_br.astype(BF16), w_out.astype(BF16))
    ffn_ws = (norm2_g.reshape(depth, 1, d), w_up.astype(BF16), ffn_w, w_down.astype(BF16), final_g2)
    st_p = st_s = ffn_p = ffn_s = None
    for l in range(depth):
        mod_p, mod_s = mods[l, :, :bp], mods[l, :, bp:]
        final = l == depth - 1

        x1, st_p = _mix_call(xp, mod_p, *mix_w, cos_p, sin_p, None, *seq_w, batch=bp, t=tp_, pos0=0,
                             layer=l, depth=depth, prev=st_p)
        xp, ffn_p = _ffn_call(x1, mod_p, *ffn_ws, None, batch=bp, t=tp_, final=final, layer=l, depth=depth,
                              prev=None if ffn_p is None else (ffn_p,))
        x1, st_s = _mix_call(xs, mod_s, *mix_w, cos_s, sin_s, sample_states, *seq_w, batch=bs, t=ts,
                             pos0=PAST_LEN, layer=l, depth=depth, prev=st_s)
        xs, ffn_s = _ffn_call(x1, mod_s, *ffn_ws, state_ffn, batch=bs, t=ts, final=final, layer=l, depth=depth,
                              prev=None if ffn_s is None else (ffn_s,))

    def finish(st, ffn_st, batch):
        pool, ret, sc, sh, wkv = st
        return (pool, ret, sc, sh.reshape(depth, batch, d), wkv, ffn_st)

    return (xp.reshape(bp, tp_, d), xs.reshape(bs, ts, d), *finish(st_p, ffn_p, bp), *finish(st_s, ffn_s, bs))
```

```python
import functools
import math

import jax
import jax.numpy as jnp
from jax import lax
from jax.experimental import pallas as pl
from jax.experimental.pallas import tpu as pltpu

F32 = jnp.float32
BF16 = jnp.bfloat16

D_MODEL = 1024
MIX_W = 256
HEADS = 4
HEAD_D = 64
POOL_PAST = 15
ROPE_BASE = 10000.0
LORA_W = 64
LORA_A = 64
LORA_G = 128
D_FF = 2816
PAST_LEN = 16384
NORM_EPS = 1e-6
GN_EPS = 1e-6
RWKV_LN_EPS = 64e-5
L2_EPS = 1e-12

COL_RET = MIX_W
COL_SC = COL_RET + 4 * MIX_W
COL_RWKV = COL_SC + 3 * MIX_W
COL_GATE = COL_RWKV + 3 * MIX_W + LORA_W + LORA_A + LORA_G
IN_COLS = COL_GATE + 4 * D_MODEL

SUBLANES = 8
VMEM_LIMIT = 56 * 1024 * 1024
RWKV_CHUNK = 64
INV_BASE = 16
POOL_HIST = 16
HIST = SUBLANES
CONV_COLS = 512
MIX_SPLIT = 4

(V_POOL_SCALE, V_SC0, V_SC1, V_SC2, V_W0, V_A0, V_KK, V_KA, V_RK, V_LNG, V_LNB) = range(11)
N_VECS = 16


def _bdot(a, b):
    return jnp.dot(a.astype(BF16), b.astype(BF16), preferred_element_type=F32)


def _bdot_nt(a, b):
    return lax.dot_general(a.astype(BF16), b.astype(BF16), (((1,), (1,)), ((), ())), preferred_element_type=F32)


def _bdot_tn(a, b):
    return lax.dot_general(a.astype(BF16), b.astype(BF16), (((0,), (0,)), ((), ())), preferred_element_type=F32)


def _wdot(a, w_bf16):
    return jnp.dot(a.astype(BF16), w_bf16, preferred_element_type=F32)


def _sigmoid(x):
    return 1.0 / (1.0 + jnp.exp(-x))


def _silu(x):
    half = 0.5 * x
    return half + half * jnp.tanh(half)


def _softplus(x):
    return jnp.maximum(x, 0.0) + jnp.log(1.0 + jnp.exp(-jnp.abs(x)))


def _rmsnorm(x, g):
    return x * lax.rsqrt(jnp.mean(x * x, axis=-1, keepdims=True) + NORM_EPS) * g


def _round_up(n, m):
    return (n + m - 1) // m * m


def _mod_row(mod_ref, k, lb):
    if len(mod_ref.shape) == 4:
        return mod_ref[k, lb]
    return mod_ref[k, lb:lb + 1, :]


def _mod_operand(mod, bb):
    six, _, d = mod.shape
    if bb % SUBLANES:
        return mod[:, :, None, :], pl.BlockSpec((six, bb, 1, d), lambda i, j: (0, i, 0, 0))
    return mod, pl.BlockSpec((six, bb, d), lambda i, j: (0, i, 0))


def _read_rows(ref):
    val = ref[...]
    return val.reshape(-1, val.shape[-1]) if val.ndim == 3 else val


def _write_seq_rows(ref, lb, tc, val):
    if len(ref.shape) == 3:
        ref[lb] = val
    else:
        ref[lb * tc:(lb + 1) * tc] = val


def _mod_kernel(c_ref, w_ref, b_ref, o_ref):
    o_ref[0, 0] = _wdot(_silu(c_ref[...]), w_ref[0].astype(BF16)) + b_ref[0]


def _mod_call(c_all, w_ada, b_ada):
    depth, d, cols = w_ada.shape
    rows = c_all.shape[0]
    cb = D_MODEL
    return pl.pallas_call(
        _mod_kernel,
        grid=(depth, cols // cb),
        in_specs=[pl.BlockSpec((rows, d), lambda l, c: (0, 0)),
                  pl.BlockSpec((1, d, cb), lambda l, c: (l, 0, c)),
                  pl.BlockSpec((1, 1, cb), lambda l, c: (l, 0, c))],
        out_specs=pl.BlockSpec((1, 1, rows, cb), lambda l, c: (l, c, 0, 0)),
        out_shape=jax.ShapeDtypeStruct((depth, cols // cb, rows, cb), F32),
        name="adaln_mod",
    )(c_all, w_ada, b_ada.reshape(depth, 1, cols))


def _tri_inverse_minus_eye(lowers, n):
    i = lax.broadcasted_iota(jnp.int32, (n, n), 0)
    j = lax.broadcasted_iota(jnp.int32, (n, n), 1)

    def same_block(size):
        sh = int(math.log2(size))
        return jnp.right_shift(i, sh) == jnp.right_shift(j, sh)

    base = min(n, INV_BASE)
    if base < n:
        diag_mask = same_block(base)
        xs = [jnp.where(diag_mask, low, 0.0) for low in lowers]
    else:
        xs = list(lowers)
    powers, p = xs, 1
    while 2 * p < base:
        powers = [_bdot(pw, pw) for pw in powers]
        prods = [_bdot(x, pw) for x, pw in zip(xs, powers)]
        xs = [x + pw + pr for x, pw, pr in zip(xs, powers, prods)]
        p *= 2
    size = base
    while size < n:
        off_mask = same_block(2 * size) & jnp.logical_not(same_block(size))
        offs = [jnp.where(off_mask, low, 0.0) for low in lowers]
        lefts = [off + _bdot(x, off) for x, off in zip(xs, offs)]
        xs = [x + left + _bdot(left, x) for x, left in zip(xs, lefts)]
        size *= 2
    return xs


def _mix_kernel(*refs, bb, tc, pos0, zero_init):
    refs = list(refs)
    x_ref, mod_ref, g_ref, win_ref, cos_ref, sin_ref = refs[:6]
    del refs[:6]
    if not zero_init:
        pool0_ref, ret0_ref, sc0_ref, sh0_ref, wkv0_ref = refs[:5]
        del refs[:5]
    poolw_ref, vecs_ref, mu_ref, wlora_ref, alora_ref, glora_ref, wbr_ref, wout_ref = refs[:8]
    del refs[:8 + (5 if zero_init else 0)]
    x1_ref, pool_ref, ret_ref, sc_ref, sh_ref, wkv_ref = refs[:6]
    h_s, z_s, y_s, dm_s, pool_ext, sc_ext, sh_ext, ywkv, wsum_s, conv_s, prev_s = refs[6:]
    j = pl.program_id(1)
    seqs = range(bb)
    heads = range(HEADS)
    hsl = [slice(h * HEAD_D, (h + 1) * HEAD_D) for h in heads]

    def rows_of(lb):
        return slice(lb * tc, (lb + 1) * tc)

    lane = lax.broadcasted_iota(jnp.int32, (1, MIX_W), 1)
    row_i = lax.broadcasted_iota(jnp.int32, (tc, 1), 0)
    col_i = lax.broadcasted_iota(jnp.int32, (1, tc), 1)
    row_f = row_i.astype(F32)
    lgs = [math.log1p(-(2.0 ** (-5.0 - h))) for h in heads]

    @pl.when(j == 0)
    def _init():
        diff = (row_i - col_i).astype(F32)
        for h in heads:
            dm_s[h] = jnp.where(diff >= 0, jnp.exp(lgs[h] * jnp.maximum(diff, 0.0)), 0.0)
        pool_ext[...] = jnp.zeros(pool_ext.shape, F32)
        sc_ext[...] = jnp.zeros(sc_ext.shape, F32)
        sh_ext[...] = jnp.zeros(sh_ext.shape, F32)
        if zero_init:
            ret_ref[...] = jnp.zeros(ret_ref.shape, F32)
            wkv_ref[...] = jnp.zeros(wkv_ref.shape, F32)
        else:
            pool_ext[:, POOL_HIST - POOL_PAST:POOL_HIST] = pool0_ref[...]
            sc_ext[:, HIST - 2:HIST] = sc0_ref[...]
            sh_ext[:, HIST - 1:HIST] = sh0_ref[...]
            ret_ref[...] = ret0_ref[...]
            wkv_ref[...] = wkv0_ref[...]

    def vec(row):
        return vecs_ref[row:row + 1, :]

    x = _read_rows(x_ref)
    hn = _rmsnorm(x, g_ref[...])
    for lb in seqs:
        h_s[rows_of(lb)] = hn[rows_of(lb)] * (1.0 + _mod_row(mod_ref, 1, lb)) + _mod_row(mod_ref, 0, lb)
    z_s[...] = _wdot(h_s[...], win_ref[...])

    def gated_proj(n):
        proj = _wdot(y_s[:, n * MIX_W:(n + 1) * MIX_W], wbr_ref[n])
        return _sigmoid(z_s[:, COL_GATE + n * D_MODEL:COL_GATE + (n + 1) * D_MODEL]) * proj

    rows_all = bb * tc
    grp = jnp.right_shift(lane, 6)
    win = jnp.where(grp == 0, 2, jnp.where(grp == 1, 4, jnp.where(grp == 2, 8, 16)))
    brow = lax.broadcasted_iota(jnp.int32, (rows_all, 1), 0)
    cnt = jnp.minimum(win, pos0 + j * tc + jnp.bitwise_and(brow, tc - 1) + 1).astype(F32)
    first_half = jnp.bitwise_and(lane, HEAD_D - 1) < HEAD_D // 2
    cos = cos_ref[...]
    sin = sin_ref[...]

    def rope(t):
        swapped = jnp.where(first_half, pltpu.roll(t, MIX_W - HEAD_D // 2, 1), pltpu.roll(t, HEAD_D // 2, 1))
        return t * cos + swapped * sin

    si = lax.broadcasted_iota(jnp.int32, (MIX_W, MIX_W), 0)
    sj = lax.broadcasted_iota(jnp.int32, (MIX_W, MIX_W), 1)
    head_ones = (jnp.right_shift(si, 6) == jnp.right_shift(sj, 6)).astype(BF16)

    def split_dot(ones, t):
        hi = t.astype(BF16)
        lo = (t - hi.astype(F32)).astype(BF16)
        return jnp.dot(ones, hi, preferred_element_type=F32) + jnp.dot(ones, lo, preferred_element_type=F32)

    def head_sum(t):
        hi = t.astype(BF16)
        lo = (t - hi.astype(F32)).astype(BF16)
        return (jnp.dot(hi, head_ones, preferred_element_type=F32)
                + jnp.dot(lo, head_ones, preferred_element_type=F32))

    c = min(RWKV_CHUNK, tc)
    nsub = tc // c
    merged = c % SUBLANES == 0
    ci = lax.broadcasted_iota(jnp.int32, (c, c), 0)
    cj = lax.broadcasted_iota(jnp.int32, (c, c), 1)
    strict = ci > cj
    incl = ci >= cj
    ti = lax.broadcasted_iota(jnp.int32, (rows_all, rows_all), 0)
    tj = lax.broadcasted_iota(jnp.int32, (rows_all, rows_all), 1)
    shift_c = int(math.log2(c))
    tri = ((ti >= tj) & (jnp.right_shift(ti, shift_c) == jnp.right_shift(tj, shift_c))).astype(BF16)

    for lb in seqs:
        rows = rows_of(lb)
        pool_ext[lb, POOL_HIST:POOL_HIST + tc] = z_s[rows, 0:MIX_W]
        e = pool_ext[lb]
        s2 = e + pltpu.roll(e, 1, 0)
        s4 = s2 + pltpu.roll(s2, 2, 0)
        s8 = s4 + pltpu.roll(s4, 4, 0)
        s16 = s8 + pltpu.roll(s8, 8, 0)
        wsum = jnp.where(grp == 0, s2, jnp.where(grp == 1, s4, jnp.where(grp == 2, s8, s16)))
        wsum_s[rows] = wsum[POOL_HIST:POOL_HIST + tc]
        new_pool = pool_ext[lb, tc + POOL_HIST - POOL_PAST:tc + POOL_HIST]
        pool_ext[lb, POOL_HIST - POOL_PAST:POOL_HIST] = new_pool
        pool_ref[lb] = new_pool

        sc_ext[lb, HIST:HIST + tc] = (z_s[rows, COL_SC + 2 * MIX_W:COL_SC + 3 * MIX_W]
                                      * z_s[rows, COL_SC:COL_SC + MIX_W])
        e = sc_ext[lb]
        conv = vec(V_SC2) * e + vec(V_SC1) * pltpu.roll(e, 1, 0) + vec(V_SC0) * pltpu.roll(e, 2, 0)
        conv_s[rows] = conv[HIST:HIST + tc]
        new_sc = sc_ext[lb, HIST + tc - 2:HIST + tc]
        sc_ext[lb, HIST - 2:HIST] = new_sc
        sc_ref[lb] = new_sc

        sh_ext[lb, HIST:HIST + tc] = z_s[rows, COL_RWKV:COL_GATE]
        prev_s[rows] = pltpu.roll(sh_ext[lb], 1, 0)[HIST:HIST + tc]
        new_sh = sh_ext[lb, HIST + tc - 1:HIST + tc]
        sh_ext[lb, HIST - 1:HIST] = new_sh
        sh_ref[lb] = new_sh

    u = z_s[:, 0:MIX_W]
    y_s[:, 0:MIX_W] = _wdot(wsum_s[...] / cnt - u, poolw_ref[...]) * vec(V_POOL_SCALE)
    y_s[:, 2 * MIX_W:3 * MIX_W] = z_s[:, COL_SC + MIX_W:COL_SC + 2 * MIX_W] * conv_s[...]

    q_all = rope(z_s[:, COL_RET:COL_RET + MIX_W])
    k_all = rope(z_s[:, COL_RET + MIX_W:COL_RET + 2 * MIX_W]) * (HEAD_D ** -0.5)
    v_all = z_s[:, COL_RET + 2 * MIX_W:COL_RET + 3 * MIX_W]

    zz = z_s[:, COL_RWKV:COL_GATE]
    zs = zz + (prev_s[...] - zz) * mu_ref[...]
    r = zs[:, 0:MIX_W]
    kx = zs[:, MIX_W:2 * MIX_W]
    vx = zs[:, 2 * MIX_W:3 * MIX_W]
    o3 = 3 * MIX_W
    wl = zs[:, o3:o3 + LORA_W]
    al = zs[:, o3 + LORA_W:o3 + LORA_W + LORA_A]
    gl = zs[:, o3 + LORA_W + LORA_A:]
    wlog = -_softplus(-(vec(V_W0) + _bdot(jnp.tanh(wl), wlora_ref[...]))) - 0.5
    logw = -jnp.exp(wlog)
    asig = _sigmoid(vec(V_A0) + _bdot(al, alora_ref[...]))
    g_out = _bdot(_sigmoid(gl), glora_ref[...])
    k2 = kx * (1.0 + (asig - 1.0) * vec(V_KA))
    kk_raw = kx * vec(V_KK)
    kk = kk_raw * lax.rsqrt(jnp.maximum(head_sum(kk_raw * kk_raw), L2_EPS * L2_EPS))
    cum = split_dot(tri, logw)
    p_in = jnp.exp(cum)
    p_inv = jnp.exp(-cum)
    rw = dict(a=-kk * jnp.exp(cum - logw), b=kk * asig * p_inv, k=k2 * p_inv, r=r * p_in, v=vx, p_in=p_in)

    pairs = [(lb, h) for lb in seqs for h in heads]
    qs = [q_all[rows_of(lb), hsl[h]] for lb, h in pairs]
    ks = [k_all[rows_of(lb), hsl[h]] for lb, h in pairs]
    vs = [v_all[rows_of(lb), hsl[h]] for lb, h in pairs]
    states = [ret_ref[lb, h] for lb, h in pairs]
    np_ = range(len(pairs))
    scores = [_bdot_nt(qs[i], ks[i]) for i in np_]
    cross = [_bdot(qs[i] * jnp.exp(lgs[pairs[i][1]] * (row_f + 1.0)), states[i]) for i in np_]
    kv = [_bdot_tn(ks[i] * jnp.exp(lgs[pairs[i][1]] * (tc - 1.0 - row_f)), vs[i]) for i in np_]
    outs = [_bdot(scores[i] * dm_s[pairs[i][1]], vs[i]) + cross[i] for i in np_]
    for i, (lb, h) in enumerate(pairs):
        ret_ref[lb, h] = states[i] * math.exp(lgs[h] * tc) + kv[i]
        y_s[rows_of(lb), MIX_W + h * HEAD_D:MIX_W + (h + 1) * HEAD_D] = outs[i]
    o_all = y_s[:, MIX_W:2 * MIX_W]
    o_all = o_all * lax.rsqrt(head_sum(o_all * o_all) * (1.0 / HEAD_D) + GN_EPS)
    y_s[:, MIX_W:2 * MIX_W] = o_all * _silu(z_s[:, COL_RET + 3 * MIX_W:COL_RET + 4 * MIX_W])

    blocks = [(lb, sub, h) for lb in seqs for sub in range(nsub) for h in heads]

    def blk(name):
        return [rw[name][lb * tc + sub * c:lb * tc + (sub + 1) * c, hsl[h]] for lb, sub, h in blocks]

    a_l, b_l, k_l, r_l, v_l = blk("a"), blk("b"), blk("k"), blk("r"), blk("v")
    nb = range(len(blocks))
    if merged:
        bk_l = [jnp.concatenate([b_l[i], k_l[i]], axis=0) for i in nb]
        gram = [_bdot_nt(jnp.concatenate([a_l[i], r_l[i]], axis=0), bk_l[i]) for i in nb]
        g_ab, g_ak = [g[:c, :c] for g in gram], [g[:c, c:] for g in gram]
        g_rb, g_rk = [g[c:, :c] for g in gram], [g[c:, c:] for g in gram]
    else:
        g_ab = [_bdot_nt(a_l[i], b_l[i]) for i in nb]
        g_ak = [_bdot_nt(a_l[i], k_l[i]) for i in nb]
        g_rb = [_bdot_nt(r_l[i], b_l[i]) for i in nb]
        g_rk = [_bdot_nt(r_l[i], k_l[i]) for i in nb]
    l_ab = [jnp.where(strict, g, 0.0) for g in g_ab]
    l_ak = [jnp.where(strict, g, 0.0) for g in g_ak]
    m_rb = [jnp.where(incl, g, 0.0) for g in g_rb]
    m_rk = [jnp.where(incl, g, 0.0) for g in g_rk]
    n_inv = _tri_inverse_minus_eye(l_ab, c)
    if merged:
        lv = [_bdot(jnp.concatenate([l_ak[i], m_rk[i]], axis=0), v_l[i]) for i in nb]
        lakv, mv = [t[:c] for t in lv], [t[c:] for t in lv]
    else:
        lakv = [_bdot(l_ak[i], v_l[i]) for i in nb]
        mv = [_bdot(m_rk[i], v_l[i]) for i in nb]
    nx = [_bdot(n_inv[i], jnp.concatenate([lakv[i], a_l[i]], axis=1)) for i in nb]
    tlv = [lakv[i] + nx[i][:, :HEAD_D] for i in nb]
    ta = [a_l[i] + nx[i][:, HEAD_D:] for i in nb]

    wkv_states = [wkv_ref[lb, h] for lb, h in pairs]
    for sub in range(nsub):
        ids = [(lb * nsub + sub) * HEADS + h for lb, h in pairs]
        p_end = [rw["p_in"][lb * tc + (sub + 1) * c - 1:lb * tc + (sub + 1) * c, hsl[h]] for lb, h in pairs]
        u_mat = [_bdot_nt(ta[i], wkv_states[n]) + tlv[i] for n, i in enumerate(ids)]
        y_st = [_bdot_nt(r_l[i], wkv_states[n]) for n, i in enumerate(ids)]
        if merged:
            upd = [_bdot_tn(jnp.concatenate([u_mat[n], v_l[i]], axis=0), bk_l[i] * p_end[n])
                   for n, i in enumerate(ids)]
        else:
            upd = [_bdot_tn(u_mat[n], b_l[i] * p_end[n]) + _bdot_tn(v_l[i], k_l[i] * p_end[n])
                   for n, i in enumerate(ids)]
        y_u = [_bdot(m_rb[i], u_mat[n]) for n, i in enumerate(ids)]
        for n, i in enumerate(ids):
            lb, h = pairs[n]
            ywkv[lb * tc + sub * c:lb * tc + (sub + 1) * c, hsl[h]] = y_st[n] + y_u[n] + mv[i]
        wkv_states = [wkv_states[n] * p_end[n] + upd[n] for n in np_]
    for n, (lb, h) in enumerate(pairs):
        wkv_ref[lb, h] = wkv_states[n]

    y_all = ywkv[...]
    cen = y_all - head_sum(y_all) * (1.0 / HEAD_D)
    var = head_sum(cen * cen) * (1.0 / HEAD_D)
    yn = cen * lax.rsqrt(var + RWKV_LN_EPS) * vec(V_LNG) + vec(V_LNB)
    bonus = head_sum(r * k2 * vec(V_RK)) * vx
    y_s[:, 3 * MIX_W:4 * MIX_W] = (yn + bonus) * g_out

    out = _wdot(gated_proj(0) + gated_proj(1) + gated_proj(2) + gated_proj(3), wout_ref[...])
    for lb in seqs:
        _write_seq_rows(x1_ref, lb, tc, x[rows_of(lb)] + _mod_row(mod_ref, 2, lb) * out[rows_of(lb)])


def _layer_block_spec(arr, bb, layer):
    shape = arr.shape[2:]
    zeros = (0,) * len(shape)
    return pl.BlockSpec((None, bb) + shape, lambda i, j: (layer, i) + zeros)


def _resident_spec(arr, layer=None):
    if layer is None:
        zeros = (0,) * arr.ndim
        return pl.BlockSpec(arr.shape, lambda i, j: zeros, pipeline_mode=pl.Buffered(1))
    zeros = (0,) * (arr.ndim - 1)
    return pl.BlockSpec((None,) + arr.shape[1:], lambda i, j: (layer,) + zeros, pipeline_mode=pl.Buffered(1))


def _rows_operand(x2d, batch, t, bb, tc):
    d = x2d.shape[1]
    if tc == t:
        return x2d, pl.BlockSpec((bb * tc, d), lambda i, j: (i, 0))
    return x2d.reshape(batch, t, d), pl.BlockSpec((bb, tc, d), lambda i, j: (i, j, 0))


def _params():
    return pltpu.CompilerParams(dimension_semantics=("parallel", "arbitrary"), vmem_limit_bytes=VMEM_LIMIT)


def _blocking(batch, t, split=1):
    if t >= 256:
        return (split, 256 // split) if batch % split == 0 else (1, 256)
    return min(batch, 64 // t), t


def _mix_call(x2d, mod, g1, w_in, cos_t, sin_t, states, poolw, vecs, mu, wlora, alora, glora, w_br, w_out,
              batch, t, pos0, layer, zero_init):
    bb, tc = _blocking(batch, t, MIX_SPLIT)
    nt = t // tc
    tp = _round_up(tc, SUBLANES)
    d = x2d.shape[1]
    out_structs = [jax.ShapeDtypeStruct(st.shape, F32) for st in states]
    state_specs = [_layer_block_spec(st, bb, layer) for st in states]
    weights = (poolw, vecs, mu, wlora, alora, glora, w_br, w_out)
    mod, mod_spec = _mod_operand(mod, bb)
    if zero_init:
        operands = (*weights, *states)
        operand_specs = [_resident_spec(a, layer) for a in weights] + [pl.BlockSpec(memory_space=pl.ANY)] * 5
        first_state = 6 + len(weights)
    else:
        operands = (*states, *weights)
        operand_specs = state_specs + [_resident_spec(a, layer) for a in weights]
        first_state = 6
    x_op, x_spec = _rows_operand(x2d, batch, t, bb, tc)

    def block_table(tab):
        return jnp.tile(tab.reshape(nt, tc, MIX_W), (1, bb, 1)).reshape(nt * bb * tc, MIX_W)

    outs = pl.pallas_call(
        functools.partial(_mix_kernel, bb=bb, tc=tc, pos0=pos0, zero_init=zero_init),
        grid=(batch // bb, nt),
        in_specs=[x_spec, mod_spec,
                  _resident_spec(g1, layer), _resident_spec(w_in, layer),
                  pl.BlockSpec((bb * tc, MIX_W), lambda i, j: (j, 0)),
                  pl.BlockSpec((bb * tc, MIX_W), lambda i, j: (j, 0))]
        + operand_specs,
        out_specs=[x_spec] + state_specs,
        out_shape=[jax.ShapeDtypeStruct(x_op.shape, F32)] + out_structs,
        input_output_aliases={first_state + k: 1 + k for k in range(5)},
        scratch_shapes=[pltpu.VMEM((bb * tc, d), F32),
                        pltpu.VMEM((bb * tc, IN_COLS), F32),
                        pltpu.VMEM((bb * tc, d), F32),
                        pltpu.VMEM((HEADS, tc, tc), F32),
                        pltpu.VMEM((bb, POOL_HIST + tp, MIX_W), F32),
                        pltpu.VMEM((bb, HIST + tp, MIX_W), F32),
                        pltpu.VMEM((bb, HIST + tp, d), F32),
                        pltpu.VMEM((bb * tc, MIX_W), F32),
                        pltpu.VMEM((bb * tc, MIX_W), F32),
                        pltpu.VMEM((bb * tc, MIX_W), F32),
                        pltpu.VMEM((bb * tc, COL_GATE - COL_RWKV), F32)],
        compiler_params=_params(),
        name="mix",
    )(x_op, mod, g1, w_in, block_table(cos_t), block_table(sin_t), *operands)
    return outs[0].reshape(x2d.shape), tuple(outs[1:])


def _ffn_kernel(*refs, bb, tc, final, zero_init):
    refs = list(refs)
    x1_ref, mod_ref, g_ref, wup_ref, ffnw_ref, wdown_ref, fg_ref = refs[:7]
    del refs[:7]
    st0_ref = refs.pop(0)
    o_ref, st_ref, h_s, ext, act_s = refs
    j = pl.program_id(1)
    seqs = range(bb)

    def rows_of(lb):
        return slice(lb * tc, (lb + 1) * tc)

    @pl.when(j == 0)
    def _init():
        ext[...] = jnp.zeros(ext.shape, F32)
        if not zero_init:
            ext[:, HIST - 2:HIST] = st0_ref[...]

    x1 = _read_rows(x1_ref)
    hn = _rmsnorm(x1, g_ref[...])
    for lb in seqs:
        h_s[rows_of(lb)] = hn[rows_of(lb)] * (1.0 + _mod_row(mod_ref, 4, lb)) + _mod_row(mod_ref, 3, lb)
    if bb == 1:
        ext[0, HIST:HIST + tc] = _wdot(h_s[...], wup_ref[...])
    else:
        up = _wdot(h_s[...], wup_ref[...])
        for lb in seqs:
            ext[lb, HIST:HIST + tc] = up[rows_of(lb)]

    for lb in seqs:
        def conv_cols(lo, width):
            e = ext[lb, :, lo:lo + width]
            w = ffnw_ref[:, lo:lo + width]
            y = w[2:3] * e + w[1:2] * pltpu.roll(e, 1, 0) + w[0:1] * pltpu.roll(e, 2, 0)
            return y[HIST:HIST + tc]

        for lo in range(0, D_FF, CONV_COLS):
            width = min(CONV_COLS, D_FF - lo)
            act_s[rows_of(lb), lo:lo + width] = _silu(conv_cols(lo, width)) * conv_cols(D_FF + lo, width)
        new_st = ext[lb, HIST + tc - 2:HIST + tc]
        ext[lb, HIST - 2:HIST] = new_st
        st_ref[lb] = new_st

    dn = _wdot(act_s[...], wdown_ref[...])
    for lb in seqs:
        _write_seq_rows(o_ref, lb, tc, x1[rows_of(lb)] + _mod_row(mod_ref, 5, lb) * dn[rows_of(lb)])
    if final:
        o_ref[...] = _rmsnorm(o_ref[...], fg_ref[...])


def _ffn_call(x1, mod, g2, w_up, ffn_w, w_down, final_g, state, batch, t, final, layer, zero_init):
    bb, tc = _blocking(batch, t)
    nt = t // tc
    tp = _round_up(tc, SUBLANES)
    d = x1.shape[1]
    st_struct = jax.ShapeDtypeStruct(state.shape, F32)
    st_spec = _layer_block_spec(state, bb, layer)
    mod, mod_spec = _mod_operand(mod, bb)
    x_op, x_spec = _rows_operand(x1, batch, t, bb, tc)
    out, new_state = pl.pallas_call(
        functools.partial(_ffn_kernel, bb=bb, tc=tc, final=final, zero_init=zero_init),
        grid=(batch // bb, nt),
        in_specs=[x_spec, mod_spec,
                  _resident_spec(g2, layer), _resident_spec(w_up, layer), _resident_spec(ffn_w, layer),
                  _resident_spec(w_down, layer), _resident_spec(final_g)]
        + [pl.BlockSpec(memory_space=pl.ANY) if zero_init else st_spec],
        out_specs=[x_spec, st_spec],
        out_shape=[jax.ShapeDtypeStruct(x_op.shape, F32), st_struct],
        input_output_aliases={7: 1},
        scratch_shapes=[pltpu.VMEM((bb * tc, d), F32),
                        pltpu.VMEM((bb, HIST + tp, 2 * D_FF), F32),
                        pltpu.VMEM((bb * tc, D_FF), F32)],
        compiler_params=_params(),
        name="ffn",
    )(x_op, mod, g2, w_up, ffn_w, w_down, final_g, state)
    return out.reshape(x1.shape), new_state


def _rope_tables(t, pos0):
    half = HEAD_D // 2
    inv = ROPE_BASE ** (-jnp.arange(half, dtype=F32) / half)
    pos = pos0 + jnp.arange(t, dtype=jnp.int32)
    ang = pos.astype(F32)[:, None] * inv[None, :]
    cos, sin = jnp.cos(ang), jnp.sin(ang)
    cos_t = jnp.tile(jnp.concatenate([cos, cos], axis=-1), (1, HEADS))
    sin_t = jnp.tile(jnp.concatenate([-sin, sin], axis=-1), (1, HEADS))
    return cos_t, sin_t


def _block_diag(pool_w):
    groups, gw, _ = pool_w.shape
    out = jnp.zeros((groups * gw, groups * gw), pool_w.dtype)
    for g in range(groups):
        out = out.at[g * gw:(g + 1) * gw, g * gw:(g + 1) * gw].set(pool_w[g])
    return out


def kernel(x_prompt, x_sample, c_prompt, c_sample, state_pool, state_ret, state_sconv, state_shift, state_wkv, state_ffn, w_ada, b_ada, norm1_g, norm2_g, w_in, pool_w, pool_scale, sc_w, rw_mu, rw_w0, rw_w_lora, rw_a0, rw_a_lora, rw_g_lora, rw_k_k, rw_k_a, rw_r_k, rw_ln_g, rw_ln_b, w_br, w_out, w_up, ffn_w, w_down, final_g):
    depth = w_in.shape[0]
    bp, tp_, d = x_prompt.shape
    bs, ts, _ = x_sample.shape

    mods = _mod_call(jnp.concatenate([c_prompt, c_sample], axis=0), w_ada, b_ada)
    cos_p, sin_p = _rope_tables(tp_, 0)
    cos_s, sin_s = _rope_tables(ts, PAST_LEN)
    final_g2 = final_g.reshape(1, d)

    xp = x_prompt.reshape(bp * tp_, d)
    xs = x_sample.reshape(bs * ts, d)
    st_s = (state_pool, state_ret, state_sconv, state_shift.reshape(depth, bs, 1, d), state_wkv)
    st_p = tuple(jnp.zeros((depth, bp) + st.shape[2:], F32) for st in st_s)
    ffn_s = state_ffn
    ffn_p = jnp.zeros((depth, bp) + state_ffn.shape[2:], F32)
    vec_rows = [pool_scale, sc_w[:, 0], sc_w[:, 1], sc_w[:, 2], rw_w0, rw_a0, rw_k_k, rw_k_a,
                rw_r_k.reshape(depth, MIX_W), rw_ln_g, rw_ln_b]
    vecs = jnp.concatenate([jnp.stack(vec_rows, axis=1),
                            jnp.zeros((depth, N_VECS - len(vec_rows), MIX_W), F32)], axis=1)
    mix_w = (norm1_g.reshape(depth, 1, d), w_in.astype(BF16))
    seq_w = (jnp.stack([_block_diag(pool_w[l]) for l in range(depth)]).astype(BF16), vecs,
             rw_mu.reshape(depth, 1, d), rw_w_lora, rw_a_lora, rw_g_lora, w_br.astype(BF16), w_out.astype(BF16))
    ffn_ws = (norm2_g.reshape(depth, 1, d), w_up.astype(BF16), ffn_w, w_down.astype(BF16), final_g2)
    for l in range(depth):
        mod_p, mod_s = mods[l, :, :bp], mods[l, :, bp:]
        final = l == depth - 1

        x1, st_p = _mix_call(xp, mod_p, *mix_w, cos_p, sin_p, st_p, *seq_w, batch=bp, t=tp_, pos0=0,
                             layer=l, zero_init=True)
        xp, ffn_p = _ffn_call(x1, mod_p, *ffn_ws, ffn_p, batch=bp, t=tp_, final=final, layer=l, zero_init=True)
        x1, st_s = _mix_call(xs, mod_s, *mix_w, cos_s, sin_s, st_s, *seq_w, batch=bs, t=ts, pos0=PAST_LEN,
                             layer=l, zero_init=False)
        xs, ffn_s = _ffn_call(x1, mod_s, *ffn_ws, ffn_s, batch=bs, t=ts, final=final, layer=l, zero_init=False)

    def finish(st, ffn_st, batch):
        pool, ret, sc, sh, wkv = st
        return (pool, ret, sc, sh.reshape(depth, batch, d), wkv, ffn_st)

    return (xp.reshape(bp, tp_, d), xs.reshape(bs, ts, d), *finish(st_p, ffn_p, bp), *finish(st_s, ffn_s, bs))
```

```python
import functools
import math

import jax
import jax.numpy as jnp
from jax import lax
from jax.experimental import pallas as pl
from jax.experimental.pallas import tpu as pltpu

F32 = jnp.float32
BF16 = jnp.bfloat16

D_MODEL = 1024
MIX_W = 256
HEADS = 4
HEAD_D = 64
POOL_PAST = 15
ROPE_BASE = 10000.0
LORA_W = 64
LORA_A = 64
LORA_G = 128
D_FF = 2816
PAST_LEN = 16384
NORM_EPS = 1e-6
GN_EPS = 1e-6
RWKV_LN_EPS = 64e-5
L2_EPS = 1e-12

COL_RET = MIX_W
COL_SC = COL_RET + 4 * MIX_W
COL_RWKV = COL_SC + 3 * MIX_W
COL_GATE = COL_RWKV + 3 * MIX_W + LORA_W + LORA_A + LORA_G
IN_COLS = COL_GATE + 4 * D_MODEL

SUBLANES = 8
VMEM_LIMIT = 56 * 1024 * 1024
RWKV_CHUNK = 64
INV_BASE = 16
POOL_HIST = 16
HIST = SUBLANES
CONV_COLS = 512
MIX_SPLIT = 8
MIX_ROWS = 512

(V_POOL_SCALE, V_SC0, V_SC1, V_SC2, V_W0, V_A0, V_KK, V_KA, V_RK, V_LNG, V_LNB) = range(11)
N_VECS = 16


def _bdot(a, b):
    return jnp.dot(a.astype(BF16), b.astype(BF16), preferred_element_type=F32)


def _bdot_nt(a, b):
    return lax.dot_general(a.astype(BF16), b.astype(BF16), (((1,), (1,)), ((), ())), preferred_element_type=F32)


def _bdot_tn(a, b):
    return lax.dot_general(a.astype(BF16), b.astype(BF16), (((0,), (0,)), ((), ())), preferred_element_type=F32)


def _wdot(a, w_bf16):
    return jnp.dot(a.astype(BF16), w_bf16, preferred_element_type=F32)


def _sigmoid(x):
    return 1.0 / (1.0 + jnp.exp(-x))


def _silu(x):
    half = 0.5 * x
    return half + half * jnp.tanh(half)


def _softplus(x):
    return jnp.maximum(x, 0.0) + jnp.log(1.0 + jnp.exp(-jnp.abs(x)))


def _rmsnorm(x, g):
    return x * lax.rsqrt(jnp.mean(x * x, axis=-1, keepdims=True) + NORM_EPS) * g


def _round_up(n, m):
    return (n + m - 1) // m * m


def _mod_row(mod_ref, k, lb):
    if len(mod_ref.shape) == 4:
        return mod_ref[k, lb]
    return mod_ref[k, lb:lb + 1, :]


def _mod_operand(mod, bb):
    six, _, d = mod.shape
    if bb % SUBLANES:
        return mod[:, :, None, :], pl.BlockSpec((six, bb, 1, d), lambda i, j: (0, i, 0, 0))
    return mod, pl.BlockSpec((six, bb, d), lambda i, j: (0, i, 0))


def _read_rows(ref):
    val = ref[...]
    return val.reshape(-1, val.shape[-1]) if val.ndim == 3 else val


def _write_seq_rows(ref, lb, tc, val):
    if len(ref.shape) == 3:
        ref[lb] = val
    else:
        ref[lb * tc:(lb + 1) * tc] = val


def _mod_kernel(c_ref, w_ref, b_ref, o_ref):
    o_ref[0, 0] = _wdot(_silu(c_ref[...]), w_ref[0].astype(BF16)) + b_ref[0]


def _mod_call(c_all, w_ada, b_ada):
    depth, d, cols = w_ada.shape
    rows = c_all.shape[0]
    cb = D_MODEL
    return pl.pallas_call(
        _mod_kernel,
        grid=(depth, cols // cb),
        in_specs=[pl.BlockSpec((rows, d), lambda l, c: (0, 0)),
                  pl.BlockSpec((1, d, cb), lambda l, c: (l, 0, c)),
                  pl.BlockSpec((1, 1, cb), lambda l, c: (l, 0, c))],
        out_specs=pl.BlockSpec((1, 1, rows, cb), lambda l, c: (l, c, 0, 0)),
        out_shape=jax.ShapeDtypeStruct((depth, cols // cb, rows, cb), F32),
        name="adaln_mod",
    )(c_all, w_ada, b_ada.reshape(depth, 1, cols))


def _tri_inverse_minus_eye(lowers, n):
    i = lax.broadcasted_iota(jnp.int32, (n, n), 0)
    j = lax.broadcasted_iota(jnp.int32, (n, n), 1)

    def same_block(size):
        sh = int(math.log2(size))
        return jnp.right_shift(i, sh) == jnp.right_shift(j, sh)

    base = min(n, INV_BASE)
    if base < n:
        diag_mask = same_block(base)
        xs = [jnp.where(diag_mask, low, 0.0) for low in lowers]
    else:
        xs = list(lowers)
    powers, p = xs, 1
    while 2 * p < base:
        powers = [_bdot(pw, pw) for pw in powers]
        prods = [_bdot(x, pw) for x, pw in zip(xs, powers)]
        xs = [x + pw + pr for x, pw, pr in zip(xs, powers, prods)]
        p *= 2
    size = base
    while size < n:
        off_mask = same_block(2 * size) & jnp.logical_not(same_block(size))
        offs = [jnp.where(off_mask, low, 0.0) for low in lowers]
        lefts = [off + _bdot(x, off) for x, off in zip(xs, offs)]
        xs = [x + left + _bdot(left, x) for x, left in zip(xs, lefts)]
        size *= 2
    return xs


def _mix_kernel(*refs, bb, tc, pos0, zero_init):
    refs = list(refs)
    x_ref, mod_ref, g_ref, win_ref, cos_ref, sin_ref = refs[:6]
    del refs[:6]
    if not zero_init:
        pool0_ref, ret0_ref, sc0_ref, sh0_ref, wkv0_ref = refs[:5]
        del refs[:5]
    poolw_ref, vecs_ref, mu_ref, wlora_ref, alora_ref, glora_ref, wbr_ref, wout_ref = refs[:8]
    del refs[:8 + (5 if zero_init else 0)]
    x1_ref, pool_ref, ret_ref, sc_ref, sh_ref, wkv_ref = refs[:6]
    h_s, z_s, y_s, dm_s, pool_ext, sc_ext, sh_ext, ywkv, wsum_s, conv_s, prev_s = refs[6:]
    j = pl.program_id(1)
    seqs = range(bb)
    heads = range(HEADS)
    hsl = [slice(h * HEAD_D, (h + 1) * HEAD_D) for h in heads]

    def rows_of(lb):
        return slice(lb * tc, (lb + 1) * tc)

    lane = lax.broadcasted_iota(jnp.int32, (1, MIX_W), 1)
    row_i = lax.broadcasted_iota(jnp.int32, (tc, 1), 0)
    col_i = lax.broadcasted_iota(jnp.int32, (1, tc), 1)
    row_f = row_i.astype(F32)
    lgs = [math.log1p(-(2.0 ** (-5.0 - h))) for h in heads]

    @pl.when(j == 0)
    def _init():
        diff = (row_i - col_i).astype(F32)
        for h in heads:
            dm_s[h] = jnp.where(diff >= 0, jnp.exp(lgs[h] * jnp.maximum(diff, 0.0)), 0.0)
        pool_ext[...] = jnp.zeros(pool_ext.shape, F32)
        sc_ext[...] = jnp.zeros(sc_ext.shape, F32)
        sh_ext[...] = jnp.zeros(sh_ext.shape, F32)
        if zero_init:
            ret_ref[...] = jnp.zeros(ret_ref.shape, F32)
            wkv_ref[...] = jnp.zeros(wkv_ref.shape, F32)
        else:
            pool_ext[:, POOL_HIST - POOL_PAST:POOL_HIST] = pool0_ref[...]
            sc_ext[:, HIST - 2:HIST] = sc0_ref[...]
            sh_ext[:, HIST - 1:HIST] = sh0_ref[...]
            ret_ref[...] = ret0_ref[...]
            wkv_ref[...] = wkv0_ref[...]

    def vec(row):
        return vecs_ref[row:row + 1, :]

    x = _read_rows(x_ref)
    hn = _rmsnorm(x, g_ref[...])
    for lb in seqs:
        h_s[rows_of(lb)] = hn[rows_of(lb)] * (1.0 + _mod_row(mod_ref, 1, lb)) + _mod_row(mod_ref, 0, lb)
    z_s[...] = _wdot(h_s[...], win_ref[...])

    def gated_proj(n):
        proj = _wdot(y_s[:, n * MIX_W:(n + 1) * MIX_W], wbr_ref[n])
        return _sigmoid(z_s[:, COL_GATE + n * D_MODEL:COL_GATE + (n + 1) * D_MODEL]) * proj

    rows_all = bb * tc
    grp = jnp.right_shift(lane, 6)
    win = jnp.where(grp == 0, 2, jnp.where(grp == 1, 4, jnp.where(grp == 2, 8, 16)))
    brow = lax.broadcasted_iota(jnp.int32, (rows_all, 1), 0)
    cnt = jnp.minimum(win, pos0 + j * tc + jnp.bitwise_and(brow, tc - 1) + 1).astype(F32)
    first_half = jnp.bitwise_and(lane, HEAD_D - 1) < HEAD_D // 2
    cos = cos_ref[...]
    sin = sin_ref[...]

    def rope(t):
        swapped = jnp.where(first_half, pltpu.roll(t, MIX_W - HEAD_D // 2, 1), pltpu.roll(t, HEAD_D // 2, 1))
        return t * cos + swapped * sin

    si = lax.broadcasted_iota(jnp.int32, (MIX_W, MIX_W), 0)
    sj = lax.broadcasted_iota(jnp.int32, (MIX_W, MIX_W), 1)
    head_ones = (jnp.right_shift(si, 6) == jnp.right_shift(sj, 6)).astype(BF16)

    def split_dot(ones, t):
        hi = t.astype(BF16)
        lo = (t - hi.astype(F32)).astype(BF16)
        return jnp.dot(ones, hi, preferred_element_type=F32) + jnp.dot(ones, lo, preferred_element_type=F32)

    def head_sum(t):
        hi = t.astype(BF16)
        lo = (t - hi.astype(F32)).astype(BF16)
        return (jnp.dot(hi, head_ones, preferred_element_type=F32)
                + jnp.dot(lo, head_ones, preferred_element_type=F32))

    c = min(RWKV_CHUNK, tc)
    nsub = tc // c
    merged = c % SUBLANES == 0
    ci = lax.broadcasted_iota(jnp.int32, (c, c), 0)
    cj = lax.broadcasted_iota(jnp.int32, (c, c), 1)
    strict = ci > cj
    incl = ci >= cj
    ti = lax.broadcasted_iota(jnp.int32, (rows_all, rows_all), 0)
    tj = lax.broadcasted_iota(jnp.int32, (rows_all, rows_all), 1)
    shift_c = int(math.log2(c))
    tri = ((ti >= tj) & (jnp.right_shift(ti, shift_c) == jnp.right_shift(tj, shift_c))).astype(BF16)

    for lb in seqs:
        rows = rows_of(lb)
        pool_ext[lb, POOL_HIST:POOL_HIST + tc] = z_s[rows, 0:MIX_W]
        e = pool_ext[lb]
        s2 = e + pltpu.roll(e, 1, 0)
        s4 = s2 + pltpu.roll(s2, 2, 0)
        s8 = s4 + pltpu.roll(s4, 4, 0)
        s16 = s8 + pltpu.roll(s8, 8, 0)
        wsum = jnp.where(grp == 0, s2, jnp.where(grp == 1, s4, jnp.where(grp == 2, s8, s16)))
        wsum_s[rows] = wsum[POOL_HIST:POOL_HIST + tc]
        new_pool = pool_ext[lb, tc + POOL_HIST - POOL_PAST:tc + POOL_HIST]
        pool_ext[lb, POOL_HIST - POOL_PAST:POOL_HIST] = new_pool
        pool_ref[lb] = new_pool

        sc_ext[lb, HIST:HIST + tc] = (z_s[rows, COL_SC + 2 * MIX_W:COL_SC + 3 * MIX_W]
                                      * z_s[rows, COL_SC:COL_SC + MIX_W])
        e = sc_ext[lb]
        conv = vec(V_SC2) * e + vec(V_SC1) * pltpu.roll(e, 1, 0) + vec(V_SC0) * pltpu.roll(e, 2, 0)
        conv_s[rows] = conv[HIST:HIST + tc]
        new_sc = sc_ext[lb, HIST + tc - 2:HIST + tc]
        sc_ext[lb, HIST - 2:HIST] = new_sc
        sc_ref[lb] = new_sc

        sh_ext[lb, HIST:HIST + tc] = z_s[rows, COL_RWKV:COL_GATE]
        prev_s[rows] = pltpu.roll(sh_ext[lb], 1, 0)[HIST:HIST + tc]
        new_sh = sh_ext[lb, HIST + tc - 1:HIST + tc]
        sh_ext[lb, HIST - 1:HIST] = new_sh
        sh_ref[lb] = new_sh

    u = z_s[:, 0:MIX_W]
    y_s[:, 0:MIX_W] = _wdot(wsum_s[...] / cnt - u, poolw_ref[...]) * vec(V_POOL_SCALE)
    y_s[:, 2 * MIX_W:3 * MIX_W] = z_s[:, COL_SC + MIX_W:COL_SC + 2 * MIX_W] * conv_s[...]

    q_all = rope(z_s[:, COL_RET:COL_RET + MIX_W])
    k_all = rope(z_s[:, COL_RET + MIX_W:COL_RET + 2 * MIX_W]) * (HEAD_D ** -0.5)
    v_all = z_s[:, COL_RET + 2 * MIX_W:COL_RET + 3 * MIX_W]

    zz = z_s[:, COL_RWKV:COL_GATE]
    zs = zz + (prev_s[...] - zz) * mu_ref[...]
    r = zs[:, 0:MIX_W]
    kx = zs[:, MIX_W:2 * MIX_W]
    vx = zs[:, 2 * MIX_W:3 * MIX_W]
    o3 = 3 * MIX_W
    wl = zs[:, o3:o3 + LORA_W]
    al = zs[:, o3 + LORA_W:o3 + LORA_W + LORA_A]
    gl = zs[:, o3 + LORA_W + LORA_A:]
    wlog = -_softplus(-(vec(V_W0) + _bdot(jnp.tanh(wl), wlora_ref[...]))) - 0.5
    logw = -jnp.exp(wlog)
    asig = _sigmoid(vec(V_A0) + _bdot(al, alora_ref[...]))
    g_out = _bdot(_sigmoid(gl), glora_ref[...])
    k2 = kx * (1.0 + (asig - 1.0) * vec(V_KA))
    kk_raw = kx * vec(V_KK)
    kk = kk_raw * lax.rsqrt(jnp.maximum(head_sum(kk_raw * kk_raw), L2_EPS * L2_EPS))
    cum = split_dot(tri, logw)
    p_in = jnp.exp(cum)
    p_inv = jnp.exp(-cum)
    rw = dict(a=-kk * jnp.exp(cum - logw), b=kk * asig * p_inv, k=k2 * p_inv, r=r * p_in, v=vx, p_in=p_in)

    pairs = [(lb, h) for lb in seqs for h in heads]
    qs = [q_all[rows_of(lb), hsl[h]] for lb, h in pairs]
    ks = [k_all[rows_of(lb), hsl[h]] for lb, h in pairs]
    vs = [v_all[rows_of(lb), hsl[h]] for lb, h in pairs]
    states = [ret_ref[lb, h] for lb, h in pairs]
    np_ = range(len(pairs))
    scores = [_bdot_nt(qs[i], ks[i]) for i in np_]
    cross = [_bdot(qs[i] * jnp.exp(lgs[pairs[i][1]] * (row_f + 1.0)), states[i]) for i in np_]
    kv = [_bdot_tn(ks[i] * jnp.exp(lgs[pairs[i][1]] * (tc - 1.0 - row_f)), vs[i]) for i in np_]
    outs = [_bdot(scores[i] * dm_s[pairs[i][1]], vs[i]) + cross[i] for i in np_]
    for i, (lb, h) in enumerate(pairs):
        ret_ref[lb, h] = states[i] * math.exp(lgs[h] * tc) + kv[i]
        y_s[rows_of(lb), MIX_W + h * HEAD_D:MIX_W + (h + 1) * HEAD_D] = outs[i]
    o_all = y_s[:, MIX_W:2 * MIX_W]
    o_all = o_all * lax.rsqrt(head_sum(o_all * o_all) * (1.0 / HEAD_D) + GN_EPS)
    y_s[:, MIX_W:2 * MIX_W] = o_all * _silu(z_s[:, COL_RET + 3 * MIX_W:COL_RET + 4 * MIX_W])

    blocks = [(lb, sub, h) for lb in seqs for sub in range(nsub) for h in heads]

    def blk(name):
        return [rw[name][lb * tc + sub * c:lb * tc + (sub + 1) * c, hsl[h]] for lb, sub, h in blocks]

    a_l, b_l, k_l, r_l, v_l = blk("a"), blk("b"), blk("k"), blk("r"), blk("v")
    nb = range(len(blocks))
    if merged:
        bk_l = [jnp.concatenate([b_l[i], k_l[i]], axis=0) for i in nb]
        gram = [_bdot_nt(jnp.concatenate([a_l[i], r_l[i]], axis=0), bk_l[i]) for i in nb]
        g_ab, g_ak = [g[:c, :c] for g in gram], [g[:c, c:] for g in gram]
        g_rb, g_rk = [g[c:, :c] for g in gram], [g[c:, c:] for g in gram]
    else:
        g_ab = [_bdot_nt(a_l[i], b_l[i]) for i in nb]
        g_ak = [_bdot_nt(a_l[i], k_l[i]) for i in nb]
        g_rb = [_bdot_nt(r_l[i], b_l[i]) for i in nb]
        g_rk = [_bdot_nt(r_l[i], k_l[i]) for i in nb]
    l_ab = [jnp.where(strict, g, 0.0) for g in g_ab]
    l_ak = [jnp.where(strict, g, 0.0) for g in g_ak]
    m_rb = [jnp.where(incl, g, 0.0) for g in g_rb]
    m_rk = [jnp.where(incl, g, 0.0) for g in g_rk]
    n_inv = _tri_inverse_minus_eye(l_ab, c)
    if merged:
        lv = [_bdot(jnp.concatenate([l_ak[i], m_rk[i]], axis=0), v_l[i]) for i in nb]
        lakv, mv = [t[:c] for t in lv], [t[c:] for t in lv]
    else:
        lakv = [_bdot(l_ak[i], v_l[i]) for i in nb]
        mv = [_bdot(m_rk[i], v_l[i]) for i in nb]
    nx = [_bdot(n_inv[i], jnp.concatenate([lakv[i], a_l[i]], axis=1)) for i in nb]
    tlv = [lakv[i] + nx[i][:, :HEAD_D] for i in nb]
    ta = [a_l[i] + nx[i][:, HEAD_D:] for i in nb]

    wkv_states = [wkv_ref[lb, h] for lb, h in pairs]
    for sub in range(nsub):
        ids = [(lb * nsub + sub) * HEADS + h for lb, h in pairs]
        p_end = [rw["p_in"][lb * tc + (sub + 1) * c - 1:lb * tc + (sub + 1) * c, hsl[h]] for lb, h in pairs]
        u_mat = [_bdot_nt(ta[i], wkv_states[n]) + tlv[i] for n, i in enumerate(ids)]
        y_st = [_bdot_nt(r_l[i], wkv_states[n]) for n, i in enumerate(ids)]
        if merged:
            upd = [_bdot_tn(jnp.concatenate([u_mat[n], v_l[i]], axis=0), bk_l[i] * p_end[n])
                   for n, i in enumerate(ids)]
        else:
            upd = [_bdot_tn(u_mat[n], b_l[i] * p_end[n]) + _bdot_tn(v_l[i], k_l[i] * p_end[n])
                   for n, i in enumerate(ids)]
        y_u = [_bdot(m_rb[i], u_mat[n]) for n, i in enumerate(ids)]
        for n, i in enumerate(ids):
            lb, h = pairs[n]
            ywkv[lb * tc + sub * c:lb * tc + (sub + 1) * c, hsl[h]] = y_st[n] + y_u[n] + mv[i]
        wkv_states = [wkv_states[n] * p_end[n] + upd[n] for n in np_]
    for n, (lb, h) in enumerate(pairs):
        wkv_ref[lb, h] = wkv_states[n]

    y_all = ywkv[...]
    cen = y_all - head_sum(y_all) * (1.0 / HEAD_D)
    var = head_sum(cen * cen) * (1.0 / HEAD_D)
    yn = cen * lax.rsqrt(var + RWKV_LN_EPS) * vec(V_LNG) + vec(V_LNB)
    bonus = head_sum(r * k2 * vec(V_RK)) * vx
    y_s[:, 3 * MIX_W:4 * MIX_W] = (yn + bonus) * g_out

    out = _wdot(gated_proj(0) + gated_proj(1) + gated_proj(2) + gated_proj(3), wout_ref[...])
    for lb in seqs:
        _write_seq_rows(x1_ref, lb, tc, x[rows_of(lb)] + _mod_row(mod_ref, 2, lb) * out[rows_of(lb)])


def _layer_block_spec(arr, bb, layer):
    shape = arr.shape[2:]
    zeros = (0,) * len(shape)
    return pl.BlockSpec((None, bb) + shape, lambda i, j: (layer, i) + zeros)


def _resident_spec(arr, layer=None):
    if layer is None:
        zeros = (0,) * arr.ndim
        return pl.BlockSpec(arr.shape, lambda i, j: zeros, pipeline_mode=pl.Buffered(1))
    zeros = (0,) * (arr.ndim - 1)
    return pl.BlockSpec((None,) + arr.shape[1:], lambda i, j: (layer,) + zeros, pipeline_mode=pl.Buffered(1))


def _rows_operand(x2d, batch, t, bb, tc):
    d = x2d.shape[1]
    if tc == t:
        return x2d, pl.BlockSpec((bb * tc, d), lambda i, j: (i, 0))
    return x2d.reshape(batch, t, d), pl.BlockSpec((bb, tc, d), lambda i, j: (i, j, 0))


def _params():
    return pltpu.CompilerParams(dimension_semantics=("parallel", "arbitrary"), vmem_limit_bytes=VMEM_LIMIT)


def _blocking(batch, t, split=1, rows=256):
    if t >= 64:
        split = split if batch % split == 0 else 1
        return split, min(t, rows // split)
    return min(batch, 64 // t), t


def _mix_call(x2d, mod, g1, w_in, cos_t, sin_t, states, poolw, vecs, mu, wlora, alora, glora, w_br, w_out,
              batch, t, pos0, layer, zero_init):
    bb, tc = _blocking(batch, t, MIX_SPLIT, MIX_ROWS)
    nt = t // tc
    tp = _round_up(tc, SUBLANES)
    d = x2d.shape[1]
    out_structs = [jax.ShapeDtypeStruct(st.shape, F32) for st in states]
    state_specs = [_layer_block_spec(st, bb, layer) for st in states]
    weights = (poolw, vecs, mu, wlora, alora, glora, w_br, w_out)
    mod, mod_spec = _mod_operand(mod, bb)
    if zero_init:
        operands = (*weights, *states)
        operand_specs = [_resident_spec(a, layer) for a in weights] + [pl.BlockSpec(memory_space=pl.ANY)] * 5
        first_state = 6 + len(weights)
    else:
        operands = (*states, *weights)
        operand_specs = state_specs + [_resident_spec(a, layer) for a in weights]
        first_state = 6
    x_op, x_spec = _rows_operand(x2d, batch, t, bb, tc)

    def block_table(tab):
        return jnp.tile(tab.reshape(nt, tc, MIX_W), (1, bb, 1)).reshape(nt * bb * tc, MIX_W)

    outs = pl.pallas_call(
        functools.partial(_mix_kernel, bb=bb, tc=tc, pos0=pos0, zero_init=zero_init),
        grid=(batch // bb, nt),
        in_specs=[x_spec, mod_spec,
                  _resident_spec(g1, layer), _resident_spec(w_in, layer),
                  pl.BlockSpec((bb * tc, MIX_W), lambda i, j: (j, 0)),
                  pl.BlockSpec((bb * tc, MIX_W), lambda i, j: (j, 0))]
        + operand_specs,
        out_specs=[x_spec] + state_specs,
        out_shape=[jax.ShapeDtypeStruct(x_op.shape, F32)] + out_structs,
        input_output_aliases={first_state + k: 1 + k for k in range(5)},
        scratch_shapes=[pltpu.VMEM((bb * tc, d), F32),
                        pltpu.VMEM((bb * tc, IN_COLS), F32),
                        pltpu.VMEM((bb * tc, d), F32),
                        pltpu.VMEM((HEADS, tc, tc), F32),
                        pltpu.VMEM((bb, POOL_HIST + tp, MIX_W), F32),
                        pltpu.VMEM((bb, HIST + tp, MIX_W), F32),
                        pltpu.VMEM((bb, HIST + tp, d), F32),
                        pltpu.VMEM((bb * tc, MIX_W), F32),
                        pltpu.VMEM((bb * tc, MIX_W), F32),
                        pltpu.VMEM((bb * tc, MIX_W), F32),
                        pltpu.VMEM((bb * tc, COL_GATE - COL_RWKV), F32)],
        compiler_params=_params(),
        name="mix",
    )(x_op, mod, g1, w_in, block_table(cos_t), block_table(sin_t), *operands)
    return outs[0].reshape(x2d.shape), tuple(outs[1:])


def _ffn_kernel(*refs, bb, tc, final, zero_init):
    refs = list(refs)
    x1_ref, mod_ref, g_ref, wup_ref, ffnw_ref, wdown_ref, fg_ref = refs[:7]
    del refs[:7]
    st0_ref = refs.pop(0)
    o_ref, st_ref, h_s, ext, act_s = refs
    j = pl.program_id(1)
    seqs = range(bb)

    def rows_of(lb):
        return slice(lb * tc, (lb + 1) * tc)

    @pl.when(j == 0)
    def _init():
        ext[...] = jnp.zeros(ext.shape, F32)
        if not zero_init:
            ext[:, HIST - 2:HIST] = st0_ref[...]

    x1 = _read_rows(x1_ref)
    hn = _rmsnorm(x1, g_ref[...])
    for lb in seqs:
        h_s[rows_of(lb)] = hn[rows_of(lb)] * (1.0 + _mod_row(mod_ref, 4, lb)) + _mod_row(mod_ref, 3, lb)
    if bb == 1:
        ext[0, HIST:HIST + tc] = _wdot(h_s[...], wup_ref[...])
    else:
        up = _wdot(h_s[...], wup_ref[...])
        for lb in seqs:
            ext[lb, HIST:HIST + tc] = up[rows_of(lb)]

    for lb in seqs:
        def conv_cols(lo, width):
            e = ext[lb, :, lo:lo + width]
            w = ffnw_ref[:, lo:lo + width]
            y = w[2:3] * e + w[1:2] * pltpu.roll(e, 1, 0) + w[0:1] * pltpu.roll(e, 2, 0)
            return y[HIST:HIST + tc]

        for lo in range(0, D_FF, CONV_COLS):
            width = min(CONV_COLS, D_FF - lo)
            act_s[rows_of(lb), lo:lo + width] = _silu(conv_cols(lo, width)) * conv_cols(D_FF + lo, width)
        new_st = ext[lb, HIST + tc - 2:HIST + tc]
        ext[lb, HIST - 2:HIST] = new_st
        st_ref[lb] = new_st

    dn = _wdot(act_s[...], wdown_ref[...])
    for lb in seqs:
        _write_seq_rows(o_ref, lb, tc, x1[rows_of(lb)] + _mod_row(mod_ref, 5, lb) * dn[rows_of(lb)])
    if final:
        o_ref[...] = _rmsnorm(o_ref[...], fg_ref[...])


def _ffn_call(x1, mod, g2, w_up, ffn_w, w_down, final_g, state, batch, t, final, layer, zero_init):
    bb, tc = _blocking(batch, t)
    nt = t // tc
    tp = _round_up(tc, SUBLANES)
    d = x1.shape[1]
    st_struct = jax.ShapeDtypeStruct(state.shape, F32)
    st_spec = _layer_block_spec(state, bb, layer)
    mod, mod_spec = _mod_operand(mod, bb)
    x_op, x_spec = _rows_operand(x1, batch, t, bb, tc)
    out, new_state = pl.pallas_call(
        functools.partial(_ffn_kernel, bb=bb, tc=tc, final=final, zero_init=zero_init),
        grid=(batch // bb, nt),
        in_specs=[x_spec, mod_spec,
                  _resident_spec(g2, layer), _resident_spec(w_up, layer), _resident_spec(ffn_w, layer),
                  _resident_spec(w_down, layer), _resident_spec(final_g)]
        + [pl.BlockSpec(memory_space=pl.ANY) if zero_init else st_spec],
        out_specs=[x_spec, st_spec],
        out_shape=[jax.ShapeDtypeStruct(x_op.shape, F32), st_struct],
        input_output_aliases={7: 1},
        scratch_shapes=[pltpu.VMEM((bb * tc, d), F32),
                        pltpu.VMEM((bb, HIST + tp, 2 * D_FF), F32),
                        pltpu.VMEM((bb * tc, D_FF), F32)],
        compiler_params=_params(),
        name="ffn",
    )(x_op, mod, g2, w_up, ffn_w, w_down, final_g, state)
    return out.reshape(x1.shape), new_state


def _rope_tables(t, pos0):
    half = HEAD_D // 2
    inv = ROPE_BASE ** (-jnp.arange(half, dtype=F32) / half)
    pos = pos0 + jnp.arange(t, dtype=jnp.int32)
    ang = pos.astype(F32)[:, None] * inv[None, :]
    cos, sin = jnp.cos(ang), jnp.sin(ang)
    cos_t = jnp.tile(jnp.concatenate([cos, cos], axis=-1), (1, HEADS))
    sin_t = jnp.tile(jnp.concatenate([-sin, sin], axis=-1), (1, HEADS))
    return cos_t, sin_t


def _block_diag(pool_w):
    groups, gw, _ = pool_w.shape
    out = jnp.zeros((groups * gw, groups * gw), pool_w.dtype)
    for g in range(groups):
        out = out.at[g * gw:(g + 1) * gw, g * gw:(g + 1) * gw].set(pool_w[g])
    return out


def kernel(x_prompt, x_sample, c_prompt, c_sample, state_pool, state_ret, state_sconv, state_shift, state_wkv, state_ffn, w_ada, b_ada, norm1_g, norm2_g, w_in, pool_w, pool_scale, sc_w, rw_mu, rw_w0, rw_w_lora, rw_a0, rw_a_lora, rw_g_lora, rw_k_k, rw_k_a, rw_r_k, rw_ln_g, rw_ln_b, w_br, w_out, w_up, ffn_w, w_down, final_g):
    depth = w_in.shape[0]
    bp, tp_, d = x_prompt.shape
    bs, ts, _ = x_sample.shape

    mods = _mod_call(jnp.concatenate([c_prompt, c_sample], axis=0), w_ada, b_ada)
    cos_p, sin_p = _rope_tables(tp_, 0)
    cos_s, sin_s = _rope_tables(ts, PAST_LEN)
    final_g2 = final_g.reshape(1, d)

    xp = x_prompt.reshape(bp * tp_, d)
    xs = x_sample.reshape(bs * ts, d)
    st_s = (state_pool, state_ret, state_sconv, state_shift.reshape(depth, bs, 1, d), state_wkv)
    st_p = tuple(jnp.zeros((depth, bp) + st.shape[2:], F32) for st in st_s)
    ffn_s = state_ffn
    ffn_p = jnp.zeros((depth, bp) + state_ffn.shape[2:], F32)
    vec_rows = [pool_scale, sc_w[:, 0], sc_w[:, 1], sc_w[:, 2], rw_w0, rw_a0, rw_k_k, rw_k_a,
                rw_r_k.reshape(depth, MIX_W), rw_ln_g, rw_ln_b]
    vecs = jnp.concatenate([jnp.stack(vec_rows, axis=1),
                            jnp.zeros((depth, N_VECS - len(vec_rows), MIX_W), F32)], axis=1)
    mix_w = (norm1_g.reshape(depth, 1, d), w_in.astype(BF16))
    seq_w = (jnp.stack([_block_diag(pool_w[l]) for l in range(depth)]).astype(BF16), vecs,
             rw_mu.reshape(depth, 1, d), rw_w_lora, rw_a_lora, rw_g_lora, w_br.astype(BF16), w_out.astype(BF16))
    ffn_ws = (norm2_g.reshape(depth, 1, d), w_up.astype(BF16), ffn_w, w_down.astype(BF16), final_g2)
    for l in range(depth):
        mod_p, mod_s = mods[l, :, :bp], mods[l, :, bp:]
        final = l == depth - 1

        x1, st_p = _mix_call(xp, mod_p, *mix_w, cos_p, sin_p, st_p, *seq_w, batch=bp, t=tp_, pos0=0,
                             layer=l, zero_init=True)
        xp, ffn_p = _ffn_call(x1, mod_p, *ffn_ws, ffn_p, batch=bp, t=tp_, final=final, layer=l, zero_init=True)
        x1, st_s = _mix_call(xs, mod_s, *mix_w, cos_s, sin_s, st_s, *seq_w, batch=bs, t=ts, pos0=PAST_LEN,
                             layer=l, zero_init=False)
        xs, ffn_s = _ffn_call(x1, mod_s, *ffn_ws, ffn_s, batch=bs, t=ts, final=final, layer=l, zero_init=False)

    def finish(st, ffn_st, batch):
        pool, ret, sc, sh, wkv = st
        return (pool, ret, sc, sh.reshape(depth, batch, d), wkv, ffn_st)

    return (xp.reshape(bp, tp_, d), xs.reshape(bs, ts, d), *finish(st_p, ffn_p, bp), *finish(st_s, ffn_s, bs))
```

```python
import functools
import math

import jax
import jax.numpy as jnp
from jax import lax
from jax.experimental import pallas as pl
from jax.experimental.pallas import tpu as pltpu

F32 = jnp.float32
BF16 = jnp.bfloat16

D_MODEL = 1024
MIX_W = 256
HEADS = 4
HEAD_D = 64
POOL_PAST = 15
ROPE_BASE = 10000.0
LORA_W = 64
LORA_A = 64
LORA_G = 128
D_FF = 2816
PAST_LEN = 16384
NORM_EPS = 1e-6
GN_EPS = 1e-6
RWKV_LN_EPS = 64e-5
L2_EPS = 1e-12

COL_RET = MIX_W
COL_SC = COL_RET + 4 * MIX_W
COL_RWKV = COL_SC + 3 * MIX_W
COL_GATE = COL_RWKV + 3 * MIX_W + LORA_W + LORA_A + LORA_G
IN_COLS = COL_GATE + 4 * D_MODEL

SUBLANES = 8
VMEM_LIMIT = 56 * 1024 * 1024
RWKV_CHUNK = 64
INV_BASE = 16
POOL_HIST = 16
HIST = SUBLANES
CONV_COLS = 512
MIX_SPLIT = 8
MIX_ROWS = 512
FFN_ROWS = 512

(V_POOL_SCALE, V_SC0, V_SC1, V_SC2, V_W0, V_A0, V_KK, V_KA, V_RK, V_LNG, V_LNB) = range(11)
N_VECS = 16


def _bdot(a, b):
    return jnp.dot(a.astype(BF16), b.astype(BF16), preferred_element_type=F32)


def _bdot_nt(a, b):
    return lax.dot_general(a.astype(BF16), b.astype(BF16), (((1,), (1,)), ((), ())), preferred_element_type=F32)


def _bdot_tn(a, b):
    return lax.dot_general(a.astype(BF16), b.astype(BF16), (((0,), (0,)), ((), ())), preferred_element_type=F32)


def _wdot(a, w_bf16):
    return jnp.dot(a.astype(BF16), w_bf16, preferred_element_type=F32)


def _sigmoid(x):
    return 1.0 / (1.0 + jnp.exp(-x))


def _silu(x):
    half = 0.5 * x
    return half + half * jnp.tanh(half)


def _softplus(x):
    return jnp.maximum(x, 0.0) + jnp.log(1.0 + jnp.exp(-jnp.abs(x)))


def _rmsnorm(x, g):
    return x * lax.rsqrt(jnp.mean(x * x, axis=-1, keepdims=True) + NORM_EPS) * g


def _round_up(n, m):
    return (n + m - 1) // m * m


def _mod_row(mod_ref, k, lb):
    if len(mod_ref.shape) == 4:
        return mod_ref[k, lb]
    return mod_ref[k, lb:lb + 1, :]


def _mod_operand(mod, bb):
    six, _, d = mod.shape
    if bb % SUBLANES:
        return mod[:, :, None, :], pl.BlockSpec((six, bb, 1, d), lambda i, j: (0, i, 0, 0))
    return mod, pl.BlockSpec((six, bb, d), lambda i, j: (0, i, 0))


def _read_rows(ref):
    val = ref[...]
    return val.reshape(-1, val.shape[-1]) if val.ndim == 3 else val


def _write_seq_rows(ref, lb, tc, val):
    if len(ref.shape) == 3:
        ref[lb] = val
    else:
        ref[lb * tc:(lb + 1) * tc] = val


def _mod_kernel(c_ref, w_ref, b_ref, o_ref):
    o_ref[0, 0] = _wdot(_silu(c_ref[...]), w_ref[0].astype(BF16)) + b_ref[0]


def _mod_call(c_all, w_ada, b_ada):
    depth, d, cols = w_ada.shape
    rows = c_all.shape[0]
    cb = D_MODEL
    return pl.pallas_call(
        _mod_kernel,
        grid=(depth, cols // cb),
        in_specs=[pl.BlockSpec((rows, d), lambda l, c: (0, 0)),
                  pl.BlockSpec((1, d, cb), lambda l, c: (l, 0, c)),
                  pl.BlockSpec((1, 1, cb), lambda l, c: (l, 0, c))],
        out_specs=pl.BlockSpec((1, 1, rows, cb), lambda l, c: (l, c, 0, 0)),
        out_shape=jax.ShapeDtypeStruct((depth, cols // cb, rows, cb), F32),
        name="adaln_mod",
    )(c_all, w_ada, b_ada.reshape(depth, 1, cols))


def _tri_inverse_minus_eye(lowers, n):
    i = lax.broadcasted_iota(jnp.int32, (n, n), 0)
    j = lax.broadcasted_iota(jnp.int32, (n, n), 1)

    def same_block(size):
        sh = int(math.log2(size))
        return jnp.right_shift(i, sh) == jnp.right_shift(j, sh)

    base = min(n, INV_BASE)
    if base < n:
        diag_mask = same_block(base)
        xs = [jnp.where(diag_mask, low, 0.0) for low in lowers]
    else:
        xs = list(lowers)
    powers, p = xs, 1
    while 2 * p < base:
        powers = [_bdot(pw, pw) for pw in powers]
        prods = [_bdot(x, pw) for x, pw in zip(xs, powers)]
        xs = [x + pw + pr for x, pw, pr in zip(xs, powers, prods)]
        p *= 2
    size = base
    while size < n:
        off_mask = same_block(2 * size) & jnp.logical_not(same_block(size))
        offs = [jnp.where(off_mask, low, 0.0) for low in lowers]
        lefts = [off + _bdot(x, off) for x, off in zip(xs, offs)]
        xs = [x + left + _bdot(left, x) for x, left in zip(xs, lefts)]
        size *= 2
    return xs


def _mix_kernel(*refs, bb, tc, pos0, zero_init):
    refs = list(refs)
    x_ref, mod_ref, g_ref, win_ref, cos_ref, sin_ref = refs[:6]
    del refs[:6]
    if not zero_init:
        pool0_ref, ret0_ref, sc0_ref, sh0_ref, wkv0_ref = refs[:5]
        del refs[:5]
    poolw_ref, vecs_ref, mu_ref, wlora_ref, alora_ref, glora_ref, wbr_ref, wout_ref = refs[:8]
    del refs[:8 + (5 if zero_init else 0)]
    x1_ref, pool_ref, ret_ref, sc_ref, sh_ref, wkv_ref = refs[:6]
    h_s, z_s, y_s, dm_s, pool_ext, sc_ext, sh_ext, ywkv, wsum_s, conv_s, prev_s = refs[6:]
    j = pl.program_id(1)
    seqs = range(bb)
    heads = range(HEADS)
    hsl = [slice(h * HEAD_D, (h + 1) * HEAD_D) for h in heads]

    def rows_of(lb):
        return slice(lb * tc, (lb + 1) * tc)

    lane = lax.broadcasted_iota(jnp.int32, (1, MIX_W), 1)
    row_i = lax.broadcasted_iota(jnp.int32, (tc, 1), 0)
    col_i = lax.broadcasted_iota(jnp.int32, (1, tc), 1)
    row_f = row_i.astype(F32)
    lgs = [math.log1p(-(2.0 ** (-5.0 - h))) for h in heads]

    @pl.when(j == 0)
    def _init():
        diff = (row_i - col_i).astype(F32)
        for h in heads:
            dm_s[h] = jnp.where(diff >= 0, jnp.exp(lgs[h] * jnp.maximum(diff, 0.0)), 0.0)
        pool_ext[...] = jnp.zeros(pool_ext.shape, F32)
        sc_ext[...] = jnp.zeros(sc_ext.shape, F32)
        sh_ext[...] = jnp.zeros(sh_ext.shape, F32)
        if zero_init:
            ret_ref[...] = jnp.zeros(ret_ref.shape, F32)
            wkv_ref[...] = jnp.zeros(wkv_ref.shape, F32)
        else:
            pool_ext[:, POOL_HIST - POOL_PAST:POOL_HIST] = pool0_ref[...]
            sc_ext[:, HIST - 2:HIST] = sc0_ref[...]
            sh_ext[:, HIST - 1:HIST] = sh0_ref[...]
            ret_ref[...] = ret0_ref[...]
            wkv_ref[...] = wkv0_ref[...]

    def vec(row):
        return vecs_ref[row:row + 1, :]

    x = _read_rows(x_ref)
    hn = _rmsnorm(x, g_ref[...])
    for lb in seqs:
        h_s[rows_of(lb)] = hn[rows_of(lb)] * (1.0 + _mod_row(mod_ref, 1, lb)) + _mod_row(mod_ref, 0, lb)
    z_s[...] = _wdot(h_s[...], win_ref[...])

    def gated_proj(n):
        proj = _wdot(y_s[:, n * MIX_W:(n + 1) * MIX_W], wbr_ref[n])
        return _sigmoid(z_s[:, COL_GATE + n * D_MODEL:COL_GATE + (n + 1) * D_MODEL]) * proj

    rows_all = bb * tc
    grp = jnp.right_shift(lane, 6)
    win = jnp.where(grp == 0, 2, jnp.where(grp == 1, 4, jnp.where(grp == 2, 8, 16)))
    brow = lax.broadcasted_iota(jnp.int32, (rows_all, 1), 0)
    cnt = jnp.minimum(win, pos0 + j * tc + jnp.bitwise_and(brow, tc - 1) + 1).astype(F32)
    first_half = jnp.bitwise_and(lane, HEAD_D - 1) < HEAD_D // 2
    cos = cos_ref[...]
    sin = sin_ref[...]

    def rope(t):
        swapped = jnp.where(first_half, pltpu.roll(t, MIX_W - HEAD_D // 2, 1), pltpu.roll(t, HEAD_D // 2, 1))
        return t * cos + swapped * sin

    si = lax.broadcasted_iota(jnp.int32, (MIX_W, MIX_W), 0)
    sj = lax.broadcasted_iota(jnp.int32, (MIX_W, MIX_W), 1)
    head_ones = (jnp.right_shift(si, 6) == jnp.right_shift(sj, 6)).astype(BF16)

    def split_dot(ones, t):
        hi = t.astype(BF16)
        lo = (t - hi.astype(F32)).astype(BF16)
        return jnp.dot(ones, hi, preferred_element_type=F32) + jnp.dot(ones, lo, preferred_element_type=F32)

    def head_sum(t):
        hi = t.astype(BF16)
        lo = (t - hi.astype(F32)).astype(BF16)
        return (jnp.dot(hi, head_ones, preferred_element_type=F32)
                + jnp.dot(lo, head_ones, preferred_element_type=F32))

    c = min(RWKV_CHUNK, tc)
    nsub = tc // c
    merged = c % SUBLANES == 0
    ci = lax.broadcasted_iota(jnp.int32, (c, c), 0)
    cj = lax.broadcasted_iota(jnp.int32, (c, c), 1)
    strict = ci > cj
    incl = ci >= cj
    ti = lax.broadcasted_iota(jnp.int32, (rows_all, rows_all), 0)
    tj = lax.broadcasted_iota(jnp.int32, (rows_all, rows_all), 1)
    shift_c = int(math.log2(c))
    tri = ((ti >= tj) & (jnp.right_shift(ti, shift_c) == jnp.right_shift(tj, shift_c))).astype(BF16)

    for lb in seqs:
        rows = rows_of(lb)
        pool_ext[lb, POOL_HIST:POOL_HIST + tc] = z_s[rows, 0:MIX_W]
        e = pool_ext[lb]
        s2 = e + pltpu.roll(e, 1, 0)
        s4 = s2 + pltpu.roll(s2, 2, 0)
        s8 = s4 + pltpu.roll(s4, 4, 0)
        s16 = s8 + pltpu.roll(s8, 8, 0)
        wsum = jnp.where(grp == 0, s2, jnp.where(grp == 1, s4, jnp.where(grp == 2, s8, s16)))
        wsum_s[rows] = wsum[POOL_HIST:POOL_HIST + tc]
        new_pool = pool_ext[lb, tc + POOL_HIST - POOL_PAST:tc + POOL_HIST]
        pool_ext[lb, POOL_HIST - POOL_PAST:POOL_HIST] = new_pool
        pool_ref[lb] = new_pool

        sc_ext[lb, HIST:HIST + tc] = (z_s[rows, COL_SC + 2 * MIX_W:COL_SC + 3 * MIX_W]
                                      * z_s[rows, COL_SC:COL_SC + MIX_W])
        e = sc_ext[lb]
        conv = vec(V_SC2) * e + vec(V_SC1) * pltpu.roll(e, 1, 0) + vec(V_SC0) * pltpu.roll(e, 2, 0)
        conv_s[rows] = conv[HIST:HIST + tc]
        new_sc = sc_ext[lb, HIST + tc - 2:HIST + tc]
        sc_ext[lb, HIST - 2:HIST] = new_sc
        sc_ref[lb] = new_sc

        sh_ext[lb, HIST:HIST + tc] = z_s[rows, COL_RWKV:COL_GATE]
        prev_s[rows] = pltpu.roll(sh_ext[lb], 1, 0)[HIST:HIST + tc]
        new_sh = sh_ext[lb, HIST + tc - 1:HIST + tc]
        sh_ext[lb, HIST - 1:HIST] = new_sh
        sh_ref[lb] = new_sh

    u = z_s[:, 0:MIX_W]
    y_s[:, 0:MIX_W] = _wdot(wsum_s[...] / cnt - u, poolw_ref[...]) * vec(V_POOL_SCALE)
    y_s[:, 2 * MIX_W:3 * MIX_W] = z_s[:, COL_SC + MIX_W:COL_SC + 2 * MIX_W] * conv_s[...]

    q_all = rope(z_s[:, COL_RET:COL_RET + MIX_W])
    k_all = rope(z_s[:, COL_RET + MIX_W:COL_RET + 2 * MIX_W]) * (HEAD_D ** -0.5)
    v_all = z_s[:, COL_RET + 2 * MIX_W:COL_RET + 3 * MIX_W]

    zz = z_s[:, COL_RWKV:COL_GATE]
    zs = zz + (prev_s[...] - zz) * mu_ref[...]
    r = zs[:, 0:MIX_W]
    kx = zs[:, MIX_W:2 * MIX_W]
    vx = zs[:, 2 * MIX_W:3 * MIX_W]
    o3 = 3 * MIX_W
    wl = zs[:, o3:o3 + LORA_W]
    al = zs[:, o3 + LORA_W:o3 + LORA_W + LORA_A]
    gl = zs[:, o3 + LORA_W + LORA_A:]
    wlog = -_softplus(-(vec(V_W0) + _bdot(jnp.tanh(wl), wlora_ref[...]))) - 0.5
    logw = -jnp.exp(wlog)
    asig = _sigmoid(vec(V_A0) + _bdot(al, alora_ref[...]))
    g_out = _bdot(_sigmoid(gl), glora_ref[...])
    k2 = kx * (1.0 + (asig - 1.0) * vec(V_KA))
    kk_raw = kx * vec(V_KK)
    kk = kk_raw * lax.rsqrt(jnp.maximum(head_sum(kk_raw * kk_raw), L2_EPS * L2_EPS))
    cum = split_dot(tri, logw)
    p_in = jnp.exp(cum)
    p_inv = jnp.exp(-cum)
    rw = dict(a=-kk * jnp.exp(cum - logw), b=kk * asig * p_inv, k=k2 * p_inv, r=r * p_in, v=vx, p_in=p_in)

    pairs = [(lb, h) for lb in seqs for h in heads]
    qs = [q_all[rows_of(lb), hsl[h]] for lb, h in pairs]
    ks = [k_all[rows_of(lb), hsl[h]] for lb, h in pairs]
    vs = [v_all[rows_of(lb), hsl[h]] for lb, h in pairs]
    states = [ret_ref[lb, h] for lb, h in pairs]
    np_ = range(len(pairs))
    scores = [_bdot_nt(qs[i], ks[i]) for i in np_]
    cross = [_bdot(qs[i] * jnp.exp(lgs[pairs[i][1]] * (row_f + 1.0)), states[i]) for i in np_]
    kv = [_bdot_tn(ks[i] * jnp.exp(lgs[pairs[i][1]] * (tc - 1.0 - row_f)), vs[i]) for i in np_]
    outs = [_bdot(scores[i] * dm_s[pairs[i][1]], vs[i]) + cross[i] for i in np_]
    for i, (lb, h) in enumerate(pairs):
        ret_ref[lb, h] = states[i] * math.exp(lgs[h] * tc) + kv[i]
        y_s[rows_of(lb), MIX_W + h * HEAD_D:MIX_W + (h + 1) * HEAD_D] = outs[i]
    o_all = y_s[:, MIX_W:2 * MIX_W]
    o_all = o_all * lax.rsqrt(head_sum(o_all * o_all) * (1.0 / HEAD_D) + GN_EPS)
    y_s[:, MIX_W:2 * MIX_W] = o_all * _silu(z_s[:, COL_RET + 3 * MIX_W:COL_RET + 4 * MIX_W])

    blocks = [(lb, sub, h) for lb in seqs for sub in range(nsub) for h in heads]

    def blk(name):
        return [rw[name][lb * tc + sub * c:lb * tc + (sub + 1) * c, hsl[h]] for lb, sub, h in blocks]

    a_l, b_l, k_l, r_l, v_l = blk("a"), blk("b"), blk("k"), blk("r"), blk("v")
    nb = range(len(blocks))
    if merged:
        bk_l = [jnp.concatenate([b_l[i], k_l[i]], axis=0) for i in nb]
        gram = [_bdot_nt(jnp.concatenate([a_l[i], r_l[i]], axis=0), bk_l[i]) for i in nb]
        g_ab, g_ak = [g[:c, :c] for g in gram], [g[:c, c:] for g in gram]
        g_rb, g_rk = [g[c:, :c] for g in gram], [g[c:, c:] for g in gram]
    else:
        g_ab = [_bdot_nt(a_l[i], b_l[i]) for i in nb]
        g_ak = [_bdot_nt(a_l[i], k_l[i]) for i in nb]
        g_rb = [_bdot_nt(r_l[i], b_l[i]) for i in nb]
        g_rk = [_bdot_nt(r_l[i], k_l[i]) for i in nb]
    l_ab = [jnp.where(strict, g, 0.0) for g in g_ab]
    l_ak = [jnp.where(strict, g, 0.0) for g in g_ak]
    m_rb = [jnp.where(incl, g, 0.0) for g in g_rb]
    m_rk = [jnp.where(incl, g, 0.0) for g in g_rk]
    n_inv = _tri_inverse_minus_eye(l_ab, c)
    if merged:
        lv = [_bdot(jnp.concatenate([l_ak[i], m_rk[i]], axis=0), v_l[i]) for i in nb]
        lakv, mv = [t[:c] for t in lv], [t[c:] for t in lv]
    else:
        lakv = [_bdot(l_ak[i], v_l[i]) for i in nb]
        mv = [_bdot(m_rk[i], v_l[i]) for i in nb]
    nx = [_bdot(n_inv[i], jnp.concatenate([lakv[i], a_l[i]], axis=1)) for i in nb]
    tlv = [lakv[i] + nx[i][:, :HEAD_D] for i in nb]
    ta = [a_l[i] + nx[i][:, HEAD_D:] for i in nb]

    wkv_states = [wkv_ref[lb, h] for lb, h in pairs]
    for sub in range(nsub):
        ids = [(lb * nsub + sub) * HEADS + h for lb, h in pairs]
        p_end = [rw["p_in"][lb * tc + (sub + 1) * c - 1:lb * tc + (sub + 1) * c, hsl[h]] for lb, h in pairs]
        u_mat = [_bdot_nt(ta[i], wkv_states[n]) + tlv[i] for n, i in enumerate(ids)]
        y_st = [_bdot_nt(r_l[i], wkv_states[n]) for n, i in enumerate(ids)]
        if merged:
            upd = [_bdot_tn(jnp.concatenate([u_mat[n], v_l[i]], axis=0), bk_l[i] * p_end[n])
                   for n, i in enumerate(ids)]
        else:
            upd = [_bdot_tn(u_mat[n], b_l[i] * p_end[n]) + _bdot_tn(v_l[i], k_l[i] * p_end[n])
                   for n, i in enumerate(ids)]
        y_u = [_bdot(m_rb[i], u_mat[n]) for n, i in enumerate(ids)]
        for n, i in enumerate(ids):
            lb, h = pairs[n]
            ywkv[lb * tc + sub * c:lb * tc + (sub + 1) * c, hsl[h]] = y_st[n] + y_u[n] + mv[i]
        wkv_states = [wkv_states[n] * p_end[n] + upd[n] for n in np_]
    for n, (lb, h) in enumerate(pairs):
        wkv_ref[lb, h] = wkv_states[n]

    y_all = ywkv[...]
    cen = y_all - head_sum(y_all) * (1.0 / HEAD_D)
    var = head_sum(cen * cen) * (1.0 / HEAD_D)
    yn = cen * lax.rsqrt(var + RWKV_LN_EPS) * vec(V_LNG) + vec(V_LNB)
    bonus = head_sum(r * k2 * vec(V_RK)) * vx
    y_s[:, 3 * MIX_W:4 * MIX_W] = (yn + bonus) * g_out

    out = _wdot(gated_proj(0) + gated_proj(1) + gated_proj(2) + gated_proj(3), wout_ref[...])
    for lb in seqs:
        _write_seq_rows(x1_ref, lb, tc, x[rows_of(lb)] + _mod_row(mod_ref, 2, lb) * out[rows_of(lb)])


def _layer_block_spec(arr, bb, layer):
    shape = arr.shape[2:]
    zeros = (0,) * len(shape)
    return pl.BlockSpec((None, bb) + shape, lambda i, j: (layer, i) + zeros)


def _resident_spec(arr, layer=None):
    if layer is None:
        zeros = (0,) * arr.ndim
        return pl.BlockSpec(arr.shape, lambda i, j: zeros, pipeline_mode=pl.Buffered(1))
    zeros = (0,) * (arr.ndim - 1)
    return pl.BlockSpec((None,) + arr.shape[1:], lambda i, j: (layer,) + zeros, pipeline_mode=pl.Buffered(1))


def _rows_operand(x2d, batch, t, bb, tc):
    d = x2d.shape[1]
    if tc == t:
        return x2d, pl.BlockSpec((bb * tc, d), lambda i, j: (i, 0))
    return x2d.reshape(batch, t, d), pl.BlockSpec((bb, tc, d), lambda i, j: (i, j, 0))


def _params():
    return pltpu.CompilerParams(dimension_semantics=("parallel", "arbitrary"), vmem_limit_bytes=VMEM_LIMIT)


def _blocking(batch, t, split=1, rows=256):
    if t >= 64:
        split = split if batch % split == 0 else 1
        return split, min(t, rows // split)
    return min(batch, 64 // t), t


def _mix_call(x2d, mod, g1, w_in, cos_t, sin_t, states, poolw, vecs, mu, wlora, alora, glora, w_br, w_out,
              batch, t, pos0, layer, zero_init):
    bb, tc = _blocking(batch, t, MIX_SPLIT, MIX_ROWS)
    nt = t // tc
    tp = _round_up(tc, SUBLANES)
    d = x2d.shape[1]
    out_structs = [jax.ShapeDtypeStruct(st.shape, F32) for st in states]
    state_specs = [_layer_block_spec(st, bb, layer) for st in states]
    weights = (poolw, vecs, mu, wlora, alora, glora, w_br, w_out)
    mod, mod_spec = _mod_operand(mod, bb)
    if zero_init:
        operands = (*weights, *states)
        operand_specs = [_resident_spec(a, layer) for a in weights] + [pl.BlockSpec(memory_space=pl.ANY)] * 5
        first_state = 6 + len(weights)
    else:
        operands = (*states, *weights)
        operand_specs = state_specs + [_resident_spec(a, layer) for a in weights]
        first_state = 6
    x_op, x_spec = _rows_operand(x2d, batch, t, bb, tc)

    def block_table(tab):
        return jnp.tile(tab.reshape(nt, tc, MIX_W), (1, bb, 1)).reshape(nt * bb * tc, MIX_W)

    outs = pl.pallas_call(
        functools.partial(_mix_kernel, bb=bb, tc=tc, pos0=pos0, zero_init=zero_init),
        grid=(batch // bb, nt),
        in_specs=[x_spec, mod_spec,
                  _resident_spec(g1, layer), _resident_spec(w_in, layer),
                  pl.BlockSpec((bb * tc, MIX_W), lambda i, j: (j, 0)),
                  pl.BlockSpec((bb * tc, MIX_W), lambda i, j: (j, 0))]
        + operand_specs,
        out_specs=[x_spec] + state_specs,
        out_shape=[jax.ShapeDtypeStruct(x_op.shape, F32)] + out_structs,
        input_output_aliases={first_state + k: 1 + k for k in range(5)},
        scratch_shapes=[pltpu.VMEM((bb * tc, d), F32),
                        pltpu.VMEM((bb * tc, IN_COLS), F32),
                        pltpu.VMEM((bb * tc, d), F32),
                        pltpu.VMEM((HEADS, tc, tc), F32),
                        pltpu.VMEM((bb, POOL_HIST + tp, MIX_W), F32),
                        pltpu.VMEM((bb, HIST + tp, MIX_W), F32),
                        pltpu.VMEM((bb, HIST + tp, d), F32),
                        pltpu.VMEM((bb * tc, MIX_W), F32),
                        pltpu.VMEM((bb * tc, MIX_W), F32),
                        pltpu.VMEM((bb * tc, MIX_W), F32),
                        pltpu.VMEM((bb * tc, COL_GATE - COL_RWKV), F32)],
        compiler_params=_params(),
        name="mix",
    )(x_op, mod, g1, w_in, block_table(cos_t), block_table(sin_t), *operands)
    return outs[0].reshape(x2d.shape), tuple(outs[1:])


def _ffn_kernel(*refs, bb, tc, final, zero_init):
    refs = list(refs)
    x1_ref, mod_ref, g_ref, wup_ref, ffnw_ref, wdown_ref, fg_ref = refs[:7]
    del refs[:7]
    st0_ref = refs.pop(0)
    o_ref, st_ref, h_s, ext, act_s = refs
    j = pl.program_id(1)
    seqs = range(bb)

    def rows_of(lb):
        return slice(lb * tc, (lb + 1) * tc)

    @pl.when(j == 0)
    def _init():
        ext[...] = jnp.zeros(ext.shape, F32)
        if not zero_init:
            ext[:, HIST - 2:HIST] = st0_ref[...]

    x1 = _read_rows(x1_ref)
    hn = _rmsnorm(x1, g_ref[...])
    for lb in seqs:
        h_s[rows_of(lb)] = hn[rows_of(lb)] * (1.0 + _mod_row(mod_ref, 4, lb)) + _mod_row(mod_ref, 3, lb)
    if bb == 1:
        ext[0, HIST:HIST + tc] = _wdot(h_s[...], wup_ref[...])
    else:
        up = _wdot(h_s[...], wup_ref[...])
        for lb in seqs:
            ext[lb, HIST:HIST + tc] = up[rows_of(lb)]

    for lb in seqs:
        def conv_cols(lo, width):
            e = ext[lb, :, lo:lo + width]
            w = ffnw_ref[:, lo:lo + width]
            y = w[2:3] * e + w[1:2] * pltpu.roll(e, 1, 0) + w[0:1] * pltpu.roll(e, 2, 0)
            return y[HIST:HIST + tc]

        for lo in range(0, D_FF, CONV_COLS):
            width = min(CONV_COLS, D_FF - lo)
            act_s[rows_of(lb), lo:lo + width] = _silu(conv_cols(lo, width)) * conv_cols(D_FF + lo, width)
        new_st = ext[lb, HIST + tc - 2:HIST + tc]
        ext[lb, HIST - 2:HIST] = new_st
        st_ref[lb] = new_st

    dn = _wdot(act_s[...], wdown_ref[...])
    for lb in seqs:
        _write_seq_rows(o_ref, lb, tc, x1[rows_of(lb)] + _mod_row(mod_ref, 5, lb) * dn[rows_of(lb)])
    if final:
        o_ref[...] = _rmsnorm(o_ref[...], fg_ref[...])


def _ffn_call(x1, mod, g2, w_up, ffn_w, w_down, final_g, state, batch, t, final, layer, zero_init):
    bb, tc = _blocking(batch, t, 1, FFN_ROWS)
    nt = t // tc
    tp = _round_up(tc, SUBLANES)
    d = x1.shape[1]
    st_struct = jax.ShapeDtypeStruct(state.shape, F32)
    st_spec = _layer_block_spec(state, bb, layer)
    mod, mod_spec = _mod_operand(mod, bb)
    x_op, x_spec = _rows_operand(x1, batch, t, bb, tc)
    out, new_state = pl.pallas_call(
        functools.partial(_ffn_kernel, bb=bb, tc=tc, final=final, zero_init=zero_init),
        grid=(batch // bb, nt),
        in_specs=[x_spec, mod_spec,
                  _resident_spec(g2, layer), _resident_spec(w_up, layer), _resident_spec(ffn_w, layer),
                  _resident_spec(w_down, layer), _resident_spec(final_g)]
        + [pl.BlockSpec(memory_space=pl.ANY) if zero_init else st_spec],
        out_specs=[x_spec, st_spec],
        out_shape=[jax.ShapeDtypeStruct(x_op.shape, F32), st_struct],
        input_output_aliases={7: 1},
        scratch_shapes=[pltpu.VMEM((bb * tc, d), F32),
                        pltpu.VMEM((bb, HIST + tp, 2 * D_FF), F32),
                        pltpu.VMEM((bb * tc, D_FF), F32)],
        compiler_params=_params(),
        name="ffn",
    )(x_op, mod, g2, w_up, ffn_w, w_down, final_g, state)
    return out.reshape(x1.shape), new_state


def _rope_tables(t, pos0):
    half = HEAD_D // 2
    inv = ROPE_BASE ** (-jnp.arange(half, dtype=F32) / half)
    pos = pos0 + jnp.arange(t, dtype=jnp.int32)
    ang = pos.astype(F32)[:, None] * inv[None, :]
    cos, sin = jnp.cos(ang), jnp.sin(ang)
    cos_t = jnp.tile(jnp.concatenate([cos, cos], axis=-1), (1, HEADS))
    sin_t = jnp.tile(jnp.concatenate([-sin, sin], axis=-1), (1, HEADS))
    return cos_t, sin_t


def _block_diag(pool_w):
    groups, gw, _ = pool_w.shape
    out = jnp.zeros((groups * gw, groups * gw), pool_w.dtype)
    for g in range(groups):
        out = out.at[g * gw:(g + 1) * gw, g * gw:(g + 1) * gw].set(pool_w[g])
    return out


def kernel(x_prompt, x_sample, c_prompt, c_sample, state_pool, state_ret, state_sconv, state_shift, state_wkv, state_ffn, w_ada, b_ada, norm1_g, norm2_g, w_in, pool_w, pool_scale, sc_w, rw_mu, rw_w0, rw_w_lora, rw_a0, rw_a_lora, rw_g_lora, rw_k_k, rw_k_a, rw_r_k, rw_ln_g, rw_ln_b, w_br, w_out, w_up, ffn_w, w_down, final_g):
    depth = w_in.shape[0]
    bp, tp_, d = x_prompt.shape
    bs, ts, _ = x_sample.shape

    mods = _mod_call(jnp.concatenate([c_prompt, c_sample], axis=0), w_ada, b_ada)
    cos_p, sin_p = _rope_tables(tp_, 0)
    cos_s, sin_s = _rope_tables(ts, PAST_LEN)
    final_g2 = final_g.reshape(1, d)

    xp = x_prompt.reshape(bp * tp_, d)
    xs = x_sample.reshape(bs * ts, d)
    st_s = (state_pool, state_ret, state_sconv, state_shift.reshape(depth, bs, 1, d), state_wkv)
    st_p = tuple(jnp.zeros((depth, bp) + st.shape[2:], F32) for st in st_s)
    ffn_s = state_ffn
    ffn_p = jnp.zeros((depth, bp) + state_ffn.shape[2:], F32)
    vec_rows = [pool_scale, sc_w[:, 0], sc_w[:, 1], sc_w[:, 2], rw_w0, rw_a0, rw_k_k, rw_k_a,
                rw_r_k.reshape(depth, MIX_W), rw_ln_g, rw_ln_b]
    vecs = jnp.concatenate([jnp.stack(vec_rows, axis=1),
                            jnp.zeros((depth, N_VECS - len(vec_rows), MIX_W), F32)], axis=1)
    mix_w = (norm1_g.reshape(depth, 1, d), w_in.astype(BF16))
    seq_w = (jnp.stack([_block_diag(pool_w[l]) for l in range(depth)]).astype(BF16), vecs,
             rw_mu.reshape(depth, 1, d), rw_w_lora, rw_a_lora, rw_g_lora, w_br.astype(BF16), w_out.astype(BF16))
    ffn_ws = (norm2_g.reshape(depth, 1, d), w_up.astype(BF16), ffn_w, w_down.astype(BF16), final_g2)
    for l in range(depth):
        mod_p, mod_s = mods[l, :, :bp], mods[l, :, bp:]
        final = l == depth - 1

        x1, st_p = _mix_call(xp, mod_p, *mix_w, cos_p, sin_p, st_p, *seq_w, batch=bp, t=tp_, pos0=0,
                             layer=l, zero_init=True)
        xp, ffn_p = _ffn_call(x1, mod_p, *ffn_ws, ffn_p, batch=bp, t=tp_, final=final, layer=l, zero_init=True)
        x1, st_s = _mix_call(xs, mod_s, *mix_w, cos_s, sin_s, st_s, *seq_w, batch=bs, t=ts, pos0=PAST_LEN,
                             layer=l, zero_init=False)
        xs, ffn_s = _ffn_call(x1, mod_s, *ffn_ws, ffn_s, batch=bs, t=ts, final=final, layer=l, zero_init=False)

    def finish(st, ffn_st, batch):
        pool, ret, sc, sh, wkv = st
        return (pool, ret, sc, sh.reshape(depth, batch, d), wkv, ffn_st)

    return (xp.reshape(bp, tp_, d), xs.reshape(bs, ts, d), *finish(st_p, ffn_p, bp), *finish(st_s, ffn_s, bs))
```

```python
import functools
import math

import jax
import jax.numpy as jnp
from jax import lax
from jax.experimental import pallas as pl
from jax.experimental.pallas import tpu as pltpu

F32 = jnp.float32
BF16 = jnp.bfloat16

D_MODEL = 1024
MIX_W = 256
HEADS = 4
HEAD_D = 64
POOL_PAST = 15
ROPE_BASE = 10000.0
LORA_W = 64
LORA_A = 64
LORA_G = 128
D_FF = 2816
PAST_LEN = 16384
NORM_EPS = 1e-6
GN_EPS = 1e-6
RWKV_LN_EPS = 64e-5
L2_EPS = 1e-12

COL_RET = MIX_W
COL_SC = COL_RET + 4 * MIX_W
COL_RWKV = COL_SC + 3 * MIX_W
COL_GATE = COL_RWKV + 3 * MIX_W + LORA_W + LORA_A + LORA_G
IN_COLS = COL_GATE + 4 * D_MODEL

SUBLANES = 8
VMEM_LIMIT = 56 * 1024 * 1024
RWKV_CHUNK = 64
INV_BASE = 16
POOL_HIST = 16
HIST = SUBLANES
CONV_COLS = 512
MIX_SPLIT = 8
MIX_ROWS = 512
FFN_ROWS = 512

(V_POOL_SCALE, V_SC0, V_SC1, V_SC2, V_W0, V_A0, V_KK, V_KA, V_RK, V_LNG, V_LNB) = range(11)
N_VECS = 16


def _bdot(a, b):
    return jnp.dot(a.astype(BF16), b.astype(BF16), preferred_element_type=F32)


def _bdot_nt(a, b):
    return lax.dot_general(a.astype(BF16), b.astype(BF16), (((1,), (1,)), ((), ())), preferred_element_type=F32)


def _bdot_tn(a, b):
    return lax.dot_general(a.astype(BF16), b.astype(BF16), (((0,), (0,)), ((), ())), preferred_element_type=F32)


def _wdot(a, w_bf16):
    return jnp.dot(a.astype(BF16), w_bf16, preferred_element_type=F32)


def _sigmoid(x):
    return 1.0 / (1.0 + jnp.exp(-x))


def _silu(x):
    half = 0.5 * x
    return half + half * jnp.tanh(half)


def _softplus(x):
    return jnp.maximum(x, 0.0) + jnp.log(1.0 + jnp.exp(-jnp.abs(x)))


def _rmsnorm(x, g):
    return x * lax.rsqrt(jnp.mean(x * x, axis=-1, keepdims=True) + NORM_EPS) * g


def _round_up(n, m):
    return (n + m - 1) // m * m


def _mod_row(mod_ref, k, lb):
    if len(mod_ref.shape) == 4:
        return mod_ref[k, lb]
    return mod_ref[k, lb:lb + 1, :]


def _mod_rows(mod_ref, k):
    rows = mod_ref[k]
    return rows.reshape(rows.shape[0], rows.shape[-1])


def _mod_operand(mod, bb):
    six, _, d = mod.shape
    if bb % SUBLANES:
        return mod[:, :, None, :], pl.BlockSpec((six, bb, 1, d), lambda i, j: (0, i, 0, 0))
    return mod, pl.BlockSpec((six, bb, d), lambda i, j: (0, i, 0))


def _read_rows(ref):
    val = ref[...]
    return val.reshape(-1, val.shape[-1]) if val.ndim == 3 else val


def _write_seq_rows(ref, lb, tc, val):
    if len(ref.shape) == 3:
        ref[lb] = val
    else:
        ref[lb * tc:(lb + 1) * tc] = val


def _mod_kernel(c_ref, w_ref, b_ref, o_ref):
    o_ref[0, 0] = _wdot(_silu(c_ref[...]), w_ref[0].astype(BF16)) + b_ref[0]


def _mod_call(c_all, w_ada, b_ada):
    depth, d, cols = w_ada.shape
    rows = c_all.shape[0]
    cb = D_MODEL
    return pl.pallas_call(
        _mod_kernel,
        grid=(depth, cols // cb),
        in_specs=[pl.BlockSpec((rows, d), lambda l, c: (0, 0)),
                  pl.BlockSpec((1, d, cb), lambda l, c: (l, 0, c)),
                  pl.BlockSpec((1, 1, cb), lambda l, c: (l, 0, c))],
        out_specs=pl.BlockSpec((1, 1, rows, cb), lambda l, c: (l, c, 0, 0)),
        out_shape=jax.ShapeDtypeStruct((depth, cols // cb, rows, cb), F32),
        name="adaln_mod",
    )(c_all, w_ada, b_ada.reshape(depth, 1, cols))


def _tri_inverse_minus_eye(lowers, n):
    i = lax.broadcasted_iota(jnp.int32, (n, n), 0)
    j = lax.broadcasted_iota(jnp.int32, (n, n), 1)

    def same_block(size):
        sh = int(math.log2(size))
        return jnp.right_shift(i, sh) == jnp.right_shift(j, sh)

    base = min(n, INV_BASE)
    if base < n:
        diag_mask = same_block(base)
        xs = [jnp.where(diag_mask, low, 0.0) for low in lowers]
    else:
        xs = list(lowers)
    powers, p = xs, 1
    while 2 * p < base:
        powers = [_bdot(pw, pw) for pw in powers]
        prods = [_bdot(x, pw) for x, pw in zip(xs, powers)]
        xs = [x + pw + pr for x, pw, pr in zip(xs, powers, prods)]
        p *= 2
    size = base
    while size < n:
        off_mask = same_block(2 * size) & jnp.logical_not(same_block(size))
        offs = [jnp.where(off_mask, low, 0.0) for low in lowers]
        lefts = [off + _bdot(x, off) for x, off in zip(xs, offs)]
        xs = [x + left + _bdot(left, x) for x, left in zip(xs, lefts)]
        size *= 2
    return xs


def _mix_kernel(*refs, bb, tc, pos0, zero_init):
    refs = list(refs)
    x_ref, mod_ref, g_ref, win_ref, cos_ref, sin_ref = refs[:6]
    del refs[:6]
    if not zero_init:
        pool0_ref, ret0_ref, sc0_ref, sh0_ref, wkv0_ref = refs[:5]
        del refs[:5]
    poolw_ref, vecs_ref, mu_ref, wlora_ref, alora_ref, glora_ref, wbr_ref, wout_ref = refs[:8]
    del refs[:8 + (5 if zero_init else 0)]
    x1_ref, pool_ref, ret_ref, sc_ref, sh_ref, wkv_ref = refs[:6]
    h_s, z_s, y_s, dm_s, pool_ext, sc_ext, sh_ext, ywkv, wsum_s, conv_s, prev_s = refs[6:]
    j = pl.program_id(1)
    seqs = range(bb)
    heads = range(HEADS)
    hsl = [slice(h * HEAD_D, (h + 1) * HEAD_D) for h in heads]

    def rows_of(lb):
        return slice(lb * tc, (lb + 1) * tc)

    lane = lax.broadcasted_iota(jnp.int32, (1, MIX_W), 1)
    row_i = lax.broadcasted_iota(jnp.int32, (tc, 1), 0)
    col_i = lax.broadcasted_iota(jnp.int32, (1, tc), 1)
    row_f = row_i.astype(F32)
    lgs = [math.log1p(-(2.0 ** (-5.0 - h))) for h in heads]

    @pl.when(j == 0)
    def _init():
        diff = (row_i - col_i).astype(F32)
        for h in heads:
            dm_s[h] = jnp.where(diff >= 0, jnp.exp(lgs[h] * jnp.maximum(diff, 0.0)), 0.0)
        pool_ext[...] = jnp.zeros(pool_ext.shape, F32)
        sc_ext[...] = jnp.zeros(sc_ext.shape, F32)
        sh_ext[...] = jnp.zeros(sh_ext.shape, F32)
        if zero_init:
            ret_ref[...] = jnp.zeros(ret_ref.shape, F32)
            wkv_ref[...] = jnp.zeros(wkv_ref.shape, F32)
        else:
            pool_ext[:, POOL_HIST - POOL_PAST:POOL_HIST] = pool0_ref[...]
            sc_ext[:, HIST - 2:HIST] = sc0_ref[...]
            sh_ext[:, HIST - 1:HIST] = sh0_ref[...]
            ret_ref[...] = ret0_ref[...]
            wkv_ref[...] = wkv0_ref[...]

    def vec(row):
        return vecs_ref[row:row + 1, :]

    x = _read_rows(x_ref)
    hn = _rmsnorm(x, g_ref[...])
    for lb in seqs:
        h_s[rows_of(lb)] = hn[rows_of(lb)] * (1.0 + _mod_row(mod_ref, 1, lb)) + _mod_row(mod_ref, 0, lb)
    z_s[...] = _wdot(h_s[...], win_ref[...])

    def gated_proj(n):
        proj = _wdot(y_s[:, n * MIX_W:(n + 1) * MIX_W], wbr_ref[n])
        return _sigmoid(z_s[:, COL_GATE + n * D_MODEL:COL_GATE + (n + 1) * D_MODEL]) * proj

    rows_all = bb * tc
    grp = jnp.right_shift(lane, 6)
    win = jnp.where(grp == 0, 2, jnp.where(grp == 1, 4, jnp.where(grp == 2, 8, 16)))
    brow = lax.broadcasted_iota(jnp.int32, (rows_all, 1), 0)
    cnt = jnp.minimum(win, pos0 + j * tc + jnp.bitwise_and(brow, tc - 1) + 1).astype(F32)
    first_half = jnp.bitwise_and(lane, HEAD_D - 1) < HEAD_D // 2
    cos = cos_ref[...]
    sin = sin_ref[...]

    def rope(t):
        swapped = jnp.where(first_half, pltpu.roll(t, MIX_W - HEAD_D // 2, 1), pltpu.roll(t, HEAD_D // 2, 1))
        return t * cos + swapped * sin

    si = lax.broadcasted_iota(jnp.int32, (MIX_W, MIX_W), 0)
    sj = lax.broadcasted_iota(jnp.int32, (MIX_W, MIX_W), 1)
    head_ones = (jnp.right_shift(si, 6) == jnp.right_shift(sj, 6)).astype(BF16)

    def split_dot(ones, t):
        hi = t.astype(BF16)
        lo = (t - hi.astype(F32)).astype(BF16)
        return jnp.dot(ones, hi, preferred_element_type=F32) + jnp.dot(ones, lo, preferred_element_type=F32)

    def head_sum(t):
        hi = t.astype(BF16)
        lo = (t - hi.astype(F32)).astype(BF16)
        return (jnp.dot(hi, head_ones, preferred_element_type=F32)
                + jnp.dot(lo, head_ones, preferred_element_type=F32))

    c = min(RWKV_CHUNK, tc)
    nsub = tc // c
    merged = c % SUBLANES == 0
    ci = lax.broadcasted_iota(jnp.int32, (c, c), 0)
    cj = lax.broadcasted_iota(jnp.int32, (c, c), 1)
    strict = ci > cj
    incl = ci >= cj
    ti = lax.broadcasted_iota(jnp.int32, (rows_all, rows_all), 0)
    tj = lax.broadcasted_iota(jnp.int32, (rows_all, rows_all), 1)
    shift_c = int(math.log2(c))
    tri = ((ti >= tj) & (jnp.right_shift(ti, shift_c) == jnp.right_shift(tj, shift_c))).astype(BF16)

    for lb in seqs:
        rows = rows_of(lb)
        pool_ext[lb, POOL_HIST:POOL_HIST + tc] = z_s[rows, 0:MIX_W]
        e = pool_ext[lb]
        s2 = e + pltpu.roll(e, 1, 0)
        s4 = s2 + pltpu.roll(s2, 2, 0)
        s8 = s4 + pltpu.roll(s4, 4, 0)
        s16 = s8 + pltpu.roll(s8, 8, 0)
        wsum = jnp.where(grp == 0, s2, jnp.where(grp == 1, s4, jnp.where(grp == 2, s8, s16)))
        wsum_s[rows] = wsum[POOL_HIST:POOL_HIST + tc]
        new_pool = pool_ext[lb, tc + POOL_HIST - POOL_PAST:tc + POOL_HIST]
        pool_ext[lb, POOL_HIST - POOL_PAST:POOL_HIST] = new_pool
        pool_ref[lb] = new_pool

        sc_ext[lb, HIST:HIST + tc] = (z_s[rows, COL_SC + 2 * MIX_W:COL_SC + 3 * MIX_W]
                                      * z_s[rows, COL_SC:COL_SC + MIX_W])
        e = sc_ext[lb]
        conv = vec(V_SC2) * e + vec(V_SC1) * pltpu.roll(e, 1, 0) + vec(V_SC0) * pltpu.roll(e, 2, 0)
        conv_s[rows] = conv[HIST:HIST + tc]
        new_sc = sc_ext[lb, HIST + tc - 2:HIST + tc]
        sc_ext[lb, HIST - 2:HIST] = new_sc
        sc_ref[lb] = new_sc

        sh_ext[lb, HIST:HIST + tc] = z_s[rows, COL_RWKV:COL_GATE]
        prev_s[rows] = pltpu.roll(sh_ext[lb], 1, 0)[HIST:HIST + tc]
        new_sh = sh_ext[lb, HIST + tc - 1:HIST + tc]
        sh_ext[lb, HIST - 1:HIST] = new_sh
        sh_ref[lb] = new_sh

    u = z_s[:, 0:MIX_W]
    y_s[:, 0:MIX_W] = _wdot(wsum_s[...] / cnt - u, poolw_ref[...]) * vec(V_POOL_SCALE)
    y_s[:, 2 * MIX_W:3 * MIX_W] = z_s[:, COL_SC + MIX_W:COL_SC + 2 * MIX_W] * conv_s[...]

    q_all = rope(z_s[:, COL_RET:COL_RET + MIX_W])
    k_all = rope(z_s[:, COL_RET + MIX_W:COL_RET + 2 * MIX_W]) * (HEAD_D ** -0.5)
    v_all = z_s[:, COL_RET + 2 * MIX_W:COL_RET + 3 * MIX_W]

    zz = z_s[:, COL_RWKV:COL_GATE]
    zs = zz + (prev_s[...] - zz) * mu_ref[...]
    r = zs[:, 0:MIX_W]
    kx = zs[:, MIX_W:2 * MIX_W]
    vx = zs[:, 2 * MIX_W:3 * MIX_W]
    o3 = 3 * MIX_W
    wl = zs[:, o3:o3 + LORA_W]
    al = zs[:, o3 + LORA_W:o3 + LORA_W + LORA_A]
    gl = zs[:, o3 + LORA_W + LORA_A:]
    wlog = -_softplus(-(vec(V_W0) + _bdot(jnp.tanh(wl), wlora_ref[...]))) - 0.5
    logw = -jnp.exp(wlog)
    asig = _sigmoid(vec(V_A0) + _bdot(al, alora_ref[...]))
    g_out = _bdot(_sigmoid(gl), glora_ref[...])
    k2 = kx * (1.0 + (asig - 1.0) * vec(V_KA))
    kk_raw = kx * vec(V_KK)
    kk = kk_raw * lax.rsqrt(jnp.maximum(head_sum(kk_raw * kk_raw), L2_EPS * L2_EPS))
    cum = split_dot(tri, logw)
    p_in = jnp.exp(cum)
    p_inv = jnp.exp(-cum)
    rw = dict(a=-kk * jnp.exp(cum - logw), b=kk * asig * p_inv, k=k2 * p_inv, r=r * p_in, v=vx, p_in=p_in)

    pairs = [(lb, h) for lb in seqs for h in heads]
    qs = [q_all[rows_of(lb), hsl[h]] for lb, h in pairs]
    ks = [k_all[rows_of(lb), hsl[h]] for lb, h in pairs]
    vs = [v_all[rows_of(lb), hsl[h]] for lb, h in pairs]
    states = [ret_ref[lb, h] for lb, h in pairs]
    np_ = range(len(pairs))
    scores = [_bdot_nt(qs[i], ks[i]) for i in np_]
    cross = [_bdot(qs[i] * jnp.exp(lgs[pairs[i][1]] * (row_f + 1.0)), states[i]) for i in np_]
    kv = [_bdot_tn(ks[i] * jnp.exp(lgs[pairs[i][1]] * (tc - 1.0 - row_f)), vs[i]) for i in np_]
    outs = [_bdot(scores[i] * dm_s[pairs[i][1]], vs[i]) + cross[i] for i in np_]
    for i, (lb, h) in enumerate(pairs):
        ret_ref[lb, h] = states[i] * math.exp(lgs[h] * tc) + kv[i]
        y_s[rows_of(lb), MIX_W + h * HEAD_D:MIX_W + (h + 1) * HEAD_D] = outs[i]
    o_all = y_s[:, MIX_W:2 * MIX_W]
    o_all = o_all * lax.rsqrt(head_sum(o_all * o_all) * (1.0 / HEAD_D) + GN_EPS)
    y_s[:, MIX_W:2 * MIX_W] = o_all * _silu(z_s[:, COL_RET + 3 * MIX_W:COL_RET + 4 * MIX_W])

    blocks = [(lb, sub, h) for lb in seqs for sub in range(nsub) for h in heads]

    def blk(name):
        return [rw[name][lb * tc + sub * c:lb * tc + (sub + 1) * c, hsl[h]] for lb, sub, h in blocks]

    a_l, b_l, k_l, r_l, v_l = blk("a"), blk("b"), blk("k"), blk("r"), blk("v")
    nb = range(len(blocks))
    if merged:
        bk_l = [jnp.concatenate([b_l[i], k_l[i]], axis=0) for i in nb]
        gram = [_bdot_nt(jnp.concatenate([a_l[i], r_l[i]], axis=0), bk_l[i]) for i in nb]
        g_ab, g_ak = [g[:c, :c] for g in gram], [g[:c, c:] for g in gram]
        g_rb, g_rk = [g[c:, :c] for g in gram], [g[c:, c:] for g in gram]
    else:
        g_ab = [_bdot_nt(a_l[i], b_l[i]) for i in nb]
        g_ak = [_bdot_nt(a_l[i], k_l[i]) for i in nb]
        g_rb = [_bdot_nt(r_l[i], b_l[i]) for i in nb]
        g_rk = [_bdot_nt(r_l[i], k_l[i]) for i in nb]
    l_ab = [jnp.where(strict, g, 0.0) for g in g_ab]
    l_ak = [jnp.where(strict, g, 0.0) for g in g_ak]
    m_rb = [jnp.where(incl, g, 0.0) for g in g_rb]
    m_rk = [jnp.where(incl, g, 0.0) for g in g_rk]
    n_inv = _tri_inverse_minus_eye(l_ab, c)
    if merged:
        lv = [_bdot(jnp.concatenate([l_ak[i], m_rk[i]], axis=0), v_l[i]) for i in nb]
        lakv, mv = [t[:c] for t in lv], [t[c:] for t in lv]
    else:
        lakv = [_bdot(l_ak[i], v_l[i]) for i in nb]
        mv = [_bdot(m_rk[i], v_l[i]) for i in nb]
    nx = [_bdot(n_inv[i], jnp.concatenate([lakv[i], a_l[i]], axis=1)) for i in nb]
    tlv = [lakv[i] + nx[i][:, :HEAD_D] for i in nb]
    ta = [a_l[i] + nx[i][:, HEAD_D:] for i in nb]

    wkv_states = [wkv_ref[lb, h] for lb, h in pairs]
    for sub in range(nsub):
        ids = [(lb * nsub + sub) * HEADS + h for lb, h in pairs]
        p_end = [rw["p_in"][lb * tc + (sub + 1) * c - 1:lb * tc + (sub + 1) * c, hsl[h]] for lb, h in pairs]
        u_mat = [_bdot_nt(ta[i], wkv_states[n]) + tlv[i] for n, i in enumerate(ids)]
        y_st = [_bdot_nt(r_l[i], wkv_states[n]) for n, i in enumerate(ids)]
        if merged:
            upd = [_bdot_tn(jnp.concatenate([u_mat[n], v_l[i]], axis=0), bk_l[i] * p_end[n])
                   for n, i in enumerate(ids)]
        else:
            upd = [_bdot_tn(u_mat[n], b_l[i] * p_end[n]) + _bdot_tn(v_l[i], k_l[i] * p_end[n])
                   for n, i in enumerate(ids)]
        y_u = [_bdot(m_rb[i], u_mat[n]) for n, i in enumerate(ids)]
        for n, i in enumerate(ids):
            lb, h = pairs[n]
            ywkv[lb * tc + sub * c:lb * tc + (sub + 1) * c, hsl[h]] = y_st[n] + y_u[n] + mv[i]
        wkv_states = [wkv_states[n] * p_end[n] + upd[n] for n in np_]
    for n, (lb, h) in enumerate(pairs):
        wkv_ref[lb, h] = wkv_states[n]

    y_all = ywkv[...]
    cen = y_all - head_sum(y_all) * (1.0 / HEAD_D)
    var = head_sum(cen * cen) * (1.0 / HEAD_D)
    yn = cen * lax.rsqrt(var + RWKV_LN_EPS) * vec(V_LNG) + vec(V_LNB)
    bonus = head_sum(r * k2 * vec(V_RK)) * vx
    y_s[:, 3 * MIX_W:4 * MIX_W] = (yn + bonus) * g_out

    out = _wdot(gated_proj(0) + gated_proj(1) + gated_proj(2) + gated_proj(3), wout_ref[...])
    for lb in seqs:
        _write_seq_rows(x1_ref, lb, tc, x[rows_of(lb)] + _mod_row(mod_ref, 2, lb) * out[rows_of(lb)])


def _layer_block_spec(arr, bb, layer):
    shape = arr.shape[2:]
    zeros = (0,) * len(shape)
    return pl.BlockSpec((None, bb) + shape, lambda i, j: (layer, i) + zeros)


def _resident_spec(arr, layer=None):
    if layer is None:
        zeros = (0,) * arr.ndim
        return pl.BlockSpec(arr.shape, lambda i, j: zeros, pipeline_mode=pl.Buffered(1))
    zeros = (0,) * (arr.ndim - 1)
    return pl.BlockSpec((None,) + arr.shape[1:], lambda i, j: (layer,) + zeros, pipeline_mode=pl.Buffered(1))


def _rows_operand(x2d, batch, t, bb, tc):
    d = x2d.shape[1]
    if tc == t:
        return x2d, pl.BlockSpec((bb * tc, d), lambda i, j: (i, 0))
    return x2d.reshape(batch, t, d), pl.BlockSpec((bb, tc, d), lambda i, j: (i, j, 0))


def _params():
    return pltpu.CompilerParams(dimension_semantics=("parallel", "arbitrary"), vmem_limit_bytes=VMEM_LIMIT)


def _blocking(batch, t, split=1, rows=256):
    if t >= 64:
        split = split if batch % split == 0 else 1
        return split, min(t, rows // split)
    return min(batch, 64 // t), t


def _mix_call(x2d, mod, g1, w_in, cos_t, sin_t, states, poolw, vecs, mu, wlora, alora, glora, w_br, w_out,
              batch, t, pos0, layer, zero_init):
    bb, tc = _blocking(batch, t, MIX_SPLIT, MIX_ROWS)
    nt = t // tc
    tp = _round_up(tc, SUBLANES)
    d = x2d.shape[1]
    out_structs = [jax.ShapeDtypeStruct(st.shape, F32) for st in states]
    state_specs = [_layer_block_spec(st, bb, layer) for st in states]
    weights = (poolw, vecs, mu, wlora, alora, glora, w_br, w_out)
    mod, mod_spec = _mod_operand(mod, bb)
    if zero_init:
        operands = (*weights, *states)
        operand_specs = [_resident_spec(a, layer) for a in weights] + [pl.BlockSpec(memory_space=pl.ANY)] * 5
        first_state = 6 + len(weights)
    else:
        operands = (*states, *weights)
        operand_specs = state_specs + [_resident_spec(a, layer) for a in weights]
        first_state = 6
    x_op, x_spec = _rows_operand(x2d, batch, t, bb, tc)

    def block_table(tab):
        return jnp.tile(tab.reshape(nt, tc, MIX_W), (1, bb, 1)).reshape(nt * bb * tc, MIX_W)

    outs = pl.pallas_call(
        functools.partial(_mix_kernel, bb=bb, tc=tc, pos0=pos0, zero_init=zero_init),
        grid=(batch // bb, nt),
        in_specs=[x_spec, mod_spec,
                  _resident_spec(g1, layer), _resident_spec(w_in, layer),
                  pl.BlockSpec((bb * tc, MIX_W), lambda i, j: (j, 0)),
                  pl.BlockSpec((bb * tc, MIX_W), lambda i, j: (j, 0))]
        + operand_specs,
        out_specs=[x_spec] + state_specs,
        out_shape=[jax.ShapeDtypeStruct(x_op.shape, F32)] + out_structs,
        input_output_aliases={first_state + k: 1 + k for k in range(5)},
        scratch_shapes=[pltpu.VMEM((bb * tc, d), F32),
                        pltpu.VMEM((bb * tc, IN_COLS), F32),
                        pltpu.VMEM((bb * tc, d), F32),
                        pltpu.VMEM((HEADS, tc, tc), F32),
                        pltpu.VMEM((bb, POOL_HIST + tp, MIX_W), F32),
                        pltpu.VMEM((bb, HIST + tp, MIX_W), F32),
                        pltpu.VMEM((bb, HIST + tp, d), F32),
                        pltpu.VMEM((bb * tc, MIX_W), F32),
                        pltpu.VMEM((bb * tc, MIX_W), F32),
                        pltpu.VMEM((bb * tc, MIX_W), F32),
                        pltpu.VMEM((bb * tc, COL_GATE - COL_RWKV), F32)],
        compiler_params=_params(),
        name="mix",
    )(x_op, mod, g1, w_in, block_table(cos_t), block_table(sin_t), *operands)
    return outs[0].reshape(x2d.shape), tuple(outs[1:])


def _ffn_kernel(*refs, bb, tc, final, zero_init):
    refs = list(refs)
    x1_ref, mod_ref, g_ref, wup_ref, ffnw_ref, wdown_ref, fg_ref = refs[:7]
    del refs[:7]
    st0_ref = refs.pop(0)
    o_ref, st_ref, h_s, ext, act_s = refs
    j = pl.program_id(1)
    seqs = range(bb)

    def rows_of(lb):
        return slice(lb * tc, (lb + 1) * tc)

    if tc < SUBLANES:
        def step_rows(t):
            return slice(t * bb, (t + 1) * bb)

        x1 = x1_ref[...]
        hn = _rmsnorm(x1, g_ref[...])
        for t in range(tc):
            h_s[step_rows(t)] = hn[step_rows(t)] * (1.0 + _mod_rows(mod_ref, 4)) + _mod_rows(mod_ref, 3)
        rows_hist = 2 * bb
        if zero_init:
            ext[0:rows_hist] = jnp.zeros((rows_hist, 2 * D_FF), F32)
        else:
            for lb in seqs:
                for k in range(2):
                    ext[k * bb + lb:k * bb + lb + 1] = st0_ref[lb, k:k + 1, :]
        ext[rows_hist:rows_hist + tc * bb] = _wdot(h_s[...], wup_ref[...])

        def slab(t, lo, width):
            return ext[rows_hist + t * bb:rows_hist + (t + 1) * bb, lo:lo + width]

        def conv_step(t, lo, width):
            w = ffnw_ref[:, lo:lo + width]
            return (w[2:3] * slab(t, lo, width) + w[1:2] * slab(t - 1, lo, width)
                    + w[0:1] * slab(t - 2, lo, width))

        for lo in range(0, D_FF, CONV_COLS):
            width = min(CONV_COLS, D_FF - lo)
            for t in range(tc):
                act_s[step_rows(t), lo:lo + width] = _silu(conv_step(t, lo, width)) * conv_step(t, D_FF + lo, width)
        for lb in seqs:
            for k in range(2):
                row = rows_hist + (tc - 2 + k) * bb + lb
                st_ref[lb, k:k + 1, :] = ext[row:row + 1]
        dn = _wdot(act_s[...], wdown_ref[...])
        for t in range(tc):
            o_ref[step_rows(t)] = x1[step_rows(t)] + _mod_rows(mod_ref, 5) * dn[step_rows(t)]
        if final:
            o_ref[...] = _rmsnorm(o_ref[...], fg_ref[...])
        return

    x1 = _read_rows(x1_ref)
    hn = _rmsnorm(x1, g_ref[...])
    for lb in seqs:
        h_s[rows_of(lb)] = hn[rows_of(lb)] * (1.0 + _mod_row(mod_ref, 4, lb)) + _mod_row(mod_ref, 3, lb)

    @pl.when(j == 0)
    def _init():
        ext[...] = jnp.zeros(ext.shape, F32)
        if not zero_init:
            ext[:, HIST - 2:HIST] = st0_ref[...]

    if bb == 1:
        ext[0, HIST:HIST + tc] = _wdot(h_s[...], wup_ref[...])
    else:
        up = _wdot(h_s[...], wup_ref[...])
        for lb in seqs:
            ext[lb, HIST:HIST + tc] = up[rows_of(lb)]

    for lb in seqs:
        def conv_cols(lo, width):
            e = ext[lb, :, lo:lo + width]
            w = ffnw_ref[:, lo:lo + width]
            y = w[2:3] * e + w[1:2] * pltpu.roll(e, 1, 0) + w[0:1] * pltpu.roll(e, 2, 0)
            return y[HIST:HIST + tc]

        for lo in range(0, D_FF, CONV_COLS):
            width = min(CONV_COLS, D_FF - lo)
            act_s[rows_of(lb), lo:lo + width] = _silu(conv_cols(lo, width)) * conv_cols(D_FF + lo, width)
        new_st = ext[lb, HIST + tc - 2:HIST + tc]
        ext[lb, HIST - 2:HIST] = new_st
        st_ref[lb] = new_st

    dn = _wdot(act_s[...], wdown_ref[...])
    for lb in seqs:
        _write_seq_rows(o_ref, lb, tc, x1[rows_of(lb)] + _mod_row(mod_ref, 5, lb) * dn[rows_of(lb)])
    if final:
        o_ref[...] = _rmsnorm(o_ref[...], fg_ref[...])


def _ffn_call(x1, mod, g2, w_up, ffn_w, w_down, final_g, state, batch, t, final, layer, zero_init):
    bb, tc = _blocking(batch, t, 1, FFN_ROWS)
    nt = t // tc
    tp = _round_up(tc, SUBLANES)
    d = x1.shape[1]
    st_struct = jax.ShapeDtypeStruct(state.shape, F32)
    st_spec = _layer_block_spec(state, bb, layer)
    mod, mod_spec = _mod_operand(mod, bb)
    time_major = tc < SUBLANES
    if time_major:
        x1 = x1.reshape(batch // bb, bb, tc, d).transpose(0, 2, 1, 3).reshape(batch * tc, d)
    x_op, x_spec = _rows_operand(x1, batch, t, bb, tc)
    out, new_state = pl.pallas_call(
        functools.partial(_ffn_kernel, bb=bb, tc=tc, final=final, zero_init=zero_init),
        grid=(batch // bb, nt),
        in_specs=[x_spec, mod_spec,
                  _resident_spec(g2, layer), _resident_spec(w_up, layer), _resident_spec(ffn_w, layer),
                  _resident_spec(w_down, layer), _resident_spec(final_g)]
        + [pl.BlockSpec(memory_space=pl.ANY) if zero_init else st_spec],
        out_specs=[x_spec, st_spec],
        out_shape=[jax.ShapeDtypeStruct(x_op.shape, F32), st_struct],
        input_output_aliases={7: 1},
        scratch_shapes=[pltpu.VMEM((bb * tc, d), F32),
                        pltpu.VMEM(((tc + 2) * bb, 2 * D_FF) if tc < SUBLANES else (bb, HIST + tp, 2 * D_FF), F32),
                        pltpu.VMEM((bb * tc, D_FF), F32)],
        compiler_params=_params(),
        name="ffn",
    )(x_op, mod, g2, w_up, ffn_w, w_down, final_g, state)
    out = out.reshape(x1.shape)
    if time_major:
        out = out.reshape(batch // bb, tc, bb, d).transpose(0, 2, 1, 3).reshape(batch * tc, d)
    return out, new_state


def _rope_tables(t, pos0):
    half = HEAD_D // 2
    inv = ROPE_BASE ** (-jnp.arange(half, dtype=F32) / half)
    pos = pos0 + jnp.arange(t, dtype=jnp.int32)
    ang = pos.astype(F32)[:, None] * inv[None, :]
    cos, sin = jnp.cos(ang), jnp.sin(ang)
    cos_t = jnp.tile(jnp.concatenate([cos, cos], axis=-1), (1, HEADS))
    sin_t = jnp.tile(jnp.concatenate([-sin, sin], axis=-1), (1, HEADS))
    return cos_t, sin_t


def _block_diag(pool_w):
    groups, gw, _ = pool_w.shape
    out = jnp.zeros((groups * gw, groups * gw), pool_w.dtype)
    for g in range(groups):
        out = out.at[g * gw:(g + 1) * gw, g * gw:(g + 1) * gw].set(pool_w[g])
    return out


def kernel(x_prompt, x_sample, c_prompt, c_sample, state_pool, state_ret, state_sconv, state_shift, state_wkv, state_ffn, w_ada, b_ada, norm1_g, norm2_g, w_in, pool_w, pool_scale, sc_w, rw_mu, rw_w0, rw_w_lora, rw_a0, rw_a_lora, rw_g_lora, rw_k_k, rw_k_a, rw_r_k, rw_ln_g, rw_ln_b, w_br, w_out, w_up, ffn_w, w_down, final_g):
    depth = w_in.shape[0]
    bp, tp_, d = x_prompt.shape
    bs, ts, _ = x_sample.shape

    mods = _mod_call(jnp.concatenate([c_prompt, c_sample], axis=0), w_ada, b_ada)
    cos_p, sin_p = _rope_tables(tp_, 0)
    cos_s, sin_s = _rope_tables(ts, PAST_LEN)
    final_g2 = final_g.reshape(1, d)

    xp = x_prompt.reshape(bp * tp_, d)
    xs = x_sample.reshape(bs * ts, d)
    st_s = (state_pool, state_ret, state_sconv, state_shift.reshape(depth, bs, 1, d), state_wkv)
    st_p = tuple(jnp.zeros((depth, bp) + st.shape[2:], F32) for st in st_s)
    ffn_s = state_ffn
    ffn_p = jnp.zeros((depth, bp) + state_ffn.shape[2:], F32)
    vec_rows = [pool_scale, sc_w[:, 0], sc_w[:, 1], sc_w[:, 2], rw_w0, rw_a0, rw_k_k, rw_k_a,
                rw_r_k.reshape(depth, MIX_W), rw_ln_g, rw_ln_b]
    vecs = jnp.concatenate([jnp.stack(vec_rows, axis=1),
                            jnp.zeros((depth, N_VECS - len(vec_rows), MIX_W), F32)], axis=1)
    mix_w = (norm1_g.reshape(depth, 1, d), w_in.astype(BF16))
    seq_w = (jnp.stack([_block_diag(pool_w[l]) for l in range(depth)]).astype(BF16), vecs,
             rw_mu.reshape(depth, 1, d), rw_w_lora, rw_a_lora, rw_g_lora, w_br.astype(BF16), w_out.astype(BF16))
    ffn_ws = (norm2_g.reshape(depth, 1, d), w_up.astype(BF16), ffn_w, w_down.astype(BF16), final_g2)
    for l in range(depth):
        mod_p, mod_s = mods[l, :, :bp], mods[l, :, bp:]
        final = l == depth - 1

        x1, st_p = _mix_call(xp, mod_p, *mix_w, cos_p, sin_p, st_p, *seq_w, batch=bp, t=tp_, pos0=0,
                             layer=l, zero_init=True)
        xp, ffn_p = _ffn_call(x1, mod_p, *ffn_ws, ffn_p, batch=bp, t=tp_, final=final, layer=l, zero_init=True)
        x1, st_s = _mix_call(xs, mod_s, *mix_w, cos_s, sin_s, st_s, *seq_w, batch=bs, t=ts, pos0=PAST_LEN,
                             layer=l, zero_init=False)
        xs, ffn_s = _ffn_call(x1, mod_s, *ffn_ws, ffn_s, batch=bs, t=ts, final=final, layer=l, zero_init=False)

    def finish(st, ffn_st, batch):
        pool, ret, sc, sh, wkv = st
        return (pool, ret, sc, sh.reshape(depth, batch, d), wkv, ffn_st)

    return (xp.reshape(bp, tp_, d), xs.reshape(bs, ts, d), *finish(st_p, ffn_p, bp), *finish(st_s, ffn_s, bs))
```

```python
import functools
import math

import jax
import jax.numpy as jnp
from jax import lax
from jax.experimental import pallas as pl
from jax.experimental.pallas import tpu as pltpu

F32 = jnp.float32
BF16 = jnp.bfloat16

D_MODEL = 1024
MIX_W = 256
HEADS = 4
HEAD_D = 64
POOL_PAST = 15
ROPE_BASE = 10000.0
LORA_W = 64
LORA_A = 64
LORA_G = 128
D_FF = 2816
PAST_LEN = 16384
NORM_EPS = 1e-6
GN_EPS = 1e-6
RWKV_LN_EPS = 64e-5
L2_EPS = 1e-12

COL_RET = MIX_W
COL_SC = COL_RET + 4 * MIX_W
COL_RWKV = COL_SC + 3 * MIX_W
COL_GATE = COL_RWKV + 3 * MIX_W + LORA_W + LORA_A + LORA_G
IN_COLS = COL_GATE + 4 * D_MODEL

SUBLANES = 8
VMEM_LIMIT = 56 * 1024 * 1024
RWKV_CHUNK = 64
INV_BASE = 16
POOL_HIST = 16
HIST = SUBLANES
CONV_COLS = 512
MIX_SPLIT = 8
MIX_ROWS = 512
FFN_ROWS = 512
SHORT_ROWS = 64
POOL_WINDOWS = (2, 4, 8, 16)
HEAD_SHIFT = 6

(V_POOL_SCALE, V_SC0, V_SC1, V_SC2, V_W0, V_A0, V_KK, V_KA, V_RK, V_LNG, V_LNB) = range(11)
N_VECS = 16


def _bdot(a, b):
    return jnp.dot(a.astype(BF16), b.astype(BF16), preferred_element_type=F32)


def _bdot_nt(a, b):
    return lax.dot_general(a.astype(BF16), b.astype(BF16), (((1,), (1,)), ((), ())), preferred_element_type=F32)


def _bdot_tn(a, b):
    return lax.dot_general(a.astype(BF16), b.astype(BF16), (((0,), (0,)), ((), ())), preferred_element_type=F32)


def _wdot(a, w_bf16):
    return jnp.dot(a.astype(BF16), w_bf16, preferred_element_type=F32)


def _sigmoid(x):
    return 1.0 / (1.0 + jnp.exp(-x))


def _silu(x):
    half = 0.5 * x
    return half + half * jnp.tanh(half)


def _softplus(x):
    return jnp.maximum(x, 0.0) + jnp.log(1.0 + jnp.exp(-jnp.abs(x)))


def _rmsnorm(x, g):
    return x * lax.rsqrt(jnp.mean(x * x, axis=-1, keepdims=True) + NORM_EPS) * g


def _round_up(n, m):
    return (n + m - 1) // m * m


def _mod_row(mod_ref, k, lb):
    if len(mod_ref.shape) == 4:
        return mod_ref[k, lb]
    return mod_ref[k, lb:lb + 1, :]


def _mod_rows(mod_ref, k):
    rows = mod_ref[k]
    return rows.reshape(rows.shape[0], rows.shape[-1])


def _mod_operand(mod, bb):
    six, _, d = mod.shape
    if bb % SUBLANES:
        return mod[:, :, None, :], pl.BlockSpec((six, bb, 1, d), lambda i, j: (0, i, 0, 0))
    return mod, pl.BlockSpec((six, bb, d), lambda i, j: (0, i, 0))


def _read_rows(ref):
    val = ref[...]
    return val.reshape(-1, val.shape[-1]) if val.ndim == 3 else val


def _write_seq_rows(ref, lb, tc, val):
    if len(ref.shape) == 3:
        ref[lb] = val
    else:
        ref[lb * tc:(lb + 1) * tc] = val


def _mod_kernel(c_ref, w_ref, b_ref, o_ref):
    o_ref[0, 0] = _wdot(_silu(c_ref[...]), w_ref[0].astype(BF16)) + b_ref[0]


def _mod_call(c_all, w_ada, b_ada):
    depth, d, cols = w_ada.shape
    rows = c_all.shape[0]
    cb = D_MODEL
    return pl.pallas_call(
        _mod_kernel,
        grid=(depth, cols // cb),
        in_specs=[pl.BlockSpec((rows, d), lambda l, c: (0, 0)),
                  pl.BlockSpec((1, d, cb), lambda l, c: (l, 0, c)),
                  pl.BlockSpec((1, 1, cb), lambda l, c: (l, 0, c))],
        out_specs=pl.BlockSpec((1, 1, rows, cb), lambda l, c: (l, c, 0, 0)),
        out_shape=jax.ShapeDtypeStruct((depth, cols // cb, rows, cb), F32),
        name="adaln_mod",
    )(c_all, w_ada, b_ada.reshape(depth, 1, cols))


def _tri_inverse_minus_eye(lowers, n):
    i = lax.broadcasted_iota(jnp.int32, (n, n), 0)
    j = lax.broadcasted_iota(jnp.int32, (n, n), 1)

    def same_block(size):
        sh = int(math.log2(size))
        return jnp.right_shift(i, sh) == jnp.right_shift(j, sh)

    base = min(n, INV_BASE)
    if base < n:
        diag_mask = same_block(base)
        xs = [jnp.where(diag_mask, low, 0.0) for low in lowers]
    else:
        xs = list(lowers)
    powers, p = xs, 1
    while 2 * p < base:
        powers = [_bdot(pw, pw) for pw in powers]
        prods = [_bdot(x, pw) for x, pw in zip(xs, powers)]
        xs = [x + pw + pr for x, pw, pr in zip(xs, powers, prods)]
        p *= 2
    size = base
    while size < n:
        off_mask = same_block(2 * size) & jnp.logical_not(same_block(size))
        offs = [jnp.where(off_mask, low, 0.0) for low in lowers]
        lefts = [off + _bdot(x, off) for x, off in zip(xs, offs)]
        xs = [x + left + _bdot(left, x) for x, left in zip(xs, lefts)]
        size *= 2
    return xs


def _mix_kernel(*refs, bb, tc, pos0, zero_init):
    refs = list(refs)
    x_ref, mod_ref, g_ref, win_ref, cos_ref, sin_ref = refs[:6]
    del refs[:6]
    if not zero_init:
        pool0_ref, ret0_ref, sc0_ref, sh0_ref, wkv0_ref = refs[:5]
        del refs[:5]
    poolw_ref, vecs_ref, mu_ref, wlora_ref, alora_ref, glora_ref, wbr_ref, wout_ref = refs[:8]
    del refs[:8 + (5 if zero_init else 0)]
    x1_ref, pool_ref, ret_ref, sc_ref, sh_ref, wkv_ref = refs[:6]
    h_s, z_s, y_s, dm_s, pool_ext, sc_ext, sh_ext, ywkv, wsum_s, conv_s, prev_s = refs[6:]
    j = pl.program_id(1)
    seqs = range(bb)
    heads = range(HEADS)
    hsl = [slice(h * HEAD_D, (h + 1) * HEAD_D) for h in heads]

    def rows_of(lb):
        return slice(lb * tc, (lb + 1) * tc)

    lane = lax.broadcasted_iota(jnp.int32, (1, MIX_W), 1)
    row_i = lax.broadcasted_iota(jnp.int32, (tc, 1), 0)
    col_i = lax.broadcasted_iota(jnp.int32, (1, tc), 1)
    row_f = row_i.astype(F32)
    lgs = [math.log1p(-(2.0 ** (-5.0 - h))) for h in heads]

    @pl.when(j == 0)
    def _init():
        diff = (row_i - col_i).astype(F32)
        for h in heads:
            dm_s[h] = jnp.where(diff >= 0, jnp.exp(lgs[h] * jnp.maximum(diff, 0.0)), 0.0)
        pool_ext[...] = jnp.zeros(pool_ext.shape, F32)
        sc_ext[...] = jnp.zeros(sc_ext.shape, F32)
        sh_ext[...] = jnp.zeros(sh_ext.shape, F32)
        if zero_init:
            ret_ref[...] = jnp.zeros(ret_ref.shape, F32)
            wkv_ref[...] = jnp.zeros(wkv_ref.shape, F32)
        else:
            pool_ext[:, POOL_HIST - POOL_PAST:POOL_HIST] = pool0_ref[...]
            sc_ext[:, HIST - 2:HIST] = sc0_ref[...]
            sh_ext[:, HIST - 1:HIST] = sh0_ref[...]
            ret_ref[...] = ret0_ref[...]
            wkv_ref[...] = wkv0_ref[...]

    def vec(row):
        return vecs_ref[row:row + 1, :]

    x = _read_rows(x_ref)
    hn = _rmsnorm(x, g_ref[...])
    for lb in seqs:
        h_s[rows_of(lb)] = hn[rows_of(lb)] * (1.0 + _mod_row(mod_ref, 1, lb)) + _mod_row(mod_ref, 0, lb)
    z_s[...] = _wdot(h_s[...], win_ref[...])

    def gated_proj(n):
        proj = _wdot(y_s[:, n * MIX_W:(n + 1) * MIX_W], wbr_ref[n])
        return _sigmoid(z_s[:, COL_GATE + n * D_MODEL:COL_GATE + (n + 1) * D_MODEL]) * proj

    rows_all = bb * tc
    grp = jnp.right_shift(lane, HEAD_SHIFT)

    def by_group(values):
        out = values[-1]
        for g in range(len(values) - 2, -1, -1):
            out = jnp.where(grp == g, values[g], out)
        return out

    win = by_group(POOL_WINDOWS)
    brow = lax.broadcasted_iota(jnp.int32, (rows_all, 1), 0)
    cnt = jnp.minimum(win, pos0 + j * tc + jnp.bitwise_and(brow, tc - 1) + 1).astype(F32)
    first_half = jnp.bitwise_and(lane, HEAD_D - 1) < HEAD_D // 2
    cos = cos_ref[...]
    sin = sin_ref[...]

    def rope(t):
        swapped = jnp.where(first_half, pltpu.roll(t, MIX_W - HEAD_D // 2, 1), pltpu.roll(t, HEAD_D // 2, 1))
        return t * cos + swapped * sin

    si = lax.broadcasted_iota(jnp.int32, (MIX_W, MIX_W), 0)
    sj = lax.broadcasted_iota(jnp.int32, (MIX_W, MIX_W), 1)
    head_ones = (jnp.right_shift(si, HEAD_SHIFT) == jnp.right_shift(sj, HEAD_SHIFT)).astype(BF16)

    def split_dot(ones, t):
        hi = t.astype(BF16)
        lo = (t - hi.astype(F32)).astype(BF16)
        return jnp.dot(ones, hi, preferred_element_type=F32) + jnp.dot(ones, lo, preferred_element_type=F32)

    def head_sum(t):
        hi = t.astype(BF16)
        lo = (t - hi.astype(F32)).astype(BF16)
        return (jnp.dot(hi, head_ones, preferred_element_type=F32)
                + jnp.dot(lo, head_ones, preferred_element_type=F32))

    c = min(RWKV_CHUNK, tc)
    nsub = tc // c
    merged = c % SUBLANES == 0
    ci = lax.broadcasted_iota(jnp.int32, (c, c), 0)
    cj = lax.broadcasted_iota(jnp.int32, (c, c), 1)
    strict = ci > cj
    incl = ci >= cj
    ti = lax.broadcasted_iota(jnp.int32, (rows_all, rows_all), 0)
    tj = lax.broadcasted_iota(jnp.int32, (rows_all, rows_all), 1)
    shift_c = int(math.log2(c))
    tri = ((ti >= tj) & (jnp.right_shift(ti, shift_c) == jnp.right_shift(tj, shift_c))).astype(BF16)

    for lb in seqs:
        rows = rows_of(lb)
        pool_ext[lb, POOL_HIST:POOL_HIST + tc] = z_s[rows, 0:MIX_W]
        e = pool_ext[lb]
        sums, width = [], 1
        for window in POOL_WINDOWS:
            while width < window:
                e = e + pltpu.roll(e, width, 0)
                width *= 2
            sums.append(e)
        wsum = by_group(sums)
        wsum_s[rows] = wsum[POOL_HIST:POOL_HIST + tc]
        new_pool = pool_ext[lb, tc + POOL_HIST - POOL_PAST:tc + POOL_HIST]
        pool_ext[lb, POOL_HIST - POOL_PAST:POOL_HIST] = new_pool
        pool_ref[lb] = new_pool

        sc_ext[lb, HIST:HIST + tc] = (z_s[rows, COL_SC + 2 * MIX_W:COL_SC + 3 * MIX_W]
                                      * z_s[rows, COL_SC:COL_SC + MIX_W])
        e = sc_ext[lb]
        conv = vec(V_SC2) * e + vec(V_SC1) * pltpu.roll(e, 1, 0) + vec(V_SC0) * pltpu.roll(e, 2, 0)
        conv_s[rows] = conv[HIST:HIST + tc]
        new_sc = sc_ext[lb, HIST + tc - 2:HIST + tc]
        sc_ext[lb, HIST - 2:HIST] = new_sc
        sc_ref[lb] = new_sc

        sh_ext[lb, HIST:HIST + tc] = z_s[rows, COL_RWKV:COL_GATE]
        prev_s[rows] = pltpu.roll(sh_ext[lb], 1, 0)[HIST:HIST + tc]
        new_sh = sh_ext[lb, HIST + tc - 1:HIST + tc]
        sh_ext[lb, HIST - 1:HIST] = new_sh
        sh_ref[lb] = new_sh

    u = z_s[:, 0:MIX_W]
    y_s[:, 0:MIX_W] = _wdot(wsum_s[...] / cnt - u, poolw_ref[...]) * vec(V_POOL_SCALE)
    y_s[:, 2 * MIX_W:3 * MIX_W] = z_s[:, COL_SC + MIX_W:COL_SC + 2 * MIX_W] * conv_s[...]

    q_all = rope(z_s[:, COL_RET:COL_RET + MIX_W])
    k_all = rope(z_s[:, COL_RET + MIX_W:COL_RET + 2 * MIX_W]) * (HEAD_D ** -0.5)
    v_all = z_s[:, COL_RET + 2 * MIX_W:COL_RET + 3 * MIX_W]

    zz = z_s[:, COL_RWKV:COL_GATE]
    zs = zz + (prev_s[...] - zz) * mu_ref[...]
    r = zs[:, 0:MIX_W]
    kx = zs[:, MIX_W:2 * MIX_W]
    vx = zs[:, 2 * MIX_W:3 * MIX_W]
    o3 = 3 * MIX_W
    wl = zs[:, o3:o3 + LORA_W]
    al = zs[:, o3 + LORA_W:o3 + LORA_W + LORA_A]
    gl = zs[:, o3 + LORA_W + LORA_A:]
    wlog = -_softplus(-(vec(V_W0) + _bdot(jnp.tanh(wl), wlora_ref[...]))) - 0.5
    logw = -jnp.exp(wlog)
    asig = _sigmoid(vec(V_A0) + _bdot(al, alora_ref[...]))
    g_out = _bdot(_sigmoid(gl), glora_ref[...])
    k2 = kx * (1.0 + (asig - 1.0) * vec(V_KA))
    kk_raw = kx * vec(V_KK)
    kk = kk_raw * lax.rsqrt(jnp.maximum(head_sum(kk_raw * kk_raw), L2_EPS * L2_EPS))
    cum = split_dot(tri, logw)
    p_in = jnp.exp(cum)
    p_inv = jnp.exp(-cum)
    rw = dict(a=-kk * jnp.exp(cum - logw), b=kk * asig * p_inv, k=k2 * p_inv, r=r * p_in, v=vx, p_in=p_in)

    pairs = [(lb, h) for lb in seqs for h in heads]
    qs = [q_all[rows_of(lb), hsl[h]] for lb, h in pairs]
    ks = [k_all[rows_of(lb), hsl[h]] for lb, h in pairs]
    vs = [v_all[rows_of(lb), hsl[h]] for lb, h in pairs]
    states = [ret_ref[lb, h] for lb, h in pairs]
    np_ = range(len(pairs))
    scores = [_bdot_nt(qs[i], ks[i]) for i in np_]
    cross = [_bdot(qs[i] * jnp.exp(lgs[pairs[i][1]] * (row_f + 1.0)), states[i]) for i in np_]
    kv = [_bdot_tn(ks[i] * jnp.exp(lgs[pairs[i][1]] * (tc - 1.0 - row_f)), vs[i]) for i in np_]
    outs = [_bdot(scores[i] * dm_s[pairs[i][1]], vs[i]) + cross[i] for i in np_]
    for i, (lb, h) in enumerate(pairs):
        ret_ref[lb, h] = states[i] * math.exp(lgs[h] * tc) + kv[i]
        y_s[rows_of(lb), MIX_W + h * HEAD_D:MIX_W + (h + 1) * HEAD_D] = outs[i]
    o_all = y_s[:, MIX_W:2 * MIX_W]
    o_all = o_all * lax.rsqrt(head_sum(o_all * o_all) * (1.0 / HEAD_D) + GN_EPS)
    y_s[:, MIX_W:2 * MIX_W] = o_all * _silu(z_s[:, COL_RET + 3 * MIX_W:COL_RET + 4 * MIX_W])

    blocks = [(lb, sub, h) for lb in seqs for sub in range(nsub) for h in heads]

    def blk(name):
        return [rw[name][lb * tc + sub * c:lb * tc + (sub + 1) * c, hsl[h]] for lb, sub, h in blocks]

    a_l, b_l, k_l, r_l, v_l = blk("a"), blk("b"), blk("k"), blk("r"), blk("v")
    nb = range(len(blocks))
    if merged:
        bk_l = [jnp.concatenate([b_l[i], k_l[i]], axis=0) for i in nb]
        gram = [_bdot_nt(jnp.concatenate([a_l[i], r_l[i]], axis=0), bk_l[i]) for i in nb]
        g_ab, g_ak = [g[:c, :c] for g in gram], [g[:c, c:] for g in gram]
        g_rb, g_rk = [g[c:, :c] for g in gram], [g[c:, c:] for g in gram]
    else:
        g_ab = [_bdot_nt(a_l[i], b_l[i]) for i in nb]
        g_ak = [_bdot_nt(a_l[i], k_l[i]) for i in nb]
        g_rb = [_bdot_nt(r_l[i], b_l[i]) for i in nb]
        g_rk = [_bdot_nt(r_l[i], k_l[i]) for i in nb]
    l_ab = [jnp.where(strict, g, 0.0) for g in g_ab]
    l_ak = [jnp.where(strict, g, 0.0) for g in g_ak]
    m_rb = [jnp.where(incl, g, 0.0) for g in g_rb]
    m_rk = [jnp.where(incl, g, 0.0) for g in g_rk]
    n_inv = _tri_inverse_minus_eye(l_ab, c)
    if merged:
        lv = [_bdot(jnp.concatenate([l_ak[i], m_rk[i]], axis=0), v_l[i]) for i in nb]
        lakv, mv = [t[:c] for t in lv], [t[c:] for t in lv]
    else:
        lakv = [_bdot(l_ak[i], v_l[i]) for i in nb]
        mv = [_bdot(m_rk[i], v_l[i]) for i in nb]
    nx = [_bdot(n_inv[i], jnp.concatenate([lakv[i], a_l[i]], axis=1)) for i in nb]
    tlv = [lakv[i] + nx[i][:, :HEAD_D] for i in nb]
    ta = [a_l[i] + nx[i][:, HEAD_D:] for i in nb]

    wkv_states = [wkv_ref[lb, h] for lb, h in pairs]
    for sub in range(nsub):
        ids = [(lb * nsub + sub) * HEADS + h for lb, h in pairs]
        p_end = [rw["p_in"][lb * tc + (sub + 1) * c - 1:lb * tc + (sub + 1) * c, hsl[h]] for lb, h in pairs]
        u_mat = [_bdot_nt(ta[i], wkv_states[n]) + tlv[i] for n, i in enumerate(ids)]
        y_st = [_bdot_nt(r_l[i], wkv_states[n]) for n, i in enumerate(ids)]
        if merged:
            upd = [_bdot_tn(jnp.concatenate([u_mat[n], v_l[i]], axis=0), bk_l[i] * p_end[n])
                   for n, i in enumerate(ids)]
        else:
            upd = [_bdot_tn(u_mat[n], b_l[i] * p_end[n]) + _bdot_tn(v_l[i], k_l[i] * p_end[n])
                   for n, i in enumerate(ids)]
        y_u = [_bdot(m_rb[i], u_mat[n]) for n, i in enumerate(ids)]
        for n, i in enumerate(ids):
            lb, h = pairs[n]
            ywkv[lb * tc + sub * c:lb * tc + (sub + 1) * c, hsl[h]] = y_st[n] + y_u[n] + mv[i]
        wkv_states = [wkv_states[n] * p_end[n] + upd[n] for n in np_]
    for n, (lb, h) in enumerate(pairs):
        wkv_ref[lb, h] = wkv_states[n]

    y_all = ywkv[...]
    cen = y_all - head_sum(y_all) * (1.0 / HEAD_D)
    var = head_sum(cen * cen) * (1.0 / HEAD_D)
    yn = cen * lax.rsqrt(var + RWKV_LN_EPS) * vec(V_LNG) + vec(V_LNB)
    bonus = head_sum(r * k2 * vec(V_RK)) * vx
    y_s[:, 3 * MIX_W:4 * MIX_W] = (yn + bonus) * g_out

    out = _wdot(gated_proj(0) + gated_proj(1) + gated_proj(2) + gated_proj(3), wout_ref[...])
    for lb in seqs:
        _write_seq_rows(x1_ref, lb, tc, x[rows_of(lb)] + _mod_row(mod_ref, 2, lb) * out[rows_of(lb)])


def _layer_block_spec(arr, bb, layer):
    shape = arr.shape[2:]
    zeros = (0,) * len(shape)
    return pl.BlockSpec((None, bb) + shape, lambda i, j: (layer, i) + zeros)


def _resident_spec(arr, layer=None):
    if layer is None:
        zeros = (0,) * arr.ndim
        return pl.BlockSpec(arr.shape, lambda i, j: zeros, pipeline_mode=pl.Buffered(1))
    zeros = (0,) * (arr.ndim - 1)
    return pl.BlockSpec((None,) + arr.shape[1:], lambda i, j: (layer,) + zeros, pipeline_mode=pl.Buffered(1))


def _rows_operand(x2d, batch, t, bb, tc):
    d = x2d.shape[1]
    if tc == t:
        return x2d, pl.BlockSpec((bb * tc, d), lambda i, j: (i, 0))
    return x2d.reshape(batch, t, d), pl.BlockSpec((bb, tc, d), lambda i, j: (i, j, 0))


def _params():
    return pltpu.CompilerParams(dimension_semantics=("parallel", "arbitrary"), vmem_limit_bytes=VMEM_LIMIT)


def _blocking(batch, t, split=1, rows=256):
    if t >= SHORT_ROWS:
        split = split if batch % split == 0 else 1
        return split, min(t, rows // split)
    return min(batch, SHORT_ROWS // t), t


def _mix_call(x2d, mod, g1, w_in, cos_t, sin_t, states, poolw, vecs, mu, wlora, alora, glora, w_br, w_out,
              batch, t, pos0, layer, zero_init):
    bb, tc = _blocking(batch, t, MIX_SPLIT, MIX_ROWS)
    nt = t // tc
    tp = _round_up(tc, SUBLANES)
    d = x2d.shape[1]
    out_structs = [jax.ShapeDtypeStruct(st.shape, F32) for st in states]
    state_specs = [_layer_block_spec(st, bb, layer) for st in states]
    weights = (poolw, vecs, mu, wlora, alora, glora, w_br, w_out)
    mod, mod_spec = _mod_operand(mod, bb)
    if zero_init:
        operands = (*weights, *states)
        operand_specs = [_resident_spec(a, layer) for a in weights] + [pl.BlockSpec(memory_space=pl.ANY)] * 5
        first_state = 6 + len(weights)
    else:
        operands = (*states, *weights)
        operand_specs = state_specs + [_resident_spec(a, layer) for a in weights]
        first_state = 6
    x_op, x_spec = _rows_operand(x2d, batch, t, bb, tc)

    def block_table(tab):
        return jnp.tile(tab.reshape(nt, tc, MIX_W), (1, bb, 1)).reshape(nt * bb * tc, MIX_W)

    outs = pl.pallas_call(
        functools.partial(_mix_kernel, bb=bb, tc=tc, pos0=pos0, zero_init=zero_init),
        grid=(batch // bb, nt),
        in_specs=[x_spec, mod_spec,
                  _resident_spec(g1, layer), _resident_spec(w_in, layer),
                  pl.BlockSpec((bb * tc, MIX_W), lambda i, j: (j, 0)),
                  pl.BlockSpec((bb * tc, MIX_W), lambda i, j: (j, 0))]
        + operand_specs,
        out_specs=[x_spec] + state_specs,
        out_shape=[jax.ShapeDtypeStruct(x_op.shape, F32)] + out_structs,
        input_output_aliases={first_state + k: 1 + k for k in range(5)},
        scratch_shapes=[pltpu.VMEM((bb * tc, d), F32),
                        pltpu.VMEM((bb * tc, IN_COLS), F32),
                        pltpu.VMEM((bb * tc, d), F32),
                        pltpu.VMEM((HEADS, tc, tc), F32),
                        pltpu.VMEM((bb, POOL_HIST + tp, MIX_W), F32),
                        pltpu.VMEM((bb, HIST + tp, MIX_W), F32),
                        pltpu.VMEM((bb, HIST + tp, d), F32),
                        pltpu.VMEM((bb * tc, MIX_W), F32),
                        pltpu.VMEM((bb * tc, MIX_W), F32),
                        pltpu.VMEM((bb * tc, MIX_W), F32),
                        pltpu.VMEM((bb * tc, COL_GATE - COL_RWKV), F32)],
        compiler_params=_params(),
        name="mix",
    )(x_op, mod, g1, w_in, block_table(cos_t), block_table(sin_t), *operands)
    return outs[0].reshape(x2d.shape), tuple(outs[1:])


def _ffn_kernel(*refs, bb, tc, final, zero_init):
    refs = list(refs)
    x1_ref, mod_ref, g_ref, wup_ref, ffnw_ref, wdown_ref, fg_ref = refs[:7]
    del refs[:7]
    st0_ref = refs.pop(0)
    o_ref, st_ref, h_s, ext, act_s = refs
    j = pl.program_id(1)
    seqs = range(bb)

    def rows_of(lb):
        return slice(lb * tc, (lb + 1) * tc)

    if tc < SUBLANES:
        def step_rows(t):
            return slice(t * bb, (t + 1) * bb)

        x1 = x1_ref[...]
        hn = _rmsnorm(x1, g_ref[...])
        for t in range(tc):
            h_s[step_rows(t)] = hn[step_rows(t)] * (1.0 + _mod_rows(mod_ref, 4)) + _mod_rows(mod_ref, 3)
        rows_hist = 2 * bb
        if zero_init:
            ext[0:rows_hist] = jnp.zeros((rows_hist, 2 * D_FF), F32)
        else:
            for lb in seqs:
                for k in range(2):
                    ext[k * bb + lb:k * bb + lb + 1] = st0_ref[lb, k:k + 1, :]
        ext[rows_hist:rows_hist + tc * bb] = _wdot(h_s[...], wup_ref[...])

        def slab(t, lo, width):
            return ext[rows_hist + t * bb:rows_hist + (t + 1) * bb, lo:lo + width]

        def conv_step(t, lo, width):
            w = ffnw_ref[:, lo:lo + width]
            return (w[2:3] * slab(t, lo, width) + w[1:2] * slab(t - 1, lo, width)
                    + w[0:1] * slab(t - 2, lo, width))

        for lo in range(0, D_FF, CONV_COLS):
            width = min(CONV_COLS, D_FF - lo)
            for t in range(tc):
                act_s[step_rows(t), lo:lo + width] = _silu(conv_step(t, lo, width)) * conv_step(t, D_FF + lo, width)
        for lb in seqs:
            for k in range(2):
                row = rows_hist + (tc - 2 + k) * bb + lb
                st_ref[lb, k:k + 1, :] = ext[row:row + 1]
        dn = _wdot(act_s[...], wdown_ref[...])
        for t in range(tc):
            o_ref[step_rows(t)] = x1[step_rows(t)] + _mod_rows(mod_ref, 5) * dn[step_rows(t)]
        if final:
            o_ref[...] = _rmsnorm(o_ref[...], fg_ref[...])
        return

    x1 = _read_rows(x1_ref)
    hn = _rmsnorm(x1, g_ref[...])
    for lb in seqs:
        h_s[rows_of(lb)] = hn[rows_of(lb)] * (1.0 + _mod_row(mod_ref, 4, lb)) + _mod_row(mod_ref, 3, lb)

    @pl.when(j == 0)
    def _init():
        ext[...] = jnp.zeros(ext.shape, F32)
        if not zero_init:
            ext[:, HIST - 2:HIST] = st0_ref[...]

    if bb == 1:
        ext[0, HIST:HIST + tc] = _wdot(h_s[...], wup_ref[...])
    else:
        up = _wdot(h_s[...], wup_ref[...])
        for lb in seqs:
            ext[lb, HIST:HIST + tc] = up[rows_of(lb)]

    for lb in seqs:
        def conv_cols(lo, width):
            e = ext[lb, :, lo:lo + width]
            w = ffnw_ref[:, lo:lo + width]
            y = w[2:3] * e + w[1:2] * pltpu.roll(e, 1, 0) + w[0:1] * pltpu.roll(e, 2, 0)
            return y[HIST:HIST + tc]

        for lo in range(0, D_FF, CONV_COLS):
            width = min(CONV_COLS, D_FF - lo)
            act_s[rows_of(lb), lo:lo + width] = _silu(conv_cols(lo, width)) * conv_cols(D_FF + lo, width)
        new_st = ext[lb, HIST + tc - 2:HIST + tc]
        ext[lb, HIST - 2:HIST] = new_st
        st_ref[lb] = new_st

    dn = _wdot(act_s[...], wdown_ref[...])
    for lb in seqs:
        _write_seq_rows(o_ref, lb, tc, x1[rows_of(lb)] + _mod_row(mod_ref, 5, lb) * dn[rows_of(lb)])
    if final:
        o_ref[...] = _rmsnorm(o_ref[...], fg_ref[...])


def _ffn_call(x1, mod, g2, w_up, ffn_w, w_down, final_g, state, batch, t, final, layer, zero_init):
    bb, tc = _blocking(batch, t, 1, FFN_ROWS)
    nt = t // tc
    tp = _round_up(tc, SUBLANES)
    d = x1.shape[1]
    st_struct = jax.ShapeDtypeStruct(state.shape, F32)
    st_spec = _layer_block_spec(state, bb, layer)
    mod, mod_spec = _mod_operand(mod, bb)
    time_major = tc < SUBLANES
    if time_major:
        x1 = x1.reshape(batch // bb, bb, tc, d).transpose(0, 2, 1, 3).reshape(batch * tc, d)
    x_op, x_spec = _rows_operand(x1, batch, t, bb, tc)
    out, new_state = pl.pallas_call(
        functools.partial(_ffn_kernel, bb=bb, tc=tc, final=final, zero_init=zero_init),
        grid=(batch // bb, nt),
        in_specs=[x_spec, mod_spec,
                  _resident_spec(g2, layer), _resident_spec(w_up, layer), _resident_spec(ffn_w, layer),
                  _resident_spec(w_down, layer), _resident_spec(final_g)]
        + [pl.BlockSpec(memory_space=pl.ANY) if zero_init else st_spec],
        out_specs=[x_spec, st_spec],
        out_shape=[jax.ShapeDtypeStruct(x_op.shape, F32), st_struct],
        input_output_aliases={7: 1},
        scratch_shapes=[pltpu.VMEM((bb * tc, d), F32),
                        pltpu.VMEM(((tc + 2) * bb, 2 * D_FF) if tc < SUBLANES else (bb, HIST + tp, 2 * D_FF), F32),
                        pltpu.VMEM((bb * tc, D_FF), F32)],
        compiler_params=_params(),
        name="ffn",
    )(x_op, mod, g2, w_up, ffn_w, w_down, final_g, state)
    out = out.reshape(x1.shape)
    if time_major:
        out = out.reshape(batch // bb, tc, bb, d).transpose(0, 2, 1, 3).reshape(batch * tc, d)
    return out, new_state


def _rope_tables(t, pos0):
    half = HEAD_D // 2
    inv = ROPE_BASE ** (-jnp.arange(half, dtype=F32) / half)
    pos = pos0 + jnp.arange(t, dtype=jnp.int32)
    ang = pos.astype(F32)[:, None] * inv[None, :]
    cos, sin = jnp.cos(ang), jnp.sin(ang)
    cos_t = jnp.tile(jnp.concatenate([cos, cos], axis=-1), (1, HEADS))
    sin_t = jnp.tile(jnp.concatenate([-sin, sin], axis=-1), (1, HEADS))
    return cos_t, sin_t


def _block_diag(pool_w):
    groups, gw, _ = pool_w.shape
    out = jnp.zeros((groups * gw, groups * gw), pool_w.dtype)
    for g in range(groups):
        out = out.at[g * gw:(g + 1) * gw, g * gw:(g + 1) * gw].set(pool_w[g])
    return out


def kernel(x_prompt, x_sample, c_prompt, c_sample, state_pool, state_ret, state_sconv, state_shift, state_wkv, state_ffn, w_ada, b_ada, norm1_g, norm2_g, w_in, pool_w, pool_scale, sc_w, rw_mu, rw_w0, rw_w_lora, rw_a0, rw_a_lora, rw_g_lora, rw_k_k, rw_k_a, rw_r_k, rw_ln_g, rw_ln_b, w_br, w_out, w_up, ffn_w, w_down, final_g):
    depth = w_in.shape[0]
    bp, tp_, d = x_prompt.shape
    bs, ts, _ = x_sample.shape

    mods = _mod_call(jnp.concatenate([c_prompt, c_sample], axis=0), w_ada, b_ada)
    cos_p, sin_p = _rope_tables(tp_, 0)
    cos_s, sin_s = _rope_tables(ts, PAST_LEN)
    final_g2 = final_g.reshape(1, d)

    xp = x_prompt.reshape(bp * tp_, d)
    xs = x_sample.reshape(bs * ts, d)
    st_s = (state_pool, state_ret, state_sconv, state_shift.reshape(depth, bs, 1, d), state_wkv)
    st_p = tuple(jnp.zeros((depth, bp) + st.shape[2:], F32) for st in st_s)
    ffn_s = state_ffn
    ffn_p = jnp.zeros((depth, bp) + state_ffn.shape[2:], F32)
    vec_rows = [pool_scale, sc_w[:, 0], sc_w[:, 1], sc_w[:, 2], rw_w0, rw_a0, rw_k_k, rw_k_a,
                rw_r_k.reshape(depth, MIX_W), rw_ln_g, rw_ln_b]
    vecs = jnp.concatenate([jnp.stack(vec_rows, axis=1),
                            jnp.zeros((depth, N_VECS - len(vec_rows), MIX_W), F32)], axis=1)
    mix_w = (norm1_g.reshape(depth, 1, d), w_in.astype(BF16))
    seq_w = (jnp.stack([_block_diag(pool_w[l]) for l in range(depth)]).astype(BF16), vecs,
             rw_mu.reshape(depth, 1, d), rw_w_lora, rw_a_lora, rw_g_lora, w_br.astype(BF16), w_out.astype(BF16))
    ffn_ws = (norm2_g.reshape(depth, 1, d), w_up.astype(BF16), ffn_w, w_down.astype(BF16), final_g2)
    for l in range(depth):
        mod_p, mod_s = mods[l, :, :bp], mods[l, :, bp:]
        final = l == depth - 1

        x1, st_p = _mix_call(xp, mod_p, *mix_w, cos_p, sin_p, st_p, *seq_w, batch=bp, t=tp_, pos0=0,
                             layer=l, zero_init=True)
        xp, ffn_p = _ffn_call(x1, mod_p, *ffn_ws, ffn_p, batch=bp, t=tp_, final=final, layer=l, zero_init=True)
        x1, st_s = _mix_call(xs, mod_s, *mix_w, cos_s, sin_s, st_s, *seq_w, batch=bs, t=ts, pos0=PAST_LEN,
                             layer=l, zero_init=False)
        xs, ffn_s = _ffn_call(x1, mod_s, *ffn_ws, ffn_s, batch=bs, t=ts, final=final, layer=l, zero_init=False)

    def finish(st, ffn_st, batch):
        pool, ret, sc, sh, wkv = st
        return (pool, ret, sc, sh.reshape(depth, batch, d), wkv, ffn_st)

    return (xp.reshape(bp, tp_, d), xs.reshape(bs, ts, d), *finish(st_p, ffn_p, bp), *finish(st_s, ffn_s, bs))
```

```python
import functools
import math

import jax
import jax.numpy as jnp
from jax import lax
from jax.experimental import pallas as pl
from jax.experimental.pallas import tpu as pltpu

F32 = jnp.float32
BF16 = jnp.bfloat16

D_MODEL = 1024
MIX_W = 256
HEADS = 4
HEAD_D = 64
POOL_PAST = 15
ROPE_BASE = 10000.0
LORA_W = 64
LORA_A = 64
LORA_G = 128
D_FF = 2816
PAST_LEN = 16384
NORM_EPS = 1e-6
GN_EPS = 1e-6
RWKV_LN_EPS = 64e-5
L2_EPS = 1e-12

COL_RET = MIX_W
COL_SC = COL_RET + 4 * MIX_W
COL_RWKV = COL_SC + 3 * MIX_W
COL_GATE = COL_RWKV + 3 * MIX_W + LORA_W + LORA_A + LORA_G
IN_COLS = COL_GATE + 4 * D_MODEL

SUBLANES = 8
VMEM_LIMIT = 56 * 1024 * 1024
RWKV_CHUNK = 64
INV_BASE = 16
POOL_HIST = 16
HIST = SUBLANES
CONV_COLS = 512
MIX_SPLIT = 8
MIX_ROWS = 512
FFN_ROWS = 512
SHORT_ROWS = 64
FFN_SHORT_ROWS = 128
POOL_WINDOWS = (2, 4, 8, 16)
HEAD_SHIFT = 6

(V_POOL_SCALE, V_SC0, V_SC1, V_SC2, V_W0, V_A0, V_KK, V_KA, V_RK, V_LNG, V_LNB) = range(11)
N_VECS = 16


def _bdot(a, b):
    return jnp.dot(a.astype(BF16), b.astype(BF16), preferred_element_type=F32)


def _bdot_nt(a, b):
    return lax.dot_general(a.astype(BF16), b.astype(BF16), (((1,), (1,)), ((), ())), preferred_element_type=F32)


def _bdot_tn(a, b):
    return lax.dot_general(a.astype(BF16), b.astype(BF16), (((0,), (0,)), ((), ())), preferred_element_type=F32)


def _wdot(a, w_bf16):
    return jnp.dot(a.astype(BF16), w_bf16, preferred_element_type=F32)


def _sigmoid(x):
    return 1.0 / (1.0 + jnp.exp(-x))


def _silu(x):
    half = 0.5 * x
    return half + half * jnp.tanh(half)


def _softplus(x):
    return jnp.maximum(x, 0.0) + jnp.log(1.0 + jnp.exp(-jnp.abs(x)))


def _rmsnorm(x, g):
    return x * lax.rsqrt(jnp.mean(x * x, axis=-1, keepdims=True) + NORM_EPS) * g


def _round_up(n, m):
    return (n + m - 1) // m * m


def _mod_row(mod_ref, k, lb):
    if len(mod_ref.shape) == 4:
        return mod_ref[k, lb]
    return mod_ref[k, lb:lb + 1, :]


def _mod_rows(mod_ref, k):
    rows = mod_ref[k]
    return rows.reshape(rows.shape[0], rows.shape[-1])


def _mod_operand(mod, bb):
    six, _, d = mod.shape
    if bb % SUBLANES:
        return mod[:, :, None, :], pl.BlockSpec((six, bb, 1, d), lambda i, j: (0, i, 0, 0))
    return mod, pl.BlockSpec((six, bb, d), lambda i, j: (0, i, 0))


def _read_rows(ref):
    val = ref[...]
    return val.reshape(-1, val.shape[-1]) if val.ndim == 3 else val


def _write_seq_rows(ref, lb, tc, val):
    if len(ref.shape) == 3:
        ref[lb] = val
    else:
        ref[lb * tc:(lb + 1) * tc] = val


def _mod_kernel(c_ref, w_ref, b_ref, o_ref):
    o_ref[0, 0] = _wdot(_silu(c_ref[...]), w_ref[0].astype(BF16)) + b_ref[0]


def _mod_call(c_all, w_ada, b_ada):
    depth, d, cols = w_ada.shape
    rows = c_all.shape[0]
    cb = D_MODEL
    return pl.pallas_call(
        _mod_kernel,
        grid=(depth, cols // cb),
        in_specs=[pl.BlockSpec((rows, d), lambda l, c: (0, 0)),
                  pl.BlockSpec((1, d, cb), lambda l, c: (l, 0, c)),
                  pl.BlockSpec((1, 1, cb), lambda l, c: (l, 0, c))],
        out_specs=pl.BlockSpec((1, 1, rows, cb), lambda l, c: (l, c, 0, 0)),
        out_shape=jax.ShapeDtypeStruct((depth, cols // cb, rows, cb), F32),
        name="adaln_mod",
    )(c_all, w_ada, b_ada.reshape(depth, 1, cols))


def _tri_inverse_minus_eye(lowers, n):
    i = lax.broadcasted_iota(jnp.int32, (n, n), 0)
    j = lax.broadcasted_iota(jnp.int32, (n, n), 1)

    def same_block(size):
        sh = int(math.log2(size))
        return jnp.right_shift(i, sh) == jnp.right_shift(j, sh)

    base = min(n, INV_BASE)
    if base < n:
        diag_mask = same_block(base)
        xs = [jnp.where(diag_mask, low, 0.0) for low in lowers]
    else:
        xs = list(lowers)
    powers, p = xs, 1
    while 2 * p < base:
        powers = [_bdot(pw, pw) for pw in powers]
        prods = [_bdot(x, pw) for x, pw in zip(xs, powers)]
        xs = [x + pw + pr for x, pw, pr in zip(xs, powers, prods)]
        p *= 2
    size = base
    while size < n:
        off_mask = same_block(2 * size) & jnp.logical_not(same_block(size))
        offs = [jnp.where(off_mask, low, 0.0) for low in lowers]
        lefts = [off + _bdot(x, off) for x, off in zip(xs, offs)]
        xs = [x + left + _bdot(left, x) for x, left in zip(xs, lefts)]
        size *= 2
    return xs


def _mix_kernel(*refs, bb, tc, pos0, zero_init):
    refs = list(refs)
    x_ref, mod_ref, g_ref, win_ref, cos_ref, sin_ref = refs[:6]
    del refs[:6]
    if not zero_init:
        pool0_ref, ret0_ref, sc0_ref, sh0_ref, wkv0_ref = refs[:5]
        del refs[:5]
    poolw_ref, vecs_ref, mu_ref, wlora_ref, alora_ref, glora_ref, wbr_ref, wout_ref = refs[:8]
    del refs[:8 + (5 if zero_init else 0)]
    x1_ref, pool_ref, ret_ref, sc_ref, sh_ref, wkv_ref = refs[:6]
    h_s, z_s, y_s, dm_s, pool_ext, sc_ext, sh_ext, ywkv, wsum_s, conv_s, prev_s = refs[6:]
    j = pl.program_id(1)
    seqs = range(bb)
    heads = range(HEADS)
    hsl = [slice(h * HEAD_D, (h + 1) * HEAD_D) for h in heads]

    def rows_of(lb):
        return slice(lb * tc, (lb + 1) * tc)

    lane = lax.broadcasted_iota(jnp.int32, (1, MIX_W), 1)
    row_i = lax.broadcasted_iota(jnp.int32, (tc, 1), 0)
    col_i = lax.broadcasted_iota(jnp.int32, (1, tc), 1)
    row_f = row_i.astype(F32)
    lgs = [math.log1p(-(2.0 ** (-5.0 - h))) for h in heads]

    @pl.when(j == 0)
    def _init():
        diff = (row_i - col_i).astype(F32)
        for h in heads:
            dm_s[h] = jnp.where(diff >= 0, jnp.exp(lgs[h] * jnp.maximum(diff, 0.0)), 0.0)
        pool_ext[...] = jnp.zeros(pool_ext.shape, F32)
        sc_ext[...] = jnp.zeros(sc_ext.shape, F32)
        sh_ext[...] = jnp.zeros(sh_ext.shape, F32)
        if zero_init:
            ret_ref[...] = jnp.zeros(ret_ref.shape, F32)
            wkv_ref[...] = jnp.zeros(wkv_ref.shape, F32)
        else:
            pool_ext[:, POOL_HIST - POOL_PAST:POOL_HIST] = pool0_ref[...]
            sc_ext[:, HIST - 2:HIST] = sc0_ref[...]
            sh_ext[:, HIST - 1:HIST] = sh0_ref[...]
            ret_ref[...] = ret0_ref[...]
            wkv_ref[...] = wkv0_ref[...]

    def vec(row):
        return vecs_ref[row:row + 1, :]

    x = _read_rows(x_ref)
    hn = _rmsnorm(x, g_ref[...])
    for lb in seqs:
        h_s[rows_of(lb)] = hn[rows_of(lb)] * (1.0 + _mod_row(mod_ref, 1, lb)) + _mod_row(mod_ref, 0, lb)
    z_s[...] = _wdot(h_s[...], win_ref[...])

    def gated_proj(n):
        proj = _wdot(y_s[:, n * MIX_W:(n + 1) * MIX_W], wbr_ref[n])
        return _sigmoid(z_s[:, COL_GATE + n * D_MODEL:COL_GATE + (n + 1) * D_MODEL]) * proj

    rows_all = bb * tc
    grp = jnp.right_shift(lane, HEAD_SHIFT)

    def by_group(values):
        out = values[-1]
        for g in range(len(values) - 2, -1, -1):
            out = jnp.where(grp == g, values[g], out)
        return out

    win = by_group(POOL_WINDOWS)
    brow = lax.broadcasted_iota(jnp.int32, (rows_all, 1), 0)
    cnt = jnp.minimum(win, pos0 + j * tc + jnp.bitwise_and(brow, tc - 1) + 1).astype(F32)
    first_half = jnp.bitwise_and(lane, HEAD_D - 1) < HEAD_D // 2
    cos = cos_ref[...]
    sin = sin_ref[...]

    def rope(t):
        swapped = jnp.where(first_half, pltpu.roll(t, MIX_W - HEAD_D // 2, 1), pltpu.roll(t, HEAD_D // 2, 1))
        return t * cos + swapped * sin

    si = lax.broadcasted_iota(jnp.int32, (MIX_W, MIX_W), 0)
    sj = lax.broadcasted_iota(jnp.int32, (MIX_W, MIX_W), 1)
    head_ones = (jnp.right_shift(si, HEAD_SHIFT) == jnp.right_shift(sj, HEAD_SHIFT)).astype(BF16)

    def split_dot(ones, t):
        hi = t.astype(BF16)
        lo = (t - hi.astype(F32)).astype(BF16)
        return jnp.dot(ones, hi, preferred_element_type=F32) + jnp.dot(ones, lo, preferred_element_type=F32)

    def head_sum(t):
        hi = t.astype(BF16)
        lo = (t - hi.astype(F32)).astype(BF16)
        return (jnp.dot(hi, head_ones, preferred_element_type=F32)
                + jnp.dot(lo, head_ones, preferred_element_type=F32))

    c = min(RWKV_CHUNK, tc)
    nsub = tc // c
    merged = c % SUBLANES == 0
    ci = lax.broadcasted_iota(jnp.int32, (c, c), 0)
    cj = lax.broadcasted_iota(jnp.int32, (c, c), 1)
    strict = ci > cj
    incl = ci >= cj
    ti = lax.broadcasted_iota(jnp.int32, (rows_all, rows_all), 0)
    tj = lax.broadcasted_iota(jnp.int32, (rows_all, rows_all), 1)
    shift_c = int(math.log2(c))
    tri = ((ti >= tj) & (jnp.right_shift(ti, shift_c) == jnp.right_shift(tj, shift_c))).astype(BF16)

    for lb in seqs:
        rows = rows_of(lb)
        pool_ext[lb, POOL_HIST:POOL_HIST + tc] = z_s[rows, 0:MIX_W]
        e = pool_ext[lb]
        sums, width = [], 1
        for window in POOL_WINDOWS:
            while width < window:
                e = e + pltpu.roll(e, width, 0)
                width *= 2
            sums.append(e)
        wsum = by_group(sums)
        wsum_s[rows] = wsum[POOL_HIST:POOL_HIST + tc]
        new_pool = pool_ext[lb, tc + POOL_HIST - POOL_PAST:tc + POOL_HIST]
        pool_ext[lb, POOL_HIST - POOL_PAST:POOL_HIST] = new_pool
        pool_ref[lb] = new_pool

        sc_ext[lb, HIST:HIST + tc] = (z_s[rows, COL_SC + 2 * MIX_W:COL_SC + 3 * MIX_W]
                                      * z_s[rows, COL_SC:COL_SC + MIX_W])
        e = sc_ext[lb]
        conv = vec(V_SC2) * e + vec(V_SC1) * pltpu.roll(e, 1, 0) + vec(V_SC0) * pltpu.roll(e, 2, 0)
        conv_s[rows] = conv[HIST:HIST + tc]
        new_sc = sc_ext[lb, HIST + tc - 2:HIST + tc]
        sc_ext[lb, HIST - 2:HIST] = new_sc
        sc_ref[lb] = new_sc

        sh_ext[lb, HIST:HIST + tc] = z_s[rows, COL_RWKV:COL_GATE]
        prev_s[rows] = pltpu.roll(sh_ext[lb], 1, 0)[HIST:HIST + tc]
        new_sh = sh_ext[lb, HIST + tc - 1:HIST + tc]
        sh_ext[lb, HIST - 1:HIST] = new_sh
        sh_ref[lb] = new_sh

    u = z_s[:, 0:MIX_W]
    y_s[:, 0:MIX_W] = _wdot(wsum_s[...] / cnt - u, poolw_ref[...]) * vec(V_POOL_SCALE)
    y_s[:, 2 * MIX_W:3 * MIX_W] = z_s[:, COL_SC + MIX_W:COL_SC + 2 * MIX_W] * conv_s[...]

    q_all = rope(z_s[:, COL_RET:COL_RET + MIX_W])
    k_all = rope(z_s[:, COL_RET + MIX_W:COL_RET + 2 * MIX_W]) * (HEAD_D ** -0.5)
    v_all = z_s[:, COL_RET + 2 * MIX_W:COL_RET + 3 * MIX_W]

    zz = z_s[:, COL_RWKV:COL_GATE]
    zs = zz + (prev_s[...] - zz) * mu_ref[...]
    r = zs[:, 0:MIX_W]
    kx = zs[:, MIX_W:2 * MIX_W]
    vx = zs[:, 2 * MIX_W:3 * MIX_W]
    o3 = 3 * MIX_W
    wl = zs[:, o3:o3 + LORA_W]
    al = zs[:, o3 + LORA_W:o3 + LORA_W + LORA_A]
    gl = zs[:, o3 + LORA_W + LORA_A:]
    wlog = -_softplus(-(vec(V_W0) + _bdot(jnp.tanh(wl), wlora_ref[...]))) - 0.5
    logw = -jnp.exp(wlog)
    asig = _sigmoid(vec(V_A0) + _bdot(al, alora_ref[...]))
    g_out = _bdot(_sigmoid(gl), glora_ref[...])
    k2 = kx * (1.0 + (asig - 1.0) * vec(V_KA))
    kk_raw = kx * vec(V_KK)
    kk = kk_raw * lax.rsqrt(jnp.maximum(head_sum(kk_raw * kk_raw), L2_EPS * L2_EPS))
    cum = split_dot(tri, logw)
    p_in = jnp.exp(cum)
    p_inv = jnp.exp(-cum)
    rw = dict(a=-kk * jnp.exp(cum - logw), b=kk * asig * p_inv, k=k2 * p_inv, r=r * p_in, v=vx, p_in=p_in)

    pairs = [(lb, h) for lb in seqs for h in heads]
    qs = [q_all[rows_of(lb), hsl[h]] for lb, h in pairs]
    ks = [k_all[rows_of(lb), hsl[h]] for lb, h in pairs]
    vs = [v_all[rows_of(lb), hsl[h]] for lb, h in pairs]
    states = [ret_ref[lb, h] for lb, h in pairs]
    np_ = range(len(pairs))
    scores = [_bdot_nt(qs[i], ks[i]) for i in np_]
    cross = [_bdot(qs[i] * jnp.exp(lgs[pairs[i][1]] * (row_f + 1.0)), states[i]) for i in np_]
    kv = [_bdot_tn(ks[i] * jnp.exp(lgs[pairs[i][1]] * (tc - 1.0 - row_f)), vs[i]) for i in np_]
    outs = [_bdot(scores[i] * dm_s[pairs[i][1]], vs[i]) + cross[i] for i in np_]
    for i, (lb, h) in enumerate(pairs):
        ret_ref[lb, h] = states[i] * math.exp(lgs[h] * tc) + kv[i]
        y_s[rows_of(lb), MIX_W + h * HEAD_D:MIX_W + (h + 1) * HEAD_D] = outs[i]
    o_all = y_s[:, MIX_W:2 * MIX_W]
    o_all = o_all * lax.rsqrt(head_sum(o_all * o_all) * (1.0 / HEAD_D) + GN_EPS)
    y_s[:, MIX_W:2 * MIX_W] = o_all * _silu(z_s[:, COL_RET + 3 * MIX_W:COL_RET + 4 * MIX_W])

    blocks = [(lb, sub, h) for lb in seqs for sub in range(nsub) for h in heads]

    def blk(name):
        return [rw[name][lb * tc + sub * c:lb * tc + (sub + 1) * c, hsl[h]] for lb, sub, h in blocks]

    a_l, b_l, k_l, r_l, v_l = blk("a"), blk("b"), blk("k"), blk("r"), blk("v")
    nb = range(len(blocks))
    if merged:
        bk_l = [jnp.concatenate([b_l[i], k_l[i]], axis=0) for i in nb]
        gram = [_bdot_nt(jnp.concatenate([a_l[i], r_l[i]], axis=0), bk_l[i]) for i in nb]
        g_ab, g_ak = [g[:c, :c] for g in gram], [g[:c, c:] for g in gram]
        g_rb, g_rk = [g[c:, :c] for g in gram], [g[c:, c:] for g in gram]
    else:
        g_ab = [_bdot_nt(a_l[i], b_l[i]) for i in nb]
        g_ak = [_bdot_nt(a_l[i], k_l[i]) for i in nb]
        g_rb = [_bdot_nt(r_l[i], b_l[i]) for i in nb]
        g_rk = [_bdot_nt(r_l[i], k_l[i]) for i in nb]
    l_ab = [jnp.where(strict, g, 0.0) for g in g_ab]
    l_ak = [jnp.where(strict, g, 0.0) for g in g_ak]
    m_rb = [jnp.where(incl, g, 0.0) for g in g_rb]
    m_rk = [jnp.where(incl, g, 0.0) for g in g_rk]
    n_inv = _tri_inverse_minus_eye(l_ab, c)
    if merged:
        lv = [_bdot(jnp.concatenate([l_ak[i], m_rk[i]], axis=0), v_l[i]) for i in nb]
        lakv, mv = [t[:c] for t in lv], [t[c:] for t in lv]
    else:
        lakv = [_bdot(l_ak[i], v_l[i]) for i in nb]
        mv = [_bdot(m_rk[i], v_l[i]) for i in nb]
    nx = [_bdot(n_inv[i], jnp.concatenate([lakv[i], a_l[i]], axis=1)) for i in nb]
    tlv = [lakv[i] + nx[i][:, :HEAD_D] for i in nb]
    ta = [a_l[i] + nx[i][:, HEAD_D:] for i in nb]

    wkv_states = [wkv_ref[lb, h] for lb, h in pairs]
    for sub in range(nsub):
        ids = [(lb * nsub + sub) * HEADS + h for lb, h in pairs]
        p_end = [rw["p_in"][lb * tc + (sub + 1) * c - 1:lb * tc + (sub + 1) * c, hsl[h]] for lb, h in pairs]
        u_mat = [_bdot_nt(ta[i], wkv_states[n]) + tlv[i] for n, i in enumerate(ids)]
        y_st = [_bdot_nt(r_l[i], wkv_states[n]) for n, i in enumerate(ids)]
        if merged:
            upd = [_bdot_tn(jnp.concatenate([u_mat[n], v_l[i]], axis=0), bk_l[i] * p_end[n])
                   for n, i in enumerate(ids)]
        else:
            upd = [_bdot_tn(u_mat[n], b_l[i] * p_end[n]) + _bdot_tn(v_l[i], k_l[i] * p_end[n])
                   for n, i in enumerate(ids)]
        y_u = [_bdot(m_rb[i], u_mat[n]) for n, i in enumerate(ids)]
        for n, i in enumerate(ids):
            lb, h = pairs[n]
            ywkv[lb * tc + sub * c:lb * tc + (sub + 1) * c, hsl[h]] = y_st[n] + y_u[n] + mv[i]
        wkv_states = [wkv_states[n] * p_end[n] + upd[n] for n in np_]
    for n, (lb, h) in enumerate(pairs):
        wkv_ref[lb, h] = wkv_states[n]

    y_all = ywkv[...]
    cen = y_all - head_sum(y_all) * (1.0 / HEAD_D)
    var = head_sum(cen * cen) * (1.0 / HEAD_D)
    yn = cen * lax.rsqrt(var + RWKV_LN_EPS) * vec(V_LNG) + vec(V_LNB)
    bonus = head_sum(r * k2 * vec(V_RK)) * vx
    y_s[:, 3 * MIX_W:4 * MIX_W] = (yn + bonus) * g_out

    out = _wdot(gated_proj(0) + gated_proj(1) + gated_proj(2) + gated_proj(3), wout_ref[...])
    for lb in seqs:
        _write_seq_rows(x1_ref, lb, tc, x[rows_of(lb)] + _mod_row(mod_ref, 2, lb) * out[rows_of(lb)])


def _layer_block_spec(arr, bb, layer):
    shape = arr.shape[2:]
    zeros = (0,) * len(shape)
    return pl.BlockSpec((None, bb) + shape, lambda i, j: (layer, i) + zeros)


def _resident_spec(arr, layer=None):
    if layer is None:
        zeros = (0,) * arr.ndim
        return pl.BlockSpec(arr.shape, lambda i, j: zeros, pipeline_mode=pl.Buffered(1))
    zeros = (0,) * (arr.ndim - 1)
    return pl.BlockSpec((None,) + arr.shape[1:], lambda i, j: (layer,) + zeros, pipeline_mode=pl.Buffered(1))


def _rows_operand(x2d, batch, t, bb, tc):
    d = x2d.shape[1]
    if tc == t:
        return x2d, pl.BlockSpec((bb * tc, d), lambda i, j: (i, 0))
    return x2d.reshape(batch, t, d), pl.BlockSpec((bb, tc, d), lambda i, j: (i, j, 0))


def _params():
    return pltpu.CompilerParams(dimension_semantics=("parallel", "arbitrary"), vmem_limit_bytes=VMEM_LIMIT)


def _blocking(batch, t, split=1, rows=256, short_rows=SHORT_ROWS):
    if t >= SHORT_ROWS:
        split = split if batch % split == 0 else 1
        return split, min(t, rows // split)
    return min(batch, short_rows // t), t


def _mix_call(x2d, mod, g1, w_in, cos_t, sin_t, states, poolw, vecs, mu, wlora, alora, glora, w_br, w_out,
              batch, t, pos0, layer, zero_init):
    bb, tc = _blocking(batch, t, MIX_SPLIT, MIX_ROWS)
    nt = t // tc
    tp = _round_up(tc, SUBLANES)
    d = x2d.shape[1]
    out_structs = [jax.ShapeDtypeStruct(st.shape, F32) for st in states]
    state_specs = [_layer_block_spec(st, bb, layer) for st in states]
    weights = (poolw, vecs, mu, wlora, alora, glora, w_br, w_out)
    mod, mod_spec = _mod_operand(mod, bb)
    if zero_init:
        operands = (*weights, *states)
        operand_specs = [_resident_spec(a, layer) for a in weights] + [pl.BlockSpec(memory_space=pl.ANY)] * 5
        first_state = 6 + len(weights)
    else:
        operands = (*states, *weights)
        operand_specs = state_specs + [_resident_spec(a, layer) for a in weights]
        first_state = 6
    x_op, x_spec = _rows_operand(x2d, batch, t, bb, tc)

    def block_table(tab):
        return jnp.tile(tab.reshape(nt, tc, MIX_W), (1, bb, 1)).reshape(nt * bb * tc, MIX_W)

    outs = pl.pallas_call(
        functools.partial(_mix_kernel, bb=bb, tc=tc, pos0=pos0, zero_init=zero_init),
        grid=(batch // bb, nt),
        in_specs=[x_spec, mod_spec,
                  _resident_spec(g1, layer), _resident_spec(w_in, layer),
                  pl.BlockSpec((bb * tc, MIX_W), lambda i, j: (j, 0)),
                  pl.BlockSpec((bb * tc, MIX_W), lambda i, j: (j, 0))]
        + operand_specs,
        out_specs=[x_spec] + state_specs,
        out_shape=[jax.ShapeDtypeStruct(x_op.shape, F32)] + out_structs,
        input_output_aliases={first_state + k: 1 + k for k in range(5)},
        scratch_shapes=[pltpu.VMEM((bb * tc, d), F32),
                        pltpu.VMEM((bb * tc, IN_COLS), F32),
                        pltpu.VMEM((bb * tc, d), F32),
                        pltpu.VMEM((HEADS, tc, tc), F32),
                        pltpu.VMEM((bb, POOL_HIST + tp, MIX_W), F32),
                        pltpu.VMEM((bb, HIST + tp, MIX_W), F32),
                        pltpu.VMEM((bb, HIST + tp, d), F32),
                        pltpu.VMEM((bb * tc, MIX_W), F32),
                        pltpu.VMEM((bb * tc, MIX_W), F32),
                        pltpu.VMEM((bb * tc, MIX_W), F32),
                        pltpu.VMEM((bb * tc, COL_GATE - COL_RWKV), F32)],
        compiler_params=_params(),
        name="mix",
    )(x_op, mod, g1, w_in, block_table(cos_t), block_table(sin_t), *operands)
    return outs[0].reshape(x2d.shape), tuple(outs[1:])


def _ffn_kernel(*refs, bb, tc, final, zero_init):
    refs = list(refs)
    x1_ref, mod_ref, g_ref, wup_ref, ffnw_ref, wdown_ref, fg_ref = refs[:7]
    del refs[:7]
    st0_ref = refs.pop(0)
    o_ref, st_ref, h_s, ext, act_s = refs
    j = pl.program_id(1)
    seqs = range(bb)

    def rows_of(lb):
        return slice(lb * tc, (lb + 1) * tc)

    if tc < SUBLANES:
        def step_rows(t):
            return slice(t * bb, (t + 1) * bb)

        x1 = x1_ref[...]
        hn = _rmsnorm(x1, g_ref[...])
        for t in range(tc):
            h_s[step_rows(t)] = hn[step_rows(t)] * (1.0 + _mod_rows(mod_ref, 4)) + _mod_rows(mod_ref, 3)
        rows_hist = 2 * bb
        if zero_init:
            ext[0:rows_hist] = jnp.zeros((rows_hist, 2 * D_FF), F32)
        else:
            for lb in seqs:
                for k in range(2):
                    ext[k * bb + lb:k * bb + lb + 1] = st0_ref[lb, k:k + 1, :]
        ext[rows_hist:rows_hist + tc * bb] = _wdot(h_s[...], wup_ref[...])

        def slab(t, lo, width):
            return ext[rows_hist + t * bb:rows_hist + (t + 1) * bb, lo:lo + width]

        def conv_step(t, lo, width):
            w = ffnw_ref[:, lo:lo + width]
            return (w[2:3] * slab(t, lo, width) + w[1:2] * slab(t - 1, lo, width)
                    + w[0:1] * slab(t - 2, lo, width))

        for lo in range(0, D_FF, CONV_COLS):
            width = min(CONV_COLS, D_FF - lo)
            for t in range(tc):
                act_s[step_rows(t), lo:lo + width] = _silu(conv_step(t, lo, width)) * conv_step(t, D_FF + lo, width)
        for lb in seqs:
            for k in range(2):
                row = rows_hist + (tc - 2 + k) * bb + lb
                st_ref[lb, k:k + 1, :] = ext[row:row + 1]
        dn = _wdot(act_s[...], wdown_ref[...])
        for t in range(tc):
            o_ref[step_rows(t)] = x1[step_rows(t)] + _mod_rows(mod_ref, 5) * dn[step_rows(t)]
        if final:
            o_ref[...] = _rmsnorm(o_ref[...], fg_ref[...])
        return

    x1 = _read_rows(x1_ref)
    hn = _rmsnorm(x1, g_ref[...])
    for lb in seqs:
        h_s[rows_of(lb)] = hn[rows_of(lb)] * (1.0 + _mod_row(mod_ref, 4, lb)) + _mod_row(mod_ref, 3, lb)

    @pl.when(j == 0)
    def _init():
        ext[...] = jnp.zeros(ext.shape, F32)
        if not zero_init:
            ext[:, HIST - 2:HIST] = st0_ref[...]

    if bb == 1:
        ext[0, HIST:HIST + tc] = _wdot(h_s[...], wup_ref[...])
    else:
        up = _wdot(h_s[...], wup_ref[...])
        for lb in seqs:
            ext[lb, HIST:HIST + tc] = up[rows_of(lb)]

    for lb in seqs:
        def conv_cols(lo, width):
            e = ext[lb, :, lo:lo + width]
            w = ffnw_ref[:, lo:lo + width]
            y = w[2:3] * e + w[1:2] * pltpu.roll(e, 1, 0) + w[0:1] * pltpu.roll(e, 2, 0)
            return y[HIST:HIST + tc]

        for lo in range(0, D_FF, CONV_COLS):
            width = min(CONV_COLS, D_FF - lo)
            act_s[rows_of(lb), lo:lo + width] = _silu(conv_cols(lo, width)) * conv_cols(D_FF + lo, width)
        new_st = ext[lb, HIST + tc - 2:HIST + tc]
        ext[lb, HIST - 2:HIST] = new_st
        st_ref[lb] = new_st

    dn = _wdot(act_s[...], wdown_ref[...])
    for lb in seqs:
        _write_seq_rows(o_ref, lb, tc, x1[rows_of(lb)] + _mod_row(mod_ref, 5, lb) * dn[rows_of(lb)])
    if final:
        o_ref[...] = _rmsnorm(o_ref[...], fg_ref[...])


def _ffn_call(x1, mod, g2, w_up, ffn_w, w_down, final_g, state, batch, t, final, layer, zero_init):
    bb, tc = _blocking(batch, t, 1, FFN_ROWS, FFN_SHORT_ROWS)
    nt = t // tc
    tp = _round_up(tc, SUBLANES)
    d = x1.shape[1]
    st_struct = jax.ShapeDtypeStruct(state.shape, F32)
    st_spec = _layer_block_spec(state, bb, layer)
    mod, mod_spec = _mod_operand(mod, bb)
    time_major = tc < SUBLANES
    if time_major:
        x1 = x1.reshape(batch // bb, bb, tc, d).transpose(0, 2, 1, 3).reshape(batch * tc, d)
    x_op, x_spec = _rows_operand(x1, batch, t, bb, tc)
    out, new_state = pl.pallas_call(
        functools.partial(_ffn_kernel, bb=bb, tc=tc, final=final, zero_init=zero_init),
        grid=(batch // bb, nt),
        in_specs=[x_spec, mod_spec,
                  _resident_spec(g2, layer), _resident_spec(w_up, layer), _resident_spec(ffn_w, layer),
                  _resident_spec(w_down, layer), _resident_spec(final_g)]
        + [pl.BlockSpec(memory_space=pl.ANY) if zero_init else st_spec],
        out_specs=[x_spec, st_spec],
        out_shape=[jax.ShapeDtypeStruct(x_op.shape, F32), st_struct],
        input_output_aliases={7: 1},
        scratch_shapes=[pltpu.VMEM((bb * tc, d), F32),
                        pltpu.VMEM(((tc + 2) * bb, 2 * D_FF) if tc < SUBLANES else (bb, HIST + tp, 2 * D_FF), F32),
                        pltpu.VMEM((bb * tc, D_FF), F32)],
        compiler_params=_params(),
        name="ffn",
    )(x_op, mod, g2, w_up, ffn_w, w_down, final_g, state)
    out = out.reshape(x1.shape)
    if time_major:
        out = out.reshape(batch // bb, tc, bb, d).transpose(0, 2, 1, 3).reshape(batch * tc, d)
    return out, new_state


def _rope_tables(t, pos0):
    half = HEAD_D // 2
    inv = ROPE_BASE ** (-jnp.arange(half, dtype=F32) / half)
    pos = pos0 + jnp.arange(t, dtype=jnp.int32)
    ang = pos.astype(F32)[:, None] * inv[None, :]
    cos, sin = jnp.cos(ang), jnp.sin(ang)
    cos_t = jnp.tile(jnp.concatenate([cos, cos], axis=-1), (1, HEADS))
    sin_t = jnp.tile(jnp.concatenate([-sin, sin], axis=-1), (1, HEADS))
    return cos_t, sin_t


def _block_diag(pool_w):
    groups, gw, _ = pool_w.shape
    out = jnp.zeros((groups * gw, groups * gw), pool_w.dtype)
    for g in range(groups):
        out = out.at[g * gw:(g + 1) * gw, g * gw:(g + 1) * gw].set(pool_w[g])
    return out


def kernel(x_prompt, x_sample, c_prompt, c_sample, state_pool, state_ret, state_sconv, state_shift, state_wkv, state_ffn, w_ada, b_ada, norm1_g, norm2_g, w_in, pool_w, pool_scale, sc_w, rw_mu, rw_w0, rw_w_lora, rw_a0, rw_a_lora, rw_g_lora, rw_k_k, rw_k_a, rw_r_k, rw_ln_g, rw_ln_b, w_br, w_out, w_up, ffn_w, w_down, final_g):
    depth = w_in.shape[0]
    bp, tp_, d = x_prompt.shape
    bs, ts, _ = x_sample.shape

    mods = _mod_call(jnp.concatenate([c_prompt, c_sample], axis=0), w_ada, b_ada)
    cos_p, sin_p = _rope_tables(tp_, 0)
    cos_s, sin_s = _rope_tables(ts, PAST_LEN)
    final_g2 = final_g.reshape(1, d)

    xp = x_prompt.reshape(bp * tp_, d)
    xs = x_sample.reshape(bs * ts, d)
    st_s = (state_pool, state_ret, state_sconv, state_shift.reshape(depth, bs, 1, d), state_wkv)
    st_p = tuple(jnp.zeros((depth, bp) + st.shape[2:], F32) for st in st_s)
    ffn_s = state_ffn
    ffn_p = jnp.zeros((depth, bp) + state_ffn.shape[2:], F32)
    vec_rows = [pool_scale, sc_w[:, 0], sc_w[:, 1], sc_w[:, 2], rw_w0, rw_a0, rw_k_k, rw_k_a,
                rw_r_k.reshape(depth, MIX_W), rw_ln_g, rw_ln_b]
    vecs = jnp.concatenate([jnp.stack(vec_rows, axis=1),
                            jnp.zeros((depth, N_VECS - len(vec_rows), MIX_W), F32)], axis=1)
    mix_w = (norm1_g.reshape(depth, 1, d), w_in.astype(BF16))
    seq_w = (jnp.stack([_block_diag(pool_w[l]) for l in range(depth)]).astype(BF16), vecs,
             rw_mu.reshape(depth, 1, d), rw_w_lora, rw_a_lora, rw_g_lora, w_br.astype(BF16), w_out.astype(BF16))
    ffn_ws = (norm2_g.reshape(depth, 1, d), w_up.astype(BF16), ffn_w, w_down.astype(BF16), final_g2)
    for l in range(depth):
        mod_p, mod_s = mods[l, :, :bp], mods[l, :, bp:]
        final = l == depth - 1

        x1, st_p = _mix_call(xp, mod_p, *mix_w, cos_p, sin_p, st_p, *seq_w, batch=bp, t=tp_, pos0=0,
                             layer=l, zero_init=True)
        xp, ffn_p = _ffn_call(x1, mod_p, *ffn_ws, ffn_p, batch=bp, t=tp_, final=final, layer=l, zero_init=True)
        x1, st_s = _mix_call(xs, mod_s, *mix_w, cos_s, sin_s, st_s, *seq_w, batch=bs, t=ts, pos0=PAST_LEN,
                             layer=l, zero_init=False)
        xs, ffn_s = _ffn_call(x1, mod_s, *ffn_ws, ffn_s, batch=bs, t=ts, final=final, layer=l, zero_init=False)

    def finish(st, ffn_st, batch):
        pool, ret, sc, sh, wkv = st
        return (pool, ret, sc, sh.reshape(depth, batch, d), wkv, ffn_st)

    return (xp.reshape(bp, tp_, d), xs.reshape(bs, ts, d), *finish(st_p, ffn_p, bp), *finish(st_s, ffn_s, bs))
```

```python
import functools
import math

import jax
import jax.numpy as jnp
from jax import lax
from jax.experimental import pallas as pl
from jax.experimental.pallas import tpu as pltpu

F32 = jnp.float32
BF16 = jnp.bfloat16

D_MODEL = 1024
MIX_W = 256
HEADS = 4
HEAD_D = 64
POOL_PAST = 15
ROPE_BASE = 10000.0
LORA_W = 64
LORA_A = 64
LORA_G = 128
D_FF = 2816
PAST_LEN = 16384
NORM_EPS = 1e-6
GN_EPS = 1e-6
RWKV_LN_EPS = 64e-5
L2_EPS = 1e-12

COL_RET = MIX_W
COL_SC = COL_RET + 4 * MIX_W
COL_RWKV = COL_SC + 3 * MIX_W
COL_GATE = COL_RWKV + 3 * MIX_W + LORA_W + LORA_A + LORA_G
IN_COLS = COL_GATE + 4 * D_MODEL

SUBLANES = 8
VMEM_LIMIT = 56 * 1024 * 1024
RWKV_CHUNK = 64
INV_BASE = 16
POOL_HIST = 16
HIST = SUBLANES
CONV_COLS = 512
MIX_SPLIT = 8
MIX_ROWS = 512
FFN_ROWS = 512
SHORT_ROWS = 64
FFN_SHORT_ROWS = 128
POOL_WINDOWS = (2, 4, 8, 16)
HEAD_SHIFT = 6

(V_POOL_SCALE, V_SC0, V_SC1, V_SC2, V_W0, V_A0, V_KK, V_KA, V_RK, V_LNG, V_LNB) = range(11)
N_VECS = 16


def _bdot(a, b):
    return jnp.dot(a.astype(BF16), b.astype(BF16), preferred_element_type=F32)


def _bdot_nt(a, b):
    return lax.dot_general(a.astype(BF16), b.astype(BF16), (((1,), (1,)), ((), ())), preferred_element_type=F32)


def _bdot_tn(a, b):
    return lax.dot_general(a.astype(BF16), b.astype(BF16), (((0,), (0,)), ((), ())), preferred_element_type=F32)


def _wdot(a, w_bf16):
    return jnp.dot(a.astype(BF16), w_bf16, preferred_element_type=F32)


def _sigmoid(x):
    return 1.0 / (1.0 + jnp.exp(-x))


def _silu(x):
    half = 0.5 * x
    return half + half * jnp.tanh(half)


def _softplus(x):
    return jnp.maximum(x, 0.0) + jnp.log(1.0 + jnp.exp(-jnp.abs(x)))


def _rmsnorm(x, g):
    return x * lax.rsqrt(jnp.mean(x * x, axis=-1, keepdims=True) + NORM_EPS) * g


def _round_up(n, m):
    return (n + m - 1) // m * m


def _mod_row(mod_ref, k, lb):
    if len(mod_ref.shape) == 4:
        return mod_ref[k, lb]
    return mod_ref[k, lb:lb + 1, :]


def _mod_rows(mod_ref, k):
    rows = mod_ref[k]
    return rows.reshape(rows.shape[0], rows.shape[-1])


def _mod_operand(mod, bb):
    six, _, d = mod.shape
    if bb % SUBLANES:
        return mod[:, :, None, :], pl.BlockSpec((six, bb, 1, d), lambda i, j: (0, i, 0, 0))
    return mod, pl.BlockSpec((six, bb, d), lambda i, j: (0, i, 0))


def _read_rows(ref):
    val = ref[...]
    return val.reshape(-1, val.shape[-1]) if val.ndim == 3 else val


def _write_seq_rows(ref, lb, tc, val):
    if len(ref.shape) == 3:
        ref[lb] = val
    else:
        ref[lb * tc:(lb + 1) * tc] = val


def _mod_kernel(c_ref, w_ref, b_ref, o_ref):
    o_ref[0, 0] = _wdot(_silu(c_ref[...]), w_ref[0].astype(BF16)) + b_ref[0]


def _mod_call(c_all, w_ada, b_ada):
    depth, d, cols = w_ada.shape
    rows = c_all.shape[0]
    cb = D_MODEL
    return pl.pallas_call(
        _mod_kernel,
        grid=(depth, cols // cb),
        in_specs=[pl.BlockSpec((rows, d), lambda l, c: (0, 0)),
                  pl.BlockSpec((1, d, cb), lambda l, c: (l, 0, c)),
                  pl.BlockSpec((1, 1, cb), lambda l, c: (l, 0, c))],
        out_specs=pl.BlockSpec((1, 1, rows, cb), lambda l, c: (l, c, 0, 0)),
        out_shape=jax.ShapeDtypeStruct((depth, cols // cb, rows, cb), F32),
        name="adaln_mod",
    )(c_all, w_ada, b_ada.reshape(depth, 1, cols))


def _tri_inverse_minus_eye(lowers, n):
    i = lax.broadcasted_iota(jnp.int32, (n, n), 0)
    j = lax.broadcasted_iota(jnp.int32, (n, n), 1)

    def same_block(size):
        sh = int(math.log2(size))
        return jnp.right_shift(i, sh) == jnp.right_shift(j, sh)

    base = min(n, INV_BASE)
    if base < n:
        diag_mask = same_block(base)
        xs = [jnp.where(diag_mask, low, 0.0) for low in lowers]
    else:
        xs = list(lowers)
    powers, p = xs, 1
    while 2 * p < base:
        powers = [_bdot(pw, pw) for pw in powers]
        prods = [_bdot(x, pw) for x, pw in zip(xs, powers)]
        xs = [x + pw + pr for x, pw, pr in zip(xs, powers, prods)]
        p *= 2
    size = base
    while size < n:
        off_mask = same_block(2 * size) & jnp.logical_not(same_block(size))
        offs = [jnp.where(off_mask, low, 0.0) for low in lowers]
        lefts = [off + _bdot(x, off) for x, off in zip(xs, offs)]
        xs = [x + left + _bdot(left, x) for x, left in zip(xs, lefts)]
        size *= 2
    return xs


def _mix_kernel(*refs, bb, tc, pos0, zero_init):
    refs = list(refs)
    x_ref, mod_ref, g_ref, win_ref, cos_ref, sin_ref = refs[:6]
    del refs[:6]
    if not zero_init:
        pool0_ref, ret0_ref, sc0_ref, sh0_ref, wkv0_ref = refs[:5]
        del refs[:5]
    poolw_ref, vecs_ref, mu_ref, wlora_ref, alora_ref, glora_ref, wbr_ref, wout_ref = refs[:8]
    del refs[:8 + (5 if zero_init else 0)]
    x1_ref, pool_ref, ret_ref, sc_ref, sh_ref, wkv_ref = refs[:6]
    h_s, z_s, y_s, dm_s, pool_ext, sc_ext, sh_ext, ywkv, wsum_s, conv_s, prev_s = refs[6:]
    j = pl.program_id(1)
    seqs = range(bb)
    heads = range(HEADS)
    hsl = [slice(h * HEAD_D, (h + 1) * HEAD_D) for h in heads]

    def rows_of(lb):
        return slice(lb * tc, (lb + 1) * tc)

    lane = lax.broadcasted_iota(jnp.int32, (1, MIX_W), 1)
    row_i = lax.broadcasted_iota(jnp.int32, (tc, 1), 0)
    col_i = lax.broadcasted_iota(jnp.int32, (1, tc), 1)
    row_f = row_i.astype(F32)
    lgs = [math.log1p(-(2.0 ** (-5.0 - h))) for h in heads]

    @pl.when(j == 0)
    def _init():
        diff = (row_i - col_i).astype(F32)
        for h in heads:
            dm_s[h] = jnp.where(diff >= 0, jnp.exp(lgs[h] * jnp.maximum(diff, 0.0)), 0.0)
        pool_ext[...] = jnp.zeros(pool_ext.shape, F32)
        sc_ext[...] = jnp.zeros(sc_ext.shape, F32)
        sh_ext[...] = jnp.zeros(sh_ext.shape, F32)
        if zero_init:
            ret_ref[...] = jnp.zeros(ret_ref.shape, F32)
            wkv_ref[...] = jnp.zeros(wkv_ref.shape, F32)
        else:
            pool_ext[:, POOL_HIST - POOL_PAST:POOL_HIST] = pool0_ref[...]
            sc_ext[:, HIST - 2:HIST] = sc0_ref[...]
            sh_ext[:, HIST - 1:HIST] = sh0_ref[...]
            ret_ref[...] = ret0_ref[...]
            wkv_ref[...] = wkv0_ref[...]

    def vec(row):
        return vecs_ref[row:row + 1, :]

    x = _read_rows(x_ref)
    hn = _rmsnorm(x, g_ref[...])
    for lb in seqs:
        h_s[rows_of(lb)] = hn[rows_of(lb)] * (1.0 + _mod_row(mod_ref, 1, lb)) + _mod_row(mod_ref, 0, lb)
    z_s[...] = _wdot(h_s[...], win_ref[...])

    def gated_proj(n):
        proj = _wdot(y_s[:, n * MIX_W:(n + 1) * MIX_W], wbr_ref[n])
        return _sigmoid(z_s[:, COL_GATE + n * D_MODEL:COL_GATE + (n + 1) * D_MODEL]) * proj

    rows_all = bb * tc
    grp = jnp.right_shift(lane, HEAD_SHIFT)

    def by_group(values):
        out = values[-1]
        for g in range(len(values) - 2, -1, -1):
            out = jnp.where(grp == g, values[g], out)
        return out

    win = by_group(POOL_WINDOWS)
    brow = lax.broadcasted_iota(jnp.int32, (rows_all, 1), 0)
    cnt = jnp.minimum(win, pos0 + j * tc + jnp.bitwise_and(brow, tc - 1) + 1).astype(F32)
    first_half = jnp.bitwise_and(lane, HEAD_D - 1) < HEAD_D // 2
    cos = cos_ref[...]
    sin = sin_ref[...]
    if cos.shape[0] != rows_all:
        cos = jnp.concatenate([cos] * bb, axis=0)
        sin = jnp.concatenate([sin] * bb, axis=0)

    def rope(t):
        swapped = jnp.where(first_half, pltpu.roll(t, MIX_W - HEAD_D // 2, 1), pltpu.roll(t, HEAD_D // 2, 1))
        return t * cos + swapped * sin

    si = lax.broadcasted_iota(jnp.int32, (MIX_W, MIX_W), 0)
    sj = lax.broadcasted_iota(jnp.int32, (MIX_W, MIX_W), 1)
    head_ones = (jnp.right_shift(si, HEAD_SHIFT) == jnp.right_shift(sj, HEAD_SHIFT)).astype(BF16)

    def split_dot(ones, t):
        hi = t.astype(BF16)
        lo = (t - hi.astype(F32)).astype(BF16)
        return jnp.dot(ones, hi, preferred_element_type=F32) + jnp.dot(ones, lo, preferred_element_type=F32)

    def head_sum(t):
        hi = t.astype(BF16)
        lo = (t - hi.astype(F32)).astype(BF16)
        return (jnp.dot(hi, head_ones, preferred_element_type=F32)
                + jnp.dot(lo, head_ones, preferred_element_type=F32))

    c = min(RWKV_CHUNK, tc)
    nsub = tc // c
    merged = c % SUBLANES == 0
    ci = lax.broadcasted_iota(jnp.int32, (c, c), 0)
    cj = lax.broadcasted_iota(jnp.int32, (c, c), 1)
    strict = ci > cj
    incl = ci >= cj
    ti = lax.broadcasted_iota(jnp.int32, (rows_all, rows_all), 0)
    tj = lax.broadcasted_iota(jnp.int32, (rows_all, rows_all), 1)
    shift_c = int(math.log2(c))
    tri = ((ti >= tj) & (jnp.right_shift(ti, shift_c) == jnp.right_shift(tj, shift_c))).astype(BF16)

    for lb in seqs:
        rows = rows_of(lb)
        pool_ext[lb, POOL_HIST:POOL_HIST + tc] = z_s[rows, 0:MIX_W]
        e = pool_ext[lb]
        sums, width = [], 1
        for window in POOL_WINDOWS:
            while width < window:
                e = e + pltpu.roll(e, width, 0)
                width *= 2
            sums.append(e)
        wsum = by_group(sums)
        wsum_s[rows] = wsum[POOL_HIST:POOL_HIST + tc]
        new_pool = pool_ext[lb, tc + POOL_HIST - POOL_PAST:tc + POOL_HIST]
        pool_ext[lb, POOL_HIST - POOL_PAST:POOL_HIST] = new_pool
        pool_ref[lb] = new_pool

        sc_ext[lb, HIST:HIST + tc] = (z_s[rows, COL_SC + 2 * MIX_W:COL_SC + 3 * MIX_W]
                                      * z_s[rows, COL_SC:COL_SC + MIX_W])
        e = sc_ext[lb]
        conv = vec(V_SC2) * e + vec(V_SC1) * pltpu.roll(e, 1, 0) + vec(V_SC0) * pltpu.roll(e, 2, 0)
        conv_s[rows] = conv[HIST:HIST + tc]
        new_sc = sc_ext[lb, HIST + tc - 2:HIST + tc]
        sc_ext[lb, HIST - 2:HIST] = new_sc
        sc_ref[lb] = new_sc

        sh_ext[lb, HIST:HIST + tc] = z_s[rows, COL_RWKV:COL_GATE]
        prev_s[rows] = pltpu.roll(sh_ext[lb], 1, 0)[HIST:HIST + tc]
        new_sh = sh_ext[lb, HIST + tc - 1:HIST + tc]
        sh_ext[lb, HIST - 1:HIST] = new_sh
        sh_ref[lb] = new_sh

    u = z_s[:, 0:MIX_W]
    y_s[:, 0:MIX_W] = _wdot(wsum_s[...] / cnt - u, poolw_ref[...]) * vec(V_POOL_SCALE)
    y_s[:, 2 * MIX_W:3 * MIX_W] = z_s[:, COL_SC + MIX_W:COL_SC + 2 * MIX_W] * conv_s[...]

    q_all = rope(z_s[:, COL_RET:COL_RET + MIX_W])
    k_all = rope(z_s[:, COL_RET + MIX_W:COL_RET + 2 * MIX_W]) * (HEAD_D ** -0.5)
    v_all = z_s[:, COL_RET + 2 * MIX_W:COL_RET + 3 * MIX_W]

    zz = z_s[:, COL_RWKV:COL_GATE]
    zs = zz + (prev_s[...] - zz) * mu_ref[...]
    r = zs[:, 0:MIX_W]
    kx = zs[:, MIX_W:2 * MIX_W]
    vx = zs[:, 2 * MIX_W:3 * MIX_W]
    o3 = 3 * MIX_W
    wl = zs[:, o3:o3 + LORA_W]
    al = zs[:, o3 + LORA_W:o3 + LORA_W + LORA_A]
    gl = zs[:, o3 + LORA_W + LORA_A:]
    wlog = -_softplus(-(vec(V_W0) + _bdot(jnp.tanh(wl), wlora_ref[...]))) - 0.5
    logw = -jnp.exp(wlog)
    asig = _sigmoid(vec(V_A0) + _bdot(al, alora_ref[...]))
    g_out = _bdot(_sigmoid(gl), glora_ref[...])
    k2 = kx * (1.0 + (asig - 1.0) * vec(V_KA))
    kk_raw = kx * vec(V_KK)
    kk = kk_raw * lax.rsqrt(jnp.maximum(head_sum(kk_raw * kk_raw), L2_EPS * L2_EPS))
    cum = split_dot(tri, logw)
    p_in = jnp.exp(cum)
    p_inv = jnp.exp(-cum)
    rw = dict(a=-kk * jnp.exp(cum - logw), b=kk * asig * p_inv, k=k2 * p_inv, r=r * p_in, v=vx, p_in=p_in)

    pairs = [(lb, h) for lb in seqs for h in heads]
    qs = [q_all[rows_of(lb), hsl[h]] for lb, h in pairs]
    ks = [k_all[rows_of(lb), hsl[h]] for lb, h in pairs]
    vs = [v_all[rows_of(lb), hsl[h]] for lb, h in pairs]
    states = [ret_ref[lb, h] for lb, h in pairs]
    np_ = range(len(pairs))
    scores = [_bdot_nt(qs[i], ks[i]) for i in np_]
    cross = [_bdot(qs[i] * jnp.exp(lgs[pairs[i][1]] * (row_f + 1.0)), states[i]) for i in np_]
    kv = [_bdot_tn(ks[i] * jnp.exp(lgs[pairs[i][1]] * (tc - 1.0 - row_f)), vs[i]) for i in np_]
    outs = [_bdot(scores[i] * dm_s[pairs[i][1]], vs[i]) + cross[i] for i in np_]
    for i, (lb, h) in enumerate(pairs):
        ret_ref[lb, h] = states[i] * math.exp(lgs[h] * tc) + kv[i]
        y_s[rows_of(lb), MIX_W + h * HEAD_D:MIX_W + (h + 1) * HEAD_D] = outs[i]
    o_all = y_s[:, MIX_W:2 * MIX_W]
    o_all = o_all * lax.rsqrt(head_sum(o_all * o_all) * (1.0 / HEAD_D) + GN_EPS)
    y_s[:, MIX_W:2 * MIX_W] = o_all * _silu(z_s[:, COL_RET + 3 * MIX_W:COL_RET + 4 * MIX_W])

    blocks = [(lb, sub, h) for lb in seqs for sub in range(nsub) for h in heads]

    def blk(name):
        return [rw[name][lb * tc + sub * c:lb * tc + (sub + 1) * c, hsl[h]] for lb, sub, h in blocks]

    a_l, b_l, k_l, r_l, v_l = blk("a"), blk("b"), blk("k"), blk("r"), blk("v")
    nb = range(len(blocks))
    if merged:
        bk_l = [jnp.concatenate([b_l[i], k_l[i]], axis=0) for i in nb]
        gram = [_bdot_nt(jnp.concatenate([a_l[i], r_l[i]], axis=0), bk_l[i]) for i in nb]
        g_ab, g_ak = [g[:c, :c] for g in gram], [g[:c, c:] for g in gram]
        g_rb, g_rk = [g[c:, :c] for g in gram], [g[c:, c:] for g in gram]
    else:
        g_ab = [_bdot_nt(a_l[i], b_l[i]) for i in nb]
        g_ak = [_bdot_nt(a_l[i], k_l[i]) for i in nb]
        g_rb = [_bdot_nt(r_l[i], b_l[i]) for i in nb]
        g_rk = [_bdot_nt(r_l[i], k_l[i]) for i in nb]
    l_ab = [jnp.where(strict, g, 0.0) for g in g_ab]
    l_ak = [jnp.where(strict, g, 0.0) for g in g_ak]
    m_rb = [jnp.where(incl, g, 0.0) for g in g_rb]
    m_rk = [jnp.where(incl, g, 0.0) for g in g_rk]
    n_inv = _tri_inverse_minus_eye(l_ab, c)
    if merged:
        lv = [_bdot(jnp.concatenate([l_ak[i], m_rk[i]], axis=0), v_l[i]) for i in nb]
        lakv, mv = [t[:c] for t in lv], [t[c:] for t in lv]
    else:
        lakv = [_bdot(l_ak[i], v_l[i]) for i in nb]
        mv = [_bdot(m_rk[i], v_l[i]) for i in nb]
    nx = [_bdot(n_inv[i], jnp.concatenate([lakv[i], a_l[i]], axis=1)) for i in nb]
    tlv = [lakv[i] + nx[i][:, :HEAD_D] for i in nb]
    ta = [a_l[i] + nx[i][:, HEAD_D:] for i in nb]

    wkv_states = [wkv_ref[lb, h] for lb, h in pairs]
    for sub in range(nsub):
        ids = [(lb * nsub + sub) * HEADS + h for lb, h in pairs]
        p_end = [rw["p_in"][lb * tc + (sub + 1) * c - 1:lb * tc + (sub + 1) * c, hsl[h]] for lb, h in pairs]
        u_mat = [_bdot_nt(ta[i], wkv_states[n]) + tlv[i] for n, i in enumerate(ids)]
        y_st = [_bdot_nt(r_l[i], wkv_states[n]) for n, i in enumerate(ids)]
        if merged:
            upd = [_bdot_tn(jnp.concatenate([u_mat[n], v_l[i]], axis=0), bk_l[i] * p_end[n])
                   for n, i in enumerate(ids)]
        else:
            upd = [_bdot_tn(u_mat[n], b_l[i] * p_end[n]) + _bdot_tn(v_l[i], k_l[i] * p_end[n])
                   for n, i in enumerate(ids)]
        y_u = [_bdot(m_rb[i], u_mat[n]) for n, i in enumerate(ids)]
        for n, i in enumerate(ids):
            lb, h = pairs[n]
            ywkv[lb * tc + sub * c:lb * tc + (sub + 1) * c, hsl[h]] = y_st[n] + y_u[n] + mv[i]
        wkv_states = [wkv_states[n] * p_end[n] + upd[n] for n in np_]
    for n, (lb, h) in enumerate(pairs):
        wkv_ref[lb, h] = wkv_states[n]

    y_all = ywkv[...]
    cen = y_all - head_sum(y_all) * (1.0 / HEAD_D)
    var = head_sum(cen * cen) * (1.0 / HEAD_D)
    yn = cen * lax.rsqrt(var + RWKV_LN_EPS) * vec(V_LNG) + vec(V_LNB)
    bonus = head_sum(r * k2 * vec(V_RK)) * vx
    y_s[:, 3 * MIX_W:4 * MIX_W] = (yn + bonus) * g_out

    out = _wdot(gated_proj(0) + gated_proj(1) + gated_proj(2) + gated_proj(3), wout_ref[...])
    for lb in seqs:
        _write_seq_rows(x1_ref, lb, tc, x[rows_of(lb)] + _mod_row(mod_ref, 2, lb) * out[rows_of(lb)])


def _layer_block_spec(arr, bb, layer):
    shape = arr.shape[2:]
    zeros = (0,) * len(shape)
    return pl.BlockSpec((None, bb) + shape, lambda i, j: (layer, i) + zeros)


def _resident_spec(arr, layer=None):
    if layer is None:
        zeros = (0,) * arr.ndim
        return pl.BlockSpec(arr.shape, lambda i, j: zeros, pipeline_mode=pl.Buffered(1))
    zeros = (0,) * (arr.ndim - 1)
    return pl.BlockSpec((None,) + arr.shape[1:], lambda i, j: (layer,) + zeros, pipeline_mode=pl.Buffered(1))


def _rows_operand(x2d, batch, t, bb, tc):
    d = x2d.shape[1]
    if tc == t:
        return x2d, pl.BlockSpec((bb * tc, d), lambda i, j: (i, 0))
    return x2d.reshape(batch, t, d), pl.BlockSpec((bb, tc, d), lambda i, j: (i, j, 0))


def _params():
    return pltpu.CompilerParams(dimension_semantics=("parallel", "arbitrary"), vmem_limit_bytes=VMEM_LIMIT)


def _blocking(batch, t, split=1, rows=256, short_rows=SHORT_ROWS):
    if t >= SHORT_ROWS:
        split = split if batch % split == 0 else 1
        return split, min(t, rows // split)
    return min(batch, short_rows // t), t


def _mix_call(x2d, mod, g1, w_in, cos_t, sin_t, states, poolw, vecs, mu, wlora, alora, glora, w_br, w_out,
              batch, t, pos0, layer, zero_init):
    bb, tc = _blocking(batch, t, MIX_SPLIT, MIX_ROWS)
    nt = t // tc
    tp = _round_up(tc, SUBLANES)
    d = x2d.shape[1]
    out_structs = [jax.ShapeDtypeStruct(st.shape, F32) for st in states]
    state_specs = [_layer_block_spec(st, bb, layer) for st in states]
    weights = (poolw, vecs, mu, wlora, alora, glora, w_br, w_out)
    mod, mod_spec = _mod_operand(mod, bb)
    if zero_init:
        operands = (*weights, *states)
        operand_specs = [_resident_spec(a, layer) for a in weights] + [pl.BlockSpec(memory_space=pl.ANY)] * 5
        first_state = 6 + len(weights)
    else:
        operands = (*states, *weights)
        operand_specs = state_specs + [_resident_spec(a, layer) for a in weights]
        first_state = 6
    x_op, x_spec = _rows_operand(x2d, batch, t, bb, tc)

    table_rows = tc if tc % SUBLANES == 0 else bb * tc

    def block_table(tab):
        if table_rows == tc:
            return tab
        return jnp.tile(tab.reshape(nt, tc, MIX_W), (1, bb, 1)).reshape(nt * bb * tc, MIX_W)

    outs = pl.pallas_call(
        functools.partial(_mix_kernel, bb=bb, tc=tc, pos0=pos0, zero_init=zero_init),
        grid=(batch // bb, nt),
        in_specs=[x_spec, mod_spec,
                  _resident_spec(g1, layer), _resident_spec(w_in, layer),
                  pl.BlockSpec((table_rows, MIX_W), lambda i, j: (j, 0)),
                  pl.BlockSpec((table_rows, MIX_W), lambda i, j: (j, 0))]
        + operand_specs,
        out_specs=[x_spec] + state_specs,
        out_shape=[jax.ShapeDtypeStruct(x_op.shape, F32)] + out_structs,
        input_output_aliases={first_state + k: 1 + k for k in range(5)},
        scratch_shapes=[pltpu.VMEM((bb * tc, d), F32),
                        pltpu.VMEM((bb * tc, IN_COLS), F32),
                        pltpu.VMEM((bb * tc, d), F32),
                        pltpu.VMEM((HEADS, tc, tc), F32),
                        pltpu.VMEM((bb, POOL_HIST + tp, MIX_W), F32),
                        pltpu.VMEM((bb, HIST + tp, MIX_W), F32),
                        pltpu.VMEM((bb, HIST + tp, d), F32),
                        pltpu.VMEM((bb * tc, MIX_W), F32),
                        pltpu.VMEM((bb * tc, MIX_W), F32),
                        pltpu.VMEM((bb * tc, MIX_W), F32),
                        pltpu.VMEM((bb * tc, COL_GATE - COL_RWKV), F32)],
        compiler_params=_params(),
        name="mix",
    )(x_op, mod, g1, w_in, block_table(cos_t), block_table(sin_t), *operands)
    return outs[0].reshape(x2d.shape), tuple(outs[1:])


def _ffn_kernel(*refs, bb, tc, final, zero_init):
    refs = list(refs)
    x1_ref, mod_ref, g_ref, wup_ref, ffnw_ref, wdown_ref, fg_ref = refs[:7]
    del refs[:7]
    st0_ref = refs.pop(0)
    o_ref, st_ref, h_s, ext, act_s = refs
    j = pl.program_id(1)
    seqs = range(bb)

    def rows_of(lb):
        return slice(lb * tc, (lb + 1) * tc)

    if tc < SUBLANES:
        def step_rows(t):
            return slice(t * bb, (t + 1) * bb)

        x1 = x1_ref[...]
        hn = _rmsnorm(x1, g_ref[...])
        for t in range(tc):
            h_s[step_rows(t)] = hn[step_rows(t)] * (1.0 + _mod_rows(mod_ref, 4)) + _mod_rows(mod_ref, 3)
        rows_hist = 2 * bb
        if zero_init:
            ext[0:rows_hist] = jnp.zeros((rows_hist, 2 * D_FF), F32)
        else:
            for lb in seqs:
                for k in range(2):
                    ext[k * bb + lb:k * bb + lb + 1] = st0_ref[lb, k:k + 1, :]
        ext[rows_hist:rows_hist + tc * bb] = _wdot(h_s[...], wup_ref[...])

        def slab(t, lo, width):
            return ext[rows_hist + t * bb:rows_hist + (t + 1) * bb, lo:lo + width]

        def conv_step(t, lo, width):
            w = ffnw_ref[:, lo:lo + width]
            return (w[2:3] * slab(t, lo, width) + w[1:2] * slab(t - 1, lo, width)
                    + w[0:1] * slab(t - 2, lo, width))

        for lo in range(0, D_FF, CONV_COLS):
            width = min(CONV_COLS, D_FF - lo)
            for t in range(tc):
                act_s[step_rows(t), lo:lo + width] = _silu(conv_step(t, lo, width)) * conv_step(t, D_FF + lo, width)
        for lb in seqs:
            for k in range(2):
                row = rows_hist + (tc - 2 + k) * bb + lb
                st_ref[lb, k:k + 1, :] = ext[row:row + 1]
        dn = _wdot(act_s[...], wdown_ref[...])
        for t in range(tc):
            o_ref[step_rows(t)] = x1[step_rows(t)] + _mod_rows(mod_ref, 5) * dn[step_rows(t)]
        if final:
            o_ref[...] = _rmsnorm(o_ref[...], fg_ref[...])
        return

    x1 = _read_rows(x1_ref)
    hn = _rmsnorm(x1, g_ref[...])
    for lb in seqs:
        h_s[rows_of(lb)] = hn[rows_of(lb)] * (1.0 + _mod_row(mod_ref, 4, lb)) + _mod_row(mod_ref, 3, lb)

    @pl.when(j == 0)
    def _init():
        ext[...] = jnp.zeros(ext.shape, F32)
        if not zero_init:
            ext[:, HIST - 2:HIST] = st0_ref[...]

    if bb == 1:
        ext[0, HIST:HIST + tc] = _wdot(h_s[...], wup_ref[...])
    else:
        up = _wdot(h_s[...], wup_ref[...])
        for lb in seqs:
            ext[lb, HIST:HIST + tc] = up[rows_of(lb)]

    for lb in seqs:
        def conv_cols(lo, width):
            e = ext[lb, :, lo:lo + width]
            w = ffnw_ref[:, lo:lo + width]
            y = w[2:3] * e + w[1:2] * pltpu.roll(e, 1, 0) + w[0:1] * pltpu.roll(e, 2, 0)
            return y[HIST:HIST + tc]

        for lo in range(0, D_FF, CONV_COLS):
            width = min(CONV_COLS, D_FF - lo)
            act_s[rows_of(lb), lo:lo + width] = _silu(conv_cols(lo, width)) * conv_cols(D_FF + lo, width)
        new_st = ext[lb, HIST + tc - 2:HIST + tc]
        ext[lb, HIST - 2:HIST] = new_st
        st_ref[lb] = new_st

    dn = _wdot(act_s[...], wdown_ref[...])
    for lb in seqs:
        _write_seq_rows(o_ref, lb, tc, x1[rows_of(lb)] + _mod_row(mod_ref, 5, lb) * dn[rows_of(lb)])
    if final:
        o_ref[...] = _rmsnorm(o_ref[...], fg_ref[...])


def _ffn_call(x1, mod, g2, w_up, ffn_w, w_down, final_g, state, batch, t, final, layer, zero_init):
    bb, tc = _blocking(batch, t, 1, FFN_ROWS, FFN_SHORT_ROWS)
    nt = t // tc
    tp = _round_up(tc, SUBLANES)
    d = x1.shape[1]
    st_struct = jax.ShapeDtypeStruct(state.shape, F32)
    st_spec = _layer_block_spec(state, bb, layer)
    mod, mod_spec = _mod_operand(mod, bb)
    time_major = tc < SUBLANES
    if time_major:
        x1 = x1.reshape(batch // bb, bb, tc, d).transpose(0, 2, 1, 3).reshape(batch * tc, d)
    x_op, x_spec = _rows_operand(x1, batch, t, bb, tc)
    out, new_state = pl.pallas_call(
        functools.partial(_ffn_kernel, bb=bb, tc=tc, final=final, zero_init=zero_init),
        grid=(batch // bb, nt),
        in_specs=[x_spec, mod_spec,
                  _resident_spec(g2, layer), _resident_spec(w_up, layer), _resident_spec(ffn_w, layer),
                  _resident_spec(w_down, layer), _resident_spec(final_g)]
        + [pl.BlockSpec(memory_space=pl.ANY) if zero_init else st_spec],
        out_specs=[x_spec, st_spec],
        out_shape=[jax.ShapeDtypeStruct(x_op.shape, F32), st_struct],
        input_output_aliases={7: 1},
        scratch_shapes=[pltpu.VMEM((bb * tc, d), F32),
                        pltpu.VMEM(((tc + 2) * bb, 2 * D_FF) if tc < SUBLANES else (bb, HIST + tp, 2 * D_FF), F32),
                        pltpu.VMEM((bb * tc, D_FF), F32)],
        compiler_params=_params(),
        name="ffn",
    )(x_op, mod, g2, w_up, ffn_w, w_down, final_g, state)
    out = out.reshape(x1.shape)
    if time_major:
        out = out.reshape(batch // bb, tc, bb, d).transpose(0, 2, 1, 3).reshape(batch * tc, d)
    return out, new_state


def _rope_tables(t, pos0):
    half = HEAD_D // 2
    inv = ROPE_BASE ** (-jnp.arange(half, dtype=F32) / half)
    pos = pos0 + jnp.arange(t, dtype=jnp.int32)
    ang = pos.astype(F32)[:, None] * inv[None, :]
    cos, sin = jnp.cos(ang), jnp.sin(ang)
    cos_t = jnp.tile(jnp.concatenate([cos, cos], axis=-1), (1, HEADS))
    sin_t = jnp.tile(jnp.concatenate([-sin, sin], axis=-1), (1, HEADS))
    return cos_t, sin_t


def _block_diag(pool_w):
    groups, gw, _ = pool_w.shape
    out = jnp.zeros((groups * gw, groups * gw), pool_w.dtype)
    for g in range(groups):
        out = out.at[g * gw:(g + 1) * gw, g * gw:(g + 1) * gw].set(pool_w[g])
    return out


def kernel(x_prompt, x_sample, c_prompt, c_sample, state_pool, state_ret, state_sconv, state_shift, state_wkv, state_ffn, w_ada, b_ada, norm1_g, norm2_g, w_in, pool_w, pool_scale, sc_w, rw_mu, rw_w0, rw_w_lora, rw_a0, rw_a_lora, rw_g_lora, rw_k_k, rw_k_a, rw_r_k, rw_ln_g, rw_ln_b, w_br, w_out, w_up, ffn_w, w_down, final_g):
    depth = w_in.shape[0]
    bp, tp_, d = x_prompt.shape
    bs, ts, _ = x_sample.shape

    mods = _mod_call(jnp.concatenate([c_prompt, c_sample], axis=0), w_ada, b_ada)
    cos_p, sin_p = _rope_tables(tp_, 0)
    cos_s, sin_s = _rope_tables(ts, PAST_LEN)
    final_g2 = final_g.reshape(1, d)

    xp = x_prompt.reshape(bp * tp_, d)
    xs = x_sample.reshape(bs * ts, d)
    st_s = (state_pool, state_ret, state_sconv, state_shift.reshape(depth, bs, 1, d), state_wkv)
    st_p = tuple(jnp.zeros((depth, bp) + st.shape[2:], F32) for st in st_s)
    ffn_s = state_ffn
    ffn_p = jnp.zeros((depth, bp) + state_ffn.shape[2:], F32)
    vec_rows = [pool_scale, sc_w[:, 0], sc_w[:, 1], sc_w[:, 2], rw_w0, rw_a0, rw_k_k, rw_k_a,
                rw_r_k.reshape(depth, MIX_W), rw_ln_g, rw_ln_b]
    vecs = jnp.concatenate([jnp.stack(vec_rows, axis=1),
                            jnp.zeros((depth, N_VECS - len(vec_rows), MIX_W), F32)], axis=1)
    mix_w = (norm1_g.reshape(depth, 1, d), w_in.astype(BF16))
    seq_w = (jnp.stack([_block_diag(pool_w[l]) for l in range(depth)]).astype(BF16), vecs,
             rw_mu.reshape(depth, 1, d), rw_w_lora, rw_a_lora, rw_g_lora, w_br.astype(BF16), w_out.astype(BF16))
    ffn_ws = (norm2_g.reshape(depth, 1, d), w_up.astype(BF16), ffn_w, w_down.astype(BF16), final_g2)
    for l in range(depth):
        mod_p, mod_s = mods[l, :, :bp], mods[l, :, bp:]
        final = l == depth - 1

        x1, st_p = _mix_call(xp, mod_p, *mix_w, cos_p, sin_p, st_p, *seq_w, batch=bp, t=tp_, pos0=0,
                             layer=l, zero_init=True)
        xp, ffn_p = _ffn_call(x1, mod_p, *ffn_ws, ffn_p, batch=bp, t=tp_, final=final, layer=l, zero_init=True)
        x1, st_s = _mix_call(xs, mod_s, *mix_w, cos_s, sin_s, st_s, *seq_w, batch=bs, t=ts, pos0=PAST_LEN,
                             layer=l, zero_init=False)
        xs, ffn_s = _ffn_call(x1, mod_s, *ffn_ws, ffn_s, batch=bs, t=ts, final=final, layer=l, zero_init=False)

    def finish(st, ffn_st, batch):
        pool, ret, sc, sh, wkv = st
        return (pool, ret, sc, sh.reshape(depth, batch, d), wkv, ffn_st)

    return (xp.reshape(bp, tp_, d), xs.reshape(bs, ts, d), *finish(st_p, ffn_p, bp), *finish(st_s, ffn_s, bs))
```

```python
import functools
import math

import jax
import jax.numpy as jnp
from jax import lax
from jax.experimental import pallas as pl
from jax.experimental.pallas import tpu as pltpu

F32 = jnp.float32
BF16 = jnp.bfloat16

D_MODEL = 1024
MIX_W = 256
HEADS = 4
HEAD_D = 64
POOL_PAST = 15
ROPE_BASE = 10000.0
LORA_W = 64
LORA_A = 64
LORA_G = 128
D_FF = 2816
PAST_LEN = 16384
NORM_EPS = 1e-6
GN_EPS = 1e-6
RWKV_LN_EPS = 64e-5
L2_EPS = 1e-12

COL_RET = MIX_W
COL_SC = COL_RET + 4 * MIX_W
COL_RWKV = COL_SC + 3 * MIX_W
COL_GATE = COL_RWKV + 3 * MIX_W + LORA_W + LORA_A + LORA_G
IN_COLS = COL_GATE + 4 * D_MODEL

SUBLANES = 8
VMEM_LIMIT = 56 * 1024 * 1024
RWKV_CHUNK = 64
INV_BASE = 16
POOL_HIST = 16
HIST = SUBLANES
CONV_COLS = 512
MIX_SPLIT = 8
MIX_ROWS = 512
FFN_ROWS = 512
SHORT_ROWS = 64
FFN_SHORT_ROWS = 128
FFN_COLS = 1408
POOL_WINDOWS = (2, 4, 8, 16)
HEAD_SHIFT = 6

(V_POOL_SCALE, V_SC0, V_SC1, V_SC2, V_W0, V_A0, V_KK, V_KA, V_RK, V_LNG, V_LNB) = range(11)
N_VECS = 16


def _bdot(a, b):
    return jnp.dot(a.astype(BF16), b.astype(BF16), preferred_element_type=F32)


def _bdot_nt(a, b):
    return lax.dot_general(a.astype(BF16), b.astype(BF16), (((1,), (1,)), ((), ())), preferred_element_type=F32)


def _bdot_tn(a, b):
    return lax.dot_general(a.astype(BF16), b.astype(BF16), (((0,), (0,)), ((), ())), preferred_element_type=F32)


def _wdot(a, w_bf16):
    return jnp.dot(a.astype(BF16), w_bf16, preferred_element_type=F32)


def _sigmoid(x):
    return 1.0 / (1.0 + jnp.exp(-x))


def _silu(x):
    half = 0.5 * x
    return half + half * jnp.tanh(half)


def _softplus(x):
    return jnp.maximum(x, 0.0) + jnp.log(1.0 + jnp.exp(-jnp.abs(x)))


def _rmsnorm(x, g):
    return x * lax.rsqrt(jnp.mean(x * x, axis=-1, keepdims=True) + NORM_EPS) * g


def _round_up(n, m):
    return (n + m - 1) // m * m


def _mod_row(mod_ref, k, lb):
    if len(mod_ref.shape) == 4:
        return mod_ref[k, lb]
    return mod_ref[k, lb:lb + 1, :]


def _mod_rows(mod_ref, k):
    rows = mod_ref[k]
    return rows.reshape(rows.shape[0], rows.shape[-1])


def _mod_operand(mod, bb):
    six, _, d = mod.shape
    if bb % SUBLANES:
        return mod[:, :, None, :], pl.BlockSpec((six, bb, 1, d), lambda i, j: (0, i, 0, 0))
    return mod, pl.BlockSpec((six, bb, d), lambda i, j: (0, i, 0))


def _read_rows(ref):
    val = ref[...]
    return val.reshape(-1, val.shape[-1]) if val.ndim == 3 else val


def _write_seq_rows(ref, lb, tc, val):
    if len(ref.shape) == 3:
        ref[lb] = val
    else:
        ref[lb * tc:(lb + 1) * tc] = val


def _mod_kernel(c_ref, w_ref, b_ref, o_ref):
    o_ref[0, 0] = _wdot(_silu(c_ref[...]), w_ref[0].astype(BF16)) + b_ref[0]


def _mod_call(c_all, w_ada, b_ada):
    depth, d, cols = w_ada.shape
    rows = c_all.shape[0]
    cb = D_MODEL
    return pl.pallas_call(
        _mod_kernel,
        grid=(depth, cols // cb),
        in_specs=[pl.BlockSpec((rows, d), lambda l, c: (0, 0)),
                  pl.BlockSpec((1, d, cb), lambda l, c: (l, 0, c)),
                  pl.BlockSpec((1, 1, cb), lambda l, c: (l, 0, c))],
        out_specs=pl.BlockSpec((1, 1, rows, cb), lambda l, c: (l, c, 0, 0)),
        out_shape=jax.ShapeDtypeStruct((depth, cols // cb, rows, cb), F32),
        name="adaln_mod",
    )(c_all, w_ada, b_ada.reshape(depth, 1, cols))


def _tri_inverse_minus_eye(lowers, n):
    i = lax.broadcasted_iota(jnp.int32, (n, n), 0)
    j = lax.broadcasted_iota(jnp.int32, (n, n), 1)

    def same_block(size):
        sh = int(math.log2(size))
        return jnp.right_shift(i, sh) == jnp.right_shift(j, sh)

    base = min(n, INV_BASE)
    if base < n:
        diag_mask = same_block(base)
        xs = [jnp.where(diag_mask, low, 0.0) for low in lowers]
    else:
        xs = list(lowers)
    powers, p = xs, 1
    while 2 * p < base:
        powers = [_bdot(pw, pw) for pw in powers]
        prods = [_bdot(x, pw) for x, pw in zip(xs, powers)]
        xs = [x + pw + pr for x, pw, pr in zip(xs, powers, prods)]
        p *= 2
    size = base
    while size < n:
        off_mask = same_block(2 * size) & jnp.logical_not(same_block(size))
        offs = [jnp.where(off_mask, low, 0.0) for low in lowers]
        lefts = [off + _bdot(x, off) for x, off in zip(xs, offs)]
        xs = [x + left + _bdot(left, x) for x, left in zip(xs, lefts)]
        size *= 2
    return xs


def _mix_kernel(*refs, bb, tc, pos0, zero_init):
    refs = list(refs)
    x_ref, mod_ref, g_ref, win_ref, cos_ref, sin_ref = refs[:6]
    del refs[:6]
    if not zero_init:
        pool0_ref, ret0_ref, sc0_ref, sh0_ref, wkv0_ref = refs[:5]
        del refs[:5]
    poolw_ref, vecs_ref, mu_ref, wlora_ref, alora_ref, glora_ref, wbr_ref, wout_ref = refs[:8]
    del refs[:8 + (5 if zero_init else 0)]
    x1_ref, pool_ref, ret_ref, sc_ref, sh_ref, wkv_ref = refs[:6]
    h_s, z_s, y_s, dm_s, pool_ext, sc_ext, sh_ext, ywkv, wsum_s, conv_s, prev_s = refs[6:]
    j = pl.program_id(1)
    seqs = range(bb)
    heads = range(HEADS)
    hsl = [slice(h * HEAD_D, (h + 1) * HEAD_D) for h in heads]

    def rows_of(lb):
        return slice(lb * tc, (lb + 1) * tc)

    lane = lax.broadcasted_iota(jnp.int32, (1, MIX_W), 1)
    row_i = lax.broadcasted_iota(jnp.int32, (tc, 1), 0)
    col_i = lax.broadcasted_iota(jnp.int32, (1, tc), 1)
    row_f = row_i.astype(F32)
    lgs = [math.log1p(-(2.0 ** (-5.0 - h))) for h in heads]

    @pl.when(j == 0)
    def _init():
        diff = (row_i - col_i).astype(F32)
        for h in heads:
            dm_s[h] = jnp.where(diff >= 0, jnp.exp(lgs[h] * jnp.maximum(diff, 0.0)), 0.0)
        pool_ext[...] = jnp.zeros(pool_ext.shape, F32)
        sc_ext[...] = jnp.zeros(sc_ext.shape, F32)
        sh_ext[...] = jnp.zeros(sh_ext.shape, F32)
        if zero_init:
            ret_ref[...] = jnp.zeros(ret_ref.shape, F32)
            wkv_ref[...] = jnp.zeros(wkv_ref.shape, F32)
        else:
            pool_ext[:, POOL_HIST - POOL_PAST:POOL_HIST] = pool0_ref[...]
            sc_ext[:, HIST - 2:HIST] = sc0_ref[...]
            sh_ext[:, HIST - 1:HIST] = sh0_ref[...]
            ret_ref[...] = ret0_ref[...]
            wkv_ref[...] = wkv0_ref[...]

    def vec(row):
        return vecs_ref[row:row + 1, :]

    x = _read_rows(x_ref)
    hn = _rmsnorm(x, g_ref[...])
    for lb in seqs:
        h_s[rows_of(lb)] = hn[rows_of(lb)] * (1.0 + _mod_row(mod_ref, 1, lb)) + _mod_row(mod_ref, 0, lb)
    z_s[...] = _wdot(h_s[...], win_ref[...])

    def gated_proj(n):
        proj = _wdot(y_s[:, n * MIX_W:(n + 1) * MIX_W], wbr_ref[n])
        return _sigmoid(z_s[:, COL_GATE + n * D_MODEL:COL_GATE + (n + 1) * D_MODEL]) * proj

    rows_all = bb * tc
    grp = jnp.right_shift(lane, HEAD_SHIFT)

    def by_group(values):
        out = values[-1]
        for g in range(len(values) - 2, -1, -1):
            out = jnp.where(grp == g, values[g], out)
        return out

    win = by_group(POOL_WINDOWS)
    brow = lax.broadcasted_iota(jnp.int32, (rows_all, 1), 0)
    cnt = jnp.minimum(win, pos0 + j * tc + jnp.bitwise_and(brow, tc - 1) + 1).astype(F32)
    first_half = jnp.bitwise_and(lane, HEAD_D - 1) < HEAD_D // 2
    cos = cos_ref[...]
    sin = sin_ref[...]
    if cos.shape[0] != rows_all:
        cos = jnp.concatenate([cos] * bb, axis=0)
        sin = jnp.concatenate([sin] * bb, axis=0)

    def rope(t):
        swapped = jnp.where(first_half, pltpu.roll(t, MIX_W - HEAD_D // 2, 1), pltpu.roll(t, HEAD_D // 2, 1))
        return t * cos + swapped * sin

    si = lax.broadcasted_iota(jnp.int32, (MIX_W, MIX_W), 0)
    sj = lax.broadcasted_iota(jnp.int32, (MIX_W, MIX_W), 1)
    head_ones = (jnp.right_shift(si, HEAD_SHIFT) == jnp.right_shift(sj, HEAD_SHIFT)).astype(BF16)

    def split_dot(ones, t):
        hi = t.astype(BF16)
        lo = (t - hi.astype(F32)).astype(BF16)
        return jnp.dot(ones, hi, preferred_element_type=F32) + jnp.dot(ones, lo, preferred_element_type=F32)

    def head_sum(t):
        hi = t.astype(BF16)
        lo = (t - hi.astype(F32)).astype(BF16)
        return (jnp.dot(hi, head_ones, preferred_element_type=F32)
                + jnp.dot(lo, head_ones, preferred_element_type=F32))

    c = min(RWKV_CHUNK, tc)
    nsub = tc // c
    merged = c % SUBLANES == 0
    ci = lax.broadcasted_iota(jnp.int32, (c, c), 0)
    cj = lax.broadcasted_iota(jnp.int32, (c, c), 1)
    strict = ci > cj
    incl = ci >= cj
    ti = lax.broadcasted_iota(jnp.int32, (rows_all, rows_all), 0)
    tj = lax.broadcasted_iota(jnp.int32, (rows_all, rows_all), 1)
    shift_c = int(math.log2(c))
    tri = ((ti >= tj) & (jnp.right_shift(ti, shift_c) == jnp.right_shift(tj, shift_c))).astype(BF16)

    for lb in seqs:
        rows = rows_of(lb)
        pool_ext[lb, POOL_HIST:POOL_HIST + tc] = z_s[rows, 0:MIX_W]
        e = pool_ext[lb]
        sums, width = [], 1
        for window in POOL_WINDOWS:
            while width < window:
                e = e + pltpu.roll(e, width, 0)
                width *= 2
            sums.append(e)
        wsum = by_group(sums)
        wsum_s[rows] = wsum[POOL_HIST:POOL_HIST + tc]
        new_pool = pool_ext[lb, tc + POOL_HIST - POOL_PAST:tc + POOL_HIST]
        pool_ext[lb, POOL_HIST - POOL_PAST:POOL_HIST] = new_pool
        pool_ref[lb] = new_pool

        sc_ext[lb, HIST:HIST + tc] = (z_s[rows, COL_SC + 2 * MIX_W:COL_SC + 3 * MIX_W]
                                      * z_s[rows, COL_SC:COL_SC + MIX_W])
        e = sc_ext[lb]
        conv = vec(V_SC2) * e + vec(V_SC1) * pltpu.roll(e, 1, 0) + vec(V_SC0) * pltpu.roll(e, 2, 0)
        conv_s[rows] = conv[HIST:HIST + tc]
        new_sc = sc_ext[lb, HIST + tc - 2:HIST + tc]
        sc_ext[lb, HIST - 2:HIST] = new_sc
        sc_ref[lb] = new_sc

        sh_ext[lb, HIST:HIST + tc] = z_s[rows, COL_RWKV:COL_GATE]
        prev_s[rows] = pltpu.roll(sh_ext[lb], 1, 0)[HIST:HIST + tc]
        new_sh = sh_ext[lb, HIST + tc - 1:HIST + tc]
        sh_ext[lb, HIST - 1:HIST] = new_sh
        sh_ref[lb] = new_sh

    u = z_s[:, 0:MIX_W]
    y_s[:, 0:MIX_W] = _wdot(wsum_s[...] / cnt - u, poolw_ref[...]) * vec(V_POOL_SCALE)
    y_s[:, 2 * MIX_W:3 * MIX_W] = z_s[:, COL_SC + MIX_W:COL_SC + 2 * MIX_W] * conv_s[...]

    q_all = rope(z_s[:, COL_RET:COL_RET + MIX_W])
    k_all = rope(z_s[:, COL_RET + MIX_W:COL_RET + 2 * MIX_W]) * (HEAD_D ** -0.5)
    v_all = z_s[:, COL_RET + 2 * MIX_W:COL_RET + 3 * MIX_W]

    zz = z_s[:, COL_RWKV:COL_GATE]
    zs = zz + (prev_s[...] - zz) * mu_ref[...]
    r = zs[:, 0:MIX_W]
    kx = zs[:, MIX_W:2 * MIX_W]
    vx = zs[:, 2 * MIX_W:3 * MIX_W]
    o3 = 3 * MIX_W
    wl = zs[:, o3:o3 + LORA_W]
    al = zs[:, o3 + LORA_W:o3 + LORA_W + LORA_A]
    gl = zs[:, o3 + LORA_W + LORA_A:]
    wlog = -_softplus(-(vec(V_W0) + _bdot(jnp.tanh(wl), wlora_ref[...]))) - 0.5
    logw = -jnp.exp(wlog)
    asig = _sigmoid(vec(V_A0) + _bdot(al, alora_ref[...]))
    g_out = _bdot(_sigmoid(gl), glora_ref[...])
    k2 = kx * (1.0 + (asig - 1.0) * vec(V_KA))
    kk_raw = kx * vec(V_KK)
    kk = kk_raw * lax.rsqrt(jnp.maximum(head_sum(kk_raw * kk_raw), L2_EPS * L2_EPS))
    cum = split_dot(tri, logw)
    p_in = jnp.exp(cum)
    p_inv = jnp.exp(-cum)
    rw = dict(a=-kk * jnp.exp(cum - logw), b=kk * asig * p_inv, k=k2 * p_inv, r=r * p_in, v=vx, p_in=p_in)

    pairs = [(lb, h) for lb in seqs for h in heads]
    qs = [q_all[rows_of(lb), hsl[h]] for lb, h in pairs]
    ks = [k_all[rows_of(lb), hsl[h]] for lb, h in pairs]
    vs = [v_all[rows_of(lb), hsl[h]] for lb, h in pairs]
    states = [ret_ref[lb, h] for lb, h in pairs]
    np_ = range(len(pairs))
    scores = [_bdot_nt(qs[i], ks[i]) for i in np_]
    cross = [_bdot(qs[i] * jnp.exp(lgs[pairs[i][1]] * (row_f + 1.0)), states[i]) for i in np_]
    kv = [_bdot_tn(ks[i] * jnp.exp(lgs[pairs[i][1]] * (tc - 1.0 - row_f)), vs[i]) for i in np_]
    outs = [_bdot(scores[i] * dm_s[pairs[i][1]], vs[i]) + cross[i] for i in np_]
    for i, (lb, h) in enumerate(pairs):
        ret_ref[lb, h] = states[i] * math.exp(lgs[h] * tc) + kv[i]
        y_s[rows_of(lb), MIX_W + h * HEAD_D:MIX_W + (h + 1) * HEAD_D] = outs[i]
    o_all = y_s[:, MIX_W:2 * MIX_W]
    o_all = o_all * lax.rsqrt(head_sum(o_all * o_all) * (1.0 / HEAD_D) + GN_EPS)
    y_s[:, MIX_W:2 * MIX_W] = o_all * _silu(z_s[:, COL_RET + 3 * MIX_W:COL_RET + 4 * MIX_W])

    blocks = [(lb, sub, h) for lb in seqs for sub in range(nsub) for h in heads]

    def blk(name):
        return [rw[name][lb * tc + sub * c:lb * tc + (sub + 1) * c, hsl[h]] for lb, sub, h in blocks]

    a_l, b_l, k_l, r_l, v_l = blk("a"), blk("b"), blk("k"), blk("r"), blk("v")
    nb = range(len(blocks))
    if merged:
        bk_l = [jnp.concatenate([b_l[i], k_l[i]], axis=0) for i in nb]
        gram = [_bdot_nt(jnp.concatenate([a_l[i], r_l[i]], axis=0), bk_l[i]) for i in nb]
        g_ab, g_ak = [g[:c, :c] for g in gram], [g[:c, c:] for g in gram]
        g_rb, g_rk = [g[c:, :c] for g in gram], [g[c:, c:] for g in gram]
    else:
        g_ab = [_bdot_nt(a_l[i], b_l[i]) for i in nb]
        g_ak = [_bdot_nt(a_l[i], k_l[i]) for i in nb]
        g_rb = [_bdot_nt(r_l[i], b_l[i]) for i in nb]
        g_rk = [_bdot_nt(r_l[i], k_l[i]) for i in nb]
    l_ab = [jnp.where(strict, g, 0.0) for g in g_ab]
    l_ak = [jnp.where(strict, g, 0.0) for g in g_ak]
    m_rb = [jnp.where(incl, g, 0.0) for g in g_rb]
    m_rk = [jnp.where(incl, g, 0.0) for g in g_rk]
    n_inv = _tri_inverse_minus_eye(l_ab, c)
    if merged:
        lv = [_bdot(jnp.concatenate([l_ak[i], m_rk[i]], axis=0), v_l[i]) for i in nb]
        lakv, mv = [t[:c] for t in lv], [t[c:] for t in lv]
    else:
        lakv = [_bdot(l_ak[i], v_l[i]) for i in nb]
        mv = [_bdot(m_rk[i], v_l[i]) for i in nb]
    nx = [_bdot(n_inv[i], jnp.concatenate([lakv[i], a_l[i]], axis=1)) for i in nb]
    tlv = [lakv[i] + nx[i][:, :HEAD_D] for i in nb]
    ta = [a_l[i] + nx[i][:, HEAD_D:] for i in nb]

    wkv_states = [wkv_ref[lb, h] for lb, h in pairs]
    for sub in range(nsub):
        ids = [(lb * nsub + sub) * HEADS + h for lb, h in pairs]
        p_end = [rw["p_in"][lb * tc + (sub + 1) * c - 1:lb * tc + (sub + 1) * c, hsl[h]] for lb, h in pairs]
        u_mat = [_bdot_nt(ta[i], wkv_states[n]) + tlv[i] for n, i in enumerate(ids)]
        y_st = [_bdot_nt(r_l[i], wkv_states[n]) for n, i in enumerate(ids)]
        if merged:
            upd = [_bdot_tn(jnp.concatenate([u_mat[n], v_l[i]], axis=0), bk_l[i] * p_end[n])
                   for n, i in enumerate(ids)]
        else:
            upd = [_bdot_tn(u_mat[n], b_l[i] * p_end[n]) + _bdot_tn(v_l[i], k_l[i] * p_end[n])
                   for n, i in enumerate(ids)]
        y_u = [_bdot(m_rb[i], u_mat[n]) for n, i in enumerate(ids)]
        for n, i in enumerate(ids):
            lb, h = pairs[n]
            ywkv[lb * tc + sub * c:lb * tc + (sub + 1) * c, hsl[h]] = y_st[n] + y_u[n] + mv[i]
        wkv_states = [wkv_states[n] * p_end[n] + upd[n] for n in np_]
    for n, (lb, h) in enumerate(pairs):
        wkv_ref[lb, h] = wkv_states[n]

    y_all = ywkv[...]
    cen = y_all - head_sum(y_all) * (1.0 / HEAD_D)
    var = head_sum(cen * cen) * (1.0 / HEAD_D)
    yn = cen * lax.rsqrt(var + RWKV_LN_EPS) * vec(V_LNG) + vec(V_LNB)
    bonus = head_sum(r * k2 * vec(V_RK)) * vx
    y_s[:, 3 * MIX_W:4 * MIX_W] = (yn + bonus) * g_out

    out = _wdot(gated_proj(0) + gated_proj(1) + gated_proj(2) + gated_proj(3), wout_ref[...])
    for lb in seqs:
        _write_seq_rows(x1_ref, lb, tc, x[rows_of(lb)] + _mod_row(mod_ref, 2, lb) * out[rows_of(lb)])


def _layer_block_spec(arr, bb, layer):
    shape = arr.shape[2:]
    zeros = (0,) * len(shape)
    return pl.BlockSpec((None, bb) + shape, lambda i, j: (layer, i) + zeros)


def _resident_spec(arr, layer=None):
    if layer is None:
        zeros = (0,) * arr.ndim
        return pl.BlockSpec(arr.shape, lambda i, j: zeros, pipeline_mode=pl.Buffered(1))
    zeros = (0,) * (arr.ndim - 1)
    return pl.BlockSpec((None,) + arr.shape[1:], lambda i, j: (layer,) + zeros, pipeline_mode=pl.Buffered(1))


def _rows_operand(x2d, batch, t, bb, tc):
    d = x2d.shape[1]
    if tc == t:
        return x2d, pl.BlockSpec((bb * tc, d), lambda i, j: (i, 0))
    return x2d.reshape(batch, t, d), pl.BlockSpec((bb, tc, d), lambda i, j: (i, j, 0))


def _params():
    return pltpu.CompilerParams(dimension_semantics=("parallel", "arbitrary"), vmem_limit_bytes=VMEM_LIMIT)


def _blocking(batch, t, split=1, rows=256, short_rows=SHORT_ROWS):
    if t >= SHORT_ROWS:
        split = split if batch % split == 0 else 1
        return split, min(t, rows // split)
    return min(batch, short_rows // t), t


def _mix_call(x2d, mod, g1, w_in, cos_t, sin_t, states, poolw, vecs, mu, wlora, alora, glora, w_br, w_out,
              batch, t, pos0, layer, zero_init):
    bb, tc = _blocking(batch, t, MIX_SPLIT, MIX_ROWS)
    nt = t // tc
    tp = _round_up(tc, SUBLANES)
    d = x2d.shape[1]
    out_structs = [jax.ShapeDtypeStruct(st.shape, F32) for st in states]
    state_specs = [_layer_block_spec(st, bb, layer) for st in states]
    weights = (poolw, vecs, mu, wlora, alora, glora, w_br, w_out)
    mod, mod_spec = _mod_operand(mod, bb)
    if zero_init:
        operands = (*weights, *states)
        operand_specs = [_resident_spec(a, layer) for a in weights] + [pl.BlockSpec(memory_space=pl.ANY)] * 5
        first_state = 6 + len(weights)
    else:
        operands = (*states, *weights)
        operand_specs = state_specs + [_resident_spec(a, layer) for a in weights]
        first_state = 6
    x_op, x_spec = _rows_operand(x2d, batch, t, bb, tc)

    table_rows = tc if tc % SUBLANES == 0 else bb * tc

    def block_table(tab):
        if table_rows == tc:
            return tab
        return jnp.tile(tab.reshape(nt, tc, MIX_W), (1, bb, 1)).reshape(nt * bb * tc, MIX_W)

    outs = pl.pallas_call(
        functools.partial(_mix_kernel, bb=bb, tc=tc, pos0=pos0, zero_init=zero_init),
        grid=(batch // bb, nt),
        in_specs=[x_spec, mod_spec,
                  _resident_spec(g1, layer), _resident_spec(w_in, layer),
                  pl.BlockSpec((table_rows, MIX_W), lambda i, j: (j, 0)),
                  pl.BlockSpec((table_rows, MIX_W), lambda i, j: (j, 0))]
        + operand_specs,
        out_specs=[x_spec] + state_specs,
        out_shape=[jax.ShapeDtypeStruct(x_op.shape, F32)] + out_structs,
        input_output_aliases={first_state + k: 1 + k for k in range(5)},
        scratch_shapes=[pltpu.VMEM((bb * tc, d), F32),
                        pltpu.VMEM((bb * tc, IN_COLS), F32),
                        pltpu.VMEM((bb * tc, d), F32),
                        pltpu.VMEM((HEADS, tc, tc), F32),
                        pltpu.VMEM((bb, POOL_HIST + tp, MIX_W), F32),
                        pltpu.VMEM((bb, HIST + tp, MIX_W), F32),
                        pltpu.VMEM((bb, HIST + tp, d), F32),
                        pltpu.VMEM((bb * tc, MIX_W), F32),
                        pltpu.VMEM((bb * tc, MIX_W), F32),
                        pltpu.VMEM((bb * tc, MIX_W), F32),
                        pltpu.VMEM((bb * tc, COL_GATE - COL_RWKV), F32)],
        compiler_params=_params(),
        name="mix",
    )(x_op, mod, g1, w_in, block_table(cos_t), block_table(sin_t), *operands)
    return outs[0].reshape(x2d.shape), tuple(outs[1:])


def _ffn_kernel(*refs, bb, tc, final, zero_init):
    refs = list(refs)
    x1_ref, mod_ref, g_ref, wup_ref, ffnw_ref, wdown_ref, fg_ref = refs[:7]
    del refs[:7]
    st0_ref = refs.pop(0)
    o_ref, st_ref, h_s, ext, act_s = refs
    j = pl.program_id(1)
    seqs = range(bb)

    def rows_of(lb):
        return slice(lb * tc, (lb + 1) * tc)

    if tc < SUBLANES:
        def step_rows(t):
            return slice(t * bb, (t + 1) * bb)

        x1 = x1_ref[...]
        hn = _rmsnorm(x1, g_ref[...])
        for t in range(tc):
            h_s[step_rows(t)] = hn[step_rows(t)] * (1.0 + _mod_rows(mod_ref, 4)) + _mod_rows(mod_ref, 3)
        rows_hist = 2 * bb
        if zero_init:
            ext[0:rows_hist] = jnp.zeros((rows_hist, 2 * D_FF), F32)
        else:
            for lb in seqs:
                for k in range(2):
                    ext[k * bb + lb:k * bb + lb + 1] = st0_ref[lb, k:k + 1, :]
        ext[rows_hist:rows_hist + tc * bb] = _wdot(h_s[...], wup_ref[...])

        def slab(t, lo, width):
            return ext[rows_hist + t * bb:rows_hist + (t + 1) * bb, lo:lo + width]

        def conv_step(t, lo, width):
            w = ffnw_ref[:, lo:lo + width]
            return (w[2:3] * slab(t, lo, width) + w[1:2] * slab(t - 1, lo, width)
                    + w[0:1] * slab(t - 2, lo, width))

        for lo in range(0, D_FF, CONV_COLS):
            width = min(CONV_COLS, D_FF - lo)
            for t in range(tc):
                act_s[step_rows(t), lo:lo + width] = _silu(conv_step(t, lo, width)) * conv_step(t, D_FF + lo, width)
        for lb in seqs:
            for k in range(2):
                row = rows_hist + (tc - 2 + k) * bb + lb
                st_ref[lb, k:k + 1, :] = ext[row:row + 1]
        dn = _wdot(act_s[...], wdown_ref[...])
        for t in range(tc):
            o_ref[step_rows(t)] = x1[step_rows(t)] + _mod_rows(mod_ref, 5) * dn[step_rows(t)]
        if final:
            o_ref[...] = _rmsnorm(o_ref[...], fg_ref[...])
        return

    x1 = _read_rows(x1_ref)
    hn = _rmsnorm(x1, g_ref[...])
    for lb in seqs:
        h_s[rows_of(lb)] = hn[rows_of(lb)] * (1.0 + _mod_row(mod_ref, 4, lb)) + _mod_row(mod_ref, 3, lb)

    @pl.when(j == 0)
    def _init():
        ext[...] = jnp.zeros(ext.shape, F32)
        if not zero_init:
            ext[:, HIST - 2:HIST] = st0_ref[...]

    if bb == 1:
        ext[0, HIST:HIST + tc] = _wdot(h_s[...], wup_ref[...])
    else:
        up = _wdot(h_s[...], wup_ref[...])
        for lb in seqs:
            ext[lb, HIST:HIST + tc] = up[rows_of(lb)]

    for lb in seqs:
        def conv_cols(lo, width):
            e = ext[lb, :, lo:lo + width]
            w = ffnw_ref[:, lo:lo + width]
            y = w[2:3] * e + w[1:2] * pltpu.roll(e, 1, 0) + w[0:1] * pltpu.roll(e, 2, 0)
            return y[HIST:HIST + tc]

        for lo in range(0, D_FF, CONV_COLS):
            width = min(CONV_COLS, D_FF - lo)
            act_s[rows_of(lb), lo:lo + width] = _silu(conv_cols(lo, width)) * conv_cols(D_FF + lo, width)
        new_st = ext[lb, HIST + tc - 2:HIST + tc]
        ext[lb, HIST - 2:HIST] = new_st
        st_ref[lb] = new_st

    dn = _wdot(act_s[...], wdown_ref[...])
    for lb in seqs:
        _write_seq_rows(o_ref, lb, tc, x1[rows_of(lb)] + _mod_row(mod_ref, 5, lb) * dn[rows_of(lb)])
    if final:
        o_ref[...] = _rmsnorm(o_ref[...], fg_ref[...])


def _ffn_call(x1, mod, g2, w_up, ffn_w, w_down, final_g, state, batch, t, final, layer, zero_init):
    bb, tc = _blocking(batch, t, 1, FFN_ROWS, FFN_SHORT_ROWS)
    nt = t // tc
    tp = _round_up(tc, SUBLANES)
    d = x1.shape[1]
    st_struct = jax.ShapeDtypeStruct(state.shape, F32)
    st_spec = _layer_block_spec(state, bb, layer)
    mod, mod_spec = _mod_operand(mod, bb)
    time_major = tc < SUBLANES
    if time_major:
        x1 = x1.reshape(batch // bb, bb, tc, d).transpose(0, 2, 1, 3).reshape(batch * tc, d)
    x_op, x_spec = _rows_operand(x1, batch, t, bb, tc)
    out, new_state = pl.pallas_call(
        functools.partial(_ffn_kernel, bb=bb, tc=tc, final=final, zero_init=zero_init),
        grid=(batch // bb, nt),
        in_specs=[x_spec, mod_spec,
                  _resident_spec(g2, layer), _resident_spec(w_up, layer), _resident_spec(ffn_w, layer),
                  _resident_spec(w_down, layer), _resident_spec(final_g)]
        + [pl.BlockSpec(memory_space=pl.ANY) if zero_init else st_spec],
        out_specs=[x_spec, st_spec],
        out_shape=[jax.ShapeDtypeStruct(x_op.shape, F32), st_struct],
        input_output_aliases={7: 1},
        scratch_shapes=[pltpu.VMEM((bb * tc, d), F32),
                        pltpu.VMEM(((tc + 2) * bb, 2 * D_FF) if tc < SUBLANES else (bb, HIST + tp, 2 * D_FF), F32),
                        pltpu.VMEM((bb * tc, D_FF), F32)],
        compiler_params=_params(),
        name="ffn",
    )(x_op, mod, g2, w_up, ffn_w, w_down, final_g, state)
    out = out.reshape(x1.shape)
    if time_major:
        out = out.reshape(batch // bb, tc, bb, d).transpose(0, 2, 1, 3).reshape(batch * tc, d)
    return out, new_state


def _ffn_cols_kernel(x1_ref, mod_ref, g_ref, wa_ref, wb_ref, cwa_ref, cwb_ref, ha_ref, hb_ref, wdown_ref, fg_ref,
                     pa_ref, pb_ref, o_ref, sta_ref, stb_ref, h_s, acc_s, *, nseq, tc, nk, final):
    del pa_ref, pb_ref
    k = pl.program_id(0)

    def step_rows(t):
        return slice(t * nseq, (t + 1) * nseq)

    @pl.when(k == 0)
    def _first():
        hn = _rmsnorm(x1_ref[...], g_ref[...])
        for t in range(tc):
            h_s[step_rows(t)] = (hn[step_rows(t)] * (1.0 + mod_ref[4]) + mod_ref[3]).astype(BF16)
        acc_s[...] = jnp.zeros(acc_s.shape, F32)

    hb = h_s[...]
    acts = []
    for w_ref, cw_ref, hist_ref, st_ref in ((wa_ref, cwa_ref, ha_ref, sta_ref), (wb_ref, cwb_ref, hb_ref, stb_ref)):
        up = jnp.dot(hb, w_ref[...], preferred_element_type=F32)

        def slab(t, up=up, hist_ref=hist_ref):
            return up[step_rows(t)] if t >= 0 else hist_ref[2 + t]

        cw = cw_ref[...]
        acts.append([cw[2:3] * slab(t) + cw[1:2] * slab(t - 1) + cw[0:1] * slab(t - 2) for t in range(tc)])
        for j in range(2):
            st_ref[j] = slab(tc - 2 + j)
    act = jnp.concatenate([_silu(ya) * yb for ya, yb in zip(*acts)], axis=0)
    acc_s[...] += _wdot(act, wdown_ref[...])

    @pl.when(k == nk - 1)
    def _last():
        x1 = x1_ref[...]
        for t in range(tc):
            o_ref[step_rows(t)] = x1[step_rows(t)] + mod_ref[5] * acc_s[step_rows(t)]
        if final:
            o_ref[...] = _rmsnorm(o_ref[...], fg_ref[...])


def _ffn_cols_call(x1, mod, g2, w_up, ffn_w, w_down, final_g, hist, prev, batch, t, final, layer):
    d = x1.shape[1]
    nk = D_FF // FFN_COLS
    rows = batch * t
    x_tm = x1.reshape(batch, t, d).transpose(1, 0, 2).reshape(rows, d)
    whole = lambda shape: pl.BlockSpec(shape, lambda k: (0,) * len(shape))

    def cols(shape, half):
        return pl.BlockSpec((None,) + shape, lambda k: (layer,) + (0,) * (len(shape) - 1) + (k + half * nk,))

    st_struct = jax.ShapeDtypeStruct(prev[0].shape, F32)
    st_spec = pl.BlockSpec((None, 2, batch, FFN_COLS), lambda k: (layer, 0, 0, k))
    any_spec = pl.BlockSpec(memory_space=pl.ANY)
    out, st_a, st_b = pl.pallas_call(
        functools.partial(_ffn_cols_kernel, nseq=batch, tc=t, nk=nk, final=final),
        grid=(nk,),
        in_specs=[whole((rows, d)), whole(mod.shape),
                  pl.BlockSpec((None, 1, d), lambda k: (layer, 0, 0)),
                  cols((d, FFN_COLS), 0), cols((d, FFN_COLS), 1),
                  cols((3, FFN_COLS), 0), cols((3, FFN_COLS), 1),
                  cols((2, batch, FFN_COLS), 0), cols((2, batch, FFN_COLS), 1),
                  pl.BlockSpec((None, FFN_COLS, d), lambda k: (layer, k, 0)),
                  whole((1, d)), any_spec, any_spec],
        out_specs=[whole((rows, d)), st_spec, st_spec],
        out_shape=[jax.ShapeDtypeStruct((rows, d), F32), st_struct, st_struct],
        input_output_aliases={11: 1, 12: 2},
        scratch_shapes=[pltpu.VMEM((rows, d), BF16), pltpu.VMEM((rows, d), F32)],
        compiler_params=pltpu.CompilerParams(dimension_semantics=("arbitrary",), vmem_limit_bytes=VMEM_LIMIT),
        name="ffn_cols",
    )(x_tm, mod, g2, w_up, w_up, ffn_w, ffn_w, hist, hist, w_down, final_g, *prev)
    out = out.reshape(t, batch, d).transpose(1, 0, 2).reshape(rows, d)
    return out, (st_a, st_b)


def _rope_tables(t, pos0):
    half = HEAD_D // 2
    inv = ROPE_BASE ** (-jnp.arange(half, dtype=F32) / half)
    pos = pos0 + jnp.arange(t, dtype=jnp.int32)
    ang = pos.astype(F32)[:, None] * inv[None, :]
    cos, sin = jnp.cos(ang), jnp.sin(ang)
    cos_t = jnp.tile(jnp.concatenate([cos, cos], axis=-1), (1, HEADS))
    sin_t = jnp.tile(jnp.concatenate([-sin, sin], axis=-1), (1, HEADS))
    return cos_t, sin_t


def _block_diag(pool_w):
    groups, gw, _ = pool_w.shape
    out = jnp.zeros((groups * gw, groups * gw), pool_w.dtype)
    for g in range(groups):
        out = out.at[g * gw:(g + 1) * gw, g * gw:(g + 1) * gw].set(pool_w[g])
    return out


def kernel(x_prompt, x_sample, c_prompt, c_sample, state_pool, state_ret, state_sconv, state_shift, state_wkv, state_ffn, w_ada, b_ada, norm1_g, norm2_g, w_in, pool_w, pool_scale, sc_w, rw_mu, rw_w0, rw_w_lora, rw_a0, rw_a_lora, rw_g_lora, rw_k_k, rw_k_a, rw_r_k, rw_ln_g, rw_ln_b, w_br, w_out, w_up, ffn_w, w_down, final_g):
    depth = w_in.shape[0]
    bp, tp_, d = x_prompt.shape
    bs, ts, _ = x_sample.shape

    mods = _mod_call(jnp.concatenate([c_prompt, c_sample], axis=0), w_ada, b_ada)
    cos_p, sin_p = _rope_tables(tp_, 0)
    cos_s, sin_s = _rope_tables(ts, PAST_LEN)
    final_g2 = final_g.reshape(1, d)

    xp = x_prompt.reshape(bp * tp_, d)
    xs = x_sample.reshape(bs * ts, d)
    st_s = (state_pool, state_ret, state_sconv, state_shift.reshape(depth, bs, 1, d), state_wkv)
    st_p = tuple(jnp.zeros((depth, bp) + st.shape[2:], F32) for st in st_s)
    ffn_s = state_ffn
    short_s = ts < SUBLANES
    if short_s:
        hist_s = jnp.transpose(state_ffn, (0, 2, 1, 3))
        ffn_s = tuple(jnp.zeros((depth, 2, bs, D_FF), F32) for _ in range(2))
    ffn_p = jnp.zeros((depth, bp) + state_ffn.shape[2:], F32)
    vec_rows = [pool_scale, sc_w[:, 0], sc_w[:, 1], sc_w[:, 2], rw_w0, rw_a0, rw_k_k, rw_k_a,
                rw_r_k.reshape(depth, MIX_W), rw_ln_g, rw_ln_b]
    vecs = jnp.concatenate([jnp.stack(vec_rows, axis=1),
                            jnp.zeros((depth, N_VECS - len(vec_rows), MIX_W), F32)], axis=1)
    mix_w = (norm1_g.reshape(depth, 1, d), w_in.astype(BF16))
    seq_w = (jnp.stack([_block_diag(pool_w[l]) for l in range(depth)]).astype(BF16), vecs,
             rw_mu.reshape(depth, 1, d), rw_w_lora, rw_a_lora, rw_g_lora, w_br.astype(BF16), w_out.astype(BF16))
    ffn_ws = (norm2_g.reshape(depth, 1, d), w_up.astype(BF16), ffn_w, w_down.astype(BF16), final_g2)
    for l in range(depth):
        mod_p, mod_s = mods[l, :, :bp], mods[l, :, bp:]
        final = l == depth - 1

        x1, st_p = _mix_call(xp, mod_p, *mix_w, cos_p, sin_p, st_p, *seq_w, batch=bp, t=tp_, pos0=0,
                             layer=l, zero_init=True)
        xp, ffn_p = _ffn_call(x1, mod_p, *ffn_ws, ffn_p, batch=bp, t=tp_, final=final, layer=l, zero_init=True)
        x1, st_s = _mix_call(xs, mod_s, *mix_w, cos_s, sin_s, st_s, *seq_w, batch=bs, t=ts, pos0=PAST_LEN,
                             layer=l, zero_init=False)
        if short_s:
            xs, ffn_s = _ffn_cols_call(x1, mod_s, *ffn_ws, hist_s, ffn_s, batch=bs, t=ts, final=final, layer=l)
        else:
            xs, ffn_s = _ffn_call(x1, mod_s, *ffn_ws, ffn_s, batch=bs, t=ts, final=final, layer=l, zero_init=False)

    def finish(st, ffn_st, batch):
        pool, ret, sc, sh, wkv = st
        return (pool, ret, sc, sh.reshape(depth, batch, d), wkv, ffn_st)

    if short_s:
        ffn_s = jnp.transpose(jnp.concatenate(ffn_s, axis=-1), (0, 2, 1, 3))
    return (xp.reshape(bp, tp_, d), xs.reshape(bs, ts, d), *finish(st_p, ffn_p, bp), *finish(st_s, ffn_s, bs))
```

```python
import functools
import math

import jax
import jax.numpy as jnp
from jax import lax
from jax.experimental import pallas as pl
from jax.experimental.pallas import tpu as pltpu

F32 = jnp.float32
BF16 = jnp.bfloat16

D_MODEL = 1024
MIX_W = 256
HEADS = 4
HEAD_D = 64
POOL_PAST = 15
ROPE_BASE = 10000.0
LORA_W = 64
LORA_A = 64
LORA_G = 128
D_FF = 2816
PAST_LEN = 16384
NORM_EPS = 1e-6
GN_EPS = 1e-6
RWKV_LN_EPS = 64e-5
L2_EPS = 1e-12

COL_RET = MIX_W
COL_SC = COL_RET + 4 * MIX_W
COL_RWKV = COL_SC + 3 * MIX_W
COL_GATE = COL_RWKV + 3 * MIX_W + LORA_W + LORA_A + LORA_G
IN_COLS = COL_GATE + 4 * D_MODEL

SUBLANES = 8
VMEM_LIMIT = 56 * 1024 * 1024
RWKV_CHUNK = 64
INV_BASE = 16
POOL_HIST = 16
HIST = SUBLANES
CONV_COLS = 512
MIX_SPLIT = 8
MIX_ROWS = 512
FFN_ROWS = 512
SHORT_ROWS = 64
FFN_SHORT_ROWS = 128
POOL_WINDOWS = (2, 4, 8, 16)
HEAD_SHIFT = 6

(V_POOL_SCALE, V_SC0, V_SC1, V_SC2, V_W0, V_A0, V_KK, V_KA, V_RK, V_LNG, V_LNB) = range(11)
N_VECS = 16


def _bdot(a, b):
    return jnp.dot(a.astype(BF16), b.astype(BF16), preferred_element_type=F32)


def _bdot_nt(a, b):
    return lax.dot_general(a.astype(BF16), b.astype(BF16), (((1,), (1,)), ((), ())), preferred_element_type=F32)


def _bdot_tn(a, b):
    return lax.dot_general(a.astype(BF16), b.astype(BF16), (((0,), (0,)), ((), ())), preferred_element_type=F32)


def _wdot(a, w_bf16):
    return jnp.dot(a.astype(BF16), w_bf16, preferred_element_type=F32)


def _sigmoid(x):
    return 1.0 / (1.0 + jnp.exp(-x))


def _silu(x):
    half = 0.5 * x
    return half + half * jnp.tanh(half)


def _softplus(x):
    return jnp.maximum(x, 0.0) + jnp.log(1.0 + jnp.exp(-jnp.abs(x)))


def _rmsnorm(x, g):
    return x * lax.rsqrt(jnp.mean(x * x, axis=-1, keepdims=True) + NORM_EPS) * g


def _round_up(n, m):
    return (n + m - 1) // m * m


def _mod_row(mod_ref, k, lb):
    if len(mod_ref.shape) == 4:
        return mod_ref[k, lb]
    return mod_ref[k, lb:lb + 1, :]


def _mod_rows(mod_ref, k):
    rows = mod_ref[k]
    return rows.reshape(rows.shape[0], rows.shape[-1])


def _mod_operand(mod, bb):
    six, _, d = mod.shape
    if bb % SUBLANES:
        return mod[:, :, None, :], pl.BlockSpec((six, bb, 1, d), lambda i, j: (0, i, 0, 0))
    return mod, pl.BlockSpec((six, bb, d), lambda i, j: (0, i, 0))


def _read_rows(ref):
    val = ref[...]
    return val.reshape(-1, val.shape[-1]) if val.ndim == 3 else val


def _write_seq_rows(ref, lb, tc, val):
    if len(ref.shape) == 3:
        ref[lb] = val
    else:
        ref[lb * tc:(lb + 1) * tc] = val


def _mod_kernel(c_ref, w_ref, b_ref, o_ref):
    o_ref[0, 0] = _wdot(_silu(c_ref[...]), w_ref[0].astype(BF16)) + b_ref[0]


def _mod_call(c_all, w_ada, b_ada):
    depth, d, cols = w_ada.shape
    rows = c_all.shape[0]
    cb = D_MODEL
    return pl.pallas_call(
        _mod_kernel,
        grid=(depth, cols // cb),
        in_specs=[pl.BlockSpec((rows, d), lambda l, c: (0, 0)),
                  pl.BlockSpec((1, d, cb), lambda l, c: (l, 0, c)),
                  pl.BlockSpec((1, 1, cb), lambda l, c: (l, 0, c))],
        out_specs=pl.BlockSpec((1, 1, rows, cb), lambda l, c: (l, c, 0, 0)),
        out_shape=jax.ShapeDtypeStruct((depth, cols // cb, rows, cb), F32),
        name="adaln_mod",
    )(c_all, w_ada, b_ada.reshape(depth, 1, cols))


def _tri_inverse_minus_eye(lowers, n):
    i = lax.broadcasted_iota(jnp.int32, (n, n), 0)
    j = lax.broadcasted_iota(jnp.int32, (n, n), 1)

    def same_block(size):
        sh = int(math.log2(size))
        return jnp.right_shift(i, sh) == jnp.right_shift(j, sh)

    base = min(n, INV_BASE)
    if base < n:
        diag_mask = same_block(base)
        xs = [jnp.where(diag_mask, low, 0.0) for low in lowers]
    else:
        xs = list(lowers)
    powers, p = xs, 1
    while 2 * p < base:
        powers = [_bdot(pw, pw) for pw in powers]
        prods = [_bdot(x, pw) for x, pw in zip(xs, powers)]
        xs = [x + pw + pr for x, pw, pr in zip(xs, powers, prods)]
        p *= 2
    size = base
    while size < n:
        off_mask = same_block(2 * size) & jnp.logical_not(same_block(size))
        offs = [jnp.where(off_mask, low, 0.0) for low in lowers]
        lefts = [off + _bdot(x, off) for x, off in zip(xs, offs)]
        xs = [x + left + _bdot(left, x) for x, left in zip(xs, lefts)]
        size *= 2
    return xs


def _mix_kernel(*refs, bb, tc, pos0, zero_init):
    refs = list(refs)
    x_ref, mod_ref, g_ref, win_ref, cos_ref, sin_ref = refs[:6]
    del refs[:6]
    if not zero_init:
        pool0_ref, ret0_ref, sc0_ref, sh0_ref, wkv0_ref = refs[:5]
        del refs[:5]
    poolw_ref, vecs_ref, mu_ref, wlora_ref, alora_ref, glora_ref, wbr_ref, wout_ref = refs[:8]
    del refs[:8 + (5 if zero_init else 0)]
    x1_ref, pool_ref, ret_ref, sc_ref, sh_ref, wkv_ref = refs[:6]
    h_s, z_s, y_s, dm_s, pool_ext, sc_ext, sh_ext, ywkv, wsum_s, conv_s, prev_s = refs[6:]
    j = pl.program_id(1)
    seqs = range(bb)
    heads = range(HEADS)
    hsl = [slice(h * HEAD_D, (h + 1) * HEAD_D) for h in heads]

    def rows_of(lb):
        return slice(lb * tc, (lb + 1) * tc)

    lane = lax.broadcasted_iota(jnp.int32, (1, MIX_W), 1)
    row_i = lax.broadcasted_iota(jnp.int32, (tc, 1), 0)
    col_i = lax.broadcasted_iota(jnp.int32, (1, tc), 1)
    row_f = row_i.astype(F32)
    lgs = [math.log1p(-(2.0 ** (-5.0 - h))) for h in heads]

    @pl.when(j == 0)
    def _init():
        diff = (row_i - col_i).astype(F32)
        for h in heads:
            dm_s[h] = jnp.where(diff >= 0, jnp.exp(lgs[h] * jnp.maximum(diff, 0.0)), 0.0)
        pool_ext[...] = jnp.zeros(pool_ext.shape, F32)
        sc_ext[...] = jnp.zeros(sc_ext.shape, F32)
        sh_ext[...] = jnp.zeros(sh_ext.shape, F32)
        if zero_init:
            ret_ref[...] = jnp.zeros(ret_ref.shape, F32)
            wkv_ref[...] = jnp.zeros(wkv_ref.shape, F32)
        else:
            pool_ext[:, POOL_HIST - POOL_PAST:POOL_HIST] = pool0_ref[...]
            sc_ext[:, HIST - 2:HIST] = sc0_ref[...]
            sh_ext[:, HIST - 1:HIST] = sh0_ref[...]
            ret_ref[...] = ret0_ref[...]
            wkv_ref[...] = wkv0_ref[...]

    def vec(row):
        return vecs_ref[row:row + 1, :]

    x = _read_rows(x_ref)
    hn = _rmsnorm(x, g_ref[...])
    for lb in seqs:
        h_s[rows_of(lb)] = hn[rows_of(lb)] * (1.0 + _mod_row(mod_ref, 1, lb)) + _mod_row(mod_ref, 0, lb)
    z_s[...] = _wdot(h_s[...], win_ref[...])

    def gated_proj(n):
        proj = _wdot(y_s[:, n * MIX_W:(n + 1) * MIX_W], wbr_ref[n])
        return _sigmoid(z_s[:, COL_GATE + n * D_MODEL:COL_GATE + (n + 1) * D_MODEL]) * proj

    rows_all = bb * tc
    grp = jnp.right_shift(lane, HEAD_SHIFT)

    def by_group(values):
        out = values[-1]
        for g in range(len(values) - 2, -1, -1):
            out = jnp.where(grp == g, values[g], out)
        return out

    win = by_group(POOL_WINDOWS)
    brow = lax.broadcasted_iota(jnp.int32, (rows_all, 1), 0)
    cnt = jnp.minimum(win, pos0 + j * tc + jnp.bitwise_and(brow, tc - 1) + 1).astype(F32)
    first_half = jnp.bitwise_and(lane, HEAD_D - 1) < HEAD_D // 2
    cos = cos_ref[...]
    sin = sin_ref[...]
    if cos.shape[0] != rows_all:
        cos = jnp.concatenate([cos] * bb, axis=0)
        sin = jnp.concatenate([sin] * bb, axis=0)

    def rope(t):
        swapped = jnp.where(first_half, pltpu.roll(t, MIX_W - HEAD_D // 2, 1), pltpu.roll(t, HEAD_D // 2, 1))
        return t * cos + swapped * sin

    si = lax.broadcasted_iota(jnp.int32, (MIX_W, MIX_W), 0)
    sj = lax.broadcasted_iota(jnp.int32, (MIX_W, MIX_W), 1)
    head_ones = (jnp.right_shift(si, HEAD_SHIFT) == jnp.right_shift(sj, HEAD_SHIFT)).astype(BF16)

    def split_dot(ones, t):
        hi = t.astype(BF16)
        lo = (t - hi.astype(F32)).astype(BF16)
        return jnp.dot(ones, hi, preferred_element_type=F32) + jnp.dot(ones, lo, preferred_element_type=F32)

    def head_sum(t):
        hi = t.astype(BF16)
        lo = (t - hi.astype(F32)).astype(BF16)
        return (jnp.dot(hi, head_ones, preferred_element_type=F32)
                + jnp.dot(lo, head_ones, preferred_element_type=F32))

    c = min(RWKV_CHUNK, tc)
    nsub = tc // c
    merged = c % SUBLANES == 0
    ci = lax.broadcasted_iota(jnp.int32, (c, c), 0)
    cj = lax.broadcasted_iota(jnp.int32, (c, c), 1)
    strict = ci > cj
    incl = ci >= cj
    ti = lax.broadcasted_iota(jnp.int32, (rows_all, rows_all), 0)
    tj = lax.broadcasted_iota(jnp.int32, (rows_all, rows_all), 1)
    shift_c = int(math.log2(c))
    tri = ((ti >= tj) & (jnp.right_shift(ti, shift_c) == jnp.right_shift(tj, shift_c))).astype(BF16)

    for lb in seqs:
        rows = rows_of(lb)
        pool_ext[lb, POOL_HIST:POOL_HIST + tc] = z_s[rows, 0:MIX_W]
        e = pool_ext[lb]
        sums, width = [], 1
        for window in POOL_WINDOWS:
            while width < window:
                e = e + pltpu.roll(e, width, 0)
                width *= 2
            sums.append(e)
        wsum = by_group(sums)
        wsum_s[rows] = wsum[POOL_HIST:POOL_HIST + tc]
        new_pool = pool_ext[lb, tc + POOL_HIST - POOL_PAST:tc + POOL_HIST]
        pool_ext[lb, POOL_HIST - POOL_PAST:POOL_HIST] = new_pool
        pool_ref[lb] = new_pool

        sc_ext[lb, HIST:HIST + tc] = (z_s[rows, COL_SC + 2 * MIX_W:COL_SC + 3 * MIX_W]
                                      * z_s[rows, COL_SC:COL_SC + MIX_W])
        e = sc_ext[lb]
        conv = vec(V_SC2) * e + vec(V_SC1) * pltpu.roll(e, 1, 0) + vec(V_SC0) * pltpu.roll(e, 2, 0)
        conv_s[rows] = conv[HIST:HIST + tc]
        new_sc = sc_ext[lb, HIST + tc - 2:HIST + tc]
        sc_ext[lb, HIST - 2:HIST] = new_sc
        sc_ref[lb] = new_sc

        sh_ext[lb, HIST:HIST + tc] = z_s[rows, COL_RWKV:COL_GATE]
        prev_s[rows] = pltpu.roll(sh_ext[lb], 1, 0)[HIST:HIST + tc]
        new_sh = sh_ext[lb, HIST + tc - 1:HIST + tc]
        sh_ext[lb, HIST - 1:HIST] = new_sh
        sh_ref[lb] = new_sh

    u = z_s[:, 0:MIX_W]
    y_s[:, 0:MIX_W] = _wdot(wsum_s[...] / cnt - u, poolw_ref[...]) * vec(V_POOL_SCALE)
    y_s[:, 2 * MIX_W:3 * MIX_W] = z_s[:, COL_SC + MIX_W:COL_SC + 2 * MIX_W] * conv_s[...]

    q_all = rope(z_s[:, COL_RET:COL_RET + MIX_W])
    k_all = rope(z_s[:, COL_RET + MIX_W:COL_RET + 2 * MIX_W]) * (HEAD_D ** -0.5)
    v_all = z_s[:, COL_RET + 2 * MIX_W:COL_RET + 3 * MIX_W]

    zz = z_s[:, COL_RWKV:COL_GATE]
    zs = zz + (prev_s[...] - zz) * mu_ref[...]
    r = zs[:, 0:MIX_W]
    kx = zs[:, MIX_W:2 * MIX_W]
    vx = zs[:, 2 * MIX_W:3 * MIX_W]
    o3 = 3 * MIX_W
    wl = zs[:, o3:o3 + LORA_W]
    al = zs[:, o3 + LORA_W:o3 + LORA_W + LORA_A]
    gl = zs[:, o3 + LORA_W + LORA_A:]
    wlog = -_softplus(-(vec(V_W0) + _bdot(jnp.tanh(wl), wlora_ref[...]))) - 0.5
    logw = -jnp.exp(wlog)
    asig = _sigmoid(vec(V_A0) + _bdot(al, alora_ref[...]))
    g_out = _bdot(_sigmoid(gl), glora_ref[...])
    k2 = kx * (1.0 + (asig - 1.0) * vec(V_KA))
    kk_raw = kx * vec(V_KK)
    kk = kk_raw * lax.rsqrt(jnp.maximum(head_sum(kk_raw * kk_raw), L2_EPS * L2_EPS))
    cum = split_dot(tri, logw)
    p_in = jnp.exp(cum)
    p_inv = jnp.exp(-cum)
    rw = dict(a=-kk * jnp.exp(cum - logw), b=kk * asig * p_inv, k=k2 * p_inv, r=r * p_in, v=vx, p_in=p_in)

    pairs = [(lb, h) for lb in seqs for h in heads]
    qs = [q_all[rows_of(lb), hsl[h]] for lb, h in pairs]
    ks = [k_all[rows_of(lb), hsl[h]] for lb, h in pairs]
    vs = [v_all[rows_of(lb), hsl[h]] for lb, h in pairs]
    states = [ret_ref[lb, h] for lb, h in pairs]
    np_ = range(len(pairs))
    scores = [_bdot_nt(qs[i], ks[i]) for i in np_]
    cross = [_bdot(qs[i] * jnp.exp(lgs[pairs[i][1]] * (row_f + 1.0)), states[i]) for i in np_]
    kv = [_bdot_tn(ks[i] * jnp.exp(lgs[pairs[i][1]] * (tc - 1.0 - row_f)), vs[i]) for i in np_]
    outs = [_bdot(scores[i] * dm_s[pairs[i][1]], vs[i]) + cross[i] for i in np_]
    for i, (lb, h) in enumerate(pairs):
        ret_ref[lb, h] = states[i] * math.exp(lgs[h] * tc) + kv[i]
        y_s[rows_of(lb), MIX_W + h * HEAD_D:MIX_W + (h + 1) * HEAD_D] = outs[i]
    o_all = y_s[:, MIX_W:2 * MIX_W]
    o_all = o_all * lax.rsqrt(head_sum(o_all * o_all) * (1.0 / HEAD_D) + GN_EPS)
    y_s[:, MIX_W:2 * MIX_W] = o_all * _silu(z_s[:, COL_RET + 3 * MIX_W:COL_RET + 4 * MIX_W])

    blocks = [(lb, sub, h) for lb in seqs for sub in range(nsub) for h in heads]

    def blk(name):
        return [rw[name][lb * tc + sub * c:lb * tc + (sub + 1) * c, hsl[h]] for lb, sub, h in blocks]

    a_l, b_l, k_l, r_l, v_l = blk("a"), blk("b"), blk("k"), blk("r"), blk("v")
    nb = range(len(blocks))
    if merged:
        bk_l = [jnp.concatenate([b_l[i], k_l[i]], axis=0) for i in nb]
        gram = [_bdot_nt(jnp.concatenate([a_l[i], r_l[i]], axis=0), bk_l[i]) for i in nb]
        g_ab, g_ak = [g[:c, :c] for g in gram], [g[:c, c:] for g in gram]
        g_rb, g_rk = [g[c:, :c] for g in gram], [g[c:, c:] for g in gram]
    else:
        g_ab = [_bdot_nt(a_l[i], b_l[i]) for i in nb]
        g_ak = [_bdot_nt(a_l[i], k_l[i]) for i in nb]
        g_rb = [_bdot_nt(r_l[i], b_l[i]) for i in nb]
        g_rk = [_bdot_nt(r_l[i], k_l[i]) for i in nb]
    l_ab = [jnp.where(strict, g, 0.0) for g in g_ab]
    l_ak = [jnp.where(strict, g, 0.0) for g in g_ak]
    m_rb = [jnp.where(incl, g, 0.0) for g in g_rb]
    m_rk = [jnp.where(incl, g, 0.0) for g in g_rk]
    n_inv = _tri_inverse_minus_eye(l_ab, c)
    if merged:
        lv = [_bdot(jnp.concatenate([l_ak[i], m_rk[i]], axis=0), v_l[i]) for i in nb]
        lakv, mv = [t[:c] for t in lv], [t[c:] for t in lv]
    else:
        lakv = [_bdot(l_ak[i], v_l[i]) for i in nb]
        mv = [_bdot(m_rk[i], v_l[i]) for i in nb]
    nx = [_bdot(n_inv[i], jnp.concatenate([lakv[i], a_l[i]], axis=1)) for i in nb]
    tlv = [lakv[i] + nx[i][:, :HEAD_D] for i in nb]
    ta = [a_l[i] + nx[i][:, HEAD_D:] for i in nb]

    wkv_states = [wkv_ref[lb, h] for lb, h in pairs]
    for sub in range(nsub):
        ids = [(lb * nsub + sub) * HEADS + h for lb, h in pairs]
        p_end = [rw["p_in"][lb * tc + (sub + 1) * c - 1:lb * tc + (sub + 1) * c, hsl[h]] for lb, h in pairs]
        u_mat = [_bdot_nt(ta[i], wkv_states[n]) + tlv[i] for n, i in enumerate(ids)]
        y_st = [_bdot_nt(r_l[i], wkv_states[n]) for n, i in enumerate(ids)]
        if merged:
            upd = [_bdot_tn(jnp.concatenate([u_mat[n], v_l[i]], axis=0), bk_l[i] * p_end[n])
                   for n, i in enumerate(ids)]
        else:
            upd = [_bdot_tn(u_mat[n], b_l[i] * p_end[n]) + _bdot_tn(v_l[i], k_l[i] * p_end[n])
                   for n, i in enumerate(ids)]
        y_u = [_bdot(m_rb[i], u_mat[n]) for n, i in enumerate(ids)]
        for n, i in enumerate(ids):
            lb, h = pairs[n]
            ywkv[lb * tc + sub * c:lb * tc + (sub + 1) * c, hsl[h]] = y_st[n] + y_u[n] + mv[i]
        wkv_states = [wkv_states[n] * p_end[n] + upd[n] for n in np_]
    for n, (lb, h) in enumerate(pairs):
        wkv_ref[lb, h] = wkv_states[n]

    y_all = ywkv[...]
    cen = y_all - head_sum(y_all) * (1.0 / HEAD_D)
    var = head_sum(cen * cen) * (1.0 / HEAD_D)
    yn = cen * lax.rsqrt(var + RWKV_LN_EPS) * vec(V_LNG) + vec(V_LNB)
    bonus = head_sum(r * k2 * vec(V_RK)) * vx
    y_s[:, 3 * MIX_W:4 * MIX_W] = (yn + bonus) * g_out

    out = _wdot(gated_proj(0) + gated_proj(1) + gated_proj(2) + gated_proj(3), wout_ref[...])
    for lb in seqs:
        _write_seq_rows(x1_ref, lb, tc, x[rows_of(lb)] + _mod_row(mod_ref, 2, lb) * out[rows_of(lb)])


def _layer_block_spec(arr, bb, layer):
    shape = arr.shape[2:]
    zeros = (0,) * len(shape)
    return pl.BlockSpec((None, bb) + shape, lambda i, j: (layer, i) + zeros)


def _resident_spec(arr, layer=None):
    if layer is None:
        zeros = (0,) * arr.ndim
        return pl.BlockSpec(arr.shape, lambda i, j: zeros, pipeline_mode=pl.Buffered(1))
    zeros = (0,) * (arr.ndim - 1)
    return pl.BlockSpec((None,) + arr.shape[1:], lambda i, j: (layer,) + zeros, pipeline_mode=pl.Buffered(1))


def _rows_operand(x2d, batch, t, bb, tc):
    d = x2d.shape[1]
    if tc == t:
        return x2d, pl.BlockSpec((bb * tc, d), lambda i, j: (i, 0))
    return x2d.reshape(batch, t, d), pl.BlockSpec((bb, tc, d), lambda i, j: (i, j, 0))


def _params():
    return pltpu.CompilerParams(dimension_semantics=("parallel", "arbitrary"), vmem_limit_bytes=VMEM_LIMIT)


def _blocking(batch, t, split=1, rows=256, short_rows=SHORT_ROWS):
    if t >= SHORT_ROWS:
        split = split if batch % split == 0 else 1
        return split, min(t, rows // split)
    return min(batch, short_rows // t), t


def _mix_call(x2d, mod, g1, w_in, cos_t, sin_t, states, poolw, vecs, mu, wlora, alora, glora, w_br, w_out,
              batch, t, pos0, layer, zero_init):
    bb, tc = _blocking(batch, t, MIX_SPLIT, MIX_ROWS)
    nt = t // tc
    tp = _round_up(tc, SUBLANES)
    d = x2d.shape[1]
    out_structs = [jax.ShapeDtypeStruct(st.shape, F32) for st in states]
    state_specs = [_layer_block_spec(st, bb, layer) for st in states]
    weights = (poolw, vecs, mu, wlora, alora, glora, w_br, w_out)
    mod, mod_spec = _mod_operand(mod, bb)
    if zero_init:
        operands = (*weights, *states)
        operand_specs = [_resident_spec(a, layer) for a in weights] + [pl.BlockSpec(memory_space=pl.ANY)] * 5
        first_state = 6 + len(weights)
    else:
        operands = (*states, *weights)
        operand_specs = state_specs + [_resident_spec(a, layer) for a in weights]
        first_state = 6
    x_op, x_spec = _rows_operand(x2d, batch, t, bb, tc)

    table_rows = tc if tc % SUBLANES == 0 else bb * tc

    def block_table(tab):
        if table_rows == tc:
            return tab
        return jnp.tile(tab.reshape(nt, tc, MIX_W), (1, bb, 1)).reshape(nt * bb * tc, MIX_W)

    outs = pl.pallas_call(
        functools.partial(_mix_kernel, bb=bb, tc=tc, pos0=pos0, zero_init=zero_init),
        grid=(batch // bb, nt),
        in_specs=[x_spec, mod_spec,
                  _resident_spec(g1, layer), _resident_spec(w_in, layer),
                  pl.BlockSpec((table_rows, MIX_W), lambda i, j: (j, 0)),
                  pl.BlockSpec((table_rows, MIX_W), lambda i, j: (j, 0))]
        + operand_specs,
        out_specs=[x_spec] + state_specs,
        out_shape=[jax.ShapeDtypeStruct(x_op.shape, F32)] + out_structs,
        input_output_aliases={first_state + k: 1 + k for k in range(5)},
        scratch_shapes=[pltpu.VMEM((bb * tc, d), F32),
                        pltpu.VMEM((bb * tc, IN_COLS), F32),
                        pltpu.VMEM((bb * tc, d), F32),
                        pltpu.VMEM((HEADS, tc, tc), F32),
                        pltpu.VMEM((bb, POOL_HIST + tp, MIX_W), F32),
                        pltpu.VMEM((bb, HIST + tp, MIX_W), F32),
                        pltpu.VMEM((bb, HIST + tp, d), F32),
                        pltpu.VMEM((bb * tc, MIX_W), F32),
                        pltpu.VMEM((bb * tc, MIX_W), F32),
                        pltpu.VMEM((bb * tc, MIX_W), F32),
                        pltpu.VMEM((bb * tc, COL_GATE - COL_RWKV), F32)],
        compiler_params=_params(),
        name="mix",
    )(x_op, mod, g1, w_in, block_table(cos_t), block_table(sin_t), *operands)
    return outs[0].reshape(x2d.shape), tuple(outs[1:])


def _ffn_kernel(*refs, bb, tc, final, zero_init):
    refs = list(refs)
    x1_ref, mod_ref, g_ref, wup_ref, ffnw_ref, wdown_ref, fg_ref = refs[:7]
    del refs[:7]
    st0_ref = refs.pop(0)
    o_ref, st_ref, h_s, ext, act_s = refs
    j = pl.program_id(1)
    seqs = range(bb)

    def rows_of(lb):
        return slice(lb * tc, (lb + 1) * tc)

    if tc < SUBLANES:
        d = x1_ref.shape[1] // tc

        def step_rows(t):
            return slice(t * bb, (t + 1) * bb)

        def step_cols(t):
            return slice(t * d, (t + 1) * d)

        for t in range(tc):
            hn = _rmsnorm(x1_ref[:, step_cols(t)], g_ref[...])
            h_s[step_rows(t)] = hn * (1.0 + _mod_rows(mod_ref, 4)) + _mod_rows(mod_ref, 3)
        rows_hist = 2 * bb
        if zero_init:
            ext[0:rows_hist] = jnp.zeros((rows_hist, 2 * D_FF), F32)
        else:
            for lb in seqs:
                for k in range(2):
                    ext[k * bb + lb:k * bb + lb + 1] = st0_ref[lb, k:k + 1, :]
        ext[rows_hist:rows_hist + tc * bb] = _wdot(h_s[...], wup_ref[...])

        def slab(t, lo, width):
            return ext[rows_hist + t * bb:rows_hist + (t + 1) * bb, lo:lo + width]

        def conv_step(t, lo, width):
            w = ffnw_ref[:, lo:lo + width]
            return (w[2:3] * slab(t, lo, width) + w[1:2] * slab(t - 1, lo, width)
                    + w[0:1] * slab(t - 2, lo, width))

        for lo in range(0, D_FF, CONV_COLS):
            width = min(CONV_COLS, D_FF - lo)
            for t in range(tc):
                act_s[step_rows(t), lo:lo + width] = _silu(conv_step(t, lo, width)) * conv_step(t, D_FF + lo, width)
        for lb in seqs:
            for k in range(2):
                row = rows_hist + (tc - 2 + k) * bb + lb
                st_ref[lb, k:k + 1, :] = ext[row:row + 1]
        dn = _wdot(act_s[...], wdown_ref[...])
        for t in range(tc):
            y = x1_ref[:, step_cols(t)] + _mod_rows(mod_ref, 5) * dn[step_rows(t)]
            o_ref[:, step_cols(t)] = _rmsnorm(y, fg_ref[...]) if final else y
        return

    x1 = _read_rows(x1_ref)
    hn = _rmsnorm(x1, g_ref[...])
    for lb in seqs:
        h_s[rows_of(lb)] = hn[rows_of(lb)] * (1.0 + _mod_row(mod_ref, 4, lb)) + _mod_row(mod_ref, 3, lb)

    @pl.when(j == 0)
    def _init():
        ext[...] = jnp.zeros(ext.shape, F32)
        if not zero_init:
            ext[:, HIST - 2:HIST] = st0_ref[...]

    if bb == 1:
        ext[0, HIST:HIST + tc] = _wdot(h_s[...], wup_ref[...])
    else:
        up = _wdot(h_s[...], wup_ref[...])
        for lb in seqs:
            ext[lb, HIST:HIST + tc] = up[rows_of(lb)]

    for lb in seqs:
        def conv_cols(lo, width):
            e = ext[lb, :, lo:lo + width]
            w = ffnw_ref[:, lo:lo + width]
            y = w[2:3] * e + w[1:2] * pltpu.roll(e, 1, 0) + w[0:1] * pltpu.roll(e, 2, 0)
            return y[HIST:HIST + tc]

        for lo in range(0, D_FF, CONV_COLS):
            width = min(CONV_COLS, D_FF - lo)
            act_s[rows_of(lb), lo:lo + width] = _silu(conv_cols(lo, width)) * conv_cols(D_FF + lo, width)
        new_st = ext[lb, HIST + tc - 2:HIST + tc]
        ext[lb, HIST - 2:HIST] = new_st
        st_ref[lb] = new_st

    dn = _wdot(act_s[...], wdown_ref[...])
    for lb in seqs:
        _write_seq_rows(o_ref, lb, tc, x1[rows_of(lb)] + _mod_row(mod_ref, 5, lb) * dn[rows_of(lb)])
    if final:
        o_ref[...] = _rmsnorm(o_ref[...], fg_ref[...])


def _ffn_call(x1, mod, g2, w_up, ffn_w, w_down, final_g, state, batch, t, final, layer, zero_init):
    bb, tc = _blocking(batch, t, 1, FFN_ROWS, FFN_SHORT_ROWS)
    nt = t // tc
    tp = _round_up(tc, SUBLANES)
    d = x1.shape[1]
    st_struct = jax.ShapeDtypeStruct(state.shape, F32)
    st_spec = _layer_block_spec(state, bb, layer)
    mod, mod_spec = _mod_operand(mod, bb)
    if tc < SUBLANES:
        x_op = x1.reshape(batch, tc * d)
        x_spec = pl.BlockSpec((bb, tc * d), lambda i, j: (i, 0))
    else:
        x_op, x_spec = _rows_operand(x1, batch, t, bb, tc)
    out, new_state = pl.pallas_call(
        functools.partial(_ffn_kernel, bb=bb, tc=tc, final=final, zero_init=zero_init),
        grid=(batch // bb, nt),
        in_specs=[x_spec, mod_spec,
                  _resident_spec(g2, layer), _resident_spec(w_up, layer), _resident_spec(ffn_w, layer),
                  _resident_spec(w_down, layer), _resident_spec(final_g)]
        + [pl.BlockSpec(memory_space=pl.ANY) if zero_init else st_spec],
        out_specs=[x_spec, st_spec],
        out_shape=[jax.ShapeDtypeStruct(x_op.shape, F32), st_struct],
        input_output_aliases={7: 1},
        scratch_shapes=[pltpu.VMEM((bb * tc, d), F32),
                        pltpu.VMEM(((tc + 2) * bb, 2 * D_FF) if tc < SUBLANES else (bb, HIST + tp, 2 * D_FF), F32),
                        pltpu.VMEM((bb * tc, D_FF), F32)],
        compiler_params=_params(),
        name="ffn",
    )(x_op, mod, g2, w_up, ffn_w, w_down, final_g, state)
    return out.reshape(x1.shape), new_state


def _rope_tables(t, pos0):
    half = HEAD_D // 2
    inv = ROPE_BASE ** (-jnp.arange(half, dtype=F32) / half)
    pos = pos0 + jnp.arange(t, dtype=jnp.int32)
    ang = pos.astype(F32)[:, None] * inv[None, :]
    cos, sin = jnp.cos(ang), jnp.sin(ang)
    cos_t = jnp.tile(jnp.concatenate([cos, cos], axis=-1), (1, HEADS))
    sin_t = jnp.tile(jnp.concatenate([-sin, sin], axis=-1), (1, HEADS))
    return cos_t, sin_t


def _block_diag(pool_w):
    groups, gw, _ = pool_w.shape
    out = jnp.zeros((groups * gw, groups * gw), pool_w.dtype)
    for g in range(groups):
        out = out.at[g * gw:(g + 1) * gw, g * gw:(g + 1) * gw].set(pool_w[g])
    return out


def kernel(x_prompt, x_sample, c_prompt, c_sample, state_pool, state_ret, state_sconv, state_shift, state_wkv, state_ffn, w_ada, b_ada, norm1_g, norm2_g, w_in, pool_w, pool_scale, sc_w, rw_mu, rw_w0, rw_w_lora, rw_a0, rw_a_lora, rw_g_lora, rw_k_k, rw_k_a, rw_r_k, rw_ln_g, rw_ln_b, w_br, w_out, w_up, ffn_w, w_down, final_g):
    depth = w_in.shape[0]
    bp, tp_, d = x_prompt.shape
    bs, ts, _ = x_sample.shape

    mods = _mod_call(jnp.concatenate([c_prompt, c_sample], axis=0), w_ada, b_ada)
    cos_p, sin_p = _rope_tables(tp_, 0)
    cos_s, sin_s = _rope_tables(ts, PAST_LEN)
    final_g2 = final_g.reshape(1, d)

    xp = x_prompt.reshape(bp * tp_, d)
    xs = x_sample.reshape(bs * ts, d)
    st_s = (state_pool, state_ret, state_sconv, state_shift.reshape(depth, bs, 1, d), state_wkv)
    st_p = tuple(jnp.zeros((depth, bp) + st.shape[2:], F32) for st in st_s)
    ffn_s = state_ffn
    ffn_p = jnp.zeros((depth, bp) + state_ffn.shape[2:], F32)
    vec_rows = [pool_scale, sc_w[:, 0], sc_w[:, 1], sc_w[:, 2], rw_w0, rw_a0, rw_k_k, rw_k_a,
                rw_r_k.reshape(depth, MIX_W), rw_ln_g, rw_ln_b]
    vecs = jnp.concatenate([jnp.stack(vec_rows, axis=1),
                            jnp.zeros((depth, N_VECS - len(vec_rows), MIX_W), F32)], axis=1)
    mix_w = (norm1_g.reshape(depth, 1, d), w_in.astype(BF16))
    seq_w = (jnp.stack([_block_diag(pool_w[l]) for l in range(depth)]).astype(BF16), vecs,
             rw_mu.reshape(depth, 1, d), rw_w_lora, rw_a_lora, rw_g_lora, w_br.astype(BF16), w_out.astype(BF16))
    ffn_ws = (norm2_g.reshape(depth, 1, d), w_up.astype(BF16), ffn_w, w_down.astype(BF16), final_g2)
    for l in range(depth):
        mod_p, mod_s = mods[l, :, :bp], mods[l, :, bp:]
        final = l == depth - 1

        x1, st_p = _mix_call(xp, mod_p, *mix_w, cos_p, sin_p, st_p, *seq_w, batch=bp, t=tp_, pos0=0,
                             layer=l, zero_init=True)
        xp, ffn_p = _ffn_call(x1, mod_p, *ffn_ws, ffn_p, batch=bp, t=tp_, final=final, layer=l, zero_init=True)
        x1, st_s = _mix_call(xs, mod_s, *mix_w, cos_s, sin_s, st_s, *seq_w, batch=bs, t=ts, pos0=PAST_LEN,
                             layer=l, zero_init=False)
        xs, ffn_s = _ffn_call(x1, mod_s, *ffn_ws, ffn_s, batch=bs, t=ts, final=final, layer=l, zero_init=False)

    def finish(st, ffn_st, batch):
        pool, ret, sc, sh, wkv = st
        return (pool, ret, sc, sh.reshape(depth, batch, d), wkv, ffn_st)

    return (xp.reshape(bp, tp_, d), xs.reshape(bs, ts, d), *finish(st_p, ffn_p, bp), *finish(st_s, ffn_s, bs))
```

```python
import functools
import math

import jax
import jax.numpy as jnp
from jax import lax
from jax.experimental import pallas as pl
from jax.experimental.pallas import tpu as pltpu

F32 = jnp.float32
BF16 = jnp.bfloat16

D_MODEL = 1024
MIX_W = 256
HEADS = 4
HEAD_D = 64
POOL_PAST = 15
ROPE_BASE = 10000.0
LORA_W = 64
LORA_A = 64
LORA_G = 128
D_FF = 2816
PAST_LEN = 16384
NORM_EPS = 1e-6
GN_EPS = 1e-6
RWKV_LN_EPS = 64e-5
L2_EPS = 1e-12

COL_RET = MIX_W
COL_SC = COL_RET + 4 * MIX_W
COL_RWKV = COL_SC + 3 * MIX_W
COL_GATE = COL_RWKV + 3 * MIX_W + LORA_W + LORA_A + LORA_G
IN_COLS = COL_GATE + 4 * D_MODEL

SUBLANES = 8
VMEM_LIMIT = 56 * 1024 * 1024
RWKV_CHUNK = 64
INV_BASE = 16
POOL_HIST = 16
HIST = SUBLANES
CONV_COLS = 512
MIX_SPLIT = 8
MIX_ROWS = 512
FFN_ROWS = 512
SHORT_ROWS = 64
FFN_SHORT_ROWS = 128
POOL_WINDOWS = (2, 4, 8, 16)
HEAD_SHIFT = 6

(V_POOL_SCALE, V_SC0, V_SC1, V_SC2, V_W0, V_A0, V_KK, V_KA, V_RK, V_LNG, V_LNB) = range(11)
N_VECS = 16


def _bdot(a, b):
    return jnp.dot(a.astype(BF16), b.astype(BF16), preferred_element_type=F32)


def _bdot_nt(a, b):
    return lax.dot_general(a.astype(BF16), b.astype(BF16), (((1,), (1,)), ((), ())), preferred_element_type=F32)


def _bdot_tn(a, b):
    return lax.dot_general(a.astype(BF16), b.astype(BF16), (((0,), (0,)), ((), ())), preferred_element_type=F32)


def _wdot(a, w_bf16):
    return jnp.dot(a.astype(BF16), w_bf16, preferred_element_type=F32)


def _sigmoid(x):
    return 1.0 / (1.0 + jnp.exp(-x))


def _silu(x):
    half = 0.5 * x
    return half + half * jnp.tanh(half)


def _softplus(x):
    return jnp.maximum(x, 0.0) + jnp.log(1.0 + jnp.exp(-jnp.abs(x)))


def _rmsnorm(x, g):
    return x * lax.rsqrt(jnp.mean(x * x, axis=-1, keepdims=True) + NORM_EPS) * g


def _round_up(n, m):
    return (n + m - 1) // m * m


def _mod_row(mod_ref, k, lb):
    if len(mod_ref.shape) == 4:
        return mod_ref[k, lb]
    return mod_ref[k, lb:lb + 1, :]


def _mod_rows(mod_ref, k):
    rows = mod_ref[k]
    return rows.reshape(rows.shape[0], rows.shape[-1])


def _mod_operand(mod, bb, layer):
    _, six, _, d = mod.shape
    if bb % SUBLANES:
        return mod[:, :, :, None, :], pl.BlockSpec((None, six, bb, 1, d), lambda i, j: (layer, 0, i, 0, 0))
    return mod, pl.BlockSpec((None, six, bb, d), lambda i, j: (layer, 0, i, 0))


def _read_rows(ref):
    val = ref[...]
    return val.reshape(-1, val.shape[-1]) if val.ndim == 3 else val


def _write_seq_rows(ref, lb, tc, val):
    if len(ref.shape) == 3:
        ref[lb] = val
    else:
        ref[lb * tc:(lb + 1) * tc] = val


def _mod_kernel(c_ref, w_ref, b_ref, *o_refs, splits):
    out = _wdot(_silu(c_ref[...]), w_ref[0].astype(BF16)) + b_ref[0]
    lo = 0
    for o_ref, n in zip(o_refs, splits):
        o_ref[0, 0] = out[lo:lo + n]
        lo += n


def _mod_call(c_groups, w_ada, b_ada):
    depth, d, cols = w_ada.shape
    splits = tuple(c.shape[0] for c in c_groups)
    rows = sum(splits)
    cb = D_MODEL
    return pl.pallas_call(
        functools.partial(_mod_kernel, splits=splits),
        grid=(depth, cols // cb),
        in_specs=[pl.BlockSpec((rows, d), lambda l, c: (0, 0)),
                  pl.BlockSpec((1, d, cb), lambda l, c: (l, 0, c)),
                  pl.BlockSpec((1, 1, cb), lambda l, c: (l, 0, c))],
        out_specs=[pl.BlockSpec((1, 1, n, cb), lambda l, c: (l, c, 0, 0)) for n in splits],
        out_shape=[jax.ShapeDtypeStruct((depth, cols // cb, n, cb), F32) for n in splits],
        name="adaln_mod",
    )(jnp.concatenate(c_groups, axis=0), w_ada, b_ada.reshape(depth, 1, cols))


def _tri_inverse_minus_eye(lowers, n):
    i = lax.broadcasted_iota(jnp.int32, (n, n), 0)
    j = lax.broadcasted_iota(jnp.int32, (n, n), 1)

    def same_block(size):
        sh = int(math.log2(size))
        return jnp.right_shift(i, sh) == jnp.right_shift(j, sh)

    base = min(n, INV_BASE)
    if base < n:
        diag_mask = same_block(base)
        xs = [jnp.where(diag_mask, low, 0.0) for low in lowers]
    else:
        xs = list(lowers)
    powers, p = xs, 1
    while 2 * p < base:
        powers = [_bdot(pw, pw) for pw in powers]
        prods = [_bdot(x, pw) for x, pw in zip(xs, powers)]
        xs = [x + pw + pr for x, pw, pr in zip(xs, powers, prods)]
        p *= 2
    size = base
    while size < n:
        off_mask = same_block(2 * size) & jnp.logical_not(same_block(size))
        offs = [jnp.where(off_mask, low, 0.0) for low in lowers]
        lefts = [off + _bdot(x, off) for x, off in zip(xs, offs)]
        xs = [x + left + _bdot(left, x) for x, left in zip(xs, lefts)]
        size *= 2
    return xs


def _mix_kernel(*refs, bb, tc, pos0, zero_init):
    refs = list(refs)
    x_ref, mod_ref, g_ref, win_ref, cos_ref, sin_ref = refs[:6]
    del refs[:6]
    if not zero_init:
        pool0_ref, ret0_ref, sc0_ref, sh0_ref, wkv0_ref = refs[:5]
        del refs[:5]
    poolw_ref, vecs_ref, mu_ref, wlora_ref, alora_ref, glora_ref, wbr_ref, wout_ref = refs[:8]
    del refs[:8 + (5 if zero_init else 0)]
    x1_ref, pool_ref, ret_ref, sc_ref, sh_ref, wkv_ref = refs[:6]
    h_s, z_s, y_s, dm_s, pool_ext, sc_ext, sh_ext, ywkv, wsum_s, conv_s, prev_s = refs[6:]
    j = pl.program_id(1)
    seqs = range(bb)
    heads = range(HEADS)
    hsl = [slice(h * HEAD_D, (h + 1) * HEAD_D) for h in heads]

    def rows_of(lb):
        return slice(lb * tc, (lb + 1) * tc)

    lane = lax.broadcasted_iota(jnp.int32, (1, MIX_W), 1)
    row_i = lax.broadcasted_iota(jnp.int32, (tc, 1), 0)
    col_i = lax.broadcasted_iota(jnp.int32, (1, tc), 1)
    row_f = row_i.astype(F32)
    lgs = [math.log1p(-(2.0 ** (-5.0 - h))) for h in heads]

    @pl.when(j == 0)
    def _init():
        diff = (row_i - col_i).astype(F32)
        for h in heads:
            dm_s[h] = jnp.where(diff >= 0, jnp.exp(lgs[h] * jnp.maximum(diff, 0.0)), 0.0)
        pool_ext[...] = jnp.zeros(pool_ext.shape, F32)
        sc_ext[...] = jnp.zeros(sc_ext.shape, F32)
        sh_ext[...] = jnp.zeros(sh_ext.shape, F32)
        if zero_init:
            ret_ref[...] = jnp.zeros(ret_ref.shape, F32)
            wkv_ref[...] = jnp.zeros(wkv_ref.shape, F32)
        else:
            pool_ext[:, POOL_HIST - POOL_PAST:POOL_HIST] = pool0_ref[...]
            sc_ext[:, HIST - 2:HIST] = sc0_ref[...]
            sh_ext[:, HIST - 1:HIST] = sh0_ref[...]
            ret_ref[...] = ret0_ref[...]
            wkv_ref[...] = wkv0_ref[...]

    def vec(row):
        return vecs_ref[row:row + 1, :]

    x = _read_rows(x_ref)
    hn = _rmsnorm(x, g_ref[...])
    for lb in seqs:
        h_s[rows_of(lb)] = hn[rows_of(lb)] * (1.0 + _mod_row(mod_ref, 1, lb)) + _mod_row(mod_ref, 0, lb)
    z_s[...] = _wdot(h_s[...], win_ref[...])

    def gated_proj(n):
        proj = _wdot(y_s[:, n * MIX_W:(n + 1) * MIX_W], wbr_ref[n])
        return _sigmoid(z_s[:, COL_GATE + n * D_MODEL:COL_GATE + (n + 1) * D_MODEL]) * proj

    rows_all = bb * tc
    grp = jnp.right_shift(lane, HEAD_SHIFT)

    def by_group(values):
        out = values[-1]
        for g in range(len(values) - 2, -1, -1):
            out = jnp.where(grp == g, values[g], out)
        return out

    win = by_group(POOL_WINDOWS)
    brow = lax.broadcasted_iota(jnp.int32, (rows_all, 1), 0)
    cnt = jnp.minimum(win, pos0 + j * tc + jnp.bitwise_and(brow, tc - 1) + 1).astype(F32)
    first_half = jnp.bitwise_and(lane, HEAD_D - 1) < HEAD_D // 2
    cos = cos_ref[...]
    sin = sin_ref[...]
    if cos.shape[0] != rows_all:
        cos = jnp.concatenate([cos] * bb, axis=0)
        sin = jnp.concatenate([sin] * bb, axis=0)

    def rope(t):
        swapped = jnp.where(first_half, pltpu.roll(t, MIX_W - HEAD_D // 2, 1), pltpu.roll(t, HEAD_D // 2, 1))
        return t * cos + swapped * sin

    si = lax.broadcasted_iota(jnp.int32, (MIX_W, MIX_W), 0)
    sj = lax.broadcasted_iota(jnp.int32, (MIX_W, MIX_W), 1)
    head_ones = (jnp.right_shift(si, HEAD_SHIFT) == jnp.right_shift(sj, HEAD_SHIFT)).astype(BF16)

    def split_dot(ones, t):
        hi = t.astype(BF16)
        lo = (t - hi.astype(F32)).astype(BF16)
        return jnp.dot(ones, hi, preferred_element_type=F32) + jnp.dot(ones, lo, preferred_element_type=F32)

    def head_sum(t):
        hi = t.astype(BF16)
        lo = (t - hi.astype(F32)).astype(BF16)
        return (jnp.dot(hi, head_ones, preferred_element_type=F32)
                + jnp.dot(lo, head_ones, preferred_element_type=F32))

    c = min(RWKV_CHUNK, tc)
    nsub = tc // c
    merged = c % SUBLANES == 0
    ci = lax.broadcasted_iota(jnp.int32, (c, c), 0)
    cj = lax.broadcasted_iota(jnp.int32, (c, c), 1)
    strict = ci > cj
    incl = ci >= cj
    ti = lax.broadcasted_iota(jnp.int32, (rows_all, rows_all), 0)
    tj = lax.broadcasted_iota(jnp.int32, (rows_all, rows_all), 1)
    shift_c = int(math.log2(c))
    tri = ((ti >= tj) & (jnp.right_shift(ti, shift_c) == jnp.right_shift(tj, shift_c))).astype(BF16)

    for lb in seqs:
        rows = rows_of(lb)
        pool_ext[lb, POOL_HIST:POOL_HIST + tc] = z_s[rows, 0:MIX_W]
        e = pool_ext[lb]
        sums, width = [], 1
        for window in POOL_WINDOWS:
            while width < window:
                e = e + pltpu.roll(e, width, 0)
                width *= 2
            sums.append(e)
        wsum = by_group(sums)
        wsum_s[rows] = wsum[POOL_HIST:POOL_HIST + tc]
        new_pool = pool_ext[lb, tc + POOL_HIST - POOL_PAST:tc + POOL_HIST]
        pool_ext[lb, POOL_HIST - POOL_PAST:POOL_HIST] = new_pool
        pool_ref[lb] = new_pool

        sc_ext[lb, HIST:HIST + tc] = (z_s[rows, COL_SC + 2 * MIX_W:COL_SC + 3 * MIX_W]
                                      * z_s[rows, COL_SC:COL_SC + MIX_W])
        e = sc_ext[lb]
        conv = vec(V_SC2) * e + vec(V_SC1) * pltpu.roll(e, 1, 0) + vec(V_SC0) * pltpu.roll(e, 2, 0)
        conv_s[rows] = conv[HIST:HIST + tc]
        new_sc = sc_ext[lb, HIST + tc - 2:HIST + tc]
        sc_ext[lb, HIST - 2:HIST] = new_sc
        sc_ref[lb] = new_sc

        sh_ext[lb, HIST:HIST + tc] = z_s[rows, COL_RWKV:COL_GATE]
        prev_s[rows] = pltpu.roll(sh_ext[lb], 1, 0)[HIST:HIST + tc]
        new_sh = sh_ext[lb, HIST + tc - 1:HIST + tc]
        sh_ext[lb, HIST - 1:HIST] = new_sh
        sh_ref[lb] = new_sh

    u = z_s[:, 0:MIX_W]
    y_s[:, 0:MIX_W] = _wdot(wsum_s[...] / cnt - u, poolw_ref[...]) * vec(V_POOL_SCALE)
    y_s[:, 2 * MIX_W:3 * MIX_W] = z_s[:, COL_SC + MIX_W:COL_SC + 2 * MIX_W] * conv_s[...]

    q_all = rope(z_s[:, COL_RET:COL_RET + MIX_W])
    k_all = rope(z_s[:, COL_RET + MIX_W:COL_RET + 2 * MIX_W]) * (HEAD_D ** -0.5)
    v_all = z_s[:, COL_RET + 2 * MIX_W:COL_RET + 3 * MIX_W]

    zz = z_s[:, COL_RWKV:COL_GATE]
    zs = zz + (prev_s[...] - zz) * mu_ref[...]
    r = zs[:, 0:MIX_W]
    kx = zs[:, MIX_W:2 * MIX_W]
    vx = zs[:, 2 * MIX_W:3 * MIX_W]
    o3 = 3 * MIX_W
    wl = zs[:, o3:o3 + LORA_W]
    al = zs[:, o3 + LORA_W:o3 + LORA_W + LORA_A]
    gl = zs[:, o3 + LORA_W + LORA_A:]
    wlog = -_softplus(-(vec(V_W0) + _bdot(jnp.tanh(wl), wlora_ref[...]))) - 0.5
    logw = -jnp.exp(wlog)
    asig = _sigmoid(vec(V_A0) + _bdot(al, alora_ref[...]))
    g_out = _bdot(_sigmoid(gl), glora_ref[...])
    k2 = kx * (1.0 + (asig - 1.0) * vec(V_KA))
    kk_raw = kx * vec(V_KK)
    kk = kk_raw * lax.rsqrt(jnp.maximum(head_sum(kk_raw * kk_raw), L2_EPS * L2_EPS))
    cum = split_dot(tri, logw)
    p_in = jnp.exp(cum)
    p_inv = jnp.exp(-cum)
    rw = dict(a=-kk * jnp.exp(cum - logw), b=kk * asig * p_inv, k=k2 * p_inv, r=r * p_in, v=vx, p_in=p_in)

    pairs = [(lb, h) for lb in seqs for h in heads]
    qs = [q_all[rows_of(lb), hsl[h]] for lb, h in pairs]
    ks = [k_all[rows_of(lb), hsl[h]] for lb, h in pairs]
    vs = [v_all[rows_of(lb), hsl[h]] for lb, h in pairs]
    states = [ret_ref[lb, h] for lb, h in pairs]
    np_ = range(len(pairs))
    scores = [_bdot_nt(qs[i], ks[i]) for i in np_]
    cross = [_bdot(qs[i] * jnp.exp(lgs[pairs[i][1]] * (row_f + 1.0)), states[i]) for i in np_]
    kv = [_bdot_tn(ks[i] * jnp.exp(lgs[pairs[i][1]] * (tc - 1.0 - row_f)), vs[i]) for i in np_]
    outs = [_bdot(scores[i] * dm_s[pairs[i][1]], vs[i]) + cross[i] for i in np_]
    for i, (lb, h) in enumerate(pairs):
        ret_ref[lb, h] = states[i] * math.exp(lgs[h] * tc) + kv[i]
        y_s[rows_of(lb), MIX_W + h * HEAD_D:MIX_W + (h + 1) * HEAD_D] = outs[i]
    o_all = y_s[:, MIX_W:2 * MIX_W]
    o_all = o_all * lax.rsqrt(head_sum(o_all * o_all) * (1.0 / HEAD_D) + GN_EPS)
    y_s[:, MIX_W:2 * MIX_W] = o_all * _silu(z_s[:, COL_RET + 3 * MIX_W:COL_RET + 4 * MIX_W])

    blocks = [(lb, sub, h) for lb in seqs for sub in range(nsub) for h in heads]

    def blk(name):
        return [rw[name][lb * tc + sub * c:lb * tc + (sub + 1) * c, hsl[h]] for lb, sub, h in blocks]

    a_l, b_l, k_l, r_l, v_l = blk("a"), blk("b"), blk("k"), blk("r"), blk("v")
    nb = range(len(blocks))
    if merged:
        bk_l = [jnp.concatenate([b_l[i], k_l[i]], axis=0) for i in nb]
        gram = [_bdot_nt(jnp.concatenate([a_l[i], r_l[i]], axis=0), bk_l[i]) for i in nb]
        g_ab, g_ak = [g[:c, :c] for g in gram], [g[:c, c:] for g in gram]
        g_rb, g_rk = [g[c:, :c] for g in gram], [g[c:, c:] for g in gram]
    else:
        g_ab = [_bdot_nt(a_l[i], b_l[i]) for i in nb]
        g_ak = [_bdot_nt(a_l[i], k_l[i]) for i in nb]
        g_rb = [_bdot_nt(r_l[i], b_l[i]) for i in nb]
        g_rk = [_bdot_nt(r_l[i], k_l[i]) for i in nb]
    l_ab = [jnp.where(strict, g, 0.0) for g in g_ab]
    l_ak = [jnp.where(strict, g, 0.0) for g in g_ak]
    m_rb = [jnp.where(incl, g, 0.0) for g in g_rb]
    m_rk = [jnp.where(incl, g, 0.0) for g in g_rk]
    n_inv = _tri_inverse_minus_eye(l_ab, c)
    if merged:
        lv = [_bdot(jnp.concatenate([l_ak[i], m_rk[i]], axis=0), v_l[i]) for i in nb]
        lakv, mv = [t[:c] for t in lv], [t[c:] for t in lv]
    else:
        lakv = [_bdot(l_ak[i], v_l[i]) for i in nb]
        mv = [_bdot(m_rk[i], v_l[i]) for i in nb]
    nx = [_bdot(n_inv[i], jnp.concatenate([lakv[i], a_l[i]], axis=1)) for i in nb]
    tlv = [lakv[i] + nx[i][:, :HEAD_D] for i in nb]
    ta = [a_l[i] + nx[i][:, HEAD_D:] for i in nb]

    wkv_states = [wkv_ref[lb, h] for lb, h in pairs]
    for sub in range(nsub):
        ids = [(lb * nsub + sub) * HEADS + h for lb, h in pairs]
        p_end = [rw["p_in"][lb * tc + (sub + 1) * c - 1:lb * tc + (sub + 1) * c, hsl[h]] for lb, h in pairs]
        u_mat = [_bdot_nt(ta[i], wkv_states[n]) + tlv[i] for n, i in enumerate(ids)]
        y_st = [_bdot_nt(r_l[i], wkv_states[n]) for n, i in enumerate(ids)]
        if merged:
            upd = [_bdot_tn(jnp.concatenate([u_mat[n], v_l[i]], axis=0), bk_l[i] * p_end[n])
                   for n, i in enumerate(ids)]
        else:
            upd = [_bdot_tn(u_mat[n], b_l[i] * p_end[n]) + _bdot_tn(v_l[i], k_l[i] * p_end[n])
                   for n, i in enumerate(ids)]
        y_u = [_bdot(m_rb[i], u_mat[n]) for n, i in enumerate(ids)]
        for n, i in enumerate(ids):
            lb, h = pairs[n]
            ywkv[lb * tc + sub * c:lb * tc + (sub + 1) * c, hsl[h]] = y_st[n] + y_u[n] + mv[i]
        wkv_states = [wkv_states[n] * p_end[n] + upd[n] for n in np_]
    for n, (lb, h) in enumerate(pairs):
        wkv_ref[lb, h] = wkv_states[n]

    y_all = ywkv[...]
    cen = y_all - head_sum(y_all) * (1.0 / HEAD_D)
    var = head_sum(cen * cen) * (1.0 / HEAD_D)
    yn = cen * lax.rsqrt(var + RWKV_LN_EPS) * vec(V_LNG) + vec(V_LNB)
    bonus = head_sum(r * k2 * vec(V_RK)) * vx
    y_s[:, 3 * MIX_W:4 * MIX_W] = (yn + bonus) * g_out

    out = _wdot(gated_proj(0) + gated_proj(1) + gated_proj(2) + gated_proj(3), wout_ref[...])
    for lb in seqs:
        _write_seq_rows(x1_ref, lb, tc, x[rows_of(lb)] + _mod_row(mod_ref, 2, lb) * out[rows_of(lb)])


def _layer_block_spec(arr, bb, layer):
    shape = arr.shape[2:]
    zeros = (0,) * len(shape)
    return pl.BlockSpec((None, bb) + shape, lambda i, j: (layer, i) + zeros)


def _resident_spec(arr, layer=None):
    if layer is None:
        zeros = (0,) * arr.ndim
        return pl.BlockSpec(arr.shape, lambda i, j: zeros, pipeline_mode=pl.Buffered(1))
    zeros = (0,) * (arr.ndim - 1)
    return pl.BlockSpec((None,) + arr.shape[1:], lambda i, j: (layer,) + zeros, pipeline_mode=pl.Buffered(1))


def _rows_operand(x2d, batch, t, bb, tc):
    d = x2d.shape[1]
    if tc == t:
        return x2d, pl.BlockSpec((bb * tc, d), lambda i, j: (i, 0))
    return x2d.reshape(batch, t, d), pl.BlockSpec((bb, tc, d), lambda i, j: (i, j, 0))


def _params():
    return pltpu.CompilerParams(dimension_semantics=("parallel", "arbitrary"), vmem_limit_bytes=VMEM_LIMIT)


def _blocking(batch, t, split=1, rows=256, short_rows=SHORT_ROWS):
    if t >= SHORT_ROWS:
        split = split if batch % split == 0 else 1
        return split, min(t, rows // split)
    return min(batch, short_rows // t), t


def _mix_call(x2d, mod, g1, w_in, cos_t, sin_t, states, poolw, vecs, mu, wlora, alora, glora, w_br, w_out,
              batch, t, pos0, layer, zero_init):
    bb, tc = _blocking(batch, t, MIX_SPLIT, MIX_ROWS)
    nt = t // tc
    tp = _round_up(tc, SUBLANES)
    d = x2d.shape[1]
    out_structs = [jax.ShapeDtypeStruct(st.shape, F32) for st in states]
    state_specs = [_layer_block_spec(st, bb, layer) for st in states]
    weights = (poolw, vecs, mu, wlora, alora, glora, w_br, w_out)
    mod, mod_spec = _mod_operand(mod, bb, layer)
    if zero_init:
        operands = (*weights, *states)
        operand_specs = [_resident_spec(a, layer) for a in weights] + [pl.BlockSpec(memory_space=pl.ANY)] * 5
        first_state = 6 + len(weights)
    else:
        operands = (*states, *weights)
        operand_specs = state_specs + [_resident_spec(a, layer) for a in weights]
        first_state = 6
    x_op, x_spec = _rows_operand(x2d, batch, t, bb, tc)

    table_rows = tc if tc % SUBLANES == 0 else bb * tc

    def block_table(tab):
        if table_rows == tc:
            return tab
        return jnp.tile(tab.reshape(nt, tc, MIX_W), (1, bb, 1)).reshape(nt * bb * tc, MIX_W)

    outs = pl.pallas_call(
        functools.partial(_mix_kernel, bb=bb, tc=tc, pos0=pos0, zero_init=zero_init),
        grid=(batch // bb, nt),
        in_specs=[x_spec, mod_spec,
                  _resident_spec(g1, layer), _resident_spec(w_in, layer),
                  pl.BlockSpec((table_rows, MIX_W), lambda i, j: (j, 0)),
                  pl.BlockSpec((table_rows, MIX_W), lambda i, j: (j, 0))]
        + operand_specs,
        out_specs=[x_spec] + state_specs,
        out_shape=[jax.ShapeDtypeStruct(x_op.shape, F32)] + out_structs,
        input_output_aliases={first_state + k: 1 + k for k in range(5)},
        scratch_shapes=[pltpu.VMEM((bb * tc, d), F32),
                        pltpu.VMEM((bb * tc, IN_COLS), F32),
                        pltpu.VMEM((bb * tc, d), F32),
                        pltpu.VMEM((HEADS, tc, tc), F32),
                        pltpu.VMEM((bb, POOL_HIST + tp, MIX_W), F32),
                        pltpu.VMEM((bb, HIST + tp, MIX_W), F32),
                        pltpu.VMEM((bb, HIST + tp, d), F32),
                        pltpu.VMEM((bb * tc, MIX_W), F32),
                        pltpu.VMEM((bb * tc, MIX_W), F32),
                        pltpu.VMEM((bb * tc, MIX_W), F32),
                        pltpu.VMEM((bb * tc, COL_GATE - COL_RWKV), F32)],
        compiler_params=_params(),
        name="mix",
    )(x_op, mod, g1, w_in, block_table(cos_t), block_table(sin_t), *operands)
    return outs[0].reshape(x2d.shape), tuple(outs[1:])


def _ffn_kernel(*refs, bb, tc, final, zero_init):
    refs = list(refs)
    x1_ref, mod_ref, g_ref, wup_ref, ffnw_ref, wdown_ref, fg_ref = refs[:7]
    del refs[:7]
    st0_ref = refs.pop(0)
    o_ref, st_ref, h_s, ext, act_s = refs
    j = pl.program_id(1)
    seqs = range(bb)

    def rows_of(lb):
        return slice(lb * tc, (lb + 1) * tc)

    if tc < SUBLANES:
        d = x1_ref.shape[1] // tc

        def step_rows(t):
            return slice(t * bb, (t + 1) * bb)

        def step_cols(t):
            return slice(t * d, (t + 1) * d)

        for t in range(tc):
            hn = _rmsnorm(x1_ref[:, step_cols(t)], g_ref[...])
            h_s[step_rows(t)] = hn * (1.0 + _mod_rows(mod_ref, 4)) + _mod_rows(mod_ref, 3)
        rows_hist = 2 * bb
        if zero_init:
            ext[0:rows_hist] = jnp.zeros((rows_hist, 2 * D_FF), F32)
        else:
            for lb in seqs:
                for k in range(2):
                    ext[k * bb + lb:k * bb + lb + 1] = st0_ref[lb, k:k + 1, :]
        ext[rows_hist:rows_hist + tc * bb] = _wdot(h_s[...], wup_ref[...])

        def slab(t, lo, width):
            return ext[rows_hist + t * bb:rows_hist + (t + 1) * bb, lo:lo + width]

        def conv_step(t, lo, width):
            w = ffnw_ref[:, lo:lo + width]
            return (w[2:3] * slab(t, lo, width) + w[1:2] * slab(t - 1, lo, width)
                    + w[0:1] * slab(t - 2, lo, width))

        for lo in range(0, D_FF, CONV_COLS):
            width = min(CONV_COLS, D_FF - lo)
            for t in range(tc):
                act_s[step_rows(t), lo:lo + width] = _silu(conv_step(t, lo, width)) * conv_step(t, D_FF + lo, width)
        for lb in seqs:
            for k in range(2):
                row = rows_hist + (tc - 2 + k) * bb + lb
                st_ref[lb, k:k + 1, :] = ext[row:row + 1]
        dn = _wdot(act_s[...], wdown_ref[...])
        for t in range(tc):
            y = x1_ref[:, step_cols(t)] + _mod_rows(mod_ref, 5) * dn[step_rows(t)]
            o_ref[:, step_cols(t)] = _rmsnorm(y, fg_ref[...]) if final else y
        return

    x1 = _read_rows(x1_ref)
    hn = _rmsnorm(x1, g_ref[...])
    for lb in seqs:
        h_s[rows_of(lb)] = hn[rows_of(lb)] * (1.0 + _mod_row(mod_ref, 4, lb)) + _mod_row(mod_ref, 3, lb)

    @pl.when(j == 0)
    def _init():
        ext[...] = jnp.zeros(ext.shape, F32)
        if not zero_init:
            ext[:, HIST - 2:HIST] = st0_ref[...]

    if bb == 1:
        ext[0, HIST:HIST + tc] = _wdot(h_s[...], wup_ref[...])
    else:
        up = _wdot(h_s[...], wup_ref[...])
        for lb in seqs:
            ext[lb, HIST:HIST + tc] = up[rows_of(lb)]

    for lb in seqs:
        def conv_cols(lo, width):
            e = ext[lb, :, lo:lo + width]
            w = ffnw_ref[:, lo:lo + width]
            y = w[2:3] * e + w[1:2] * pltpu.roll(e, 1, 0) + w[0:1] * pltpu.roll(e, 2, 0)
            return y[HIST:HIST + tc]

        for lo in range(0, D_FF, CONV_COLS):
            width = min(CONV_COLS, D_FF - lo)
            act_s[rows_of(lb), lo:lo + width] = _silu(conv_cols(lo, width)) * conv_cols(D_FF + lo, width)
        new_st = ext[lb, HIST + tc - 2:HIST + tc]
        ext[lb, HIST - 2:HIST] = new_st
        st_ref[lb] = new_st

    dn = _wdot(act_s[...], wdown_ref[...])
    for lb in seqs:
        _write_seq_rows(o_ref, lb, tc, x1[rows_of(lb)] + _mod_row(mod_ref, 5, lb) * dn[rows_of(lb)])
    if final:
        o_ref[...] = _rmsnorm(o_ref[...], fg_ref[...])


def _ffn_call(x1, mod, g2, w_up, ffn_w, w_down, final_g, state, batch, t, final, layer, zero_init):
    bb, tc = _blocking(batch, t, 1, FFN_ROWS, FFN_SHORT_ROWS)
    nt = t // tc
    tp = _round_up(tc, SUBLANES)
    d = x1.shape[1]
    st_struct = jax.ShapeDtypeStruct(state.shape, F32)
    st_spec = _layer_block_spec(state, bb, layer)
    mod, mod_spec = _mod_operand(mod, bb, layer)
    if tc < SUBLANES:
        x_op = x1.reshape(batch, tc * d)
        x_spec = pl.BlockSpec((bb, tc * d), lambda i, j: (i, 0))
    else:
        x_op, x_spec = _rows_operand(x1, batch, t, bb, tc)
    out, new_state = pl.pallas_call(
        functools.partial(_ffn_kernel, bb=bb, tc=tc, final=final, zero_init=zero_init),
        grid=(batch // bb, nt),
        in_specs=[x_spec, mod_spec,
                  _resident_spec(g2, layer), _resident_spec(w_up, layer), _resident_spec(ffn_w, layer),
                  _resident_spec(w_down, layer), _resident_spec(final_g)]
        + [pl.BlockSpec(memory_space=pl.ANY) if zero_init else st_spec],
        out_specs=[x_spec, st_spec],
        out_shape=[jax.ShapeDtypeStruct(x_op.shape, F32), st_struct],
        input_output_aliases={7: 1},
        scratch_shapes=[pltpu.VMEM((bb * tc, d), F32),
                        pltpu.VMEM(((tc + 2) * bb, 2 * D_FF) if tc < SUBLANES else (bb, HIST + tp, 2 * D_FF), F32),
                        pltpu.VMEM((bb * tc, D_FF), F32)],
        compiler_params=_params(),
        name="ffn",
    )(x_op, mod, g2, w_up, ffn_w, w_down, final_g, state)
    return out.reshape(x1.shape), new_state


def _rope_tables(t, pos0):
    half = HEAD_D // 2
    inv = ROPE_BASE ** (-jnp.arange(half, dtype=F32) / half)
    pos = pos0 + jnp.arange(t, dtype=jnp.int32)
    ang = pos.astype(F32)[:, None] * inv[None, :]
    cos, sin = jnp.cos(ang), jnp.sin(ang)
    cos_t = jnp.tile(jnp.concatenate([cos, cos], axis=-1), (1, HEADS))
    sin_t = jnp.tile(jnp.concatenate([-sin, sin], axis=-1), (1, HEADS))
    return cos_t, sin_t


def _block_diag(pool_w):
    groups, gw, _ = pool_w.shape
    out = jnp.zeros((groups * gw, groups * gw), pool_w.dtype)
    for g in range(groups):
        out = out.at[g * gw:(g + 1) * gw, g * gw:(g + 1) * gw].set(pool_w[g])
    return out


def kernel(x_prompt, x_sample, c_prompt, c_sample, state_pool, state_ret, state_sconv, state_shift, state_wkv, state_ffn, w_ada, b_ada, norm1_g, norm2_g, w_in, pool_w, pool_scale, sc_w, rw_mu, rw_w0, rw_w_lora, rw_a0, rw_a_lora, rw_g_lora, rw_k_k, rw_k_a, rw_r_k, rw_ln_g, rw_ln_b, w_br, w_out, w_up, ffn_w, w_down, final_g):
    depth = w_in.shape[0]
    bp, tp_, d = x_prompt.shape
    bs, ts, _ = x_sample.shape

    mod_p, mod_s = _mod_call([c_prompt, c_sample], w_ada, b_ada)
    cos_p, sin_p = _rope_tables(tp_, 0)
    cos_s, sin_s = _rope_tables(ts, PAST_LEN)
    final_g2 = final_g.reshape(1, d)

    xp = x_prompt.reshape(bp * tp_, d)
    xs = x_sample.reshape(bs * ts, d)
    st_s = (state_pool, state_ret, state_sconv, state_shift.reshape(depth, bs, 1, d), state_wkv)
    st_p = tuple(jnp.zeros((depth, bp) + st.shape[2:], F32) for st in st_s)
    ffn_s = state_ffn
    ffn_p = jnp.zeros((depth, bp) + state_ffn.shape[2:], F32)
    vec_rows = [pool_scale, sc_w[:, 0], sc_w[:, 1], sc_w[:, 2], rw_w0, rw_a0, rw_k_k, rw_k_a,
                rw_r_k.reshape(depth, MIX_W), rw_ln_g, rw_ln_b]
    vecs = jnp.concatenate([jnp.stack(vec_rows, axis=1),
                            jnp.zeros((depth, N_VECS - len(vec_rows), MIX_W), F32)], axis=1)
    mix_w = (norm1_g.reshape(depth, 1, d), w_in.astype(BF16))
    seq_w = (jnp.stack([_block_diag(pool_w[l]) for l in range(depth)]).astype(BF16), vecs,
             rw_mu.reshape(depth, 1, d), rw_w_lora, rw_a_lora, rw_g_lora, w_br.astype(BF16), w_out.astype(BF16))
    ffn_ws = (norm2_g.reshape(depth, 1, d), w_up.astype(BF16), ffn_w, w_down.astype(BF16), final_g2)
    for l in range(depth):
        final = l == depth - 1

        x1, st_p = _mix_call(xp, mod_p, *mix_w, cos_p, sin_p, st_p, *seq_w, batch=bp, t=tp_, pos0=0,
                             layer=l, zero_init=True)
        xp, ffn_p = _ffn_call(x1, mod_p, *ffn_ws, ffn_p, batch=bp, t=tp_, final=final, layer=l, zero_init=True)
        x1, st_s = _mix_call(xs, mod_s, *mix_w, cos_s, sin_s, st_s, *seq_w, batch=bs, t=ts, pos0=PAST_LEN,
                             layer=l, zero_init=False)
        xs, ffn_s = _ffn_call(x1, mod_s, *ffn_ws, ffn_s, batch=bs, t=ts, final=final, layer=l, zero_init=False)

    def finish(st, ffn_st, batch):
        pool, ret, sc, sh, wkv = st
        return (pool, ret, sc, sh.reshape(depth, batch, d), wkv, ffn_st)

    return (xp.reshape(bp, tp_, d), xs.reshape(bs, ts, d), *finish(st_p, ffn_p, bp), *finish(st_s, ffn_s, bs))
```
